```python
import jax
import jax.numpy as jnp
from jax import lax
import numpy as np

D_MODEL = 1024
BATCH = 16
SEQ = 256
DEPTH = 4
DEC_BATCH = 2
DEC_SEQ = 1024
PAST_LEN = 512

GRID_W = 64
CHUNK = 128
N_MIXERS = 3
EPS = 1e-6
ROPE_BASE = 10000.0

RET_HEADS = 8
RET_DK = 128
RET_DV = 256
RET_QK = RET_HEADS * RET_DK
RET_WIDTH = RET_HEADS * RET_DV

ML_HEADS = 4
ML_WIDTH = 2 * D_MODEL
ML_HD = ML_WIDTH // ML_HEADS
ML_BLOCK = 4
ML_NBLK = ML_WIDTH // ML_BLOCK
ML_CONV = 5

ATT_HEADS = 16
ATT_KV = 4
ATT_HD = 64
ATT_GROUP = ATT_HEADS // ATT_KV
ATT_WIDTH = ATT_HEADS * ATT_HD
ATT_KVW = ATT_KV * ATT_HD
WINDOW = 128

kernel_name = 'hybrid_flow_retention_mlstm_swa_step'

F32 = jnp.float32


def _rmsnorm(x, g):
    xf = x.astype(F32)
    y = xf * lax.rsqrt(jnp.mean(xf * xf, axis=-1, keepdims=True) + EPS)
    return (y * g.astype(F32)).astype(x.dtype)


def _head_norm(x, g):
    B, T, H, hd = x.shape
    xf = x.astype(F32)
    xc = xf - jnp.mean(xf, axis=-1, keepdims=True)
    y = xc * lax.rsqrt(jnp.mean(xc * xc, axis=-1, keepdims=True) + EPS)
    return y.reshape(B, T, H * hd) * g.astype(F32)


def _adaln(cvec, w, b):
    m = jax.nn.silu(cvec) @ w + b
    sh, sc, gt = jnp.split(m, 3, axis=-1)
    return sh[..., None, :], sc[..., None, :], gt[..., None, :]


def _axial_rope(T, hd):
    rows = T // GRID_W
    row = jnp.repeat(jnp.arange(rows, dtype=F32), GRID_W)
    col = jnp.tile(jnp.arange(GRID_W, dtype=F32), rows)
    nf = hd // 4
    inv = ROPE_BASE ** (-jnp.arange(nf, dtype=F32) / nf)
    ang = jnp.concatenate([row[:, None] * inv[None, :], col[:, None] * inv[None, :]], axis=-1)
    return jnp.cos(ang), jnp.sin(ang)


def _apply_rope(x, cos, sin):
    half = x.shape[-1] // 2
    x1, x2 = x[..., :half], x[..., half:]
    c = cos[:, None, :].astype(x.dtype)
    s = sin[:, None, :].astype(x.dtype)
    return jnp.concatenate([x1 * c - x2 * s, x2 * c + x1 * s], axis=-1)


def _retention_scan(q, k, v, lg, S0):
    B, T, H, _ = q.shape
    dv = v.shape[-1]
    n = T // CHUNK

    def chunks(a):
        return a.astype(F32).reshape(B, n, CHUNK, H, a.shape[-1]).transpose(1, 0, 3, 2, 4)

    idx = jnp.arange(CHUNK, dtype=F32)
    diff = idx[:, None] - idx[None, :]
    decay_in = jnp.where(diff >= 0, jnp.exp(lg[:, None, None] * jnp.maximum(diff, 0.0)), 0.0)
    q_dec = jnp.exp(lg[:, None] * (idx + 1.0))[None, :, :, None]
    k_dec = jnp.exp(lg[:, None] * (CHUNK - 1.0 - idx))[None, :, :, None]
    g_blk = jnp.exp(lg * CHUNK)[None, :, None, None]

    def step(S, xs):
        qb, kb, vb = xs
        s = jnp.einsum('bhid,bhjd->bhij', qb, kb) * decay_in[None]
        out = jnp.einsum('bhij,bhjv->bhiv', s, vb) + jnp.einsum('bhid,bhdv->bhiv', qb, S) * q_dec
        S_new = S * g_blk + jnp.einsum('bhjd,bhjv->bhdv', kb * k_dec, vb)
        return S_new, out

    S_fin, out = lax.scan(step, S0.astype(F32), (chunks(q), chunks(k), chunks(v)))
    return out.transpose(1, 0, 3, 2, 4).reshape(B, T, H, dv), S_fin


def _retention(h, w_in, w_out, decay_f, decay_b, gn, S0_f, S0_b, rope):
    B, T, _ = h.shape
    q, k, v, z = jnp.split(h @ w_in, [RET_QK, 2 * RET_QK, 2 * RET_QK + RET_WIDTH], axis=-1)
    q = q.reshape(B, T, RET_HEADS, RET_DK)
    k = k.reshape(B, T, RET_HEADS, RET_DK) * (RET_DK ** -0.5)
    v = v.reshape(B, T, RET_HEADS, RET_DV)
    if rope is not None:
        q = _apply_rope(q, *rope)
        k = _apply_rope(k, *rope)
    o_f, S_f = _retention_scan(q, k, v, jax.nn.log_sigmoid(decay_f.astype(F32)), S0_f)
    o_b, S_b = _retention_scan(jnp.flip(q, 1), jnp.flip(k, 1), jnp.flip(v, 1),
                               jax.nn.log_sigmoid(decay_b.astype(F32)), S0_b)
    o = _head_norm(o_f + jnp.flip(o_b, 1), gn).astype(h.dtype)
    return (o * jax.nn.silu(z)) @ w_out, S_f, S_b


def _centred_dwconv(x, w, b):
    pad = ML_CONV // 2
    T = x.shape[1]
    xp = jnp.pad(x, ((0, 0), (pad, pad), (0, 0)))
    out = b
    for t in range(ML_CONV):
        out = out + xp[:, t:t + T] * w[t]
    return out


def _blockdiag(x, w):
    B, T, W = x.shape
    return jnp.einsum('btnc,ncd->btnd', x.reshape(B, T, ML_NBLK, ML_BLOCK), w).reshape(B, T, W)


def _mlstm_scan(q, k, v, li, lf, state0):
    B, T, H, hd = q.shape
    n = T // CHUNK

    def chunks(a):
        return a.astype(F32).reshape(B, n, CHUNK, H, hd).transpose(1, 0, 3, 2, 4)

    def gchunks(a):
        return a.astype(F32).reshape(B, n, CHUNK, H).transpose(1, 0, 3, 2)

    idx = jnp.arange(CHUNK)
    causal = idx[:, None] >= idx[None, :]

    def step(carry, xs):
        C, nv, m = carry
        qb, kb, vb, ib, fb = xs
        b = jnp.cumsum(fb, axis=-1)
        d_log = jnp.where(causal, b[..., :, None] - b[..., None, :] + ib[..., None, :], -jnp.inf)
        inter = b + m[..., None]
        m_i = jnp.maximum(inter, jnp.max(d_log, axis=-1))
        w_intra = jnp.exp(d_log - m_i[..., None])
        w_prev = jnp.exp(inter - m_i)
        s = jnp.einsum('bhid,bhjd->bhij', qb, kb) * w_intra
        num = jnp.einsum('bhij,bhjv->bhiv', s, vb) + w_prev[..., None] * jnp.einsum('bhid,bhdv->bhiv', qb, C)
        den = jnp.sum(s, axis=-1) + w_prev * jnp.einsum('bhid,bhd->bhi', qb, nv)
        hb = num / jnp.maximum(jnp.abs(den), jnp.exp(-m_i))[..., None]
        b_last = b[..., -1]
        w_log = b_last[..., None] - b + ib
        m_new = jnp.maximum(b_last + m, jnp.max(w_log, axis=-1))
        w_k = jnp.exp(w_log - m_new[..., None])
        dec = jnp.exp(b_last + m - m_new)
        C_new = dec[..., None, None] * C + jnp.einsum('bhjd,bhjv->bhdv', kb * w_k[..., None], vb)
        n_new = dec[..., None] * nv + jnp.einsum('bhjd,bhj->bhd', kb, w_k)
        return (C_new, n_new, m_new), hb

    C0, n0, m0 = state0
    carry0 = (C0.astype(F32), n0.astype(F32), m0.astype(F32))
    state, out = lax.scan(step, carry0, (chunks(q), chunks(k), chunks(v), gchunks(li), gchunks(lf)))
    return out.transpose(1, 0, 3, 2, 4).reshape(B, T, H, hd), state


def _mlstm(h, w_in, w_out, conv_w, conv_b, wq, wk, wv, wif_f, bif_f, wif_b, bif_b, gn, skip, st_f, st_b):
    B, T, _ = h.shape
    xm, o_pre, z = jnp.split(h @ w_in, 3, axis=-1)
    xc = jax.nn.silu(_centred_dwconv(xm, conv_w, conv_b))
    qf = _blockdiag(xc, wq)
    kf = _blockdiag(xc, wk)
    vf = _blockdiag(xm, wv)
    g_in = jnp.concatenate([qf, kf, vf], axis=-1)
    g_f = (g_in @ wif_f + bif_f).astype(F32)
    g_b = (g_in @ wif_b + bif_b).astype(F32)
    q = qf.reshape(B, T, ML_HEADS, ML_HD)
    k = kf.reshape(B, T, ML_HEADS, ML_HD) * (ML_HD ** -0.5)
    v = vf.reshape(B, T, ML_HEADS, ML_HD)
    h_f, s_f = _mlstm_scan(q, k, v, g_f[..., :ML_HEADS], jax.nn.log_sigmoid(g_f[..., ML_HEADS:]), st_f)
    h_b, s_b = _mlstm_scan(jnp.flip(q, 1), jnp.flip(k, 1), jnp.flip(v, 1),
                           jnp.flip(g_b[..., :ML_HEADS], 1),
                           jnp.flip(jax.nn.log_sigmoid(g_b[..., ML_HEADS:]), 1), st_b)
    o_gate = jax.nn.sigmoid(o_pre.astype(F32)).reshape(B, T, ML_HEADS, ML_HD)
    cell = _head_norm((h_f + jnp.flip(h_b, 1)) * o_gate, gn).astype(h.dtype)
    return ((cell + skip * xc) * jax.nn.silu(z)) @ w_out, s_f, s_b


def _ctx_attention(q, k, v, sink):
    B, T, H, hd = q.shape
    nb = T // CHUNK
    scale = hd ** -0.5
    qb = q.astype(F32).reshape(B, nb, CHUNK, ATT_KV, ATT_GROUP, hd).transpose(1, 0, 2, 3, 4, 5)
    kf = k.astype(F32)
    vf = v.astype(F32)
    sk = sink.astype(F32).reshape(ATT_KV, ATT_GROUP)[None, :, :, None]

    def block(qblk):
        s = jnp.einsum('blkgd,bskd->bkgls', qblk, kf) * scale
        m = jnp.maximum(jnp.max(s, axis=-1), sk)
        p = jnp.exp(s - m[..., None])
        den = jnp.sum(p, axis=-1) + jnp.exp(sk - m)
        return jnp.einsum('bkgls,bskd->blkgd', p, vf) / den.transpose(0, 3, 1, 2)[..., None]

    out = lax.map(block, qb)
    return out.transpose(1, 0, 2, 3, 4, 5).reshape(B, T, H * hd).astype(q.dtype)


def _window_attention(q, k, v, k_ctx, v_ctx, sink):
    B, T, H, hd = q.shape
    L = CHUNK
    nb = T // L
    scale = hd ** -0.5
    pad = ((0, 0), (L, L), (0, 0), (0, 0))
    kp = jnp.pad(k.astype(F32), pad).reshape(B, nb + 2, L, ATT_KV, hd)
    vp = jnp.pad(v.astype(F32), pad).reshape(B, nb + 2, L, ATT_KV, hd)
    kband = jnp.concatenate([kp[:, :-2], kp[:, 1:-1], kp[:, 2:]], axis=2)
    vband = jnp.concatenate([vp[:, :-2], vp[:, 1:-1], vp[:, 2:]], axis=2)
    a = jnp.arange(L)
    cpos = jnp.arange(3 * L)
    rel = a[:, None] - cpos[None, :] + L
    jpos = jnp.arange(nb)[:, None] * L - L + cpos[None, :]
    mask = (jnp.abs(rel) <= WINDOW)[None] & ((jpos >= 0) & (jpos < T))[:, None, :]
    qb = q.astype(F32).reshape(B, nb, L, ATT_KV, ATT_GROUP, hd)
    s_band = jnp.einsum('bnlkgd,bnckd->bnkglc', qb, kband) * scale
    s_band = jnp.where(mask[None, :, None, None], s_band, -jnp.inf)
    s_ctx = jnp.einsum('bnlkgd,bskd->bnkgls', qb, k_ctx.astype(F32)) * scale
    sk = sink.astype(F32).reshape(ATT_KV, ATT_GROUP)[None, None, :, :, None]
    m = jnp.maximum(jnp.maximum(jnp.max(s_band, axis=-1), jnp.max(s_ctx, axis=-1)), sk)
    p_band = jnp.exp(s_band - m[..., None])
    p_ctx = jnp.exp(s_ctx - m[..., None])
    den = jnp.sum(p_band, axis=-1) + jnp.sum(p_ctx, axis=-1) + jnp.exp(sk - m)
    o = (jnp.einsum('bnkglc,bnckd->bnlkgd', p_band, vband)
         + jnp.einsum('bnkgls,bskd->bnlkgd', p_ctx, v_ctx.astype(F32)))
    o = o / den.transpose(0, 1, 4, 2, 3)[..., None]
    return o.reshape(B, T, H * hd).astype(q.dtype)


def _split_qkvz(h, w_in):
    B, T, _ = h.shape
    q, k, v, z = jnp.split(h @ w_in, [ATT_WIDTH, ATT_WIDTH + ATT_KVW, ATT_WIDTH + 2 * ATT_KVW], axis=-1)
    return (q.reshape(B, T, ATT_HEADS, ATT_HD), k.reshape(B, T, ATT_KV, ATT_HD),
            v.reshape(B, T, ATT_KV, ATT_HD), z)


def _attention_ctx(h, w_in, w_out, sink):
    q, k, v, z = _split_qkvz(h, w_in)
    o = _ctx_attention(q, k, v, sink)
    return (o * jax.nn.silu(z)) @ w_out, k, v


def _attention_latent(h, w_in, w_out, sink, k_ctx, v_ctx, rope):
    q, k, v, z = _split_qkvz(h, w_in)
    q = _apply_rope(q, *rope)
    k = _apply_rope(k, *rope)
    o = _window_attention(q, k, v, k_ctx, v_ctx, sink)
    return (o * jax.nn.silu(z)) @ w_out


def setup_inputs(seed: int = 0) -> dict:
    key = jax.random.key(seed)
    keys = iter(jax.random.split(key, 80))

    def nrm(shape, scale):
        return jax.random.normal(next(keys), shape, F32) * scale

    D = D_MODEL
    d = {}
    d['x_prompt'] = nrm((BATCH, SEQ, D), 1.0)
    d['x_sample'] = nrm((DEC_BATCH, DEC_SEQ, D), 1.0)
    d['c'] = nrm((DEC_BATCH, D), 1.0)
    d['c_ctx'] = nrm((D,), 1.0)
    d['state_l0_ret'] = nrm((DEC_BATCH, 2, RET_HEADS, RET_DK, RET_DV), 0.1)
    d['state_l1_C'] = nrm((DEC_BATCH, 2, ML_HEADS, ML_HD, ML_HD), 0.05)
    d['state_l1_n'] = nrm((DEC_BATCH, 2, ML_HEADS, ML_HD), 0.05)
    d['state_l1_m'] = nrm((DEC_BATCH, 2, ML_HEADS), 0.5)
    d['cache_l2_k'] = nrm((DEC_BATCH, PAST_LEN, ATT_KV, ATT_HD), 1.0)
    d['cache_l2_v'] = nrm((DEC_BATCH, PAST_LEN, ATT_KV, ATT_HD), 1.0)
    d['state_l3_ret'] = nrm((DEC_BATCH, 2, RET_HEADS, RET_DK, RET_DV), 0.1)

    gam = 1.0 - 2.0 ** (-5.0 - np.arange(RET_HEADS, dtype=np.float32))
    ret_logit = jnp.asarray(np.log(gam) - np.log(1.0 - gam), F32)
    f_bias = jnp.asarray(np.linspace(3.0, 6.0, ML_HEADS, dtype=np.float32))

    def common(sfx, width_in, width_out):
        d['norm' + sfx] = 1.0 + nrm((D,), 0.02)
        d['ada_w' + sfx] = nrm((D, 3 * D), D ** -0.5)
        d['ada_b' + sfx] = nrm((3 * D,), 0.02)
        d['w_in' + sfx] = nrm((D, width_in), D ** -0.5)
        d['w_out' + sfx] = nrm((width_out, D), width_out ** -0.5)

    def retention_layer(sfx):
        common(sfx, 2 * RET_QK + 2 * RET_WIDTH, RET_WIDTH)
        d['ret_decay_f' + sfx] = ret_logit + nrm((RET_HEADS,), 0.05)
        d['ret_decay_b' + sfx] = ret_logit + nrm((RET_HEADS,), 0.05)
        d['ret_gn' + sfx] = 1.0 + nrm((RET_WIDTH,), 0.02)

    retention_layer('_l0')

    common('_l1', 3 * ML_WIDTH, ML_WIDTH)
    d['conv_w_l1'] = nrm((ML_CONV, ML_WIDTH), ML_CONV ** -0.5)
    d['conv_b_l1'] = nrm((ML_WIDTH,), 0.02)
    d['wq_l1'] = nrm((ML_NBLK, ML_BLOCK, ML_BLOCK), ML_BLOCK ** -0.5)
    d['wk_l1'] = nrm((ML_NBLK, ML_BLOCK, ML_BLOCK), ML_BLOCK ** -0.5)
    d['wv_l1'] = nrm((ML_NBLK, ML_BLOCK, ML_BLOCK), ML_BLOCK ** -0.5)
    d['wif_f_l1'] = nrm((3 * ML_WIDTH, 2 * ML_HEADS), (3 * ML_WIDTH) ** -0.5)
    d['bif_f_l1'] = jnp.concatenate([nrm((ML_HEADS,), 0.1), f_bias + nrm((ML_HEADS,), 0.05)])
    d['wif_b_l1'] = nrm((3 * ML_WIDTH, 2 * ML_HEADS), (3 * ML_WIDTH) ** -0.5)
    d['bif_b_l1'] = jnp.concatenate([nrm((ML_HEADS,), 0.1), f_bias + nrm((ML_HEADS,), 0.05)])
    d['gn_l1'] = 1.0 + nrm((ML_WIDTH,), 0.02)
    d['skip_l1'] = 1.0 + nrm((ML_WIDTH,), 0.02)

    common('_l2', 2 * ATT_WIDTH + 2 * ATT_KVW, ATT_WIDTH)
    d['sink_l2'] = nrm((ATT_HEADS,), 0.5)

    retention_layer('_l3')

    d['final_norm'] = 1.0 + nrm((D,), 0.02)
    return d


def reference(x_prompt, x_sample, c, c_ctx,
              state_l0_ret, state_l1_C, state_l1_n, state_l1_m, cache_l2_k, cache_l2_v, state_l3_ret,
              norm_l0, ada_w_l0, ada_b_l0, w_in_l0, w_out_l0, ret_decay_f_l0, ret_decay_b_l0, ret_gn_l0,
              norm_l1, ada_w_l1, ada_b_l1, w_in_l1, w_out_l1, conv_w_l1, conv_b_l1, wq_l1, wk_l1, wv_l1,
              wif_f_l1, bif_f_l1, wif_b_l1, bif_b_l1, gn_l1, skip_l1,
              norm_l2, ada_w_l2, ada_b_l2, w_in_l2, w_out_l2, sink_l2,
              norm_l3, ada_w_l3, ada_b_l3, w_in_l3, w_out_l3, ret_decay_f_l3, ret_decay_b_l3, ret_gn_l3,
              final_norm):
    layers = (
        dict(norm=norm_l0, ada_w=ada_w_l0, ada_b=ada_b_l0, w_in=w_in_l0, w_out=w_out_l0,
             extra=(ret_decay_f_l0, ret_decay_b_l0, ret_gn_l0), cache=(state_l0_ret,)),
        dict(norm=norm_l1, ada_w=ada_w_l1, ada_b=ada_b_l1, w_in=w_in_l1, w_out=w_out_l1,
             extra=(conv_w_l1, conv_b_l1, wq_l1, wk_l1, wv_l1, wif_f_l1, bif_f_l1, wif_b_l1, bif_b_l1,
                    gn_l1, skip_l1),
             cache=(state_l1_C, state_l1_n, state_l1_m)),
        dict(norm=norm_l2, ada_w=ada_w_l2, ada_b=ada_b_l2, w_in=w_in_l2, w_out=w_out_l2,
             extra=(sink_l2,), cache=(cache_l2_k, cache_l2_v)),
        dict(norm=norm_l3, ada_w=ada_w_l3, ada_b=ada_b_l3, w_in=w_in_l3, w_out=w_out_l3,
             extra=(ret_decay_f_l3, ret_decay_b_l3, ret_gn_l3), cache=(state_l3_ret,)),
    )
    Bc = x_prompt.shape[0]
    T_lat = x_sample.shape[1]
    rope_ret = _axial_rope(T_lat, RET_DK)
    rope_att = _axial_rope(T_lat, ATT_HD)

    xp, xs = x_prompt, x_sample
    new_state = []
    for i in range(DEPTH):
        p = layers[i]
        kind = i % N_MIXERS
        sh_p, sc_p, gt_p = _adaln(c_ctx, p['ada_w'], p['ada_b'])
        sh_s, sc_s, gt_s = _adaln(c, p['ada_w'], p['ada_b'])
        hp = _rmsnorm(xp, p['norm']) * (1.0 + sc_p) + sh_p
        hs = _rmsnorm(xs, p['norm']) * (1.0 + sc_s) + sh_s
        if kind == 0:
            decay_f, decay_b, gn = p['extra']
            (st,) = p['cache']
            z0 = jnp.zeros((Bc, RET_HEADS, RET_DK, RET_DV), F32)
            yp, S_f, S_b = _retention(hp, p['w_in'], p['w_out'], decay_f, decay_b, gn, z0, z0, None)
            ys, _, _ = _retention(hs, p['w_in'], p['w_out'], decay_f, decay_b, gn,
                                  st[:, 0], st[:, 1], rope_ret)
            new_state.append(jnp.stack([S_f, S_b], axis=1))
        elif kind == 1:
            ml = p['extra']
            C_c, n_c, m_c = p['cache']
            zc = (jnp.zeros((Bc, ML_HEADS, ML_HD, ML_HD), F32), jnp.zeros((Bc, ML_HEADS, ML_HD), F32),
                  jnp.zeros((Bc, ML_HEADS), F32))
            yp, s_f, s_b = _mlstm(hp, p['w_in'], p['w_out'], *ml, zc, zc)
            ys, _, _ = _mlstm(hs, p['w_in'], p['w_out'], *ml,
                              (C_c[:, 0], n_c[:, 0], m_c[:, 0]), (C_c[:, 1], n_c[:, 1], m_c[:, 1]))
            new_state.append(jnp.stack([s_f[0], s_b[0]], axis=1))
            new_state.append(jnp.stack([s_f[1], s_b[1]], axis=1))
            new_state.append(jnp.stack([s_f[2], s_b[2]], axis=1))
        else:
            (sink,) = p['extra']
            k_c, v_c = p['cache']
            yp, k_new, v_new = _attention_ctx(hp, p['w_in'], p['w_out'], sink)
            ys = _attention_latent(hs, p['w_in'], p['w_out'], sink, k_c, v_c, rope_att)
            new_state.append(k_new)
            new_state.append(v_new)
        xp = xp + gt_p * yp
        xs = xs + gt_s * ys

    y_prompt = _rmsnorm(xp, final_norm)
    y_sample = _rmsnorm(xs, final_norm)
    new_l0_ret, new_l1_C, new_l1_n, new_l1_m, new_l2_k, new_l2_v, new_l3_ret = new_state
    return (y_prompt, y_sample, new_l0_ret, new_l1_C, new_l1_n, new_l1_m, new_l2_k, new_l2_v, new_l3_ret)
```

```python
import functools

import jax
import jax.numpy as jnp
from jax import lax
from jax.experimental import pallas as pl
from jax.experimental.pallas import tpu as pltpu

F32 = jnp.float32
BF16 = jnp.bfloat16

D = 1024
BATCH = 16
SEQ = 256
DEC_BATCH = 2
DEC_SEQ = 1024
PAST_LEN = 512
GRID_W = 64
CHUNK = 128
EPS = 1e-6
ROPE_BASE = 10000.0

N_PROMPT = BATCH * SEQ
N_TOK = N_PROMPT + DEC_BATCH * DEC_SEQ
MOD_ROWS = 8

RET_HEADS = 8
RET_DK = 128
RET_DV = 256
RET_QK = RET_HEADS * RET_DK
RET_WIDTH = RET_HEADS * RET_DV

ML_HEADS = 4
ML_WIDTH = 2 * D
ML_HD = ML_WIDTH // ML_HEADS
ML_BLOCK = 4
ML_CONV = 5
ML_TILE = 256

ATT_HEADS = 16
ATT_KV = 4
ATT_HD = 64
ATT_GROUP = ATT_HEADS // ATT_KV
ATT_WIDTH = ATT_HEADS * ATT_HD
ATT_KVW = ATT_KV * ATT_HD
WINDOW = 128

VMEM_LIMIT = 56 * 1024 * 1024


def _params(*sem):
    return pltpu.CompilerParams(dimension_semantics=sem, vmem_limit_bytes=VMEM_LIMIT)


def _mod_row(row0):
    return jnp.maximum((row0 - N_PROMPT) // DEC_SEQ + 1, 0)


def _dot(a, b):
    return jnp.dot(a, b, preferred_element_type=F32)


def _dot_nt(a, b):
    return lax.dot_general(a, b, (((1,), (1,)), ((), ())), preferred_element_type=F32)


def _dot_tn(a, b):
    return lax.dot_general(a, b, (((0,), (0,)), ((), ())), preferred_element_type=F32)


def _silu(x):
    return x * jax.nn.sigmoid(x)


def _log_sigmoid(x):
    return jnp.minimum(x, 0.0) - jnp.log1p(jnp.exp(-jnp.abs(x)))


def _rms(x, g):
    return x * lax.rsqrt(jnp.mean(x * x, axis=-1, keepdims=True) + EPS) * g


def _group_norm(x, g):
    xc = x - jnp.mean(x, axis=-1, keepdims=True)
    return xc * lax.rsqrt(jnp.mean(xc * xc, axis=-1, keepdims=True) + EPS) * g


def _ada_kernel(ct_ref, w_ref, b_ref, o_ref):
    s = _silu(ct_ref[...])
    w = w_ref[...]
    rows = [jnp.sum(w * s[:, r:r + 1], axis=0, keepdims=True) for r in range(1 + DEC_BATCH)]
    rows.append(jnp.zeros((MOD_ROWS - len(rows), w.shape[1]), F32))
    o_ref[...] = jnp.concatenate(rows, axis=0) + b_ref[...]


def _adaln(ct, w, b):
    tn = 1024
    return pl.pallas_call(
        _ada_kernel,
        grid=(3 * D // tn,),
        in_specs=[pl.BlockSpec((D, MOD_ROWS), lambda j: (0, 0)),
                  pl.BlockSpec((D, tn), lambda j: (0, j)),
                  pl.BlockSpec((1, tn), lambda j: (0, j))],
        out_specs=pl.BlockSpec((MOD_ROWS, tn), lambda j: (0, j)),
        out_shape=jax.ShapeDtypeStruct((MOD_ROWS, 3 * D), F32),
        compiler_params=_params("arbitrary"),
        name="adaln",
    )(ct, w, b.reshape(1, 3 * D))


def _norm_mod_kernel(x_ref, g_ref, mod_ref, h_ref, *, tm):
    m = mod_ref[pl.ds(_mod_row(pl.program_id(0) * tm), 1), :]
    y = _rms(x_ref[...], g_ref[...])
    h_ref[...] = (y * (1.0 + m[:, D:2 * D]) + m[:, :D]).astype(BF16)


def _norm_mod(x, g, mod):
    tm = 512
    return pl.pallas_call(
        functools.partial(_norm_mod_kernel, tm=tm),
        grid=(N_TOK // tm,),
        in_specs=[pl.BlockSpec((tm, D), lambda i: (i, 0)),
                  pl.BlockSpec((1, D), lambda i: (0, 0)),
                  pl.BlockSpec((MOD_ROWS, 3 * D), lambda i: (0, 0))],
        out_specs=pl.BlockSpec((tm, D), lambda i: (i, 0)),
        out_shape=jax.ShapeDtypeStruct((N_TOK, D), BF16),
        compiler_params=_params("arbitrary"),
        name="norm_mod",
    )(x, g.reshape(1, D), mod)


def _in_proj_kernel(h_ref, w_ref, o_ref, wb_ref):
    @pl.when(pl.program_id(1) == 0)
    def _():
        wb_ref[...] = w_ref[...].astype(BF16)

    o_ref[...] = _dot(h_ref[...], wb_ref[...]).astype(BF16)


def _in_proj(h, w, tn):
    tm = 1024
    n = w.shape[1]
    return pl.pallas_call(
        _in_proj_kernel,
        grid=(n // tn, N_TOK // tm),
        in_specs=[pl.BlockSpec((tm, D), lambda j, i: (i, 0)),
                  pl.BlockSpec((D, tn), lambda j, i: (0, j))],
        out_specs=pl.BlockSpec((tm, tn), lambda j, i: (i, j)),
        out_shape=jax.ShapeDtypeStruct((N_TOK, n), BF16),
        scratch_shapes=[pltpu.VMEM((D, tn), BF16)],
        compiler_params=_params("arbitrary", "arbitrary"),
        name="in_proj",
    )(h, w)


def _out_proj_kernel(a_ref, w_ref, x_ref, mod_ref, g_ref, modn_ref, xo_ref, h_ref, wb_ref, *, tm, final):
    @pl.when(pl.program_id(0) == 0)
    def _():
        wb_ref[...] = w_ref[...].astype(BF16)

    r = _mod_row(pl.program_id(0) * tm)
    gate = mod_ref[pl.ds(r, 1), :][:, 2 * D:]
    xn = x_ref[...] + gate * _dot(a_ref[...], wb_ref[...])
    xo_ref[...] = xn
    y = _rms(xn, g_ref[...])
    if final:
        h_ref[...] = y
    else:
        mn = modn_ref[pl.ds(r, 1), :]
        h_ref[...] = (y * (1.0 + mn[:, D:2 * D]) + mn[:, :D]).astype(BF16)


def _out_proj(a, w, x, mod, g_next, mod_next, final):
    tm = 512
    kw = w.shape[0]
    return pl.pallas_call(
        functools.partial(_out_proj_kernel, tm=tm, final=final),
        grid=(N_TOK // tm,),
        in_specs=[pl.BlockSpec((tm, kw), lambda i: (i, 0)),
                  pl.BlockSpec((kw, D), lambda i: (0, 0)),
                  pl.BlockSpec((tm, D), lambda i: (i, 0)),
                  pl.BlockSpec((MOD_ROWS, 3 * D), lambda i: (0, 0)),
                  pl.BlockSpec((1, D), lambda i: (0, 0)),
                  pl.BlockSpec((MOD_ROWS, 3 * D), lambda i: (0, 0))],
        out_specs=[pl.BlockSpec((tm, D), lambda i: (i, 0)),
                   pl.BlockSpec((tm, D), lambda i: (i, 0))],
        out_shape=[jax.ShapeDtypeStruct((N_TOK, D), F32),
                   jax.ShapeDtypeStruct((N_TOK, D), F32 if final else BF16)],
        scratch_shapes=[pltpu.VMEM((kw, D), BF16)],
        compiler_params=_params("arbitrary"),
        name="out_proj",
    )(a, w, x, mod, g_next.reshape(1, D), mod_next)


def _ret_kernel(*refs, T, rope, state_in, state_out, has_prev):
    refs = list(refs)
    dec_ref, q_ref, k_ref, v_ref, z_ref, gn_ref = refs[:6]
    pos = 6
    if rope:
        cos_ref, sin_ref = refs[pos:pos + 2]
        pos += 2
    if state_in:
        s0_ref = refs[pos]
        pos += 1
    if has_prev:
        pos += 1
    o_ref = refs[pos]
    pos += 1
    if state_out:
        sn_ref = refs[pos]
        pos += 1
    acc_ref = refs[pos]
    pos += 1
    if rope:
        qs_ref, ks_ref = refs[pos:pos + 2]

    h = pl.program_id(1)
    n = T // CHUNK
    L = CHUNK

    if rope:
        for src, dst in ((q_ref, qs_ref), (k_ref, ks_ref)):
            x = src[...].astype(F32)
            dst[...] = (x * cos_ref[...] + pltpu.roll(x, RET_DK // 2, 1) * sin_ref[...]).astype(BF16)
        q_src, k_src = qs_ref, ks_ref
    else:
        q_src, k_src = q_ref, k_ref

    ii = lax.broadcasted_iota(jnp.int32, (L, L), 0).astype(F32)
    jj = lax.broadcasted_iota(jnp.int32, (L, L), 1).astype(F32)
    row_v = lax.broadcasted_iota(jnp.int32, (L, RET_DV), 0).astype(F32)
    scale = RET_DK ** -0.5

    for d in range(2):
        lg = _log_sigmoid(jnp.full((1, RET_DV), dec_ref[d, h], F32))
        lg_k = lg[:, :L]
        if d == 0:
            diff = ii - jj
            q_dec = jnp.exp(lg * (row_v + 1.0))
            k_dec = jnp.exp(lg_k * (L - 1.0 - ii)) * scale
        else:
            diff = jj - ii
            q_dec = jnp.exp(lg * (L - row_v))
            k_dec = jnp.exp(lg_k * ii) * scale
        mask = jnp.where(diff >= 0, jnp.exp(lg_k * jnp.maximum(diff, 0.0)), 0.0) * scale
        g_blk = jnp.exp(lg * float(L))[:, :1]

        def body(c, S, d=d, mask=mask, q_dec=q_dec, k_dec=k_dec, g_blk=g_blk):
            cc = c if d == 0 else n - 1 - c
            rows = pl.ds(pl.multiple_of(cc * L, L), L)
            qc = q_src[rows, :]
            kc = k_src[rows, :]
            vc = v_ref[rows, :]
            s = _dot_nt(qc, kc) * mask
            o = _dot(s.astype(BF16), vc) + _dot(qc, S.astype(BF16)) * q_dec
            if d == 0:
                acc_ref[rows, :] = o
            else:
                acc_ref[rows, :] += o
            kw = (kc.astype(F32) * k_dec).astype(BF16)
            return S * g_blk + _dot_tn(kw, vc)

        S0 = s0_ref[d] if state_in else jnp.zeros((RET_DK, RET_DV), F32)
        S_fin = lax.fori_loop(0, n, body, S0)
        if state_out:
            sn_ref[d] = S_fin

    zf = z_ref[...].astype(F32)
    o_ref[...] = (_group_norm(acc_ref[...], gn_ref[...]) * _silu(zf)).astype(BF16)


def _retention(proj, dec, gn, T, nb, row_blk0, rope=None, state=None, want_state=False, prev=None):
    in_specs = [pl.BlockSpec(memory_space=pltpu.SMEM),
                pl.BlockSpec((T, RET_DK), lambda b, h: (row_blk0 + b, h)),
                pl.BlockSpec((T, RET_DK), lambda b, h: (row_blk0 + b, RET_HEADS + h)),
                pl.BlockSpec((T, RET_DV), lambda b, h: (row_blk0 + b, RET_HEADS + h)),
                pl.BlockSpec((T, RET_DV), lambda b, h: (row_blk0 + b, 2 * RET_HEADS + h)),
                pl.BlockSpec((1, RET_DV), lambda b, h: (0, h))]
    args = [dec, proj, proj, proj, proj, gn.reshape(1, RET_WIDTH)]
    if rope is not None:
        in_specs += [pl.BlockSpec((T, RET_DK), lambda b, h: (0, 0))] * 2
        args += list(rope)
    if state is not None:
        in_specs.append(pl.BlockSpec((None, 2, None, RET_DK, RET_DV), lambda b, h: (b, 0, h, 0, 0)))
        args.append(state)
    aliases = {}
    if prev is not None:
        aliases = {len(args): 0}
        in_specs.append(pl.BlockSpec(memory_space=pl.ANY))
        args.append(prev)
    out_specs = [pl.BlockSpec((T, RET_DV), lambda b, h: (row_blk0 + b, h))]
    out_shape = [jax.ShapeDtypeStruct((N_TOK, RET_WIDTH), BF16)]
    if want_state:
        out_specs.append(pl.BlockSpec((None, 2, None, RET_DK, RET_DV), lambda b, h: (b, 0, h, 0, 0)))
        out_shape.append(jax.ShapeDtypeStruct((nb, 2, RET_HEADS, RET_DK, RET_DV), F32))
    scratch = [pltpu.VMEM((T, RET_DV), F32)]
    if rope is not None:
        scratch += [pltpu.VMEM((T, RET_DK), BF16)] * 2
    return pl.pallas_call(
        functools.partial(_ret_kernel, T=T, rope=rope is not None, state_in=state is not None,
                          state_out=want_state, has_prev=prev is not None),
        grid=(nb, RET_HEADS),
        in_specs=in_specs, out_specs=out_specs, out_shape=out_shape, scratch_shapes=scratch,
        input_output_aliases=aliases,
        compiler_params=_params("arbitrary", "arbitrary"),
        name="retention",
    )(*args)


def _ml_pre_kernel(xm_ref, cw_ref, cb_ref, bq_ref, bk_ref, bv_ref, wq_ref, wk_ref, wv_ref, bias_ref, *rest, T):
    q_ref, k_ref, v_ref, xc_ref, g_ref = rest[-5:]
    j = pl.program_id(1)
    xb = xm_ref[...]
    x = xb.astype(F32)
    row = lax.broadcasted_iota(jnp.int32, x.shape, 0)
    pad = ML_CONV // 2
    conv = cb_ref[...] + x * cw_ref[pad:pad + 1, :]
    for t in range(ML_CONV):
        delta = t - pad
        if delta == 0:
            continue
        shifted = pltpu.roll(x, (-delta) % T, 0)
        ok = (row + delta >= 0) & (row + delta < T)
        conv = conv + jnp.where(ok, shifted, 0.0) * cw_ref[t:t + 1, :]
    xc = _silu(conv)
    xcb = xc.astype(BF16)
    qf = _dot(xcb, bq_ref[...])
    kf = _dot(xcb, bk_ref[...])
    vf = _dot(xb, bv_ref[...])
    qb, kb, vb = qf.astype(BF16), kf.astype(BF16), vf.astype(BF16)
    q_ref[...] = qb
    k_ref[...] = kb
    v_ref[...] = vb
    xc_ref[...] = xcb
    g = (_dot(qb, wq_ref[...].astype(BF16)) + _dot(kb, wk_ref[...].astype(BF16))
         + _dot(vb, wv_ref[...].astype(BF16)))

    @pl.when(j == 0)
    def _():
        g_ref[...] = g + bias_ref[...]

    @pl.when(j > 0)
    def _():
        g_ref[...] += g


def _ml_pre(proj, conv_w, conv_b, bdq, bdk, bdv, wif, bif, T, nb, row_blk0, prev=None):
    nj = ML_WIDTH // ML_TILE
    ng = wif.shape[1]
    tok = lambda b, j: (row_blk0 + b, j)
    bd = pl.BlockSpec((None, ML_TILE, ML_TILE), lambda b, j: (j, 0, 0))
    out_shape = [jax.ShapeDtypeStruct((N_TOK, ML_WIDTH), BF16)] * 4 + [jax.ShapeDtypeStruct((N_TOK, ng), F32)]
    in_specs = [pl.BlockSpec((T, ML_TILE), tok),
                pl.BlockSpec((ML_CONV, ML_TILE), lambda b, j: (0, j)),
                pl.BlockSpec((1, ML_TILE), lambda b, j: (0, j)),
                bd, bd, bd,
                pl.BlockSpec((ML_TILE, ng), lambda b, j: (j, 0)),
                pl.BlockSpec((ML_TILE, ng), lambda b, j: (nj + j, 0)),
                pl.BlockSpec((ML_TILE, ng), lambda b, j: (2 * nj + j, 0)),
                pl.BlockSpec((1, ng), lambda b, j: (0, 0))]
    args = [proj, conv_w, conv_b.reshape(1, ML_WIDTH), bdq, bdk, bdv, wif, wif, wif, bif.reshape(1, ng)]
    aliases = {}
    if prev is not None:
        aliases = {len(args) + i: i for i in range(5)}
        in_specs += [pl.BlockSpec(memory_space=pl.ANY)] * 5
        args += list(prev)
    return pl.pallas_call(
        functools.partial(_ml_pre_kernel, T=T),
        grid=(nb, nj),
        in_specs=in_specs,
        out_specs=[pl.BlockSpec((T, ML_TILE), tok)] * 4 + [pl.BlockSpec((T, ng), lambda b, j: (row_blk0 + b, 0))],
        out_shape=out_shape,
        input_output_aliases=aliases,
        compiler_params=_params("arbitrary", "arbitrary"),
        name="mlstm_pre",
    )(*args)


def _dot_split(lhs_bf16, rhs):
    r1 = rhs.astype(BF16)
    e1 = rhs - r1.astype(F32)
    r2 = e1.astype(BF16)
    r3 = (e1 - r2.astype(F32)).astype(BF16)
    return _dot(lhs_bf16, r1) + _dot(lhs_bf16, r2) + _dot(lhs_bf16, r3)


def _ml_scan_kernel(*refs, T, state_in, state_out, has_prev):
    refs = list(refs)
    pos = 0
    if state_in:
        m0_ref = refs[0]
        pos = 1
    q_ref, k_ref, v_ref, g_ref, op_ref, z_ref, xc_ref, gn_ref, sk_ref = refs[pos:pos + 9]
    pos += 9
    if state_in:
        c0_ref, n0_ref = refs[pos:pos + 2]
        pos += 2
    if has_prev:
        pos += 1
    o_ref = refs[pos]
    pos += 1
    if state_out:
        cn_ref, nn_ref, mn_ref = refs[pos:pos + 3]
        pos += 3
    acc_ref, gi_ref, gf_ref = refs[pos:pos + 3]
    pos += 3
    C_ref = cn_ref if state_out else refs[pos]

    b = pl.program_id(0)
    h = pl.program_id(1)
    d = pl.program_id(2)
    n = T // CHUNK
    L = CHUNK
    scale = ML_HD ** -0.5
    fwd = d == 0

    g = g_ref[...]
    col = lax.broadcasted_iota(jnp.int32, g.shape, 1)
    gi_ref[...] = jnp.sum(jnp.where(col == d * (2 * ML_HEADS) + h, g, 0.0), axis=1, keepdims=True)
    gf_ref[...] = _log_sigmoid(
        jnp.sum(jnp.where(col == d * (2 * ML_HEADS) + ML_HEADS + h, g, 0.0), axis=1, keepdims=True))

    ii = lax.broadcasted_iota(jnp.int32, (L, L), 0)
    jj = lax.broadcasted_iota(jnp.int32, (L, L), 1)
    one = jnp.ones((L, L), F32)
    zero = jnp.zeros((L, L), F32)
    tri_lo = jnp.where(jj <= ii, one, zero)
    tri_hi = jnp.where(jj >= ii, one, zero)
    eye = jnp.where(ii == jj, one, zero)
    lower = jnp.where(fwd, tri_lo, tri_hi)
    upper = jnp.where(fwd, tri_lo, tri_hi) - eye
    neg = jnp.full((L, L), -jnp.inf, F32)
    bias = jnp.where(fwd, jnp.where(ii >= jj, zero, neg), jnp.where(ii <= jj, zero, neg))
    lhs = jnp.concatenate([lower, one], axis=1).astype(BF16)

    if state_in:
        C_ref[...] = c0_ref[...]
        n_init = n0_ref[...]
        m_init = jnp.full((1, 1), m0_ref[(b * 2 + d) * ML_HEADS + h], F32)
    else:
        C_ref[...] = jnp.zeros((ML_HD, ML_HD), F32)
        n_init = jnp.zeros((1, ML_HD), F32)
        m_init = jnp.zeros((1, 1), F32)

    def body(c, carry):
        nv, m = carry
        cc = jnp.where(fwd, c, n - 1 - c)
        rows = pl.ds(pl.multiple_of(cc * L, L), L)
        qc = q_ref[rows, :]
        kc = k_ref[rows, :]
        vc = v_ref[rows, :]
        gi = gi_ref[rows, :]
        gf = gf_ref[rows, :]
        rhs = jnp.concatenate(
            [jnp.concatenate([gf * upper, gf * one], axis=1),
             jnp.concatenate([gi * eye, zero], axis=1)], axis=0)
        pb = _dot_split(lhs, rhs)
        d_log = pb[:, :L] + bias
        b_col = pb[:, L:L + 1]
        inter = b_col + m
        m_i = jnp.maximum(inter, jnp.max(d_log, axis=1, keepdims=True))
        w_intra = jnp.exp(d_log - m_i)
        w_prev = jnp.exp(inter - m_i)
        s = _dot_nt(qc, kc) * (w_intra * scale)
        num = _dot(s.astype(BF16), vc) + w_prev * _dot(qc, C_ref[...].astype(BF16))
        qn = jnp.sum(qc.astype(F32) * nv, axis=1, keepdims=True)
        den = jnp.sum(s, axis=1, keepdims=True) + w_prev * qn
        hb = num / jnp.maximum(jnp.abs(den), jnp.exp(-m_i))

        @pl.when(fwd)
        def _():
            acc_ref[rows, :] = hb

        @pl.when(jnp.logical_not(fwd))
        def _():
            acc_ref[rows, :] += hb

        b_last = jnp.sum(gf, axis=0, keepdims=True)
        w_log = b_last - b_col + gi
        m_new = jnp.maximum(b_last + m, jnp.max(w_log, axis=0, keepdims=True))
        w_k = jnp.exp(w_log - m_new) * scale
        dec = jnp.exp(b_last + m - m_new)
        kw = kc.astype(F32) * w_k
        C_ref[...] = dec * C_ref[...] + _dot_tn(kw.astype(BF16), vc)
        return dec * nv + jnp.sum(kw, axis=0, keepdims=True), m_new

    n_fin, m_fin = lax.fori_loop(0, n, body, (n_init, m_init))
    if state_out:
        nn_ref[...] = n_fin
        mn_ref[...] = jnp.broadcast_to(m_fin, mn_ref.shape)

    @pl.when(d == 1)
    def _():
        cell = _group_norm(acc_ref[...] * jax.nn.sigmoid(op_ref[...].astype(F32)), gn_ref[...])
        mixed = cell + sk_ref[...] * xc_ref[...].astype(F32)
        o_ref[...] = (mixed * _silu(z_ref[...].astype(F32))).astype(BF16)


def _ml_scan(proj, q, k, v, xc, g, gn, skip, T, nb, row_blk0, state=None, want_state=False, prev=None):
    H = ML_HEADS
    tok = lambda b, h, d: (row_blk0 + b, h)
    head_vec = pl.BlockSpec((1, ML_HD), lambda b, h, d: (0, h))
    in_specs, args = [], []
    if state is not None:
        c0, n0, m0 = state
        in_specs.append(pl.BlockSpec(memory_space=pltpu.SMEM))
        args.append(m0.reshape(nb * 2 * H))
    in_specs += [pl.BlockSpec((T, ML_HD), tok)] * 3
    in_specs += [pl.BlockSpec((T, g.shape[1]), lambda b, h, d: (row_blk0 + b, 0)),
                 pl.BlockSpec((T, ML_HD), lambda b, h, d: (row_blk0 + b, H + h)),
                 pl.BlockSpec((T, ML_HD), lambda b, h, d: (row_blk0 + b, 2 * H + h)),
                 pl.BlockSpec((T, ML_HD), tok), head_vec, head_vec]
    args += [q, k, v, g, proj, proj, xc, gn.reshape(1, ML_WIDTH), skip.reshape(1, ML_WIDTH)]
    if state is not None:
        in_specs += [pl.BlockSpec((None, None, None, ML_HD, ML_HD), lambda b, h, d: (b, d, h, 0, 0)),
                     pl.BlockSpec((None, None, 1, ML_HD), lambda b, h, d: (b, d * H + h, 0, 0))]
        args += [c0, n0.reshape(nb, 2 * H, 1, ML_HD)]
    aliases = {}
    if prev is not None:
        aliases = {len(args): 0}
        in_specs.append(pl.BlockSpec(memory_space=pl.ANY))
        args.append(prev)
    out_specs = [pl.BlockSpec((T, ML_HD), tok)]
    out_shape = [jax.ShapeDtypeStruct((N_TOK, ML_WIDTH), BF16)]
    if want_state:
        out_specs += [pl.BlockSpec((None, None, None, ML_HD, ML_HD), lambda b, h, d: (b, d, h, 0, 0)),
                      pl.BlockSpec((None, None, 1, ML_HD), lambda b, h, d: (b, d * H + h, 0, 0)),
                      pl.BlockSpec((None, None, 1, 128), lambda b, h, d: (b, d * H + h, 0, 0))]
        out_shape += [jax.ShapeDtypeStruct((nb, 2, H, ML_HD, ML_HD), F32),
                      jax.ShapeDtypeStruct((nb, 2 * H, 1, ML_HD), F32),
                      jax.ShapeDtypeStruct((nb, 2 * H, 1, 128), F32)]
    scratch = [pltpu.VMEM((T, ML_HD), F32), pltpu.VMEM((T, 1), F32), pltpu.VMEM((T, 1), F32)]
    if not want_state:
        scratch.append(pltpu.VMEM((ML_HD, ML_HD), F32))
    return pl.pallas_call(
        functools.partial(_ml_scan_kernel, T=T, state_in=state is not None, state_out=want_state,
                          has_prev=prev is not None),
        grid=(nb, H, 2),
        in_specs=in_specs, out_specs=out_specs, out_shape=out_shape, scratch_shapes=scratch,
        input_output_aliases=aliases,
        compiler_params=_params("arbitrary", "arbitrary", "arbitrary"),
        name="mlstm_scan",
    )(*args)


def _att_rope(x, cos, sin):
    lane = lax.broadcasted_iota(jnp.int32, x.shape, 1)
    half = ATT_HD // 2
    rot = jnp.where((lane & (ATT_HD - 1)) < half, pltpu.roll(x, 128 - half, 1), pltpu.roll(x, half, 1))
    return x * cos + rot * sin


def _att_heads(q_slabs, kv, sink_ref, key_sets, rows):
    lane = lax.broadcasted_iota(jnp.int32, (rows, 128), 1)
    lo = lane < ATT_HD
    zero = jnp.zeros((rows, 128), BF16)
    q_stack = jnp.concatenate([jnp.where(lo, q_slabs[0], zero), jnp.where(lo, zero, q_slabs[0]),
                               jnp.where(lo, q_slabs[1], zero), jnp.where(lo, zero, q_slabs[1])], axis=0)
    sink = jnp.concatenate([jnp.full((rows, 1), sink_ref[kv * ATT_GROUP + g], F32) for g in range(ATT_GROUP)],
                           axis=0)
    scale = ATT_HD ** -0.5
    native_lo = kv % 2 == 0

    def native(lanes):
        return lanes < ATT_HD if native_lo else lanes >= ATT_HD

    scores = []
    for k_s, _, bias in key_sets:
        kf = k_s.astype(F32)
        klane = lax.broadcasted_iota(jnp.int32, kf.shape, 1)
        k_both = jnp.where(native(klane), kf, pltpu.roll(kf, ATT_HD, 1)).astype(BF16)
        s = _dot_nt(q_stack, k_both) * scale
        if bias is not None:
            s = s + jnp.concatenate([bias] * ATT_GROUP, axis=0)
        scores.append(s)
    m = sink
    for s in scores:
        m = jnp.maximum(m, jnp.max(s, axis=1, keepdims=True))
    den = jnp.exp(sink - m)
    acc = [jnp.zeros((rows, 128), F32), jnp.zeros((rows, 128), F32)]
    for s, (_, v_s, _) in zip(scores, key_sets):
        p = jnp.exp(s - m)
        den = den + jnp.sum(p, axis=1, keepdims=True)
        vf = v_s.astype(F32)
        vlane = lax.broadcasted_iota(jnp.int32, vf.shape, 1)
        v_nat = jnp.where(native(vlane), vf, 0.0)
        v_other = pltpu.roll(v_nat, ATT_HD, 1)
        v_lo, v_hi = (v_nat, v_other) if native_lo else (v_other, v_nat)
        v_cat = jnp.concatenate([v_lo, v_hi], axis=0).astype(BF16)
        pb = p.astype(BF16)
        for pair in range(2):
            p_cat = jnp.concatenate([pb[(2 * pair) * rows:(2 * pair + 1) * rows],
                                     pb[(2 * pair + 1) * rows:(2 * pair + 2) * rows]], axis=1)
            acc[pair] = acc[pair] + _dot(p_cat, v_cat)
    inv = 1.0 / den
    out = []
    for pair in range(2):
        inv_pair = jnp.where(lo, inv[(2 * pair) * rows:(2 * pair + 1) * rows],
                             inv[(2 * pair + 1) * rows:(2 * pair + 2) * rows])
        out.append(acc[pair] * inv_pair)
    return out


def _att_ctx_kernel(sink_ref, q_ref, k_ref, v_ref, z0_ref, z1_ref, o_ref):
    T = q_ref.shape[0]
    for kv in range(ATT_KV):
        slab = (kv // 2) * 128
        k_s = k_ref[:, slab:slab + 128]
        v_s = v_ref[:, slab:slab + 128]
        q_slabs = [q_ref[:, (2 * kv + p) * 128:(2 * kv + p + 1) * 128] for p in range(2)]
        outs = _att_heads(q_slabs, kv, sink_ref, [(k_s, v_s, None)], T)
        for p in range(2):
            c0 = (2 * kv + p) * 128
            z_ref = z0_ref if c0 < ATT_WIDTH // 2 else z1_ref
            zc = c0 % (ATT_WIDTH // 2)
            zf = z_ref[:, zc:zc + 128].astype(F32)
            o_ref[:, c0:c0 + 128] = (outs[p] * _silu(zf)).astype(BF16)


def _att_ctx(proj, sink):
    T = SEQ
    half = ATT_WIDTH // 2
    return pl.pallas_call(
        _att_ctx_kernel,
        grid=(BATCH,),
        in_specs=[pl.BlockSpec(memory_space=pltpu.SMEM),
                  pl.BlockSpec((T, ATT_WIDTH), lambda b: (b, 0)),
                  pl.BlockSpec((T, ATT_KVW), lambda b: (b, ATT_WIDTH // ATT_KVW)),
                  pl.BlockSpec((T, ATT_KVW), lambda b: (b, ATT_WIDTH // ATT_KVW + 1)),
                  pl.BlockSpec((T, half), lambda b: (b, (ATT_WIDTH + 2 * ATT_KVW) // half)),
                  pl.BlockSpec((T, half), lambda b: (b, (ATT_WIDTH + 2 * ATT_KVW) // half + 1))],
        out_specs=pl.BlockSpec((T, ATT_WIDTH), lambda b: (b, 0)),
        out_shape=jax.ShapeDtypeStruct((N_TOK, ATT_WIDTH), BF16),
        compiler_params=_params("arbitrary"),
        name="att_ctx",
    )(sink, proj, proj, proj, proj, proj)


def _att_win_kernel(sink_ref, q_ref, k_ref, v_ref, z0_ref, z1_ref, kc_ref, vc_ref, cos_ref, sin_ref,
                    prev_ref, o_ref, qs_ref, ks_ref):
    del prev_ref
    T = q_ref.shape[0]
    L = CHUNK
    n = T // L
    cos = cos_ref[...]
    sin = sin_ref[...]
    for c0 in range(0, ATT_WIDTH, 128):
        qs_ref[:, c0:c0 + 128] = _att_rope(q_ref[:, c0:c0 + 128].astype(F32), cos, sin).astype(BF16)
    for c0 in range(0, ATT_KVW, 128):
        ks_ref[:, c0:c0 + 128] = _att_rope(k_ref[:, c0:c0 + 128].astype(F32), cos, sin).astype(BF16)

    ii = lax.broadcasted_iota(jnp.int32, (L, L), 0)
    jj = lax.broadcasted_iota(jnp.int32, (L, L), 1)
    bias_prev = jnp.where(jj >= ii, 0.0, -jnp.inf).astype(F32)
    bias_next = jnp.where(jj <= ii, 0.0, -jnp.inf).astype(F32)
    bias_same = jnp.zeros((L, L), F32)
    for c in range(n):
        chunks = [cb for cb in (c - 1, c, c + 1) if 0 <= cb < n]
        mask = jnp.concatenate([bias_prev if cb < c else (bias_next if cb > c else bias_same) for cb in chunks],
                               axis=1)
        r0, r1 = chunks[0] * L, (chunks[-1] + 1) * L
        for kv in range(ATT_KV):
            slab = (kv // 2) * 128
            q_slabs = [qs_ref[c * L:(c + 1) * L, (2 * kv + p) * 128:(2 * kv + p + 1) * 128] for p in range(2)]
            key_sets = [(ks_ref[r0:r1, slab:slab + 128], v_ref[r0:r1, slab:slab + 128], mask),
                        (kc_ref[:, slab:slab + 128], vc_ref[:, slab:slab + 128], None)]
            outs = _att_heads(q_slabs, kv, sink_ref, key_sets, L)
            for p in range(2):
                col = (2 * kv + p) * 128
                z_ref = z0_ref if col < ATT_WIDTH // 2 else z1_ref
                zc = col % (ATT_WIDTH // 2)
                zf = z_ref[c * L:(c + 1) * L, zc:zc + 128].astype(F32)
                o_ref[c * L:(c + 1) * L, col:col + 128] = (outs[p] * _silu(zf)).astype(BF16)


def _att_win(proj, sink, k_ctx, v_ctx, cos, sin, ctx_out):
    T = DEC_SEQ
    half = ATT_WIDTH // 2
    rb = N_PROMPT // T
    return pl.pallas_call(
        _att_win_kernel,
        grid=(DEC_BATCH,),
        in_specs=[pl.BlockSpec(memory_space=pltpu.SMEM),
                  pl.BlockSpec((T, ATT_WIDTH), lambda b: (rb + b, 0)),
                  pl.BlockSpec((T, ATT_KVW), lambda b: (rb + b, ATT_WIDTH // ATT_KVW)),
                  pl.BlockSpec((T, ATT_KVW), lambda b: (rb + b, ATT_WIDTH // ATT_KVW + 1)),
                  pl.BlockSpec((T, half), lambda b: (rb + b, (ATT_WIDTH + 2 * ATT_KVW) // half)),
                  pl.BlockSpec((T, half), lambda b: (rb + b, (ATT_WIDTH + 2 * ATT_KVW) // half + 1)),
                  pl.BlockSpec((None, PAST_LEN, ATT_KVW), lambda b: (b, 0, 0)),
                  pl.BlockSpec((None, PAST_LEN, ATT_KVW), lambda b: (b, 0, 0)),
                  pl.BlockSpec((T, 128), lambda b: (0, 0)),
                  pl.BlockSpec((T, 128), lambda b: (0, 0)),
                  pl.BlockSpec(memory_space=pl.ANY)],
        out_specs=pl.BlockSpec((T, ATT_WIDTH), lambda b: (rb + b, 0)),
        out_shape=jax.ShapeDtypeStruct((N_TOK, ATT_WIDTH), BF16),
        scratch_shapes=[pltpu.VMEM((T, ATT_WIDTH), BF16), pltpu.VMEM((T, ATT_KVW), BF16)],
        input_output_aliases={10: 0},
        compiler_params=_params("arbitrary"),
        name="att_win",
    )(sink, proj, proj, proj, proj, proj, k_ctx, v_ctx, cos, sin, ctx_out)


def _rope_tables(T, hd, reps):
    rows = T // GRID_W
    row = jnp.repeat(jnp.arange(rows, dtype=F32), GRID_W)
    col = jnp.tile(jnp.arange(GRID_W, dtype=F32), rows)
    nf = hd // 4
    inv = ROPE_BASE ** (-jnp.arange(nf, dtype=F32) / nf)
    ang = jnp.concatenate([row[:, None] * inv[None, :], col[:, None] * inv[None, :]], axis=-1)
    cos, sin = jnp.cos(ang), jnp.sin(ang)
    return (jnp.tile(jnp.concatenate([cos, cos], axis=-1), (1, reps)),
            jnp.tile(jnp.concatenate([-sin, sin], axis=-1), (1, reps)))


def _block_diag_tiles(w):
    per = ML_TILE // ML_BLOCK
    wt = w.reshape(ML_WIDTH // ML_TILE, per, ML_BLOCK, ML_BLOCK)
    eye = jnp.eye(per, dtype=w.dtype)
    dense = wt[:, :, :, None, :] * eye[None, :, None, :, None]
    return dense.reshape(ML_WIDTH // ML_TILE, ML_TILE, ML_TILE).astype(BF16)


def _retention_layer(h, w_in, decay_f, decay_b, gn, state, rope):
    proj = _in_proj(h, w_in, 1024)
    dec = jnp.stack([decay_f, decay_b]).astype(F32)
    a, new_state = _retention(proj, dec, gn, SEQ, BATCH, 0, want_state=True)
    (a,) = _retention(proj, dec, gn, DEC_SEQ, DEC_BATCH, N_PROMPT // DEC_SEQ, rope=rope, state=state, prev=a)
    return a, new_state


def kernel(x_prompt, x_sample, c, c_ctx, state_l0_ret, state_l1_C, state_l1_n, state_l1_m, cache_l2_k, cache_l2_v, state_l3_ret, norm_l0, ada_w_l0, ada_b_l0, w_in_l0, w_out_l0, ret_decay_f_l0, ret_decay_b_l0, ret_gn_l0, norm_l1, ada_w_l1, ada_b_l1, w_in_l1, w_out_l1, conv_w_l1, conv_b_l1, wq_l1, wk_l1, wv_l1, wif_f_l1, bif_f_l1, wif_b_l1, bif_b_l1, gn_l1, skip_l1, norm_l2, ada_w_l2, ada_b_l2, w_in_l2, w_out_l2, sink_l2, norm_l3, ada_w_l3, ada_b_l3, w_in_l3, w_out_l3, ret_decay_f_l3, ret_decay_b_l3, ret_gn_l3, final_norm):
    x = jnp.concatenate([x_prompt.reshape(N_PROMPT, D), x_sample.reshape(DEC_BATCH * DEC_SEQ, D)], axis=0)
    ct = jnp.concatenate([c_ctx[:, None], c.T, jnp.zeros((D, MOD_ROWS - 1 - DEC_BATCH), F32)], axis=1)
    mods = [_adaln(ct, w, b) for w, b in ((ada_w_l0, ada_b_l0), (ada_w_l1, ada_b_l1),
                                          (ada_w_l2, ada_b_l2), (ada_w_l3, ada_b_l3))]
    rope_ret = _rope_tables(DEC_SEQ, RET_DK, 1)
    rope_att = _rope_tables(DEC_SEQ, ATT_HD, 2)

    h = _norm_mod(x, norm_l0, mods[0])

    a, new_l0_ret = _retention_layer(h, w_in_l0, ret_decay_f_l0, ret_decay_b_l0, ret_gn_l0, state_l0_ret, rope_ret)
    x, h = _out_proj(a, w_out_l0, x, mods[0], norm_l1, mods[1], False)

    proj = _in_proj(h, w_in_l1, 1024)
    bdq, bdk, bdv = _block_diag_tiles(wq_l1), _block_diag_tiles(wk_l1), _block_diag_tiles(wv_l1)
    wif = jnp.concatenate([wif_f_l1, wif_b_l1], axis=1)
    bif = jnp.concatenate([bif_f_l1, bif_b_l1])
    pre = _ml_pre(proj, conv_w_l1, conv_b_l1, bdq, bdk, bdv, wif, bif, SEQ, BATCH, 0)
    q, k, v, xc, g = _ml_pre(proj, conv_w_l1, conv_b_l1, bdq, bdk, bdv, wif, bif, DEC_SEQ, DEC_BATCH,
                             N_PROMPT // DEC_SEQ, prev=pre)
    a, new_l1_C, n_new, m_new = _ml_scan(proj, q, k, v, xc, g, gn_l1, skip_l1, SEQ, BATCH, 0, want_state=True)
    (a,) = _ml_scan(proj, q, k, v, xc, g, gn_l1, skip_l1, DEC_SEQ, DEC_BATCH, N_PROMPT // DEC_SEQ,
                    state=(state_l1_C, state_l1_n, state_l1_m), prev=a)
    new_l1_n = n_new.reshape(BATCH, 2, ML_HEADS, ML_HD)
    new_l1_m = m_new[:, :, 0, 0].reshape(BATCH, 2, ML_HEADS)
    x, h = _out_proj(a, w_out_l1, x, mods[1], norm_l2, mods[2], False)

    proj = _in_proj(h, w_in_l2, 1280)
    kv_new = proj[:N_PROMPT, ATT_WIDTH:ATT_WIDTH + 2 * ATT_KVW].astype(F32)
    new_l2_k = kv_new[:, :ATT_KVW].reshape(BATCH, SEQ, ATT_KV, ATT_HD)
    new_l2_v = kv_new[:, ATT_KVW:].reshape(BATCH, SEQ, ATT_KV, ATT_HD)
    a = _att_ctx(proj, sink_l2)
    a = _att_win(proj, sink_l2, cache_l2_k.reshape(DEC_BATCH, PAST_LEN, ATT_KVW),
                 cache_l2_v.reshape(DEC_BATCH, PAST_LEN, ATT_KVW), rope_att[0], rope_att[1], a)
    x, h = _out_proj(a, w_out_l2, x, mods[2], norm_l3, mods[3], False)

    a, new_l3_ret = _retention_layer(h, w_in_l3, ret_decay_f_l3, ret_decay_b_l3, ret_gn_l3, state_l3_ret, rope_ret)
    _, y = _out_proj(a, w_out_l3, x, mods[3], final_norm, mods[3], True)

    y_prompt = y[:N_PROMPT].reshape(BATCH, SEQ, D)
    y_sample = y[N_PROMPT:].reshape(DEC_BATCH, DEC_SEQ, D)
    return (y_prompt, y_sample, new_l0_ret, new_l1_C, new_l1_n, new_l1_m, new_l2_k, new_l2_v, new_l3_ret)
```

```python
import functools

import jax
import jax.numpy as jnp
from jax import lax
from jax.experimental import pallas as pl
from jax.experimental.pallas import tpu as pltpu

F32 = jnp.float32
BF16 = jnp.bfloat16

D = 1024
BATCH = 16
SEQ = 256
DEC_BATCH = 2
DEC_SEQ = 1024
PAST_LEN = 512
GRID_W = 64
CHUNK = 128
EPS = 1e-6
ROPE_BASE = 10000.0

N_PROMPT = BATCH * SEQ
N_TOK = N_PROMPT + DEC_BATCH * DEC_SEQ
MOD_ROWS = 8

RET_HEADS = 8
RET_DK = 128
RET_DV = 256
RET_QK = RET_HEADS * RET_DK
RET_WIDTH = RET_HEADS * RET_DV
RET_HB_PROMPT = 4
RET_HB_LATENT = 1

ML_HEADS = 4
ML_WIDTH = 2 * D
ML_HD = ML_WIDTH // ML_HEADS
ML_BLOCK = 4
ML_CONV = 5
ML_TILE = 256

ATT_HEADS = 16
ATT_KV = 4
ATT_HD = 64
ATT_GROUP = ATT_HEADS // ATT_KV
ATT_WIDTH = ATT_HEADS * ATT_HD
ATT_KVW = ATT_KV * ATT_HD
WINDOW = 128

VMEM_LIMIT = 56 * 1024 * 1024


def _params(*sem):
    return pltpu.CompilerParams(dimension_semantics=sem, vmem_limit_bytes=VMEM_LIMIT)


def _mod_row(row0):
    return jnp.maximum((row0 - N_PROMPT) // DEC_SEQ + 1, 0)


def _dot(a, b):
    return jnp.dot(a, b, preferred_element_type=F32)


def _dot_nt(a, b):
    return lax.dot_general(a, b, (((1,), (1,)), ((), ())), preferred_element_type=F32)


def _dot_tn(a, b):
    return lax.dot_general(a, b, (((0,), (0,)), ((), ())), preferred_element_type=F32)


def _silu(x):
    return x * jax.nn.sigmoid(x)


def _log_sigmoid(x):
    return jnp.minimum(x, 0.0) - jnp.log1p(jnp.exp(-jnp.abs(x)))


def _rms(x, g):
    return x * lax.rsqrt(jnp.mean(x * x, axis=-1, keepdims=True) + EPS) * g


def _group_norm(x, g):
    xc = x - jnp.mean(x, axis=-1, keepdims=True)
    return xc * lax.rsqrt(jnp.mean(xc * xc, axis=-1, keepdims=True) + EPS) * g


def _ada_kernel(ct_ref, w_ref, b_ref, o_ref):
    s = _silu(ct_ref[...])
    w = w_ref[...]
    rows = [jnp.sum(w * s[:, r:r + 1], axis=0, keepdims=True) for r in range(1 + DEC_BATCH)]
    rows.append(jnp.zeros((MOD_ROWS - len(rows), w.shape[1]), F32))
    o_ref[...] = jnp.concatenate(rows, axis=0) + b_ref[...]


def _adaln(ct, w, b):
    tn = 1024
    return pl.pallas_call(
        _ada_kernel,
        grid=(3 * D // tn,),
        in_specs=[pl.BlockSpec((D, MOD_ROWS), lambda j: (0, 0)),
                  pl.BlockSpec((D, tn), lambda j: (0, j)),
                  pl.BlockSpec((1, tn), lambda j: (0, j))],
        out_specs=pl.BlockSpec((MOD_ROWS, tn), lambda j: (0, j)),
        out_shape=jax.ShapeDtypeStruct((MOD_ROWS, 3 * D), F32),
        compiler_params=_params("arbitrary"),
        name="adaln",
    )(ct, w, b.reshape(1, 3 * D))


def _norm_mod_kernel(x_ref, g_ref, mod_ref, h_ref, *, tm):
    m = mod_ref[pl.ds(_mod_row(pl.program_id(0) * tm), 1), :]
    y = _rms(x_ref[...], g_ref[...])
    h_ref[...] = (y * (1.0 + m[:, D:2 * D]) + m[:, :D]).astype(BF16)


def _norm_mod(x, g, mod):
    tm = 512
    return pl.pallas_call(
        functools.partial(_norm_mod_kernel, tm=tm),
        grid=(N_TOK // tm,),
        in_specs=[pl.BlockSpec((tm, D), lambda i: (i, 0)),
                  pl.BlockSpec((1, D), lambda i: (0, 0)),
                  pl.BlockSpec((MOD_ROWS, 3 * D), lambda i: (0, 0))],
        out_specs=pl.BlockSpec((tm, D), lambda i: (i, 0)),
        out_shape=jax.ShapeDtypeStruct((N_TOK, D), BF16),
        compiler_params=_params("arbitrary"),
        name="norm_mod",
    )(x, g.reshape(1, D), mod)


def _in_proj_kernel(h_ref, w_ref, o_ref, wb_ref):
    @pl.when(pl.program_id(1) == 0)
    def _():
        wb_ref[...] = w_ref[...].astype(BF16)

    o_ref[...] = _dot(h_ref[...], wb_ref[...]).astype(BF16)


def _in_proj(h, w, tn):
    tm = 1024
    n = w.shape[1]
    return pl.pallas_call(
        _in_proj_kernel,
        grid=(n // tn, N_TOK // tm),
        in_specs=[pl.BlockSpec((tm, D), lambda j, i: (i, 0)),
                  pl.BlockSpec((D, tn), lambda j, i: (0, j))],
        out_specs=pl.BlockSpec((tm, tn), lambda j, i: (i, j)),
        out_shape=jax.ShapeDtypeStruct((N_TOK, n), BF16),
        scratch_shapes=[pltpu.VMEM((D, tn), BF16)],
        compiler_params=_params("arbitrary", "arbitrary"),
        name="in_proj",
    )(h, w)


def _out_proj_kernel(a_ref, w_ref, x_ref, mod_ref, g_ref, modn_ref, xo_ref, h_ref, wb_ref, *, tm, final):
    @pl.when(pl.program_id(0) == 0)
    def _():
        wb_ref[...] = w_ref[...].astype(BF16)

    r = _mod_row(pl.program_id(0) * tm)
    gate = mod_ref[pl.ds(r, 1), :][:, 2 * D:]
    xn = x_ref[...] + gate * _dot(a_ref[...], wb_ref[...])
    xo_ref[...] = xn
    y = _rms(xn, g_ref[...])
    if final:
        h_ref[...] = y
    else:
        mn = modn_ref[pl.ds(r, 1), :]
        h_ref[...] = (y * (1.0 + mn[:, D:2 * D]) + mn[:, :D]).astype(BF16)


def _out_proj(a, w, x, mod, g_next, mod_next, final):
    tm = 512
    kw = w.shape[0]
    return pl.pallas_call(
        functools.partial(_out_proj_kernel, tm=tm, final=final),
        grid=(N_TOK // tm,),
        in_specs=[pl.BlockSpec((tm, kw), lambda i: (i, 0)),
                  pl.BlockSpec((kw, D), lambda i: (0, 0)),
                  pl.BlockSpec((tm, D), lambda i: (i, 0)),
                  pl.BlockSpec((MOD_ROWS, 3 * D), lambda i: (0, 0)),
                  pl.BlockSpec((1, D), lambda i: (0, 0)),
                  pl.BlockSpec((MOD_ROWS, 3 * D), lambda i: (0, 0))],
        out_specs=[pl.BlockSpec((tm, D), lambda i: (i, 0)),
                   pl.BlockSpec((tm, D), lambda i: (i, 0))],
        out_shape=[jax.ShapeDtypeStruct((N_TOK, D), F32),
                   jax.ShapeDtypeStruct((N_TOK, D), F32 if final else BF16)],
        scratch_shapes=[pltpu.VMEM((kw, D), BF16)],
        compiler_params=_params("arbitrary"),
        name="out_proj",
    )(a, w, x, mod, g_next.reshape(1, D), mod_next)


def _ret_kernel(*refs, T, HB, rope, state_in, state_out, has_prev):
    refs = list(refs)
    dec_ref, q_ref, k_ref, v_ref, z_ref, gn_ref = refs[:6]
    pos = 6
    if rope:
        cos_ref, sin_ref = refs[pos:pos + 2]
        pos += 2
    if state_in:
        s0_ref = refs[pos]
        pos += 1
    if has_prev:
        pos += 1
    o_ref = refs[pos]
    pos += 1
    if state_out:
        sn_ref = refs[pos]
        pos += 1
    kv_ref, sf_ref, sb_ref = refs[pos:pos + 3]
    pos += 3
    if rope:
        kr_ref = refs[pos]

    hg = pl.program_id(1)
    n = T // CHUNK
    L = CHUNK
    ii = lax.broadcasted_iota(jnp.int32, (L, L), 0).astype(F32)
    jj = lax.broadcasted_iota(jnp.int32, (L, L), 1).astype(F32)
    scale = RET_DK ** -0.5

    def chunk(c):
        return slice(c * L, (c + 1) * L)

    def rotate(x, c):
        return x * cos_ref[chunk(c), :] + pltpu.roll(x, RET_DK // 2, 1) * sin_ref[chunk(c), :]

    for hh in range(HB):
        head = hg * HB + hh
        qs = slice(hh * RET_DK, (hh + 1) * RET_DK)
        vs = slice(hh * RET_DV, (hh + 1) * RET_DV)
        lg_f = _log_sigmoid(jnp.full((1, RET_DV), dec_ref[0, head], F32))
        lg_b = _log_sigmoid(jnp.full((1, RET_DV), dec_ref[1, head], F32))
        lf, lb = lg_f[:, :L], lg_b[:, :L]
        k_dec_f = jnp.exp(lf * (L - 1.0 - ii)) * scale
        k_dec_b = jnp.exp(lb * ii) * scale
        q_dec_f = jnp.exp(lf * (ii + 1.0))
        q_dec_b = jnp.exp(lb * (L - ii))
        mask = (jnp.where(ii >= jj, jnp.exp(lf * jnp.maximum(ii - jj, 0.0)), 0.0)
                + jnp.where(jj >= ii, jnp.exp(lb * jnp.maximum(jj - ii, 0.0)), 0.0)) * scale
        g_f = jnp.exp(lg_f * float(L))
        g_b = jnp.exp(lg_b * float(L))

        for c in range(n):
            kc = k_ref[chunk(c), qs].astype(F32)
            if rope:
                kc = rotate(kc, c)
                kr_ref[chunk(c), qs] = kc.astype(BF16)
            kk = jnp.concatenate([kc * k_dec_f, kc * k_dec_b], axis=1).astype(BF16)
            kv_ref[hh, c] = _dot_tn(kk, v_ref[chunk(c), vs])
        S = s0_ref[0, hh] if state_in else None
        has_f = []
        for c in range(n):
            has_f.append(S is not None)
            if S is not None:
                sf_ref[hh, c] = S.astype(BF16)
            kvc = kv_ref[hh, c, :RET_DK, :]
            S = kvc if S is None else S * g_f + kvc
        if state_out:
            sn_ref[0, hh] = S
        S = s0_ref[1, hh] if state_in else None
        has_b = [False] * n
        for c in reversed(range(n)):
            has_b[c] = S is not None
            if S is not None:
                sb_ref[hh, c] = S.astype(BF16)
            kvc = kv_ref[hh, c, RET_DK:, :]
            S = kvc if S is None else S * g_b + kvc
        if state_out:
            sn_ref[1, hh] = S

        for c in range(n):
            qc = q_ref[chunk(c), qs]
            qf = qc.astype(F32)
            if rope:
                qf = rotate(qf, c)
                qc = qf.astype(BF16)
                kc = kr_ref[chunk(c), qs]
            else:
                kc = k_ref[chunk(c), qs]
            lhs = [(_dot_nt(qc, kc) * mask).astype(BF16)]
            rhs = [v_ref[chunk(c), vs]]
            if has_f[c]:
                lhs.append((qf * q_dec_f).astype(BF16))
                rhs.append(sf_ref[hh, c])
            if has_b[c]:
                lhs.append((qf * q_dec_b).astype(BF16))
                rhs.append(sb_ref[hh, c])
            o = _dot(jnp.concatenate(lhs, axis=1), jnp.concatenate(rhs, axis=0))
            zf = z_ref[chunk(c), vs].astype(F32)
            o_ref[chunk(c), vs] = (_group_norm(o, gn_ref[:, vs]) * _silu(zf)).astype(BF16)


def _retention(proj, dec, gn, T, nb, row_blk0, rope=None, state=None, want_state=False, prev=None):
    HB = RET_HB_PROMPT if T == SEQ else RET_HB_LATENT
    ng = RET_HEADS // HB
    n = T // CHUNK
    qw, vw = HB * RET_DK, HB * RET_DV
    in_specs = [pl.BlockSpec(memory_space=pltpu.SMEM),
                pl.BlockSpec((T, qw), lambda b, h: (row_blk0 + b, h)),
                pl.BlockSpec((T, qw), lambda b, h: (row_blk0 + b, ng + h)),
                pl.BlockSpec((T, vw), lambda b, h: (row_blk0 + b, ng + h)),
                pl.BlockSpec((T, vw), lambda b, h: (row_blk0 + b, 2 * ng + h)),
                pl.BlockSpec((1, vw), lambda b, h: (0, h))]
    args = [dec, proj, proj, proj, proj, gn.reshape(1, RET_WIDTH)]
    if rope is not None:
        in_specs += [pl.BlockSpec((T, RET_DK), lambda b, h: (0, 0))] * 2
        args += list(rope)
    if state is not None:
        in_specs.append(pl.BlockSpec((None, 2, HB, RET_DK, RET_DV), lambda b, h: (b, 0, h, 0, 0)))
        args.append(state)
    aliases = {}
    if prev is not None:
        aliases = {len(args): 0}
        in_specs.append(pl.BlockSpec(memory_space=pl.ANY))
        args.append(prev)
    out_specs = [pl.BlockSpec((T, vw), lambda b, h: (row_blk0 + b, h))]
    out_shape = [jax.ShapeDtypeStruct((N_TOK, RET_WIDTH), BF16)]
    if want_state:
        out_specs.append(pl.BlockSpec((None, 2, HB, RET_DK, RET_DV), lambda b, h: (b, 0, h, 0, 0)))
        out_shape.append(jax.ShapeDtypeStruct((nb, 2, RET_HEADS, RET_DK, RET_DV), F32))
    scratch = [pltpu.VMEM((HB, n, 2 * RET_DK, RET_DV), F32),
               pltpu.VMEM((HB, n, RET_DK, RET_DV), BF16),
               pltpu.VMEM((HB, n, RET_DK, RET_DV), BF16)]
    if rope is not None:
        scratch.append(pltpu.VMEM((T, qw), BF16))
    return pl.pallas_call(
        functools.partial(_ret_kernel, T=T, HB=HB, rope=rope is not None, state_in=state is not None,
                          state_out=want_state, has_prev=prev is not None),
        grid=(nb, ng),
        in_specs=in_specs, out_specs=out_specs, out_shape=out_shape, scratch_shapes=scratch,
        input_output_aliases=aliases,
        compiler_params=_params("arbitrary", "arbitrary"),
        name="retention",
    )(*args)


def _ml_pre_kernel(xm_ref, cw_ref, cb_ref, bq_ref, bk_ref, bv_ref, wq_ref, wk_ref, wv_ref, bias_ref, *rest, T):
    q_ref, k_ref, v_ref, xc_ref, g_ref = rest[-5:]
    j = pl.program_id(1)
    xb = xm_ref[...]
    x = xb.astype(F32)
    row = lax.broadcasted_iota(jnp.int32, x.shape, 0)
    pad = ML_CONV // 2
    conv = cb_ref[...] + x * cw_ref[pad:pad + 1, :]
    for t in range(ML_CONV):
        delta = t - pad
        if delta == 0:
            continue
        shifted = pltpu.roll(x, (-delta) % T, 0)
        ok = (row + delta >= 0) & (row + delta < T)
        conv = conv + jnp.where(ok, shifted, 0.0) * cw_ref[t:t + 1, :]
    xc = _silu(conv)
    xcb = xc.astype(BF16)
    qf = _dot(xcb, bq_ref[...])
    kf = _dot(xcb, bk_ref[...])
    vf = _dot(xb, bv_ref[...])
    qb, kb, vb = qf.astype(BF16), kf.astype(BF16), vf.astype(BF16)
    q_ref[...] = qb
    k_ref[...] = kb
    v_ref[...] = vb
    xc_ref[...] = xcb
    g = (_dot(qb, wq_ref[...].astype(BF16)) + _dot(kb, wk_ref[...].astype(BF16))
         + _dot(vb, wv_ref[...].astype(BF16)))

    @pl.when(j == 0)
    def _():
        g_ref[...] = g + bias_ref[...]

    @pl.when(j > 0)
    def _():
        g_ref[...] += g


def _ml_pre(proj, conv_w, conv_b, bdq, bdk, bdv, wif, bif, T, nb, row_blk0, prev=None):
    nj = ML_WIDTH // ML_TILE
    ng = wif.shape[1]
    tok = lambda b, j: (row_blk0 + b, j)
    bd = pl.BlockSpec((None, ML_TILE, ML_TILE), lambda b, j: (j, 0, 0))
    out_shape = [jax.ShapeDtypeStruct((N_TOK, ML_WIDTH), BF16)] * 4 + [jax.ShapeDtypeStruct((N_TOK, ng), F32)]
    in_specs = [pl.BlockSpec((T, ML_TILE), tok),
                pl.BlockSpec((ML_CONV, ML_TILE), lambda b, j: (0, j)),
                pl.BlockSpec((1, ML_TILE), lambda b, j: (0, j)),
                bd, bd, bd,
                pl.BlockSpec((ML_TILE, ng), lambda b, j: (j, 0)),
                pl.BlockSpec((ML_TILE, ng), lambda b, j: (nj + j, 0)),
                pl.BlockSpec((ML_TILE, ng), lambda b, j: (2 * nj + j, 0)),
                pl.BlockSpec((1, ng), lambda b, j: (0, 0))]
    args = [proj, conv_w, conv_b.reshape(1, ML_WIDTH), bdq, bdk, bdv, wif, wif, wif, bif.reshape(1, ng)]
    aliases = {}
    if prev is not None:
        aliases = {len(args) + i: i for i in range(5)}
        in_specs += [pl.BlockSpec(memory_space=pl.ANY)] * 5
        args += list(prev)
    return pl.pallas_call(
        functools.partial(_ml_pre_kernel, T=T),
        grid=(nb, nj),
        in_specs=in_specs,
        out_specs=[pl.BlockSpec((T, ML_TILE), tok)] * 4 + [pl.BlockSpec((T, ng), lambda b, j: (row_blk0 + b, 0))],
        out_shape=out_shape,
        input_output_aliases=aliases,
        compiler_params=_params("arbitrary", "arbitrary"),
        name="mlstm_pre",
    )(*args)


def _dot_split(lhs_bf16, rhs):
    r1 = rhs.astype(BF16)
    e1 = rhs - r1.astype(F32)
    r2 = e1.astype(BF16)
    r3 = (e1 - r2.astype(F32)).astype(BF16)
    return _dot(lhs_bf16, r1) + _dot(lhs_bf16, r2) + _dot(lhs_bf16, r3)


def _ml_scan_kernel(*refs, T, state_in, state_out, has_prev):
    refs = list(refs)
    pos = 0
    if state_in:
        m0_ref = refs[0]
        pos = 1
    q_ref, k_ref, v_ref, g_ref, op_ref, z_ref, xc_ref, gn_ref, sk_ref = refs[pos:pos + 9]
    pos += 9
    if state_in:
        c0_ref, n0_ref = refs[pos:pos + 2]
        pos += 2
    if has_prev:
        pos += 1
    o_ref = refs[pos]
    pos += 1
    if state_out:
        cn_ref, nn_ref, mn_ref = refs[pos:pos + 3]
        pos += 3
    acc_ref, gi_ref, gf_ref = refs[pos:pos + 3]
    pos += 3
    C_ref = cn_ref if state_out else refs[pos]

    b = pl.program_id(0)
    h = pl.program_id(1)
    d = pl.program_id(2)
    n = T // CHUNK
    L = CHUNK
    scale = ML_HD ** -0.5
    fwd = d == 0

    g = g_ref[...]
    col = lax.broadcasted_iota(jnp.int32, g.shape, 1)
    gi_ref[...] = jnp.sum(jnp.where(col == d * (2 * ML_HEADS) + h, g, 0.0), axis=1, keepdims=True)
    gf_ref[...] = _log_sigmoid(
        jnp.sum(jnp.where(col == d * (2 * ML_HEADS) + ML_HEADS + h, g, 0.0), axis=1, keepdims=True))

    ii = lax.broadcasted_iota(jnp.int32, (L, L), 0)
    jj = lax.broadcasted_iota(jnp.int32, (L, L), 1)
    one = jnp.ones((L, L), F32)
    zero = jnp.zeros((L, L), F32)
    tri_lo = jnp.where(jj <= ii, one, zero)
    tri_hi = jnp.where(jj >= ii, one, zero)
    eye = jnp.where(ii == jj, one, zero)
    lower = jnp.where(fwd, tri_lo, tri_hi)
    upper = jnp.where(fwd, tri_lo, tri_hi) - eye
    neg = jnp.full((L, L), -jnp.inf, F32)
    bias = jnp.where(fwd, jnp.where(ii >= jj, zero, neg), jnp.where(ii <= jj, zero, neg))
    lhs = jnp.concatenate([lower, one], axis=1).astype(BF16)

    if state_in:
        C_ref[...] = c0_ref[...]
        n_init = n0_ref[...]
        m_init = jnp.full((1, 1), m0_ref[(b * 2 + d) * ML_HEADS + h], F32)
    else:
        C_ref[...] = jnp.zeros((ML_HD, ML_HD), F32)
        n_init = jnp.zeros((1, ML_HD), F32)
        m_init = jnp.zeros((1, 1), F32)

    def body(c, carry):
        nv, m = carry
        cc = jnp.where(fwd, c, n - 1 - c)
        rows = pl.ds(pl.multiple_of(cc * L, L), L)
        qc = q_ref[rows, :]
        kc = k_ref[rows, :]
        vc = v_ref[rows, :]
        gi = gi_ref[rows, :]
        gf = gf_ref[rows, :]
        rhs = jnp.concatenate(
            [jnp.concatenate([gf * upper, gf * one], axis=1),
             jnp.concatenate([gi * eye, zero], axis=1)], axis=0)
        pb = _dot_split(lhs, rhs)
        d_log = pb[:, :L] + bias
        b_col = pb[:, L:L + 1]
        inter = b_col + m
        m_i = jnp.maximum(inter, jnp.max(d_log, axis=1, keepdims=True))
        w_intra = jnp.exp(d_log - m_i)
        w_prev = jnp.exp(inter - m_i)
        s = _dot_nt(qc, kc) * (w_intra * scale)
        num = _dot(s.astype(BF16), vc) + w_prev * _dot(qc, C_ref[...].astype(BF16))
        qn = jnp.sum(qc.astype(F32) * nv, axis=1, keepdims=True)
        den = jnp.sum(s, axis=1, keepdims=True) + w_prev * qn
        hb = num / jnp.maximum(jnp.abs(den), jnp.exp(-m_i))

        @pl.when(fwd)
        def _():
            acc_ref[rows, :] = hb

        @pl.when(jnp.logical_not(fwd))
        def _():
            acc_ref[rows, :] += hb

        b_last = jnp.sum(gf, axis=0, keepdims=True)
        w_log = b_last - b_col + gi
        m_new = jnp.maximum(b_last + m, jnp.max(w_log, axis=0, keepdims=True))
        w_k = jnp.exp(w_log - m_new) * scale
        dec = jnp.exp(b_last + m - m_new)
        kw = kc.astype(F32) * w_k
        C_ref[...] = dec * C_ref[...] + _dot_tn(kw.astype(BF16), vc)
        return dec * nv + jnp.sum(kw, axis=0, keepdims=True), m_new

    n_fin, m_fin = lax.fori_loop(0, n, body, (n_init, m_init))
    if state_out:
        nn_ref[...] = n_fin
        mn_ref[...] = jnp.broadcast_to(m_fin, mn_ref.shape)

    @pl.when(d == 1)
    def _():
        cell = _group_norm(acc_ref[...] * jax.nn.sigmoid(op_ref[...].astype(F32)), gn_ref[...])
        mixed = cell + sk_ref[...] * xc_ref[...].astype(F32)
        o_ref[...] = (mixed * _silu(z_ref[...].astype(F32))).astype(BF16)


def _ml_scan(proj, q, k, v, xc, g, gn, skip, T, nb, row_blk0, state=None, want_state=False, prev=None):
    H = ML_HEADS
    tok = lambda b, h, d: (row_blk0 + b, h)
    head_vec = pl.BlockSpec((1, ML_HD), lambda b, h, d: (0, h))
    in_specs, args = [], []
    if state is not None:
        c0, n0, m0 = state
        in_specs.append(pl.BlockSpec(memory_space=pltpu.SMEM))
        args.append(m0.reshape(nb * 2 * H))
    in_specs += [pl.BlockSpec((T, ML_HD), tok)] * 3
    in_specs += [pl.BlockSpec((T, g.shape[1]), lambda b, h, d: (row_blk0 + b, 0)),
                 pl.BlockSpec((T, ML_HD), lambda b, h, d: (row_blk0 + b, H + h)),
                 pl.BlockSpec((T, ML_HD), lambda b, h, d: (row_blk0 + b, 2 * H + h)),
                 pl.BlockSpec((T, ML_HD), tok), head_vec, head_vec]
    args += [q, k, v, g, proj, proj, xc, gn.reshape(1, ML_WIDTH), skip.reshape(1, ML_WIDTH)]
    if state is not None:
        in_specs += [pl.BlockSpec((None, None, None, ML_HD, ML_HD), lambda b, h, d: (b, d, h, 0, 0)),
                     pl.BlockSpec((None, None, 1, ML_HD), lambda b, h, d: (b, d * H + h, 0, 0))]
        args += [c0, n0.reshape(nb, 2 * H, 1, ML_HD)]
    aliases = {}
    if prev is not None:
        aliases = {len(args): 0}
        in_specs.append(pl.BlockSpec(memory_space=pl.ANY))
        args.append(prev)
    out_specs = [pl.BlockSpec((T, ML_HD), tok)]
    out_shape = [jax.ShapeDtypeStruct((N_TOK, ML_WIDTH), BF16)]
    if want_state:
        out_specs += [pl.BlockSpec((None, None, None, ML_HD, ML_HD), lambda b, h, d: (b, d, h, 0, 0)),
                      pl.BlockSpec((None, None, 1, ML_HD), lambda b, h, d: (b, d * H + h, 0, 0)),
                      pl.BlockSpec((None, None, 1, 128), lambda b, h, d: (b, d * H + h, 0, 0))]
        out_shape += [jax.ShapeDtypeStruct((nb, 2, H, ML_HD, ML_HD), F32),
                      jax.ShapeDtypeStruct((nb, 2 * H, 1, ML_HD), F32),
                      jax.ShapeDtypeStruct((nb, 2 * H, 1, 128), F32)]
    scratch = [pltpu.VMEM((T, ML_HD), F32), pltpu.VMEM((T, 1), F32), pltpu.VMEM((T, 1), F32)]
    if not want_state:
        scratch.append(pltpu.VMEM((ML_HD, ML_HD), F32))
    return pl.pallas_call(
        functools.partial(_ml_scan_kernel, T=T, state_in=state is not None, state_out=want_state,
                          has_prev=prev is not None),
        grid=(nb, H, 2),
        in_specs=in_specs, out_specs=out_specs, out_shape=out_shape, scratch_shapes=scratch,
        input_output_aliases=aliases,
        compiler_params=_params("arbitrary", "arbitrary", "arbitrary"),
        name="mlstm_scan",
    )(*args)


def _att_rope(x, cos, sin):
    lane = lax.broadcasted_iota(jnp.int32, x.shape, 1)
    half = ATT_HD // 2
    rot = jnp.where((lane & (ATT_HD - 1)) < half, pltpu.roll(x, 128 - half, 1), pltpu.roll(x, half, 1))
    return x * cos + rot * sin


def _att_heads(q_slabs, kv, sink_ref, key_sets, rows):
    lane = lax.broadcasted_iota(jnp.int32, (rows, 128), 1)
    lo = lane < ATT_HD
    zero = jnp.zeros((rows, 128), BF16)
    q_stack = jnp.concatenate([jnp.where(lo, q_slabs[0], zero), jnp.where(lo, zero, q_slabs[0]),
                               jnp.where(lo, q_slabs[1], zero), jnp.where(lo, zero, q_slabs[1])], axis=0)
    sink = jnp.concatenate([jnp.full((rows, 1), sink_ref[kv * ATT_GROUP + g], F32) for g in range(ATT_GROUP)],
                           axis=0)
    scale = ATT_HD ** -0.5
    native_lo = kv % 2 == 0

    def native(lanes):
        return lanes < ATT_HD if native_lo else lanes >= ATT_HD

    scores = []
    for k_s, _, bias in key_sets:
        kf = k_s.astype(F32)
        klane = lax.broadcasted_iota(jnp.int32, kf.shape, 1)
        k_both = jnp.where(native(klane), kf, pltpu.roll(kf, ATT_HD, 1)).astype(BF16)
        s = _dot_nt(q_stack, k_both) * scale
        if bias is not None:
            s = s + jnp.concatenate([bias] * ATT_GROUP, axis=0)
        scores.append(s)
    m = sink
    for s in scores:
        m = jnp.maximum(m, jnp.max(s, axis=1, keepdims=True))
    den = jnp.exp(sink - m)
    acc = [jnp.zeros((rows, 128), F32), jnp.zeros((rows, 128), F32)]
    for s, (_, v_s, _) in zip(scores, key_sets):
        p = jnp.exp(s - m)
        den = den + jnp.sum(p, axis=1, keepdims=True)
        vf = v_s.astype(F32)
        vlane = lax.broadcasted_iota(jnp.int32, vf.shape, 1)
        v_nat = jnp.where(native(vlane), vf, 0.0)
        v_other = pltpu.roll(v_nat, ATT_HD, 1)
        v_lo, v_hi = (v_nat, v_other) if native_lo else (v_other, v_nat)
        v_cat = jnp.concatenate([v_lo, v_hi], axis=0).astype(BF16)
        pb = p.astype(BF16)
        for pair in range(2):
            p_cat = jnp.concatenate([pb[(2 * pair) * rows:(2 * pair + 1) * rows],
                                     pb[(2 * pair + 1) * rows:(2 * pair + 2) * rows]], axis=1)
            acc[pair] = acc[pair] + _dot(p_cat, v_cat)
    inv = 1.0 / den
    out = []
    for pair in range(2):
        inv_pair = jnp.where(lo, inv[(2 * pair) * rows:(2 * pair + 1) * rows],
                             inv[(2 * pair + 1) * rows:(2 * pair + 2) * rows])
        out.append(acc[pair] * inv_pair)
    return out


def _att_ctx_kernel(sink_ref, q_ref, k_ref, v_ref, z0_ref, z1_ref, o_ref):
    T = q_ref.shape[0]
    for kv in range(ATT_KV):
        slab = (kv // 2) * 128
        k_s = k_ref[:, slab:slab + 128]
        v_s = v_ref[:, slab:slab + 128]
        q_slabs = [q_ref[:, (2 * kv + p) * 128:(2 * kv + p + 1) * 128] for p in range(2)]
        outs = _att_heads(q_slabs, kv, sink_ref, [(k_s, v_s, None)], T)
        for p in range(2):
            c0 = (2 * kv + p) * 128
            z_ref = z0_ref if c0 < ATT_WIDTH // 2 else z1_ref
            zc = c0 % (ATT_WIDTH // 2)
            zf = z_ref[:, zc:zc + 128].astype(F32)
            o_ref[:, c0:c0 + 128] = (outs[p] * _silu(zf)).astype(BF16)


def _att_ctx(proj, sink):
    T = SEQ
    half = ATT_WIDTH // 2
    return pl.pallas_call(
        _att_ctx_kernel,
        grid=(BATCH,),
        in_specs=[pl.BlockSpec(memory_space=pltpu.SMEM),
                  pl.BlockSpec((T, ATT_WIDTH), lambda b: (b, 0)),
                  pl.BlockSpec((T, ATT_KVW), lambda b: (b, ATT_WIDTH // ATT_KVW)),
                  pl.BlockSpec((T, ATT_KVW), lambda b: (b, ATT_WIDTH // ATT_KVW + 1)),
                  pl.BlockSpec((T, half), lambda b: (b, (ATT_WIDTH + 2 * ATT_KVW) // half)),
                  pl.BlockSpec((T, half), lambda b: (b, (ATT_WIDTH + 2 * ATT_KVW) // half + 1))],
        out_specs=pl.BlockSpec((T, ATT_WIDTH), lambda b: (b, 0)),
        out_shape=jax.ShapeDtypeStruct((N_TOK, ATT_WIDTH), BF16),
        compiler_params=_params("arbitrary"),
        name="att_ctx",
    )(sink, proj, proj, proj, proj, proj)


def _att_win_kernel(sink_ref, q_ref, k_ref, v_ref, z0_ref, z1_ref, kc_ref, vc_ref, cos_ref, sin_ref,
                    prev_ref, o_ref, qs_ref, ks_ref):
    del prev_ref
    T = q_ref.shape[0]
    L = CHUNK
    n = T // L
    cos = cos_ref[...]
    sin = sin_ref[...]
    for c0 in range(0, ATT_WIDTH, 128):
        qs_ref[:, c0:c0 + 128] = _att_rope(q_ref[:, c0:c0 + 128].astype(F32), cos, sin).astype(BF16)
    for c0 in range(0, ATT_KVW, 128):
        ks_ref[:, c0:c0 + 128] = _att_rope(k_ref[:, c0:c0 + 128].astype(F32), cos, sin).astype(BF16)

    ii = lax.broadcasted_iota(jnp.int32, (L, L), 0)
    jj = lax.broadcasted_iota(jnp.int32, (L, L), 1)
    bias_prev = jnp.where(jj >= ii, 0.0, -jnp.inf).astype(F32)
    bias_next = jnp.where(jj <= ii, 0.0, -jnp.inf).astype(F32)
    bias_same = jnp.zeros((L, L), F32)
    for c in range(n):
        chunks = [cb for cb in (c - 1, c, c + 1) if 0 <= cb < n]
        mask = jnp.concatenate([bias_prev if cb < c else (bias_next if cb > c else bias_same) for cb in chunks],
                               axis=1)
        r0, r1 = chunks[0] * L, (chunks[-1] + 1) * L
        for kv in range(ATT_KV):
            slab = (kv // 2) * 128
            q_slabs = [qs_ref[c * L:(c + 1) * L, (2 * kv + p) * 128:(2 * kv + p + 1) * 128] for p in range(2)]
            key_sets = [(ks_ref[r0:r1, slab:slab + 128], v_ref[r0:r1, slab:slab + 128], mask),
                        (kc_ref[:, slab:slab + 128], vc_ref[:, slab:slab + 128], None)]
            outs = _att_heads(q_slabs, kv, sink_ref, key_sets, L)
            for p in range(2):
                col = (2 * kv + p) * 128
                z_ref = z0_ref if col < ATT_WIDTH // 2 else z1_ref
                zc = col % (ATT_WIDTH // 2)
                zf = z_ref[c * L:(c + 1) * L, zc:zc + 128].astype(F32)
                o_ref[c * L:(c + 1) * L, col:col + 128] = (outs[p] * _silu(zf)).astype(BF16)


def _att_win(proj, sink, k_ctx, v_ctx, cos, sin, ctx_out):
    T = DEC_SEQ
    half = ATT_WIDTH // 2
    rb = N_PROMPT // T
    return pl.pallas_call(
        _att_win_kernel,
        grid=(DEC_BATCH,),
        in_specs=[pl.BlockSpec(memory_space=pltpu.SMEM),
                  pl.BlockSpec((T, ATT_WIDTH), lambda b: (rb + b, 0)),
                  pl.BlockSpec((T, ATT_KVW), lambda b: (rb + b, ATT_WIDTH // ATT_KVW)),
                  pl.BlockSpec((T, ATT_KVW), lambda b: (rb + b, ATT_WIDTH // ATT_KVW + 1)),
                  pl.BlockSpec((T, half), lambda b: (rb + b, (ATT_WIDTH + 2 * ATT_KVW) // half)),
                  pl.BlockSpec((T, half), lambda b: (rb + b, (ATT_WIDTH + 2 * ATT_KVW) // half + 1)),
                  pl.BlockSpec((None, PAST_LEN, ATT_KVW), lambda b: (b, 0, 0)),
                  pl.BlockSpec((None, PAST_LEN, ATT_KVW), lambda b: (b, 0, 0)),
                  pl.BlockSpec((T, 128), lambda b: (0, 0)),
                  pl.BlockSpec((T, 128), lambda b: (0, 0)),
                  pl.BlockSpec(memory_space=pl.ANY)],
        out_specs=pl.BlockSpec((T, ATT_WIDTH), lambda b: (rb + b, 0)),
        out_shape=jax.ShapeDtypeStruct((N_TOK, ATT_WIDTH), BF16),
        scratch_shapes=[pltpu.VMEM((T, ATT_WIDTH), BF16), pltpu.VMEM((T, ATT_KVW), BF16)],
        input_output_aliases={10: 0},
        compiler_params=_params("arbitrary"),
        name="att_win",
    )(sink, proj, proj, proj, proj, proj, k_ctx, v_ctx, cos, sin, ctx_out)


def _rope_tables(T, hd, reps):
    rows = T // GRID_W
    row = jnp.repeat(jnp.arange(rows, dtype=F32), GRID_W)
    col = jnp.tile(jnp.arange(GRID_W, dtype=F32), rows)
    nf = hd // 4
    inv = ROPE_BASE ** (-jnp.arange(nf, dtype=F32) / nf)
    ang = jnp.concatenate([row[:, None] * inv[None, :], col[:, None] * inv[None, :]], axis=-1)
    cos, sin = jnp.cos(ang), jnp.sin(ang)
    return (jnp.tile(jnp.concatenate([cos, cos], axis=-1), (1, reps)),
            jnp.tile(jnp.concatenate([-sin, sin], axis=-1), (1, reps)))


def _block_diag_tiles(w):
    per = ML_TILE // ML_BLOCK
    wt = w.reshape(ML_WIDTH // ML_TILE, per, ML_BLOCK, ML_BLOCK)
    eye = jnp.eye(per, dtype=w.dtype)
    dense = wt[:, :, :, None, :] * eye[None, :, None, :, None]
    return dense.reshape(ML_WIDTH // ML_TILE, ML_TILE, ML_TILE).astype(BF16)


def _retention_layer(h, w_in, decay_f, decay_b, gn, state, rope):
    proj = _in_proj(h, w_in, 1024)
    dec = jnp.stack([decay_f, decay_b]).astype(F32)
    a, new_state = _retention(proj, dec, gn, SEQ, BATCH, 0, want_state=True)
    (a,) = _retention(proj, dec, gn, DEC_SEQ, DEC_BATCH, N_PROMPT // DEC_SEQ, rope=rope, state=state, prev=a)
    return a, new_state


def kernel(x_prompt, x_sample, c, c_ctx, state_l0_ret, state_l1_C, state_l1_n, state_l1_m, cache_l2_k, cache_l2_v, state_l3_ret, norm_l0, ada_w_l0, ada_b_l0, w_in_l0, w_out_l0, ret_decay_f_l0, ret_decay_b_l0, ret_gn_l0, norm_l1, ada_w_l1, ada_b_l1, w_in_l1, w_out_l1, conv_w_l1, conv_b_l1, wq_l1, wk_l1, wv_l1, wif_f_l1, bif_f_l1, wif_b_l1, bif_b_l1, gn_l1, skip_l1, norm_l2, ada_w_l2, ada_b_l2, w_in_l2, w_out_l2, sink_l2, norm_l3, ada_w_l3, ada_b_l3, w_in_l3, w_out_l3, ret_decay_f_l3, ret_decay_b_l3, ret_gn_l3, final_norm):
    x = jnp.concatenate([x_prompt.reshape(N_PROMPT, D), x_sample.reshape(DEC_BATCH * DEC_SEQ, D)], axis=0)
    ct = jnp.concatenate([c_ctx[:, None], c.T, jnp.zeros((D, MOD_ROWS - 1 - DEC_BATCH), F32)], axis=1)
    mods = [_adaln(ct, w, b) for w, b in ((ada_w_l0, ada_b_l0), (ada_w_l1, ada_b_l1),
                                          (ada_w_l2, ada_b_l2), (ada_w_l3, ada_b_l3))]
    rope_ret = _rope_tables(DEC_SEQ, RET_DK, 1)
    rope_att = _rope_tables(DEC_SEQ, ATT_HD, 2)

    h = _norm_mod(x, norm_l0, mods[0])

    a, new_l0_ret = _retention_layer(h, w_in_l0, ret_decay_f_l0, ret_decay_b_l0, ret_gn_l0, state_l0_ret, rope_ret)
    x, h = _out_proj(a, w_out_l0, x, mods[0], norm_l1, mods[1], False)

    proj = _in_proj(h, w_in_l1, 1024)
    bdq, bdk, bdv = _block_diag_tiles(wq_l1), _block_diag_tiles(wk_l1), _block_diag_tiles(wv_l1)
    wif = jnp.concatenate([wif_f_l1, wif_b_l1], axis=1)
    bif = jnp.concatenate([bif_f_l1, bif_b_l1])
    pre = _ml_pre(proj, conv_w_l1, conv_b_l1, bdq, bdk, bdv, wif, bif, SEQ, BATCH, 0)
    q, k, v, xc, g = _ml_pre(proj, conv_w_l1, conv_b_l1, bdq, bdk, bdv, wif, bif, DEC_SEQ, DEC_BATCH,
                             N_PROMPT // DEC_SEQ, prev=pre)
    a, new_l1_C, n_new, m_new = _ml_scan(proj, q, k, v, xc, g, gn_l1, skip_l1, SEQ, BATCH, 0, want_state=True)
    (a,) = _ml_scan(proj, q, k, v, xc, g, gn_l1, skip_l1, DEC_SEQ, DEC_BATCH, N_PROMPT // DEC_SEQ,
                    state=(state_l1_C, state_l1_n, state_l1_m), prev=a)
    new_l1_n = n_new.reshape(BATCH, 2, ML_HEADS, ML_HD)
    new_l1_m = m_new[:, :, 0, 0].reshape(BATCH, 2, ML_HEADS)
    x, h = _out_proj(a, w_out_l1, x, mods[1], norm_l2, mods[2], False)

    proj = _in_proj(h, w_in_l2, 1280)
    kv_new = proj[:N_PROMPT, ATT_WIDTH:ATT_WIDTH + 2 * ATT_KVW].astype(F32)
    new_l2_k = kv_new[:, :ATT_KVW].reshape(BATCH, SEQ, ATT_KV, ATT_HD)
    new_l2_v = kv_new[:, ATT_KVW:].reshape(BATCH, SEQ, ATT_KV, ATT_HD)
    a = _att_ctx(proj, sink_l2)
    a = _att_win(proj, sink_l2, cache_l2_k.reshape(DEC_BATCH, PAST_LEN, ATT_KVW),
                 cache_l2_v.reshape(DEC_BATCH, PAST_LEN, ATT_KVW), rope_att[0], rope_att[1], a)
    x, h = _out_proj(a, w_out_l2, x, mods[2], norm_l3, mods[3], False)

    a, new_l3_ret = _retention_layer(h, w_in_l3, ret_decay_f_l3, ret_decay_b_l3, ret_gn_l3, state_l3_ret, rope_ret)
    _, y = _out_proj(a, w_out_l3, x, mods[3], final_norm, mods[3], True)

    y_prompt = y[:N_PROMPT].reshape(BATCH, SEQ, D)
    y_sample = y[N_PROMPT:].reshape(DEC_BATCH, DEC_SEQ, D)
    return (y_prompt, y_sample, new_l0_ret, new_l1_C, new_l1_n, new_l1_m, new_l2_k, new_l2_v, new_l3_ret)
```

```python
import functools

import jax
import jax.numpy as jnp
from jax import lax
from jax.experimental import pallas as pl
from jax.experimental.pallas import tpu as pltpu

F32 = jnp.float32
BF16 = jnp.bfloat16

D = 1024
BATCH = 16
SEQ = 256
DEC_BATCH = 2
DEC_SEQ = 1024
PAST_LEN = 512
GRID_W = 64
CHUNK = 128
EPS = 1e-6
ROPE_BASE = 10000.0

N_PROMPT = BATCH * SEQ
N_LATENT = DEC_BATCH * DEC_SEQ
N_TOK = N_PROMPT + N_LATENT
MOD_ROWS = 8

RET_HEADS = 8
RET_DK = 128
RET_DV = 256
RET_QK = RET_HEADS * RET_DK
RET_WIDTH = RET_HEADS * RET_DV
RET_HB_PROMPT = 4
RET_HB_LATENT = 1

ML_HEADS = 4
ML_WIDTH = 2 * D
ML_HD = ML_WIDTH // ML_HEADS
ML_BLOCK = 4
ML_CONV = 5
ML_TILE = 256
ML_ROWS = 1024

ATT_HEADS = 16
ATT_KV = 4
ATT_HD = 64
ATT_GROUP = ATT_HEADS // ATT_KV
ATT_WIDTH = ATT_HEADS * ATT_HD
ATT_KVW = ATT_KV * ATT_HD
WINDOW = 128

VMEM_LIMIT = 56 * 1024 * 1024


def _params(*sem):
    return pltpu.CompilerParams(dimension_semantics=sem, vmem_limit_bytes=VMEM_LIMIT)


def _mod_row(row0):
    return jnp.maximum((row0 - N_PROMPT) // DEC_SEQ + 1, 0)


def _dot(a, b):
    return jnp.dot(a, b, preferred_element_type=F32)


def _dot_nt(a, b):
    return lax.dot_general(a, b, (((1,), (1,)), ((), ())), preferred_element_type=F32)


def _dot_tn(a, b):
    return lax.dot_general(a, b, (((0,), (0,)), ((), ())), preferred_element_type=F32)


def _silu(x):
    return x * jax.nn.sigmoid(x)


def _log_sigmoid(x):
    return jnp.minimum(x, 0.0) - jnp.log1p(jnp.exp(-jnp.abs(x)))


def _rms(x, g):
    return x * lax.rsqrt(jnp.mean(x * x, axis=-1, keepdims=True) + EPS) * g


def _group_norm(x, g):
    xc = x - jnp.mean(x, axis=-1, keepdims=True)
    return xc * lax.rsqrt(jnp.mean(xc * xc, axis=-1, keepdims=True) + EPS) * g


def _two_source_specs(tm, width):
    n_p = N_PROMPT // tm
    return (pl.BlockSpec((tm, width), lambda i: (jnp.minimum(i, n_p - 1), 0)),
            pl.BlockSpec((tm, width), lambda i: (jnp.maximum(i - n_p, 0), 0)))


def _pick(i, tm, p_ref, s_ref):
    return jnp.where(i < N_PROMPT // tm, p_ref[...], s_ref[...])


def _ada_kernel(ct_ref, w_ref, b_ref, o_ref):
    s = _silu(ct_ref[...])
    w = w_ref[...]
    rows = [jnp.sum(w * s[:, r:r + 1], axis=0, keepdims=True) for r in range(1 + DEC_BATCH)]
    rows.append(jnp.zeros((MOD_ROWS - len(rows), w.shape[1]), F32))
    o_ref[...] = jnp.concatenate(rows, axis=0) + b_ref[...]


def _adaln(ct, w, b):
    tn = 1024
    return pl.pallas_call(
        _ada_kernel,
        grid=(3 * D // tn,),
        in_specs=[pl.BlockSpec((D, MOD_ROWS), lambda j: (0, 0)),
                  pl.BlockSpec((D, tn), lambda j: (0, j)),
                  pl.BlockSpec((1, tn), lambda j: (0, j))],
        out_specs=pl.BlockSpec((MOD_ROWS, tn), lambda j: (0, j)),
        out_shape=jax.ShapeDtypeStruct((MOD_ROWS, 3 * D), F32),
        compiler_params=_params("arbitrary"),
        name="adaln",
    )(ct, w, b.reshape(1, 3 * D))


def _norm_mod_kernel(xp_ref, xs_ref, g_ref, mod_ref, x_ref, h_ref, *, tm):
    i = pl.program_id(0)
    m = mod_ref[pl.ds(_mod_row(i * tm), 1), :]
    x = _pick(i, tm, xp_ref, xs_ref)
    x_ref[...] = x
    y = _rms(x, g_ref[...])
    h_ref[...] = (y * (1.0 + m[:, D:2 * D]) + m[:, :D]).astype(BF16)


def _norm_mod(xp, xs, g, mod):
    tm = 512
    return pl.pallas_call(
        functools.partial(_norm_mod_kernel, tm=tm),
        grid=(N_TOK // tm,),
        in_specs=[*_two_source_specs(tm, D),
                  pl.BlockSpec((1, D), lambda i: (0, 0)),
                  pl.BlockSpec((MOD_ROWS, 3 * D), lambda i: (0, 0))],
        out_specs=[pl.BlockSpec((tm, D), lambda i: (i, 0))] * 2,
        out_shape=[jax.ShapeDtypeStruct((N_TOK, D), F32), jax.ShapeDtypeStruct((N_TOK, D), BF16)],
        compiler_params=_params("arbitrary"),
        name="norm_mod",
    )(xp, xs, g.reshape(1, D), mod)


def _in_proj_kernel(h_ref, w_ref, o_ref, wb_ref):
    @pl.when(pl.program_id(1) == 0)
    def _():
        wb_ref[...] = w_ref[...].astype(BF16)

    o_ref[...] = _dot(h_ref[...], wb_ref[...]).astype(BF16)


def _in_proj(h, w, tn):
    tm = 1024
    n = w.shape[1]
    return pl.pallas_call(
        _in_proj_kernel,
        grid=(n // tn, N_TOK // tm),
        in_specs=[pl.BlockSpec((tm, D), lambda j, i: (i, 0)),
                  pl.BlockSpec((D, tn), lambda j, i: (0, j))],
        out_specs=pl.BlockSpec((tm, tn), lambda j, i: (i, j)),
        out_shape=jax.ShapeDtypeStruct((N_TOK, n), BF16),
        scratch_shapes=[pltpu.VMEM((D, tn), BF16)],
        compiler_params=_params("arbitrary", "arbitrary"),
        name="in_proj",
    )(h, w)


def _out_proj_kernel(ap_ref, as_ref, w_ref, x_ref, mod_ref, g_ref, modn_ref, *rest, tm, final):
    wb_ref = rest[-1]
    i = pl.program_id(0)

    @pl.when(i == 0)
    def _():
        wb_ref[...] = w_ref[...].astype(BF16)

    r = _mod_row(i * tm)
    gate = mod_ref[pl.ds(r, 1), :][:, 2 * D:]
    xn = x_ref[...] + gate * _dot(_pick(i, tm, ap_ref, as_ref), wb_ref[...])
    y = _rms(xn, g_ref[...])
    if final:
        rest[0][...] = y
    else:
        mn = modn_ref[pl.ds(r, 1), :]
        rest[0][...] = xn
        rest[1][...] = (y * (1.0 + mn[:, D:2 * D]) + mn[:, :D]).astype(BF16)


def _out_proj(a_p, a_s, w, x, mod, g_next, mod_next, final):
    tm = 512
    kw = w.shape[0]
    row = pl.BlockSpec((tm, D), lambda i: (i, 0))
    if final:
        out_specs, out_shape = [row], [jax.ShapeDtypeStruct((N_TOK, D), F32)]
    else:
        out_specs = [row, row]
        out_shape = [jax.ShapeDtypeStruct((N_TOK, D), F32), jax.ShapeDtypeStruct((N_TOK, D), BF16)]
    return pl.pallas_call(
        functools.partial(_out_proj_kernel, tm=tm, final=final),
        grid=(N_TOK // tm,),
        in_specs=[*_two_source_specs(tm, kw),
                  pl.BlockSpec((kw, D), lambda i: (0, 0)),
                  row,
                  pl.BlockSpec((MOD_ROWS, 3 * D), lambda i: (0, 0)),
                  pl.BlockSpec((1, D), lambda i: (0, 0)),
                  pl.BlockSpec((MOD_ROWS, 3 * D), lambda i: (0, 0))],
        out_specs=out_specs,
        out_shape=out_shape,
        scratch_shapes=[pltpu.VMEM((kw, D), BF16)],
        compiler_params=_params("arbitrary"),
        name="out_proj",
    )(a_p, a_s, w, x, mod, g_next.reshape(1, D), mod_next)


def _ret_kernel(*refs, T, HB, rope, state_in, state_out):
    refs = list(refs)
    dec_ref, q_ref, k_ref, v_ref, z_ref, gn_ref = refs[:6]
    pos = 6
    if rope:
        cos_ref, sin_ref = refs[pos:pos + 2]
        pos += 2
    if state_in:
        s0_ref = refs[pos]
        pos += 1
    o_ref = refs[pos]
    pos += 1
    if state_out:
        sn_ref = refs[pos]
        pos += 1
    kv_ref, sf_ref, sb_ref = refs[pos:pos + 3]
    pos += 3
    if rope:
        kr_ref = refs[pos]

    hg = pl.program_id(1)
    n = T // CHUNK
    L = CHUNK
    ii = lax.broadcasted_iota(jnp.int32, (L, L), 0).astype(F32)
    jj = lax.broadcasted_iota(jnp.int32, (L, L), 1).astype(F32)
    scale = RET_DK ** -0.5

    def chunk(c):
        return slice(c * L, (c + 1) * L)

    def rotate(x, c):
        return x * cos_ref[chunk(c), :] + pltpu.roll(x, RET_DK // 2, 1) * sin_ref[chunk(c), :]

    for hh in range(HB):
        head = hg * HB + hh
        qs = slice(hh * RET_DK, (hh + 1) * RET_DK)
        vs = slice(hh * RET_DV, (hh + 1) * RET_DV)
        lg_f = _log_sigmoid(jnp.full((1, RET_DV), dec_ref[0, head], F32))
        lg_b = _log_sigmoid(jnp.full((1, RET_DV), dec_ref[1, head], F32))
        lf, lb = lg_f[:, :L], lg_b[:, :L]
        k_dec_f = jnp.exp(lf * (L - 1.0 - ii)) * scale
        k_dec_b = jnp.exp(lb * ii) * scale
        q_dec_f = jnp.exp(lf * (ii + 1.0))
        q_dec_b = jnp.exp(lb * (L - ii))
        mask = (jnp.where(ii >= jj, jnp.exp(lf * jnp.maximum(ii - jj, 0.0)), 0.0)
                + jnp.where(jj >= ii, jnp.exp(lb * jnp.maximum(jj - ii, 0.0)), 0.0)) * scale
        g_f = jnp.exp(lg_f * float(L))
        g_b = jnp.exp(lg_b * float(L))

        for c in range(n):
            kc = k_ref[chunk(c), qs].astype(F32)
            if rope:
                kc = rotate(kc, c)
                kr_ref[chunk(c), qs] = kc.astype(BF16)
            kk = jnp.concatenate([kc * k_dec_f, kc * k_dec_b], axis=1).astype(BF16)
            kv_ref[hh, c] = _dot_tn(kk, v_ref[chunk(c), vs])
        S = s0_ref[0, hh] if state_in else None
        has_f = []
        for c in range(n):
            has_f.append(S is not None)
            if S is not None:
                sf_ref[hh, c] = S.astype(BF16)
            kvc = kv_ref[hh, c, :RET_DK, :]
            S = kvc if S is None else S * g_f + kvc
        if state_out:
            sn_ref[0, hh] = S
        S = s0_ref[1, hh] if state_in else None
        has_b = [False] * n
        for c in reversed(range(n)):
            has_b[c] = S is not None
            if S is not None:
                sb_ref[hh, c] = S.astype(BF16)
            kvc = kv_ref[hh, c, RET_DK:, :]
            S = kvc if S is None else S * g_b + kvc
        if state_out:
            sn_ref[1, hh] = S

        for c in range(n):
            qc = q_ref[chunk(c), qs]
            qf = qc.astype(F32)
            if rope:
                qf = rotate(qf, c)
                qc = qf.astype(BF16)
                kc = kr_ref[chunk(c), qs]
            else:
                kc = k_ref[chunk(c), qs]
            lhs = [(_dot_nt(qc, kc) * mask).astype(BF16)]
            rhs = [v_ref[chunk(c), vs]]
            if has_f[c]:
                lhs.append((qf * q_dec_f).astype(BF16))
                rhs.append(sf_ref[hh, c])
            if has_b[c]:
                lhs.append((qf * q_dec_b).astype(BF16))
                rhs.append(sb_ref[hh, c])
            o = _dot(jnp.concatenate(lhs, axis=1), jnp.concatenate(rhs, axis=0))
            zf = z_ref[chunk(c), vs].astype(F32)
            o_ref[chunk(c), vs] = (_group_norm(o, gn_ref[:, vs]) * _silu(zf)).astype(BF16)


def _retention(proj, dec, gn, T, nb, row_blk0, rope=None, state=None, want_state=False):
    HB = RET_HB_PROMPT if T == SEQ else RET_HB_LATENT
    ng = RET_HEADS // HB
    n = T // CHUNK
    qw, vw = HB * RET_DK, HB * RET_DV
    in_specs = [pl.BlockSpec(memory_space=pltpu.SMEM),
                pl.BlockSpec((T, qw), lambda b, h: (row_blk0 + b, h)),
                pl.BlockSpec((T, qw), lambda b, h: (row_blk0 + b, ng + h)),
                pl.BlockSpec((T, vw), lambda b, h: (row_blk0 + b, ng + h)),
                pl.BlockSpec((T, vw), lambda b, h: (row_blk0 + b, 2 * ng + h)),
                pl.BlockSpec((1, vw), lambda b, h: (0, h))]
    args = [dec, proj, proj, proj, proj, gn.reshape(1, RET_WIDTH)]
    if rope is not None:
        in_specs += [pl.BlockSpec((T, RET_DK), lambda b, h: (0, 0))] * 2
        args += list(rope)
    if state is not None:
        in_specs.append(pl.BlockSpec((None, 2, HB, RET_DK, RET_DV), lambda b, h: (b, 0, h, 0, 0)))
        args.append(state)
    out_specs = [pl.BlockSpec((T, vw), lambda b, h: (b, h))]
    out_shape = [jax.ShapeDtypeStruct((nb * T, RET_WIDTH), BF16)]
    if want_state:
        out_specs.append(pl.BlockSpec((None, 2, HB, RET_DK, RET_DV), lambda b, h: (b, 0, h, 0, 0)))
        out_shape.append(jax.ShapeDtypeStruct((nb, 2, RET_HEADS, RET_DK, RET_DV), F32))
    scratch = [pltpu.VMEM((HB, n, 2 * RET_DK, RET_DV), F32),
               pltpu.VMEM((HB, n, RET_DK, RET_DV), BF16),
               pltpu.VMEM((HB, n, RET_DK, RET_DV), BF16)]
    if rope is not None:
        scratch.append(pltpu.VMEM((T, qw), BF16))
    return pl.pallas_call(
        functools.partial(_ret_kernel, T=T, HB=HB, rope=rope is not None, state_in=state is not None,
                          state_out=want_state),
        grid=(nb, ng),
        in_specs=in_specs, out_specs=out_specs, out_shape=out_shape, scratch_shapes=scratch,
        compiler_params=_params("arbitrary", "arbitrary"),
        name="retention",
    )(*args)


def _ml_pre_kernel(xm_ref, cw_ref, cb_ref, bq_ref, bk_ref, bv_ref, wq_ref, wk_ref, wv_ref, bias_ref,
                   q_ref, k_ref, v_ref, xc_ref, g_ref):
    i = pl.program_id(0)
    j = pl.program_id(1)
    T = jnp.where(i < N_PROMPT // ML_ROWS, SEQ, DEC_SEQ)
    xb = xm_ref[...]
    x = xb.astype(F32)
    pos = lax.broadcasted_iota(jnp.int32, x.shape, 0) & (T - 1)
    pad = ML_CONV // 2
    conv = cb_ref[...] + x * cw_ref[pad:pad + 1, :]
    for t in range(ML_CONV):
        delta = t - pad
        if delta == 0:
            continue
        shifted = pltpu.roll(x, (-delta) % ML_ROWS, 0)
        ok = (pos + delta >= 0) & (pos + delta < T)
        conv = conv + jnp.where(ok, shifted, 0.0) * cw_ref[t:t + 1, :]
    xc = _silu(conv)
    xcb = xc.astype(BF16)
    qb = _dot(xcb, bq_ref[...]).astype(BF16)
    kb = _dot(xcb, bk_ref[...]).astype(BF16)
    vb = _dot(xb, bv_ref[...]).astype(BF16)
    q_ref[...] = qb
    k_ref[...] = kb
    v_ref[...] = vb
    xc_ref[...] = xcb
    g = (_dot(qb, wq_ref[...].astype(BF16)) + _dot(kb, wk_ref[...].astype(BF16))
         + _dot(vb, wv_ref[...].astype(BF16)))

    @pl.when(j == 0)
    def _():
        g_ref[...] = g + bias_ref[...]

    @pl.when(j > 0)
    def _():
        g_ref[...] += g


def _ml_pre(proj, conv_w, conv_b, bdq, bdk, bdv, wif, bif):
    nj = ML_WIDTH // ML_TILE
    ng = wif.shape[1]
    tok = pl.BlockSpec((ML_ROWS, ML_TILE), lambda i, j: (i, j))
    bd = pl.BlockSpec((None, ML_TILE, ML_TILE), lambda i, j: (j, 0, 0))
    return pl.pallas_call(
        _ml_pre_kernel,
        grid=(N_TOK // ML_ROWS, nj),
        in_specs=[tok,
                  pl.BlockSpec((ML_CONV, ML_TILE), lambda i, j: (0, j)),
                  pl.BlockSpec((1, ML_TILE), lambda i, j: (0, j)),
                  bd, bd, bd,
                  pl.BlockSpec((ML_TILE, ng), lambda i, j: (j, 0)),
                  pl.BlockSpec((ML_TILE, ng), lambda i, j: (nj + j, 0)),
                  pl.BlockSpec((ML_TILE, ng), lambda i, j: (2 * nj + j, 0)),
                  pl.BlockSpec((1, ng), lambda i, j: (0, 0))],
        out_specs=[tok] * 4 + [pl.BlockSpec((ML_ROWS, ng), lambda i, j: (i, 0))],
        out_shape=[jax.ShapeDtypeStruct((N_TOK, ML_WIDTH), BF16)] * 4 + [jax.ShapeDtypeStruct((N_TOK, ng), F32)],
        compiler_params=_params("arbitrary", "arbitrary"),
        name="mlstm_pre",
    )(proj, conv_w, conv_b.reshape(1, ML_WIDTH), bdq, bdk, bdv, wif, wif, wif, bif.reshape(1, ng))


def _dot_split(lhs_bf16, rhs):
    r1 = rhs.astype(BF16)
    e1 = rhs - r1.astype(F32)
    r2 = e1.astype(BF16)
    r3 = (e1 - r2.astype(F32)).astype(BF16)
    return _dot(lhs_bf16, r1) + _dot(lhs_bf16, r2) + _dot(lhs_bf16, r3)


def _ml_scan_kernel(*refs, T, state_in, state_out):
    refs = list(refs)
    pos = 0
    if state_in:
        m0_ref = refs[0]
        pos = 1
    q_ref, k_ref, v_ref, g_ref, op_ref, z_ref, xc_ref, gn_ref, sk_ref = refs[pos:pos + 9]
    pos += 9
    if state_in:
        c0_ref, n0_ref = refs[pos:pos + 2]
        pos += 2
    o_ref = refs[pos]
    pos += 1
    if state_out:
        cn_ref, nn_ref, mn_ref = refs[pos:pos + 3]
        pos += 3
    acc_ref, s_ref = refs[pos:pos + 2]
    pos += 2
    C_ref = cn_ref if state_out else refs[pos]

    b = pl.program_id(0)
    h = pl.program_id(1)
    n = T // CHUNK
    L = CHUNK
    scale = ML_HD ** -0.5

    def chunk(c):
        return slice(c * L, (c + 1) * L)

    ii = lax.broadcasted_iota(jnp.int32, (L, L), 0)
    jj = lax.broadcasted_iota(jnp.int32, (L, L), 1)
    neg = jnp.full((L, L), -jnp.inf, F32)
    zero = jnp.zeros((L, L), F32)
    tri = [jnp.where(jj <= ii, 1.0, 0.0).astype(BF16), jnp.where(jj >= ii, 1.0, 0.0).astype(BF16)]
    bias = [jnp.where(ii >= jj, zero, neg), jnp.where(ii <= jj, zero, neg)]

    g = g_ref[...]
    col = lax.broadcasted_iota(jnp.int32, g.shape, 1)

    def gate_column(j):
        return jnp.sum(jnp.where(col == j, g, 0.0), axis=1, keepdims=True)

    for c in range(n):
        s_ref[c] = _dot_nt(q_ref[chunk(c), :], k_ref[chunk(c), :])

    G = [[None] * n for _ in range(2)]
    for d in range(2):
        gi_all = gate_column(d * 2 * ML_HEADS + h)
        gf_all = _log_sigmoid(gate_column(d * 2 * ML_HEADS + ML_HEADS + h))
        for c in range(n):
            gi = gi_all[chunk(c)]
            gf = gf_all[chunk(c)]
            cum = _dot_split(tri[d], jnp.broadcast_to(gf, (L, L)))
            d_log = cum - jnp.transpose(cum - gi) + bias[d]
            b_col = cum[:, :1]
            b_last = jnp.sum(gf, axis=0, keepdims=True)
            w_log = b_last - b_col + gi
            G[d][c] = dict(d_log=d_log, m_intra=jnp.max(d_log, axis=1, keepdims=True), b_col=b_col,
                           b_last=b_last, w_log=w_log, w_max=jnp.max(w_log, axis=0, keepdims=True))
        m = jnp.full((1, 1), m0_ref[(b * 2 + d) * ML_HEADS + h], F32) if state_in else jnp.zeros((1, 1), F32)
        for c in (range(n) if d == 0 else reversed(range(n))):
            e = G[d][c]
            m_new = jnp.maximum(e["b_last"] + m, e["w_max"])
            e.update(m_prev=m, m_new=m_new, dec=jnp.exp(e["b_last"] + m - m_new))
            m = m_new
        if state_out:
            mn_ref[d] = jnp.broadcast_to(m, mn_ref.shape[1:])

    nv = [None, None]
    if state_in:
        C_ref[...] = c0_ref[...]
        nv = [n0_ref[0], n0_ref[1]]
    seen = set()
    for t in range(n):
        for d in range(2):
            c = t if d == 0 else n - 1 - t
            last = t == n - 1
            e = G[d][c]
            has_state = state_in or t > 0
            qc = q_ref[chunk(c), :]
            kc = k_ref[chunk(c), :]
            vc = v_ref[chunk(c), :]
            inter = e["b_col"] + e["m_prev"]
            m_i = jnp.maximum(inter, e["m_intra"])
            s = s_ref[c] * (jnp.exp(e["d_log"] - m_i) * scale)
            num = _dot(s.astype(BF16), vc)
            den = jnp.sum(s, axis=1, keepdims=True)
            if has_state:
                w_prev = jnp.exp(inter - m_i)
                qw = qc.astype(F32) * w_prev
                num = num + _dot(qw.astype(BF16), C_ref[d].astype(BF16))
                den = den + jnp.sum(qw * nv[d], axis=1, keepdims=True)
            hb = num / jnp.maximum(jnp.abs(den), jnp.exp(-m_i))
            if c not in seen:
                seen.add(c)
                acc_ref[chunk(c), :] = hb
            else:
                cell = _group_norm((acc_ref[chunk(c), :] + hb) * jax.nn.sigmoid(op_ref[chunk(c), :].astype(F32)),
                                   gn_ref[...])
                mixed = cell + sk_ref[...] * xc_ref[chunk(c), :].astype(F32)
                o_ref[chunk(c), :] = (mixed * _silu(z_ref[chunk(c), :].astype(F32))).astype(BF16)
            if last and not state_out:
                continue
            kw = kc.astype(F32) * (jnp.exp(e["w_log"] - e["m_new"]) * scale)
            kv = _dot_tn(kw.astype(BF16), vc)
            k_sum = jnp.sum(kw, axis=0, keepdims=True)
            if has_state:
                C_ref[d] = e["dec"] * C_ref[d] + kv
                nv[d] = e["dec"] * nv[d] + k_sum
            else:
                C_ref[d] = kv
                nv[d] = k_sum
    if state_out:
        nn_ref[0] = nv[0]
        nn_ref[1] = nv[1]


def _ml_scan(proj, q, k, v, xc, g, gn, skip, T, nb, row_blk0, state=None, want_state=False):
    H = ML_HEADS
    n = T // CHUNK
    tok = lambda b, h: (row_blk0 + b, h)
    head_vec = pl.BlockSpec((1, ML_HD), lambda b, h: (0, h))
    both = lambda *tail: pl.BlockSpec((None, 2, None, *tail), lambda b, h: (b, 0, h, 0, 0))
    in_specs, args = [], []
    if state is not None:
        c0, n0, m0 = state
        in_specs.append(pl.BlockSpec(memory_space=pltpu.SMEM))
        args.append(m0.reshape(nb * 2 * H))
    in_specs += [pl.BlockSpec((T, ML_HD), tok)] * 3
    in_specs += [pl.BlockSpec((T, g.shape[1]), lambda b, h: (row_blk0 + b, 0)),
                 pl.BlockSpec((T, ML_HD), lambda b, h: (row_blk0 + b, H + h)),
                 pl.BlockSpec((T, ML_HD), lambda b, h: (row_blk0 + b, 2 * H + h)),
                 pl.BlockSpec((T, ML_HD), tok), head_vec, head_vec]
    args += [q, k, v, g, proj, proj, xc, gn.reshape(1, ML_WIDTH), skip.reshape(1, ML_WIDTH)]
    if state is not None:
        in_specs += [both(ML_HD, ML_HD), both(1, ML_HD)]
        args += [c0, n0.reshape(nb, 2, H, 1, ML_HD)]
    out_specs = [pl.BlockSpec((T, ML_HD), lambda b, h: (b, h))]
    out_shape = [jax.ShapeDtypeStruct((nb * T, ML_WIDTH), BF16)]
    if want_state:
        out_specs += [both(ML_HD, ML_HD), both(1, ML_HD), both(1, 128)]
        out_shape += [jax.ShapeDtypeStruct((nb, 2, H, ML_HD, ML_HD), F32),
                      jax.ShapeDtypeStruct((nb, 2, H, 1, ML_HD), F32),
                      jax.ShapeDtypeStruct((nb, 2, H, 1, 128), F32)]
    scratch = [pltpu.VMEM((T, ML_HD), F32), pltpu.VMEM((n, CHUNK, CHUNK), F32)]
    if not want_state:
        scratch.append(pltpu.VMEM((2, ML_HD, ML_HD), F32))
    return pl.pallas_call(
        functools.partial(_ml_scan_kernel, T=T, state_in=state is not None, state_out=want_state),
        grid=(nb, H),
        in_specs=in_specs, out_specs=out_specs, out_shape=out_shape, scratch_shapes=scratch,
        compiler_params=_params("arbitrary", "arbitrary"),
        name="mlstm_scan",
    )(*args)


def _att_rope(x, cos, sin):
    lane = lax.broadcasted_iota(jnp.int32, x.shape, 1)
    half = ATT_HD // 2
    rot = jnp.where((lane & (ATT_HD - 1)) < half, pltpu.roll(x, 128 - half, 1), pltpu.roll(x, half, 1))
    return x * cos + rot * sin


def _att_heads(q_slabs, kv, sink_ref, key_sets, rows):
    lane = lax.broadcasted_iota(jnp.int32, (rows, 128), 1)
    lo = lane < ATT_HD
    zero = jnp.zeros((rows, 128), BF16)
    q_stack = jnp.concatenate([jnp.where(lo, q_slabs[0], zero), jnp.where(lo, zero, q_slabs[0]),
                               jnp.where(lo, q_slabs[1], zero), jnp.where(lo, zero, q_slabs[1])], axis=0)
    sink = jnp.concatenate([jnp.full((rows, 1), sink_ref[kv * ATT_GROUP + g], F32) for g in range(ATT_GROUP)],
                           axis=0)
    scale = ATT_HD ** -0.5
    native_lo = kv % 2 == 0

    def native(lanes):
        return lanes < ATT_HD if native_lo else lanes >= ATT_HD

    scores = []
    for k_s, _, bias in key_sets:
        kf = k_s.astype(F32)
        klane = lax.broadcasted_iota(jnp.int32, kf.shape, 1)
        k_both = jnp.where(native(klane), kf, pltpu.roll(kf, ATT_HD, 1)).astype(BF16)
        s = _dot_nt(q_stack, k_both) * scale
        if bias is not None:
            s = s + jnp.concatenate([bias] * ATT_GROUP, axis=0)
        scores.append(s)
    m = sink
    for s in scores:
        m = jnp.maximum(m, jnp.max(s, axis=1, keepdims=True))
    den = jnp.exp(sink - m)
    acc = [jnp.zeros((rows, 128), F32), jnp.zeros((rows, 128), F32)]
    for s, (_, v_s, _) in zip(scores, key_sets):
        p = jnp.exp(s - m)
        den = den + jnp.sum(p, axis=1, keepdims=True)
        vf = v_s.astype(F32)
        vlane = lax.broadcasted_iota(jnp.int32, vf.shape, 1)
        v_nat = jnp.where(native(vlane), vf, 0.0)
        v_other = pltpu.roll(v_nat, ATT_HD, 1)
        v_lo, v_hi = (v_nat, v_other) if native_lo else (v_other, v_nat)
        v_cat = jnp.concatenate([v_lo, v_hi], axis=0).astype(BF16)
        pb = p.astype(BF16)
        for pair in range(2):
            p_cat = jnp.concatenate([pb[(2 * pair) * rows:(2 * pair + 1) * rows],
                                     pb[(2 * pair + 1) * rows:(2 * pair + 2) * rows]], axis=1)
            acc[pair] = acc[pair] + _dot(p_cat, v_cat)
    inv = 1.0 / den
    out = []
    for pair in range(2):
        inv_pair = jnp.where(lo, inv[(2 * pair) * rows:(2 * pair + 1) * rows],
                             inv[(2 * pair + 1) * rows:(2 * pair + 2) * rows])
        out.append(acc[pair] * inv_pair)
    return out


def _att_ctx_kernel(sink_ref, q_ref, k_ref, v_ref, z0_ref, z1_ref, o_ref):
    T = q_ref.shape[0]
    for kv in range(ATT_KV):
        slab = (kv // 2) * 128
        k_s = k_ref[:, slab:slab + 128]
        v_s = v_ref[:, slab:slab + 128]
        q_slabs = [q_ref[:, (2 * kv + p) * 128:(2 * kv + p + 1) * 128] for p in range(2)]
        outs = _att_heads(q_slabs, kv, sink_ref, [(k_s, v_s, None)], T)
        for p in range(2):
            c0 = (2 * kv + p) * 128
            z_ref = z0_ref if c0 < ATT_WIDTH // 2 else z1_ref
            zc = c0 % (ATT_WIDTH // 2)
            zf = z_ref[:, zc:zc + 128].astype(F32)
            o_ref[:, c0:c0 + 128] = (outs[p] * _silu(zf)).astype(BF16)


def _att_ctx(proj, sink):
    T = SEQ
    half = ATT_WIDTH // 2
    return pl.pallas_call(
        _att_ctx_kernel,
        grid=(BATCH,),
        in_specs=[pl.BlockSpec(memory_space=pltpu.SMEM),
                  pl.BlockSpec((T, ATT_WIDTH), lambda b: (b, 0)),
                  pl.BlockSpec((T, ATT_KVW), lambda b: (b, ATT_WIDTH // ATT_KVW)),
                  pl.BlockSpec((T, ATT_KVW), lambda b: (b, ATT_WIDTH // ATT_KVW + 1)),
                  pl.BlockSpec((T, half), lambda b: (b, (ATT_WIDTH + 2 * ATT_KVW) // half)),
                  pl.BlockSpec((T, half), lambda b: (b, (ATT_WIDTH + 2 * ATT_KVW) // half + 1))],
        out_specs=pl.BlockSpec((T, ATT_WIDTH), lambda b: (b, 0)),
        out_shape=jax.ShapeDtypeStruct((N_PROMPT, ATT_WIDTH), BF16),
        compiler_params=_params("arbitrary"),
        name="att_ctx",
    )(sink, proj, proj, proj, proj, proj)


def _att_win_kernel(sink_ref, q_ref, k_ref, v_ref, z0_ref, z1_ref, kc_ref, vc_ref, cos_ref, sin_ref,
                    o_ref, qs_ref, ks_ref):
    T = q_ref.shape[0]
    L = CHUNK
    n = T // L
    cos = cos_ref[...]
    sin = sin_ref[...]
    for c0 in range(0, ATT_WIDTH, 128):
        qs_ref[:, c0:c0 + 128] = _att_rope(q_ref[:, c0:c0 + 128].astype(F32), cos, sin).astype(BF16)
    for c0 in range(0, ATT_KVW, 128):
        ks_ref[:, c0:c0 + 128] = _att_rope(k_ref[:, c0:c0 + 128].astype(F32), cos, sin).astype(BF16)

    ii = lax.broadcasted_iota(jnp.int32, (L, L), 0)
    jj = lax.broadcasted_iota(jnp.int32, (L, L), 1)
    bias_prev = jnp.where(jj >= ii, 0.0, -jnp.inf).astype(F32)
    bias_next = jnp.where(jj <= ii, 0.0, -jnp.inf).astype(F32)
    bias_same = jnp.zeros((L, L), F32)
    for c in range(n):
        chunks = [cb for cb in (c - 1, c, c + 1) if 0 <= cb < n]
        mask = jnp.concatenate([bias_prev if cb < c else (bias_next if cb > c else bias_same) for cb in chunks],
                               axis=1)
        r0, r1 = chunks[0] * L, (chunks[-1] + 1) * L
        for kv in range(ATT_KV):
            slab = (kv // 2) * 128
            q_slabs = [qs_ref[c * L:(c + 1) * L, (2 * kv + p) * 128:(2 * kv + p + 1) * 128] for p in range(2)]
            key_sets = [(ks_ref[r0:r1, slab:slab + 128], v_ref[r0:r1, slab:slab + 128], mask),
                        (kc_ref[:, slab:slab + 128], vc_ref[:, slab:slab + 128], None)]
            outs = _att_heads(q_slabs, kv, sink_ref, key_sets, L)
            for p in range(2):
                col = (2 * kv + p) * 128
                z_ref = z0_ref if col < ATT_WIDTH // 2 else z1_ref
                zc = col % (ATT_WIDTH // 2)
                zf = z_ref[c * L:(c + 1) * L, zc:zc + 128].astype(F32)
                o_ref[c * L:(c + 1) * L, col:col + 128] = (outs[p] * _silu(zf)).astype(BF16)


def _att_win(proj, sink, k_ctx, v_ctx, cos, sin):
    T = DEC_SEQ
    half = ATT_WIDTH // 2
    rb = N_PROMPT // T
    return pl.pallas_call(
        _att_win_kernel,
        grid=(DEC_BATCH,),
        in_specs=[pl.BlockSpec(memory_space=pltpu.SMEM),
                  pl.BlockSpec((T, ATT_WIDTH), lambda b: (rb + b, 0)),
                  pl.BlockSpec((T, ATT_KVW), lambda b: (rb + b, ATT_WIDTH // ATT_KVW)),
                  pl.BlockSpec((T, ATT_KVW), lambda b: (rb + b, ATT_WIDTH // ATT_KVW + 1)),
                  pl.BlockSpec((T, half), lambda b: (rb + b, (ATT_WIDTH + 2 * ATT_KVW) // half)),
                  pl.BlockSpec((T, half), lambda b: (rb + b, (ATT_WIDTH + 2 * ATT_KVW) // half + 1)),
                  pl.BlockSpec((None, PAST_LEN, ATT_KVW), lambda b: (b, 0, 0)),
                  pl.BlockSpec((None, PAST_LEN, ATT_KVW), lambda b: (b, 0, 0)),
                  pl.BlockSpec((T, 128), lambda b: (0, 0)),
                  pl.BlockSpec((T, 128), lambda b: (0, 0))],
        out_specs=pl.BlockSpec((T, ATT_WIDTH), lambda b: (b, 0)),
        out_shape=jax.ShapeDtypeStruct((N_LATENT, ATT_WIDTH), BF16),
        scratch_shapes=[pltpu.VMEM((T, ATT_WIDTH), BF16), pltpu.VMEM((T, ATT_KVW), BF16)],
        compiler_params=_params("arbitrary"),
        name="att_win",
    )(sink, proj, proj, proj, proj, proj, k_ctx, v_ctx, cos, sin)


def _rope_tables(T, hd, reps):
    rows = T // GRID_W
    row = jnp.repeat(jnp.arange(rows, dtype=F32), GRID_W)
    col = jnp.tile(jnp.arange(GRID_W, dtype=F32), rows)
    nf = hd // 4
    inv = ROPE_BASE ** (-jnp.arange(nf, dtype=F32) / nf)
    ang = jnp.concatenate([row[:, None] * inv[None, :], col[:, None] * inv[None, :]], axis=-1)
    cos, sin = jnp.cos(ang), jnp.sin(ang)
    return (jnp.tile(jnp.concatenate([cos, cos], axis=-1), (1, reps)),
            jnp.tile(jnp.concatenate([-sin, sin], axis=-1), (1, reps)))


def _block_diag_tiles(w):
    per = ML_TILE // ML_BLOCK
    wt = w.reshape(ML_WIDTH // ML_TILE, per, ML_BLOCK, ML_BLOCK)
    eye = jnp.eye(per, dtype=w.dtype)
    dense = wt[:, :, :, None, :] * eye[None, :, None, :, None]
    return dense.reshape(ML_WIDTH // ML_TILE, ML_TILE, ML_TILE).astype(BF16)


def _retention_layer(h, w_in, decay_f, decay_b, gn, state, rope):
    proj = _in_proj(h, w_in, 1024)
    dec = jnp.stack([decay_f, decay_b]).astype(F32)
    a_p, new_state = _retention(proj, dec, gn, SEQ, BATCH, 0, want_state=True)
    (a_s,) = _retention(proj, dec, gn, DEC_SEQ, DEC_BATCH, N_PROMPT // DEC_SEQ, rope=rope, state=state)
    return a_p, a_s, new_state


def kernel(x_prompt, x_sample, c, c_ctx, state_l0_ret, state_l1_C, state_l1_n, state_l1_m, cache_l2_k, cache_l2_v, state_l3_ret, norm_l0, ada_w_l0, ada_b_l0, w_in_l0, w_out_l0, ret_decay_f_l0, ret_decay_b_l0, ret_gn_l0, norm_l1, ada_w_l1, ada_b_l1, w_in_l1, w_out_l1, conv_w_l1, conv_b_l1, wq_l1, wk_l1, wv_l1, wif_f_l1, bif_f_l1, wif_b_l1, bif_b_l1, gn_l1, skip_l1, norm_l2, ada_w_l2, ada_b_l2, w_in_l2, w_out_l2, sink_l2, norm_l3, ada_w_l3, ada_b_l3, w_in_l3, w_out_l3, ret_decay_f_l3, ret_decay_b_l3, ret_gn_l3, final_norm):
    ct = jnp.concatenate([c_ctx[:, None], c.T, jnp.zeros((D, MOD_ROWS - 1 - DEC_BATCH), F32)], axis=1)
    mods = [_adaln(ct, w, b) for w, b in ((ada_w_l0, ada_b_l0), (ada_w_l1, ada_b_l1),
                                          (ada_w_l2, ada_b_l2), (ada_w_l3, ada_b_l3))]
    rope_ret = _rope_tables(DEC_SEQ, RET_DK, 1)
    rope_att = _rope_tables(DEC_SEQ, ATT_HD, 2)

    x, h = _norm_mod(x_prompt.reshape(N_PROMPT, D), x_sample.reshape(N_LATENT, D), norm_l0, mods[0])

    a_p, a_s, new_l0_ret = _retention_layer(h, w_in_l0, ret_decay_f_l0, ret_decay_b_l0, ret_gn_l0, state_l0_ret,
                                            rope_ret)
    x, h = _out_proj(a_p, a_s, w_out_l0, x, mods[0], norm_l1, mods[1], False)

    proj = _in_proj(h, w_in_l1, 1024)
    bdq, bdk, bdv = _block_diag_tiles(wq_l1), _block_diag_tiles(wk_l1), _block_diag_tiles(wv_l1)
    wif = jnp.concatenate([wif_f_l1, wif_b_l1], axis=1)
    bif = jnp.concatenate([bif_f_l1, bif_b_l1])
    q, k, v, xc, g = _ml_pre(proj, conv_w_l1, conv_b_l1, bdq, bdk, bdv, wif, bif)
    a_p, new_l1_C, n_new, m_new = _ml_scan(proj, q, k, v, xc, g, gn_l1, skip_l1, SEQ, BATCH, 0, want_state=True)
    (a_s,) = _ml_scan(proj, q, k, v, xc, g, gn_l1, skip_l1, DEC_SEQ, DEC_BATCH, N_PROMPT // DEC_SEQ,
                      state=(state_l1_C, state_l1_n, state_l1_m))
    new_l1_n = n_new.reshape(BATCH, 2, ML_HEADS, ML_HD)
    new_l1_m = m_new[:, :, :, 0, 0]
    x, h = _out_proj(a_p, a_s, w_out_l1, x, mods[1], norm_l2, mods[2], False)

    proj = _in_proj(h, w_in_l2, 1280)
    kv_new = proj[:N_PROMPT, ATT_WIDTH:ATT_WIDTH + 2 * ATT_KVW].astype(F32)
    new_l2_k = kv_new[:, :ATT_KVW].reshape(BATCH, SEQ, ATT_KV, ATT_HD)
    new_l2_v = kv_new[:, ATT_KVW:].reshape(BATCH, SEQ, ATT_KV, ATT_HD)
    a_p = _att_ctx(proj, sink_l2)
    a_s = _att_win(proj, sink_l2, cache_l2_k.reshape(DEC_BATCH, PAST_LEN, ATT_KVW),
                   cache_l2_v.reshape(DEC_BATCH, PAST_LEN, ATT_KVW), rope_att[0], rope_att[1])
    x, h = _out_proj(a_p, a_s, w_out_l2, x, mods[2], norm_l3, mods[3], False)

    a_p, a_s, new_l3_ret = _retention_layer(h, w_in_l3, ret_decay_f_l3, ret_decay_b_l3, ret_gn_l3, state_l3_ret,
                                            rope_ret)
    (y,) = _out_proj(a_p, a_s, w_out_l3, x, mods[3], final_norm, mods[3], True)

    y_prompt = y[:N_PROMPT].reshape(BATCH, SEQ, D)
    y_sample = y[N_PROMPT:].reshape(DEC_BATCH, DEC_SEQ, D)
    return (y_prompt, y_sample, new_l0_ret, new_l1_C, new_l1_n, new_l1_m, new_l2_k, new_l2_v, new_l3_ret)
```

```python
import functools

import jax
import jax.numpy as jnp
from jax import lax
from jax.experimental import pallas as pl
from jax.experimental.pallas import tpu as pltpu

F32 = jnp.float32
BF16 = jnp.bfloat16

D = 1024
BATCH = 16
SEQ = 256
DEC_BATCH = 2
DEC_SEQ = 1024
PAST_LEN = 512
GRID_W = 64
CHUNK = 128
EPS = 1e-6
ROPE_BASE = 10000.0

N_PROMPT = BATCH * SEQ
N_LATENT = DEC_BATCH * DEC_SEQ
N_TOK = N_PROMPT + N_LATENT
MOD_ROWS = 8

RET_HEADS = 8
RET_DK = 128
RET_DV = 256
RET_QK = RET_HEADS * RET_DK
RET_WIDTH = RET_HEADS * RET_DV
RET_HB_PROMPT = 4
RET_HB_LATENT = 1

ML_HEADS = 4
ML_WIDTH = 2 * D
ML_HD = ML_WIDTH // ML_HEADS
ML_BLOCK = 4
ML_CONV = 5
ML_TILE = 256
ML_ROWS = 1024

ATT_HEADS = 16
ATT_KV = 4
ATT_HD = 64
ATT_GROUP = ATT_HEADS // ATT_KV
ATT_WIDTH = ATT_HEADS * ATT_HD
ATT_KVW = ATT_KV * ATT_HD
WINDOW = 128

VMEM_LIMIT = 56 * 1024 * 1024


def _params(*sem):
    return pltpu.CompilerParams(dimension_semantics=sem, vmem_limit_bytes=VMEM_LIMIT)


def _mod_row(row0):
    return jnp.maximum((row0 - N_PROMPT) // DEC_SEQ + 1, 0)


def _dot(a, b):
    return jnp.dot(a, b, preferred_element_type=F32)


def _dot_nt(a, b):
    return lax.dot_general(a, b, (((1,), (1,)), ((), ())), preferred_element_type=F32)


def _dot_tn(a, b):
    return lax.dot_general(a, b, (((0,), (0,)), ((), ())), preferred_element_type=F32)


def _silu(x):
    return x * jax.nn.sigmoid(x)


def _log_sigmoid(x):
    return jnp.minimum(x, 0.0) - jnp.log1p(jnp.exp(-jnp.abs(x)))


def _rms(x, g):
    return x * lax.rsqrt(jnp.mean(x * x, axis=-1, keepdims=True) + EPS) * g


def _group_norm(x, g):
    xc = x - jnp.mean(x, axis=-1, keepdims=True)
    return xc * lax.rsqrt(jnp.mean(xc * xc, axis=-1, keepdims=True) + EPS) * g


def _two_source_specs(tm, width, tile0=0):
    n_p = N_PROMPT // tm
    return (pl.BlockSpec((tm, width), lambda i: (jnp.minimum(tile0 + i, n_p - 1), 0)),
            pl.BlockSpec((tm, width), lambda i: (jnp.maximum(tile0 + i - n_p, 0), 0)))


def _pick(i, tm, p_ref, s_ref):
    return jnp.where(i < N_PROMPT // tm, p_ref[...], s_ref[...])


def _ada_kernel(ct_ref, w_ref, b_ref, o_ref):
    s = _silu(ct_ref[...])
    w = w_ref[...]
    rows = [jnp.sum(w * s[:, r:r + 1], axis=0, keepdims=True) for r in range(1 + DEC_BATCH)]
    rows.append(jnp.zeros((MOD_ROWS - len(rows), w.shape[1]), F32))
    o_ref[...] = jnp.concatenate(rows, axis=0) + b_ref[...]


def _adaln(ct, w, b):
    tn = 1024
    return pl.pallas_call(
        _ada_kernel,
        grid=(3 * D // tn,),
        in_specs=[pl.BlockSpec((D, MOD_ROWS), lambda j: (0, 0)),
                  pl.BlockSpec((D, tn), lambda j: (0, j)),
                  pl.BlockSpec((1, tn), lambda j: (0, j))],
        out_specs=pl.BlockSpec((MOD_ROWS, tn), lambda j: (0, j)),
        out_shape=jax.ShapeDtypeStruct((MOD_ROWS, 3 * D), F32),
        compiler_params=_params("arbitrary"),
        name="adaln",
    )(ct, w, b.reshape(1, 3 * D))


def _norm_mod_kernel(xp_ref, xs_ref, g_ref, mod_ref, x_ref, h_ref, *, tm):
    i = pl.program_id(0)
    m = mod_ref[pl.ds(_mod_row(i * tm), 1), :]
    x = _pick(i, tm, xp_ref, xs_ref)
    x_ref[...] = x
    y = _rms(x, g_ref[...])
    h_ref[...] = (y * (1.0 + m[:, D:2 * D]) + m[:, :D]).astype(BF16)


def _norm_mod(xp, xs, g, mod):
    tm = 512
    return pl.pallas_call(
        functools.partial(_norm_mod_kernel, tm=tm),
        grid=(N_TOK // tm,),
        in_specs=[*_two_source_specs(tm, D),
                  pl.BlockSpec((1, D), lambda i: (0, 0)),
                  pl.BlockSpec((MOD_ROWS, 3 * D), lambda i: (0, 0))],
        out_specs=[pl.BlockSpec((tm, D), lambda i: (i, 0))] * 2,
        out_shape=[jax.ShapeDtypeStruct((N_TOK, D), F32), jax.ShapeDtypeStruct((N_TOK, D), BF16)],
        compiler_params=_params("arbitrary"),
        name="norm_mod",
    )(xp, xs, g.reshape(1, D), mod)


def _in_proj_kernel(h_ref, w_ref, o_ref, wb_ref):
    @pl.when(pl.program_id(1) == 0)
    def _():
        wb_ref[...] = w_ref[...].astype(BF16)

    o_ref[...] = _dot(h_ref[...], wb_ref[...]).astype(BF16)


def _in_proj(h, w, tn):
    tm = 1024
    n = w.shape[1]
    return pl.pallas_call(
        _in_proj_kernel,
        grid=(n // tn, N_TOK // tm),
        in_specs=[pl.BlockSpec((tm, D), lambda j, i: (i, 0)),
                  pl.BlockSpec((D, tn), lambda j, i: (0, j))],
        out_specs=pl.BlockSpec((tm, tn), lambda j, i: (i, j)),
        out_shape=jax.ShapeDtypeStruct((N_TOK, n), BF16),
        scratch_shapes=[pltpu.VMEM((D, tn), BF16)],
        compiler_params=_params("arbitrary", "arbitrary"),
        name="in_proj",
    )(h, w)


def _out_proj_kernel(ap_ref, as_ref, w_ref, x_ref, mod_ref, g_ref, modn_ref, *rest, tm, tile0, final):
    wb_ref = rest[-1]

    @pl.when(pl.program_id(0) == 0)
    def _():
        wb_ref[...] = w_ref[...].astype(BF16)

    i = tile0 + pl.program_id(0)
    r = _mod_row(i * tm)
    gate = mod_ref[pl.ds(r, 1), :][:, 2 * D:]
    xn = x_ref[...] + gate * _dot(_pick(i, tm, ap_ref, as_ref), wb_ref[...])
    y = _rms(xn, g_ref[...])
    if final:
        rest[0][...] = y
    else:
        mn = modn_ref[pl.ds(r, 1), :]
        rest[0][...] = xn
        rest[1][...] = (y * (1.0 + mn[:, D:2 * D]) + mn[:, :D]).astype(BF16)


def _out_proj(a_p, a_s, w, x, mod, g_next, mod_next, final, rows=(0, N_TOK)):
    tm = 512
    kw = w.shape[0]
    tile0, nt = rows[0] // tm, rows[1] // tm
    row = pl.BlockSpec((tm, D), lambda i: (i, 0))
    if final:
        out_specs, out_shape = [row], [jax.ShapeDtypeStruct((rows[1], D), F32)]
    else:
        out_specs = [row, row]
        out_shape = [jax.ShapeDtypeStruct((rows[1], D), F32), jax.ShapeDtypeStruct((rows[1], D), BF16)]
    return pl.pallas_call(
        functools.partial(_out_proj_kernel, tm=tm, tile0=tile0, final=final),
        grid=(nt,),
        in_specs=[*_two_source_specs(tm, kw, tile0),
                  pl.BlockSpec((kw, D), lambda i: (0, 0)),
                  pl.BlockSpec((tm, D), lambda i: (tile0 + i, 0)),
                  pl.BlockSpec((MOD_ROWS, 3 * D), lambda i: (0, 0)),
                  pl.BlockSpec((1, D), lambda i: (0, 0)),
                  pl.BlockSpec((MOD_ROWS, 3 * D), lambda i: (0, 0))],
        out_specs=out_specs,
        out_shape=out_shape,
        scratch_shapes=[pltpu.VMEM((kw, D), BF16)],
        compiler_params=_params("arbitrary"),
        name="out_proj",
    )(a_p, a_s, w, x, mod, g_next.reshape(1, D), mod_next)


def _ret_kernel(*refs, T, HB, rope, state_in, state_out):
    refs = list(refs)
    dec_ref, q_ref, k_ref, v_ref, z_ref, gn_ref = refs[:6]
    pos = 6
    if rope:
        cos_ref, sin_ref = refs[pos:pos + 2]
        pos += 2
    if state_in:
        s0_ref = refs[pos]
        pos += 1
    o_ref = refs[pos]
    pos += 1
    if state_out:
        sn_ref = refs[pos]
        pos += 1
    kv_ref, sf_ref, sb_ref = refs[pos:pos + 3]
    pos += 3
    if rope:
        kr_ref = refs[pos]

    hg = pl.program_id(1)
    n = T // CHUNK
    L = CHUNK
    ii = lax.broadcasted_iota(jnp.int32, (L, L), 0).astype(F32)
    jj = lax.broadcasted_iota(jnp.int32, (L, L), 1).astype(F32)
    scale = RET_DK ** -0.5

    def chunk(c):
        return slice(c * L, (c + 1) * L)

    def rotate(x, c):
        return x * cos_ref[chunk(c), :] + pltpu.roll(x, RET_DK // 2, 1) * sin_ref[chunk(c), :]

    for hh in range(HB):
        head = hg * HB + hh
        qs = slice(hh * RET_DK, (hh + 1) * RET_DK)
        vs = slice(hh * RET_DV, (hh + 1) * RET_DV)
        lg_f = _log_sigmoid(jnp.full((1, RET_DV), dec_ref[0, head], F32))
        lg_b = _log_sigmoid(jnp.full((1, RET_DV), dec_ref[1, head], F32))
        lf, lb = lg_f[:, :L], lg_b[:, :L]
        k_dec_f = jnp.exp(lf * (L - 1.0 - ii)) * scale
        k_dec_b = jnp.exp(lb * ii) * scale
        q_dec_f = jnp.exp(lf * (ii + 1.0))
        q_dec_b = jnp.exp(lb * (L - ii))
        mask = (jnp.where(ii >= jj, jnp.exp(lf * jnp.maximum(ii - jj, 0.0)), 0.0)
                + jnp.where(jj >= ii, jnp.exp(lb * jnp.maximum(jj - ii, 0.0)), 0.0)) * scale
        g_f = jnp.exp(lg_f * float(L))
        g_b = jnp.exp(lg_b * float(L))

        for c in range(n):
            kc = k_ref[chunk(c), qs].astype(F32)
            if rope:
                kc = rotate(kc, c)
                kr_ref[chunk(c), qs] = kc.astype(BF16)
            kk = jnp.concatenate([kc * k_dec_f, kc * k_dec_b], axis=1).astype(BF16)
            kv_ref[hh, c] = _dot_tn(kk, v_ref[chunk(c), vs])
        S = s0_ref[0, hh] if state_in else None
        has_f = []
        for c in range(n):
            has_f.append(S is not None)
            if S is not None:
                sf_ref[hh, c] = S.astype(BF16)
            kvc = kv_ref[hh, c, :RET_DK, :]
            S = kvc if S is None else S * g_f + kvc
        if state_out:
            sn_ref[0, hh] = S
        S = s0_ref[1, hh] if state_in else None
        has_b = [False] * n
        for c in reversed(range(n)):
            has_b[c] = S is not None
            if S is not None:
                sb_ref[hh, c] = S.astype(BF16)
            kvc = kv_ref[hh, c, RET_DK:, :]
            S = kvc if S is None else S * g_b + kvc
        if state_out:
            sn_ref[1, hh] = S

        for c in range(n):
            qc = q_ref[chunk(c), qs]
            qf = qc.astype(F32)
            if rope:
                qf = rotate(qf, c)
                qc = qf.astype(BF16)
                kc = kr_ref[chunk(c), qs]
            else:
                kc = k_ref[chunk(c), qs]
            lhs = [(_dot_nt(qc, kc) * mask).astype(BF16)]
            rhs = [v_ref[chunk(c), vs]]
            if has_f[c]:
                lhs.append((qf * q_dec_f).astype(BF16))
                rhs.append(sf_ref[hh, c])
            if has_b[c]:
                lhs.append((qf * q_dec_b).astype(BF16))
                rhs.append(sb_ref[hh, c])
            o = _dot(jnp.concatenate(lhs, axis=1), jnp.concatenate(rhs, axis=0))
            zf = z_ref[chunk(c), vs].astype(F32)
            o_ref[chunk(c), vs] = (_group_norm(o, gn_ref[:, vs]) * _silu(zf)).astype(BF16)


def _retention(proj, dec, gn, T, nb, row_blk0, rope=None, state=None, want_state=False):
    HB = RET_HB_PROMPT if T == SEQ else RET_HB_LATENT
    ng = RET_HEADS // HB
    n = T // CHUNK
    qw, vw = HB * RET_DK, HB * RET_DV
    in_specs = [pl.BlockSpec(memory_space=pltpu.SMEM),
                pl.BlockSpec((T, qw), lambda b, h: (row_blk0 + b, h)),
                pl.BlockSpec((T, qw), lambda b, h: (row_blk0 + b, ng + h)),
                pl.BlockSpec((T, vw), lambda b, h: (row_blk0 + b, ng + h)),
                pl.BlockSpec((T, vw), lambda b, h: (row_blk0 + b, 2 * ng + h)),
                pl.BlockSpec((1, vw), lambda b, h: (0, h))]
    args = [dec, proj, proj, proj, proj, gn.reshape(1, RET_WIDTH)]
    if rope is not None:
        in_specs += [pl.BlockSpec((T, RET_DK), lambda b, h: (0, 0))] * 2
        args += list(rope)
    if state is not None:
        in_specs.append(pl.BlockSpec((None, 2, HB, RET_DK, RET_DV), lambda b, h: (b, 0, h, 0, 0)))
        args.append(state)
    out_specs = [pl.BlockSpec((T, vw), lambda b, h: (b, h))]
    out_shape = [jax.ShapeDtypeStruct((nb * T, RET_WIDTH), BF16)]
    if want_state:
        out_specs.append(pl.BlockSpec((None, 2, HB, RET_DK, RET_DV), lambda b, h: (b, 0, h, 0, 0)))
        out_shape.append(jax.ShapeDtypeStruct((nb, 2, RET_HEADS, RET_DK, RET_DV), F32))
    scratch = [pltpu.VMEM((HB, n, 2 * RET_DK, RET_DV), F32),
               pltpu.VMEM((HB, n, RET_DK, RET_DV), BF16),
               pltpu.VMEM((HB, n, RET_DK, RET_DV), BF16)]
    if rope is not None:
        scratch.append(pltpu.VMEM((T, qw), BF16))
    return pl.pallas_call(
        functools.partial(_ret_kernel, T=T, HB=HB, rope=rope is not None, state_in=state is not None,
                          state_out=want_state),
        grid=(nb, ng),
        in_specs=in_specs, out_specs=out_specs, out_shape=out_shape, scratch_shapes=scratch,
        compiler_params=_params("arbitrary", "arbitrary"),
        name="retention",
    )(*args)


def _ml_pre_kernel(xm_ref, cw_ref, cb_ref, bq_ref, bk_ref, bv_ref, wq_ref, wk_ref, wv_ref, bias_ref,
                   q_ref, k_ref, v_ref, xc_ref, g_ref, bd_ref):
    i = pl.program_id(0)
    j = pl.program_id(1)

    @pl.when(i == 0)
    def _():
        row = lax.broadcasted_iota(jnp.int32, (ML_TILE, ML_TILE), 0)
        col = lax.broadcasted_iota(jnp.int32, (ML_TILE, ML_TILE), 1)
        shift = ML_BLOCK.bit_length() - 1
        same_block = jnp.right_shift(row, shift) == jnp.right_shift(col, shift)
        for which, w_ref in enumerate((bq_ref, bk_ref, bv_ref)):
            w = w_ref[...]
            rep = w[:, ML_BLOCK - 1:ML_BLOCK]
            for dd in range(ML_BLOCK - 1):
                rep = jnp.where((col & (ML_BLOCK - 1)) == dd, w[:, dd:dd + 1], rep)
            bd_ref[which, j] = jnp.where(same_block, rep, 0.0).astype(BF16)

    T = jnp.where(i < N_PROMPT // ML_ROWS, SEQ, DEC_SEQ)
    xb = xm_ref[...]
    x = xb.astype(F32)
    pos = lax.broadcasted_iota(jnp.int32, x.shape, 0) & (T - 1)
    pad = ML_CONV // 2
    conv = cb_ref[...] + x * cw_ref[pad:pad + 1, :]
    for t in range(ML_CONV):
        delta = t - pad
        if delta == 0:
            continue
        shifted = pltpu.roll(x, (-delta) % ML_ROWS, 0)
        ok = (pos + delta >= 0) & (pos + delta < T)
        conv = conv + jnp.where(ok, shifted, 0.0) * cw_ref[t:t + 1, :]
    xc = _silu(conv)
    xcb = xc.astype(BF16)
    qb = _dot(xcb, bd_ref[0, j]).astype(BF16)
    kb = _dot(xcb, bd_ref[1, j]).astype(BF16)
    vb = _dot(xb, bd_ref[2, j]).astype(BF16)
    q_ref[...] = qb
    k_ref[...] = kb
    v_ref[...] = vb
    xc_ref[...] = xcb
    g = (_dot(qb, wq_ref[...].astype(BF16)) + _dot(kb, wk_ref[...].astype(BF16))
         + _dot(vb, wv_ref[...].astype(BF16)))

    @pl.when(j == 0)
    def _():
        g_ref[...] = g + bias_ref[...]

    @pl.when(j > 0)
    def _():
        g_ref[...] += g


def _ml_pre(proj, conv_w, conv_b, wq, wk, wv, wif, bif):
    nj = ML_WIDTH // ML_TILE
    ng = wif.shape[1]
    tok = pl.BlockSpec((ML_ROWS, ML_TILE), lambda i, j: (i, j))
    bd = pl.BlockSpec((ML_TILE, ML_BLOCK), lambda i, j: (j, 0))
    wq, wk, wv = (w.reshape(ML_WIDTH, ML_BLOCK) for w in (wq, wk, wv))
    return pl.pallas_call(
        _ml_pre_kernel,
        grid=(N_TOK // ML_ROWS, nj),
        in_specs=[tok,
                  pl.BlockSpec((ML_CONV, ML_TILE), lambda i, j: (0, j)),
                  pl.BlockSpec((1, ML_TILE), lambda i, j: (0, j)),
                  bd, bd, bd,
                  pl.BlockSpec((ML_TILE, ng), lambda i, j: (j, 0)),
                  pl.BlockSpec((ML_TILE, ng), lambda i, j: (nj + j, 0)),
                  pl.BlockSpec((ML_TILE, ng), lambda i, j: (2 * nj + j, 0)),
                  pl.BlockSpec((1, ng), lambda i, j: (0, 0))],
        out_specs=[tok] * 4 + [pl.BlockSpec((ML_ROWS, ng), lambda i, j: (i, 0))],
        out_shape=[jax.ShapeDtypeStruct((N_TOK, ML_WIDTH), BF16)] * 4 + [jax.ShapeDtypeStruct((N_TOK, ng), F32)],
        scratch_shapes=[pltpu.VMEM((3, nj, ML_TILE, ML_TILE), BF16)],
        compiler_params=_params("arbitrary", "arbitrary"),
        name="mlstm_pre",
    )(proj, conv_w, conv_b.reshape(1, ML_WIDTH), wq, wk, wv, wif, wif, wif, bif.reshape(1, ng))


def _dot_split(lhs_bf16, rhs):
    r1 = rhs.astype(BF16)
    e1 = rhs - r1.astype(F32)
    r2 = e1.astype(BF16)
    r3 = (e1 - r2.astype(F32)).astype(BF16)
    return _dot(lhs_bf16, r1) + _dot(lhs_bf16, r2) + _dot(lhs_bf16, r3)


def _ml_scan_kernel(*refs, T, state_in, state_out):
    refs = list(refs)
    pos = 0
    if state_in:
        m0_ref = refs[0]
        pos = 1
    q_ref, k_ref, v_ref, g_ref, op_ref, z_ref, xc_ref, gn_ref, sk_ref = refs[pos:pos + 9]
    pos += 9
    if state_in:
        c0_ref, n0_ref = refs[pos:pos + 2]
        pos += 2
    o_ref = refs[pos]
    pos += 1
    if state_out:
        cn_ref, nn_ref, mn_ref = refs[pos:pos + 3]
        pos += 3
    acc_ref, s_ref = refs[pos:pos + 2]
    pos += 2
    C_ref = cn_ref if state_out else refs[pos]

    b = pl.program_id(0)
    h = pl.program_id(1)
    n = T // CHUNK
    L = CHUNK
    scale = ML_HD ** -0.5

    def chunk(c):
        return slice(c * L, (c + 1) * L)

    ii = lax.broadcasted_iota(jnp.int32, (L, L), 0)
    jj = lax.broadcasted_iota(jnp.int32, (L, L), 1)
    neg = jnp.full((L, L), -jnp.inf, F32)
    zero = jnp.zeros((L, L), F32)
    tri = [jnp.where(jj <= ii, 1.0, 0.0).astype(BF16), jnp.where(jj >= ii, 1.0, 0.0).astype(BF16)]
    bias = [jnp.where(ii >= jj, zero, neg), jnp.where(ii <= jj, zero, neg)]

    g = g_ref[...]
    col = lax.broadcasted_iota(jnp.int32, g.shape, 1)

    def gate_column(j):
        return jnp.sum(jnp.where(col == j, g, 0.0), axis=1, keepdims=True)

    for c in range(n):
        s_ref[c] = _dot_nt(q_ref[chunk(c), :], k_ref[chunk(c), :])

    G = [[None] * n for _ in range(2)]
    for d in range(2):
        gi_all = gate_column(d * 2 * ML_HEADS + h)
        gf_all = _log_sigmoid(gate_column(d * 2 * ML_HEADS + ML_HEADS + h))
        for c in range(n):
            gi = gi_all[chunk(c)]
            gf = gf_all[chunk(c)]
            cum = _dot_split(tri[d], jnp.broadcast_to(gf, (L, L)))
            d_log = cum - jnp.transpose(cum - gi) + bias[d]
            b_col = cum[:, :1]
            b_last = jnp.sum(gf, axis=0, keepdims=True)
            w_log = b_last - b_col + gi
            G[d][c] = dict(d_log=d_log, m_intra=jnp.max(d_log, axis=1, keepdims=True), b_col=b_col,
                           b_last=b_last, w_log=w_log, w_max=jnp.max(w_log, axis=0, keepdims=True))
        m = jnp.full((1, 1), m0_ref[(b * 2 + d) * ML_HEADS + h], F32) if state_in else jnp.zeros((1, 1), F32)
        for c in (range(n) if d == 0 else reversed(range(n))):
            e = G[d][c]
            m_new = jnp.maximum(e["b_last"] + m, e["w_max"])
            e.update(m_prev=m, m_new=m_new, dec=jnp.exp(e["b_last"] + m - m_new))
            m = m_new
        if state_out:
            mn_ref[d] = jnp.broadcast_to(m, mn_ref.shape[1:])

    nv = [None, None]
    if state_in:
        C_ref[...] = c0_ref[...]
        nv = [n0_ref[0], n0_ref[1]]
    seen = set()
    for t in range(n):
        for d in range(2):
            c = t if d == 0 else n - 1 - t
            last = t == n - 1
            e = G[d][c]
            has_state = state_in or t > 0
            qc = q_ref[chunk(c), :]
            kc = k_ref[chunk(c), :]
            vc = v_ref[chunk(c), :]
            inter = e["b_col"] + e["m_prev"]
            m_i = jnp.maximum(inter, e["m_intra"])
            s = s_ref[c] * (jnp.exp(e["d_log"] - m_i) * scale)
            num = _dot(s.astype(BF16), vc)
            den = jnp.sum(s, axis=1, keepdims=True)
            if has_state:
                w_prev = jnp.exp(inter - m_i)
                qw = qc.astype(F32) * w_prev
                num = num + _dot(qw.astype(BF16), C_ref[d].astype(BF16))
                den = den + jnp.sum(qw * nv[d], axis=1, keepdims=True)
            hb = num / jnp.maximum(jnp.abs(den), jnp.exp(-m_i))
            if c not in seen:
                seen.add(c)
                acc_ref[chunk(c), :] = hb
            else:
                cell = _group_norm((acc_ref[chunk(c), :] + hb) * jax.nn.sigmoid(op_ref[chunk(c), :].astype(F32)),
                                   gn_ref[...])
                mixed = cell + sk_ref[...] * xc_ref[chunk(c), :].astype(F32)
                o_ref[chunk(c), :] = (mixed * _silu(z_ref[chunk(c), :].astype(F32))).astype(BF16)
            if last and not state_out:
                continue
            kw = kc.astype(F32) * (jnp.exp(e["w_log"] - e["m_new"]) * scale)
            kv = _dot_tn(kw.astype(BF16), vc)
            k_sum = jnp.sum(kw, axis=0, keepdims=True)
            if has_state:
                C_ref[d] = e["dec"] * C_ref[d] + kv
                nv[d] = e["dec"] * nv[d] + k_sum
            else:
                C_ref[d] = kv
                nv[d] = k_sum
    if state_out:
        nn_ref[0] = nv[0]
        nn_ref[1] = nv[1]


def _ml_scan(proj, q, k, v, xc, g, gn, skip, T, nb, row_blk0, state=None, want_state=False):
    H = ML_HEADS
    n = T // CHUNK
    tok = lambda b, h: (row_blk0 + b, h)
    head_vec = pl.BlockSpec((1, ML_HD), lambda b, h: (0, h))
    both = lambda *tail: pl.BlockSpec((None, 2, None, *tail), lambda b, h: (b, 0, h, 0, 0))
    in_specs, args = [], []
    if state is not None:
        c0, n0, m0 = state
        in_specs.append(pl.BlockSpec(memory_space=pltpu.SMEM))
        args.append(m0.reshape(nb * 2 * H))
    in_specs += [pl.BlockSpec((T, ML_HD), tok)] * 3
    in_specs += [pl.BlockSpec((T, g.shape[1]), lambda b, h: (row_blk0 + b, 0)),
                 pl.BlockSpec((T, ML_HD), lambda b, h: (row_blk0 + b, H + h)),
                 pl.BlockSpec((T, ML_HD), lambda b, h: (row_blk0 + b, 2 * H + h)),
                 pl.BlockSpec((T, ML_HD), tok), head_vec, head_vec]
    args += [q, k, v, g, proj, proj, xc, gn.reshape(1, ML_WIDTH), skip.reshape(1, ML_WIDTH)]
    if state is not None:
        in_specs += [both(ML_HD, ML_HD), both(1, ML_HD)]
        args += [c0, n0.reshape(nb, 2, H, 1, ML_HD)]
    out_specs = [pl.BlockSpec((T, ML_HD), lambda b, h: (b, h))]
    out_shape = [jax.ShapeDtypeStruct((nb * T, ML_WIDTH), BF16)]
    if want_state:
        out_specs += [both(ML_HD, ML_HD), both(1, ML_HD), both(1, 128)]
        out_shape += [jax.ShapeDtypeStruct((nb, 2, H, ML_HD, ML_HD), F32),
                      jax.ShapeDtypeStruct((nb, 2, H, 1, ML_HD), F32),
                      jax.ShapeDtypeStruct((nb, 2, H, 1, 128), F32)]
    scratch = [pltpu.VMEM((T, ML_HD), F32), pltpu.VMEM((n, CHUNK, CHUNK), F32)]
    if not want_state:
        scratch.append(pltpu.VMEM((2, ML_HD, ML_HD), F32))
    return pl.pallas_call(
        functools.partial(_ml_scan_kernel, T=T, state_in=state is not None, state_out=want_state),
        grid=(nb, H),
        in_specs=in_specs, out_specs=out_specs, out_shape=out_shape, scratch_shapes=scratch,
        compiler_params=_params("arbitrary", "arbitrary"),
        name="mlstm_scan",
    )(*args)


def _att_rope(x, cos, sin):
    lane = lax.broadcasted_iota(jnp.int32, x.shape, 1)
    half = ATT_HD // 2
    rot = jnp.where((lane & (ATT_HD - 1)) < half, pltpu.roll(x, 128 - half, 1), pltpu.roll(x, half, 1))
    return x * cos + rot * sin


def _att_heads(qms, sinks, key_sets):
    scores = [[_dot_nt(k_both, qm) if bias is None else _dot_nt(k_both, qm) + bias
               for k_both, _, bias in key_sets] for qm in qms]
    ms = []
    for sink, per_set in zip(sinks, scores):
        m = sink
        for s in per_set:
            m = jnp.maximum(m, jnp.max(s, axis=0, keepdims=True))
        ms.append(m)
    probs = [[jnp.exp(s - m) for s in per_set] for m, per_set in zip(ms, scores)]
    outs = []
    for sink, m, per_set in zip(sinks, ms, probs):
        den = jnp.exp(sink - m)
        out = None
        for p, (_, v_t, _) in zip(per_set, key_sets):
            den = den + jnp.sum(p, axis=0, keepdims=True)
            o = _dot(v_t, p.astype(BF16))
            out = o if out is None else out + o
        outs.append(out * (1.0 / den))
    return outs


def _att_prepare_kv(k, v, kb_ref, vt_ref, kv, nblk):
    lane = lax.broadcasted_iota(jnp.int32, k.shape, 1)
    native = lane < ATT_HD if kv % 2 == 0 else lane >= ATT_HD
    kb_ref[kv] = jnp.where(native, k, pltpu.roll(k, ATT_HD, 1)).astype(BF16)
    r0 = (kv % 2) * ATT_HD
    blk = k.shape[0] // nblk
    for c in range(nblk):
        vt_ref[kv, c] = jnp.transpose(v[c * blk:(c + 1) * blk, :])[r0:r0 + ATT_HD, :].astype(BF16)


def _att_group(q_ref, z0_ref, z1_ref, o_ref, sink_ref, rows, kv, key_sets):
    lo = lax.broadcasted_iota(jnp.int32, (CHUNK, 128), 1) < ATT_HD
    zero = jnp.zeros((CHUNK, 128), BF16)
    qms, sinks = [], []
    for g in range(ATT_GROUP):
        col = (2 * kv + g // 2) * 128
        q = q_ref[rows, col:col + 128]
        qms.append(jnp.where(lo, q, zero) if g % 2 == 0 else jnp.where(lo, zero, q))
        sinks.append(jnp.full((1, CHUNK), sink_ref[kv * ATT_GROUP + g], F32))
    heads = _att_heads(qms, sinks, key_sets)
    for p in range(2):
        col = (2 * kv + p) * 128
        out = jnp.transpose(jnp.concatenate(heads[2 * p:2 * p + 2], axis=0))
        z_ref = z0_ref if col < ATT_WIDTH // 2 else z1_ref
        zc = col % (ATT_WIDTH // 2)
        zf = z_ref[rows, zc:zc + 128].astype(F32)
        o_ref[rows, col:col + 128] = (out * _silu(zf)).astype(BF16)


def _att_ctx_kernel(sink_ref, q_ref, k_ref, v_ref, z0_ref, z1_ref, o_ref, qs_ref, kb_ref, vt_ref):
    T = q_ref.shape[0]
    qs_ref[...] = q_ref[...] * (ATT_HD ** -0.5)
    for kv in range(ATT_KV):
        slab = (kv // 2) * 128
        _att_prepare_kv(k_ref[:, slab:slab + 128].astype(F32), v_ref[:, slab:slab + 128].astype(F32),
                        kb_ref, vt_ref, kv, 1)
    for c in range(T // CHUNK):
        rows = slice(c * CHUNK, (c + 1) * CHUNK)
        for kv in range(ATT_KV):
            _att_group(qs_ref, z0_ref, z1_ref, o_ref, sink_ref, rows, kv, [(kb_ref[kv], vt_ref[kv, 0], None)])


def _att_ctx(proj, sink):
    T = SEQ
    half = ATT_WIDTH // 2
    return pl.pallas_call(
        _att_ctx_kernel,
        grid=(BATCH,),
        in_specs=[pl.BlockSpec(memory_space=pltpu.SMEM),
                  pl.BlockSpec((T, ATT_WIDTH), lambda b: (b, 0)),
                  pl.BlockSpec((T, ATT_KVW), lambda b: (b, ATT_WIDTH // ATT_KVW)),
                  pl.BlockSpec((T, ATT_KVW), lambda b: (b, ATT_WIDTH // ATT_KVW + 1)),
                  pl.BlockSpec((T, half), lambda b: (b, (ATT_WIDTH + 2 * ATT_KVW) // half)),
                  pl.BlockSpec((T, half), lambda b: (b, (ATT_WIDTH + 2 * ATT_KVW) // half + 1))],
        out_specs=pl.BlockSpec((T, ATT_WIDTH), lambda b: (b, 0)),
        out_shape=jax.ShapeDtypeStruct((N_PROMPT, ATT_WIDTH), BF16),
        scratch_shapes=[pltpu.VMEM((T, ATT_WIDTH), BF16), pltpu.VMEM((ATT_KV, T, 128), BF16),
                        pltpu.VMEM((ATT_KV, 1, ATT_HD, T), BF16)],
        compiler_params=_params("arbitrary"),
        name="att_ctx",
    )(sink, proj, proj, proj, proj, proj)


def _att_win_kernel(sink_ref, q_ref, k_ref, v_ref, z0_ref, z1_ref, kc_ref, vc_ref, cos_ref, sin_ref,
                    o_ref, qs_ref, kb_ref, vt_ref, kcb_ref, vct_ref):
    T = q_ref.shape[0]
    L = CHUNK
    n = T // L
    cos = cos_ref[...]
    sin = sin_ref[...]
    for c0 in range(0, ATT_WIDTH, 128):
        q = _att_rope(q_ref[:, c0:c0 + 128].astype(F32), cos, sin)
        qs_ref[:, c0:c0 + 128] = (q * (ATT_HD ** -0.5)).astype(BF16)
    for kv in range(ATT_KV):
        slab = (kv // 2) * 128
        _att_prepare_kv(_att_rope(k_ref[:, slab:slab + 128].astype(F32), cos, sin),
                        v_ref[:, slab:slab + 128].astype(F32), kb_ref, vt_ref, kv, n)
        _att_prepare_kv(kc_ref[:, slab:slab + 128], vc_ref[:, slab:slab + 128], kcb_ref, vct_ref, kv, 1)

    jj = lax.broadcasted_iota(jnp.int32, (L, L), 0)
    ii = lax.broadcasted_iota(jnp.int32, (L, L), 1)
    neg = jnp.full((L, L), -jnp.inf, F32)
    bias_prev = jnp.where(jj >= ii, 0.0, neg)
    bias_next = jnp.where(jj <= ii, 0.0, neg)

    def body(c, carry):
        rows = pl.ds(pl.multiple_of(c * L, L), L)
        c_prev = jnp.maximum(c - 1, 0)
        c_next = jnp.minimum(c + 1, n - 1)
        b_prev = jnp.where(c > 0, bias_prev, neg)
        b_next = jnp.where(c < n - 1, bias_next, neg)
        for kv in range(ATT_KV):
            def keys(cb):
                return kb_ref[kv, pl.ds(pl.multiple_of(cb * L, L), L), :]
            key_sets = [(keys(c_prev), vt_ref[kv, c_prev], b_prev),
                        (keys(c), vt_ref[kv, c], None),
                        (keys(c_next), vt_ref[kv, c_next], b_next),
                        (kcb_ref[kv], vct_ref[kv, 0], None)]
            _att_group(qs_ref, z0_ref, z1_ref, o_ref, sink_ref, rows, kv, key_sets)
        return carry

    lax.fori_loop(0, n, body, 0)


def _att_win(proj, sink, k_ctx, v_ctx, cos, sin):
    T = DEC_SEQ
    half = ATT_WIDTH // 2
    rb = N_PROMPT // T
    return pl.pallas_call(
        _att_win_kernel,
        grid=(DEC_BATCH,),
        in_specs=[pl.BlockSpec(memory_space=pltpu.SMEM),
                  pl.BlockSpec((T, ATT_WIDTH), lambda b: (rb + b, 0)),
                  pl.BlockSpec((T, ATT_KVW), lambda b: (rb + b, ATT_WIDTH // ATT_KVW)),
                  pl.BlockSpec((T, ATT_KVW), lambda b: (rb + b, ATT_WIDTH // ATT_KVW + 1)),
                  pl.BlockSpec((T, half), lambda b: (rb + b, (ATT_WIDTH + 2 * ATT_KVW) // half)),
                  pl.BlockSpec((T, half), lambda b: (rb + b, (ATT_WIDTH + 2 * ATT_KVW) // half + 1)),
                  pl.BlockSpec((None, PAST_LEN, ATT_KVW), lambda b: (b, 0, 0)),
                  pl.BlockSpec((None, PAST_LEN, ATT_KVW), lambda b: (b, 0, 0)),
                  pl.BlockSpec((T, 128), lambda b: (0, 0)),
                  pl.BlockSpec((T, 128), lambda b: (0, 0))],
        out_specs=pl.BlockSpec((T, ATT_WIDTH), lambda b: (b, 0)),
        out_shape=jax.ShapeDtypeStruct((N_LATENT, ATT_WIDTH), BF16),
        scratch_shapes=[pltpu.VMEM((T, ATT_WIDTH), BF16),
                        pltpu.VMEM((ATT_KV, T, 128), BF16),
                        pltpu.VMEM((ATT_KV, T // CHUNK, ATT_HD, CHUNK), BF16),
                        pltpu.VMEM((ATT_KV, PAST_LEN, 128), BF16),
                        pltpu.VMEM((ATT_KV, 1, ATT_HD, PAST_LEN), BF16)],
        compiler_params=_params("arbitrary"),
        name="att_win",
    )(sink, proj, proj, proj, proj, proj, k_ctx, v_ctx, cos, sin)


def _rope_tables(T, hd, reps):
    rows = T // GRID_W
    row = jnp.repeat(jnp.arange(rows, dtype=F32), GRID_W)
    col = jnp.tile(jnp.arange(GRID_W, dtype=F32), rows)
    nf = hd // 4
    inv = ROPE_BASE ** (-jnp.arange(nf, dtype=F32) / nf)
    ang = jnp.concatenate([row[:, None] * inv[None, :], col[:, None] * inv[None, :]], axis=-1)
    cos, sin = jnp.cos(ang), jnp.sin(ang)
    return (jnp.tile(jnp.concatenate([cos, cos], axis=-1), (1, reps)),
            jnp.tile(jnp.concatenate([-sin, sin], axis=-1), (1, reps)))


def _retention_layer(h, w_in, decay_f, decay_b, gn, state, rope):
    proj = _in_proj(h, w_in, 1024)
    dec = jnp.stack([decay_f, decay_b]).astype(F32)
    a_p, new_state = _retention(proj, dec, gn, SEQ, BATCH, 0, want_state=True)
    (a_s,) = _retention(proj, dec, gn, DEC_SEQ, DEC_BATCH, N_PROMPT // DEC_SEQ, rope=rope, state=state)
    return a_p, a_s, new_state


def kernel(x_prompt, x_sample, c, c_ctx, state_l0_ret, state_l1_C, state_l1_n, state_l1_m, cache_l2_k, cache_l2_v, state_l3_ret, norm_l0, ada_w_l0, ada_b_l0, w_in_l0, w_out_l0, ret_decay_f_l0, ret_decay_b_l0, ret_gn_l0, norm_l1, ada_w_l1, ada_b_l1, w_in_l1, w_out_l1, conv_w_l1, conv_b_l1, wq_l1, wk_l1, wv_l1, wif_f_l1, bif_f_l1, wif_b_l1, bif_b_l1, gn_l1, skip_l1, norm_l2, ada_w_l2, ada_b_l2, w_in_l2, w_out_l2, sink_l2, norm_l3, ada_w_l3, ada_b_l3, w_in_l3, w_out_l3, ret_decay_f_l3, ret_decay_b_l3, ret_gn_l3, final_norm):
    ct = jnp.concatenate([c_ctx[:, None], c.T, jnp.zeros((D, MOD_ROWS - 1 - DEC_BATCH), F32)], axis=1)
    mods = [_adaln(ct, w, b) for w, b in ((ada_w_l0, ada_b_l0), (ada_w_l1, ada_b_l1),
                                          (ada_w_l2, ada_b_l2), (ada_w_l3, ada_b_l3))]
    rope_ret = _rope_tables(DEC_SEQ, RET_DK, 1)
    rope_att = _rope_tables(DEC_SEQ, ATT_HD, 2)

    x, h = _norm_mod(x_prompt.reshape(N_PROMPT, D), x_sample.reshape(N_LATENT, D), norm_l0, mods[0])

    a_p, a_s, new_l0_ret = _retention_layer(h, w_in_l0, ret_decay_f_l0, ret_decay_b_l0, ret_gn_l0, state_l0_ret,
                                            rope_ret)
    x, h = _out_proj(a_p, a_s, w_out_l0, x, mods[0], norm_l1, mods[1], False)

    proj = _in_proj(h, w_in_l1, 1024)
    wif = jnp.concatenate([wif_f_l1, wif_b_l1], axis=1)
    bif = jnp.concatenate([bif_f_l1, bif_b_l1])
    q, k, v, xc, g = _ml_pre(proj, conv_w_l1, conv_b_l1, wq_l1, wk_l1, wv_l1, wif, bif)
    a_p, new_l1_C, n_new, m_new = _ml_scan(proj, q, k, v, xc, g, gn_l1, skip_l1, SEQ, BATCH, 0, want_state=True)
    (a_s,) = _ml_scan(proj, q, k, v, xc, g, gn_l1, skip_l1, DEC_SEQ, DEC_BATCH, N_PROMPT // DEC_SEQ,
                      state=(state_l1_C, state_l1_n, state_l1_m))
    new_l1_n = n_new.reshape(BATCH, 2, ML_HEADS, ML_HD)
    new_l1_m = m_new[:, :, :, 0, 0]
    x, h = _out_proj(a_p, a_s, w_out_l1, x, mods[1], norm_l2, mods[2], False)

    proj = _in_proj(h, w_in_l2, 1280)
    kv_new = proj[:N_PROMPT, ATT_WIDTH:ATT_WIDTH + 2 * ATT_KVW].astype(F32)
    new_l2_k = kv_new[:, :ATT_KVW].reshape(BATCH, SEQ, ATT_KV, ATT_HD)
    new_l2_v = kv_new[:, ATT_KVW:].reshape(BATCH, SEQ, ATT_KV, ATT_HD)
    a_p = _att_ctx(proj, sink_l2)
    a_s = _att_win(proj, sink_l2, cache_l2_k.reshape(DEC_BATCH, PAST_LEN, ATT_KVW),
                   cache_l2_v.reshape(DEC_BATCH, PAST_LEN, ATT_KVW), rope_att[0], rope_att[1])
    x, h = _out_proj(a_p, a_s, w_out_l2, x, mods[2], norm_l3, mods[3], False)

    a_p, a_s, new_l3_ret = _retention_layer(h, w_in_l3, ret_decay_f_l3, ret_decay_b_l3, ret_gn_l3, state_l3_ret,
                                            rope_ret)
    (y_p,) = _out_proj(a_p, a_s, w_out_l3, x, mods[3], final_norm, mods[3], True, rows=(0, N_PROMPT))
    (y_s,) = _out_proj(a_p, a_s, w_out_l3, x, mods[3], final_norm, mods[3], True, rows=(N_PROMPT, N_LATENT))

    y_prompt = y_p.reshape(BATCH, SEQ, D)
    y_sample = y_s.reshape(DEC_BATCH, DEC_SEQ, D)
    return (y_prompt, y_sample, new_l0_ret, new_l1_C, new_l1_n, new_l1_m, new_l2_k, new_l2_v, new_l3_ret)
```

```python
import functools

import jax
import jax.numpy as jnp
import numpy as np
from jax import lax
from jax.experimental import pallas as pl
from jax.experimental.pallas import tpu as pltpu

F32 = jnp.float32
BF16 = jnp.bfloat16

D = 1024
BATCH = 16
SEQ = 256
DEC_BATCH = 2
DEC_SEQ = 1024
PAST_LEN = 512
GRID_W = 64
CHUNK = 128
EPS = 1e-6
ROPE_BASE = 10000.0

N_PROMPT = BATCH * SEQ
N_LATENT = DEC_BATCH * DEC_SEQ
N_TOK = N_PROMPT + N_LATENT
MOD_ROWS = 8

RET_HEADS = 8
RET_DK = 128
RET_DV = 256
RET_QK = RET_HEADS * RET_DK
RET_WIDTH = RET_HEADS * RET_DV
RET_HB_PROMPT = 4
RET_HB_LATENT = 1

ML_HEADS = 4
ML_WIDTH = 2 * D
ML_HD = ML_WIDTH // ML_HEADS
ML_BLOCK = 4
ML_CONV = 5
ML_TILE = 256
ML_ROWS = 1024

ATT_HEADS = 16
ATT_KV = 4
ATT_HD = 64
ATT_GROUP = ATT_HEADS // ATT_KV
ATT_WIDTH = ATT_HEADS * ATT_HD
ATT_KVW = ATT_KV * ATT_HD
WINDOW = 128

OUT_SUB = 256
IN_SUB = 4

VMEM_LIMIT = 56 * 1024 * 1024


def _params(*sem):
    return pltpu.CompilerParams(dimension_semantics=sem, vmem_limit_bytes=VMEM_LIMIT)


def _mod_row(row0):
    return jnp.maximum((row0 - N_PROMPT) // DEC_SEQ + 1, 0)


def _dot(a, b):
    return jnp.dot(a, b, preferred_element_type=F32)


def _dot_nt(a, b):
    return lax.dot_general(a, b, (((1,), (1,)), ((), ())), preferred_element_type=F32)


def _dot_tn(a, b):
    return lax.dot_general(a, b, (((0,), (0,)), ((), ())), preferred_element_type=F32)


def _silu(x):
    return x * jax.nn.sigmoid(x)


def _log_sigmoid(x):
    return jnp.minimum(x, 0.0) - jnp.log1p(jnp.exp(-jnp.abs(x)))


def _rms(x, g):
    return x * lax.rsqrt(jnp.mean(x * x, axis=-1, keepdims=True) + EPS) * g


def _group_norm(x, g):
    xc = x - jnp.mean(x, axis=-1, keepdims=True)
    return xc * lax.rsqrt(jnp.mean(xc * xc, axis=-1, keepdims=True) + EPS) * g


def _two_source_specs(tm, width, tile0=0):
    n_p = N_PROMPT // tm
    return (pl.BlockSpec((tm, width), lambda i: (jnp.minimum(tile0 + i, n_p - 1), 0)),
            pl.BlockSpec((tm, width), lambda i: (jnp.maximum(tile0 + i - n_p, 0), 0)))


def _pick(i, tm, p_ref, s_ref):
    return jnp.where(i < N_PROMPT // tm, p_ref[...], s_ref[...])


def _ada_kernel(ct_ref, w_ref, b_ref, o_ref):
    s = _silu(ct_ref[...])
    w = w_ref[...]
    rows = [jnp.sum(w * s[:, r:r + 1], axis=0, keepdims=True) for r in range(1 + DEC_BATCH)]
    rows.append(jnp.zeros((MOD_ROWS - len(rows), w.shape[1]), F32))
    o_ref[...] = jnp.concatenate(rows, axis=0) + b_ref[...]


def _adaln(ct, w, b):
    tn = 1024
    return pl.pallas_call(
        _ada_kernel,
        grid=(3 * D // tn,),
        in_specs=[pl.BlockSpec((D, MOD_ROWS), lambda j: (0, 0)),
                  pl.BlockSpec((D, tn), lambda j: (0, j)),
                  pl.BlockSpec((1, tn), lambda j: (0, j))],
        out_specs=pl.BlockSpec((MOD_ROWS, tn), lambda j: (0, j)),
        out_shape=jax.ShapeDtypeStruct((MOD_ROWS, 3 * D), F32),
        compiler_params=_params("arbitrary"),
        name="adaln",
    )(ct, w, b.reshape(1, 3 * D))


def _norm_mod_kernel(xp_ref, xs_ref, g_ref, mod_ref, x_ref, h_ref, *, tm):
    i = pl.program_id(0)
    m = mod_ref[pl.ds(_mod_row(i * tm), 1), :]
    x = _pick(i, tm, xp_ref, xs_ref)
    x_ref[...] = x
    y = _rms(x, g_ref[...])
    h_ref[...] = (y * (1.0 + m[:, D:2 * D]) + m[:, :D]).astype(BF16)


def _norm_mod(xp, xs, g, mod):
    tm = 512
    return pl.pallas_call(
        functools.partial(_norm_mod_kernel, tm=tm),
        grid=(N_TOK // tm,),
        in_specs=[*_two_source_specs(tm, D),
                  pl.BlockSpec((1, D), lambda i: (0, 0)),
                  pl.BlockSpec((MOD_ROWS, 3 * D), lambda i: (0, 0))],
        out_specs=[pl.BlockSpec((tm, D), lambda i: (i, 0))] * 2,
        out_shape=[jax.ShapeDtypeStruct((N_TOK, D), F32), jax.ShapeDtypeStruct((N_TOK, D), BF16)],
        compiler_params=_params("arbitrary"),
        name="norm_mod",
    )(xp, xs, g.reshape(1, D), mod)


def _in_proj_kernel(h_ref, w_ref, o_ref, wb_ref, *, tn, n, acts):
    j = pl.program_id(0)

    @pl.when(pl.program_id(1) == 0)
    def _():
        wb_ref[...] = w_ref[...].astype(BF16)

    sig0, silu0 = acts
    gated = (j + 1) * tn > min(sig0, silu0)

    @pl.when(jnp.logical_not(gated))
    def _():
        o_ref[...] = _dot(h_ref[...], wb_ref[...]).astype(BF16)

    @pl.when(gated)
    def _():
        sub = h_ref.shape[0] // IN_SUB
        for s in range(IN_SUB):
            rows = slice(s * sub, (s + 1) * sub)
            acc = _dot(h_ref[rows, :], wb_ref[...])
            col = j * tn + lax.broadcasted_iota(jnp.int32, acc.shape, 1)
            sig = jax.nn.sigmoid(acc)
            out = jnp.where(col >= silu0, acc * sig, jnp.where(col >= sig0, sig, acc))
            o_ref[rows, :] = out.astype(BF16)


def _in_proj(h, w, tn, acts):
    tm = 1024
    n = w.shape[1]
    return pl.pallas_call(
        functools.partial(_in_proj_kernel, tn=tn, n=n, acts=acts),
        grid=(n // tn, N_TOK // tm),
        in_specs=[pl.BlockSpec((tm, D), lambda j, i: (i, 0)),
                  pl.BlockSpec((D, tn), lambda j, i: (0, j))],
        out_specs=pl.BlockSpec((tm, tn), lambda j, i: (i, j)),
        out_shape=jax.ShapeDtypeStruct((N_TOK, n), BF16),
        scratch_shapes=[pltpu.VMEM((D, tn), BF16)],
        compiler_params=_params("arbitrary", "arbitrary"),
        name="in_proj",
    )(h, w)


def _out_proj_kernel(ap_ref, as_ref, w_ref, x_ref, mod_ref, g_ref, modn_ref, *rest, tm, tile0, final):
    wb_ref = rest[-1]

    @pl.when(pl.program_id(0) == 0)
    def _():
        wb_ref[...] = w_ref[...].astype(BF16)

    i = tile0 + pl.program_id(0)
    r = _mod_row(i * tm)
    gate = mod_ref[pl.ds(r, 1), :][:, 2 * D:]
    from_prompt = i < N_PROMPT // tm
    for s in range(tm // OUT_SUB):
        rows = slice(s * OUT_SUB, (s + 1) * OUT_SUB)
        a = jnp.where(from_prompt, ap_ref[rows, :], as_ref[rows, :])
        xn = x_ref[rows, :] + gate * _dot(a, wb_ref[...])
        y = _rms(xn, g_ref[...])
        if final:
            rest[0][rows, :] = y
        else:
            mn = modn_ref[pl.ds(r, 1), :]
            rest[0][rows, :] = xn
            rest[1][rows, :] = (y * (1.0 + mn[:, D:2 * D]) + mn[:, :D]).astype(BF16)


def _out_proj(a_p, a_s, w, x, mod, g_next, mod_next, final, rows=(0, N_TOK)):
    tm = 512
    kw = w.shape[0]
    tile0, nt = rows[0] // tm, rows[1] // tm
    row = pl.BlockSpec((tm, D), lambda i: (i, 0))
    if final:
        out_specs, out_shape = [row], [jax.ShapeDtypeStruct((rows[1], D), F32)]
    else:
        out_specs = [row, row]
        out_shape = [jax.ShapeDtypeStruct((rows[1], D), F32), jax.ShapeDtypeStruct((rows[1], D), BF16)]
    return pl.pallas_call(
        functools.partial(_out_proj_kernel, tm=tm, tile0=tile0, final=final),
        grid=(nt,),
        in_specs=[*_two_source_specs(tm, kw, tile0),
                  pl.BlockSpec((kw, D), lambda i: (0, 0)),
                  pl.BlockSpec((tm, D), lambda i: (tile0 + i, 0)),
                  pl.BlockSpec((MOD_ROWS, 3 * D), lambda i: (0, 0)),
                  pl.BlockSpec((1, D), lambda i: (0, 0)),
                  pl.BlockSpec((MOD_ROWS, 3 * D), lambda i: (0, 0))],
        out_specs=out_specs,
        out_shape=out_shape,
        scratch_shapes=[pltpu.VMEM((kw, D), BF16)],
        compiler_params=_params("arbitrary"),
        name="out_proj",
    )(a_p, a_s, w, x, mod, g_next.reshape(1, D), mod_next)


def _ret_kernel(*refs, T, HB, rope, state_in, state_out):
    refs = list(refs)
    dec_ref, q_ref, k_ref, v_ref, z_ref, gn_ref = refs[:6]
    pos = 6
    if rope:
        cos_ref, sin_ref = refs[pos:pos + 2]
        pos += 2
    if state_in:
        s0_ref = refs[pos]
        pos += 1
    o_ref = refs[pos]
    pos += 1
    if state_out:
        sn_ref = refs[pos]
        pos += 1
    kv_ref, sf_ref, sb_ref = refs[pos:pos + 3]
    pos += 3
    if rope:
        kr_ref = refs[pos]

    hg = pl.program_id(1)
    n = T // CHUNK
    L = CHUNK
    ii = lax.broadcasted_iota(jnp.int32, (L, L), 0).astype(F32)
    jj = lax.broadcasted_iota(jnp.int32, (L, L), 1).astype(F32)
    scale = RET_DK ** -0.5

    def chunk(c):
        return slice(c * L, (c + 1) * L)

    def rotate(x, c):
        return x * cos_ref[chunk(c), :] + pltpu.roll(x, RET_DK // 2, 1) * sin_ref[chunk(c), :]

    for hh in range(HB):
        head = hg * HB + hh
        qs = slice(hh * RET_DK, (hh + 1) * RET_DK)
        vs = slice(hh * RET_DV, (hh + 1) * RET_DV)
        lg_f = _log_sigmoid(jnp.full((1, RET_DV), dec_ref[0, head], F32))
        lg_b = _log_sigmoid(jnp.full((1, RET_DV), dec_ref[1, head], F32))
        lf, lb = lg_f[:, :L], lg_b[:, :L]
        k_dec_f = jnp.exp(lf * (L - 1.0 - ii)) * scale
        k_dec_b = jnp.exp(lb * ii) * scale
        q_dec_f = jnp.exp(lf * (ii + 1.0))
        q_dec_b = jnp.exp(lb * (L - ii))
        mask = (jnp.where(ii >= jj, jnp.exp(lf * jnp.maximum(ii - jj, 0.0)), 0.0)
                + jnp.where(jj >= ii, jnp.exp(lb * jnp.maximum(jj - ii, 0.0)), 0.0)) * scale
        g_f = jnp.exp(lg_f * float(L))
        g_b = jnp.exp(lg_b * float(L))

        for c in range(n):
            kc = k_ref[chunk(c), qs].astype(F32)
            if rope:
                kc = rotate(kc, c)
                kr_ref[chunk(c), qs] = kc.astype(BF16)
            kk = jnp.concatenate([kc * k_dec_f, kc * k_dec_b], axis=1).astype(BF16)
            kv_ref[hh, c] = _dot_tn(kk, v_ref[chunk(c), vs])
        S = s0_ref[0, hh] if state_in else None
        has_f = []
        for c in range(n):
            has_f.append(S is not None)
            if S is not None:
                sf_ref[hh, c] = S.astype(BF16)
            kvc = kv_ref[hh, c, :RET_DK, :]
            S = kvc if S is None else S * g_f + kvc
        if state_out:
            sn_ref[0, hh] = S
        S = s0_ref[1, hh] if state_in else None
        has_b = [False] * n
        for c in reversed(range(n)):
            has_b[c] = S is not None
            if S is not None:
                sb_ref[hh, c] = S.astype(BF16)
            kvc = kv_ref[hh, c, RET_DK:, :]
            S = kvc if S is None else S * g_b + kvc
        if state_out:
            sn_ref[1, hh] = S

        for c in range(n):
            qc = q_ref[chunk(c), qs]
            qf = qc.astype(F32)
            if rope:
                qf = rotate(qf, c)
                qc = qf.astype(BF16)
                kc = kr_ref[chunk(c), qs]
            else:
                kc = k_ref[chunk(c), qs]
            lhs = [(_dot_nt(qc, kc) * mask).astype(BF16)]
            rhs = [v_ref[chunk(c), vs]]
            if has_f[c]:
                lhs.append((qf * q_dec_f).astype(BF16))
                rhs.append(sf_ref[hh, c])
            if has_b[c]:
                lhs.append((qf * q_dec_b).astype(BF16))
                rhs.append(sb_ref[hh, c])
            o = _dot(jnp.concatenate(lhs, axis=1), jnp.concatenate(rhs, axis=0))
            zf = z_ref[chunk(c), vs].astype(F32)
            o_ref[chunk(c), vs] = (_group_norm(o, gn_ref[:, vs]) * zf).astype(BF16)


def _retention(proj, dec, gn, T, nb, row_blk0, rope=None, state=None, want_state=False):
    HB = RET_HB_PROMPT if T == SEQ else RET_HB_LATENT
    ng = RET_HEADS // HB
    n = T // CHUNK
    qw, vw = HB * RET_DK, HB * RET_DV
    in_specs = [pl.BlockSpec(memory_space=pltpu.SMEM),
                pl.BlockSpec((T, qw), lambda b, h: (row_blk0 + b, h)),
                pl.BlockSpec((T, qw), lambda b, h: (row_blk0 + b, ng + h)),
                pl.BlockSpec((T, vw), lambda b, h: (row_blk0 + b, ng + h)),
                pl.BlockSpec((T, vw), lambda b, h: (row_blk0 + b, 2 * ng + h)),
                pl.BlockSpec((1, vw), lambda b, h: (0, h))]
    args = [dec, proj, proj, proj, proj, gn.reshape(1, RET_WIDTH)]
    if rope is not None:
        in_specs += [pl.BlockSpec((T, RET_DK), lambda b, h: (0, 0))] * 2
        args += list(rope)
    if state is not None:
        in_specs.append(pl.BlockSpec((None, 2, HB, RET_DK, RET_DV), lambda b, h: (b, 0, h, 0, 0)))
        args.append(state)
    out_specs = [pl.BlockSpec((T, vw), lambda b, h: (b, h))]
    out_shape = [jax.ShapeDtypeStruct((nb * T, RET_WIDTH), BF16)]
    if want_state:
        out_specs.append(pl.BlockSpec((None, 2, HB, RET_DK, RET_DV), lambda b, h: (b, 0, h, 0, 0)))
        out_shape.append(jax.ShapeDtypeStruct((nb, 2, RET_HEADS, RET_DK, RET_DV), F32))
    scratch = [pltpu.VMEM((HB, n, 2 * RET_DK, RET_DV), F32),
               pltpu.VMEM((HB, n, RET_DK, RET_DV), BF16),
               pltpu.VMEM((HB, n, RET_DK, RET_DV), BF16)]
    if rope is not None:
        scratch.append(pltpu.VMEM((T, qw), BF16))
    return pl.pallas_call(
        functools.partial(_ret_kernel, T=T, HB=HB, rope=rope is not None, state_in=state is not None,
                          state_out=want_state),
        grid=(nb, ng),
        in_specs=in_specs, out_specs=out_specs, out_shape=out_shape, scratch_shapes=scratch,
        compiler_params=_params("arbitrary", "arbitrary"),
        name="retention",
    )(*args)


def _ml_pre_kernel(xm_ref, cw_ref, cb_ref, bq_ref, bk_ref, bv_ref, wq_ref, wk_ref, wv_ref, bias_ref,
                   q_ref, k_ref, v_ref, xc_ref, g_ref, bd_ref):
    i = pl.program_id(0)
    j = pl.program_id(1)

    @pl.when(i == 0)
    def _():
        row = lax.broadcasted_iota(jnp.int32, (ML_TILE, ML_TILE), 0)
        col = lax.broadcasted_iota(jnp.int32, (ML_TILE, ML_TILE), 1)
        shift = ML_BLOCK.bit_length() - 1
        same_block = jnp.right_shift(row, shift) == jnp.right_shift(col, shift)
        for which, w_ref in enumerate((bq_ref, bk_ref, bv_ref)):
            w = w_ref[...]
            rep = w[:, ML_BLOCK - 1:ML_BLOCK]
            for dd in range(ML_BLOCK - 1):
                rep = jnp.where((col & (ML_BLOCK - 1)) == dd, w[:, dd:dd + 1], rep)
            bd_ref[which, j] = jnp.where(same_block, rep, 0.0).astype(BF16)

    T = jnp.where(i < N_PROMPT // ML_ROWS, SEQ, DEC_SEQ)
    xb = xm_ref[...]
    x = xb.astype(F32)
    pos = lax.broadcasted_iota(jnp.int32, x.shape, 0) & (T - 1)
    pad = ML_CONV // 2
    conv = cb_ref[...] + x * cw_ref[pad:pad + 1, :]
    for t in range(ML_CONV):
        delta = t - pad
        if delta == 0:
            continue
        shifted = pltpu.roll(x, (-delta) % ML_ROWS, 0)
        ok = (pos + delta >= 0) & (pos + delta < T)
        conv = conv + jnp.where(ok, shifted, 0.0) * cw_ref[t:t + 1, :]
    xc = _silu(conv)
    xcb = xc.astype(BF16)
    qb = _dot(xcb, bd_ref[0, j]).astype(BF16)
    kb = _dot(xcb, bd_ref[1, j]).astype(BF16)
    vb = _dot(xb, bd_ref[2, j]).astype(BF16)
    q_ref[...] = qb
    k_ref[...] = kb
    v_ref[...] = vb
    xc_ref[...] = xcb
    g = (_dot(qb, wq_ref[...].astype(BF16)) + _dot(kb, wk_ref[...].astype(BF16))
         + _dot(vb, wv_ref[...].astype(BF16)))

    @pl.when(j == 0)
    def _():
        g_ref[...] = g + bias_ref[...]

    @pl.when(j > 0)
    def _():
        g_ref[...] += g


def _ml_pre(proj, conv_w, conv_b, wq, wk, wv, wif, bif):
    nj = ML_WIDTH // ML_TILE
    ng = wif.shape[1]
    tok = pl.BlockSpec((ML_ROWS, ML_TILE), lambda i, j: (i, j))
    bd = pl.BlockSpec((ML_TILE, ML_BLOCK), lambda i, j: (j, 0))
    wq, wk, wv = (w.reshape(ML_WIDTH, ML_BLOCK) for w in (wq, wk, wv))
    return pl.pallas_call(
        _ml_pre_kernel,
        grid=(N_TOK // ML_ROWS, nj),
        in_specs=[tok,
                  pl.BlockSpec((ML_CONV, ML_TILE), lambda i, j: (0, j)),
                  pl.BlockSpec((1, ML_TILE), lambda i, j: (0, j)),
                  bd, bd, bd,
                  pl.BlockSpec((ML_TILE, ng), lambda i, j: (j, 0)),
                  pl.BlockSpec((ML_TILE, ng), lambda i, j: (nj + j, 0)),
                  pl.BlockSpec((ML_TILE, ng), lambda i, j: (2 * nj + j, 0)),
                  pl.BlockSpec((1, ng), lambda i, j: (0, 0))],
        out_specs=[tok] * 4 + [pl.BlockSpec((ML_ROWS, ng), lambda i, j: (i, 0))],
        out_shape=[jax.ShapeDtypeStruct((N_TOK, ML_WIDTH), BF16)] * 4 + [jax.ShapeDtypeStruct((N_TOK, ng), F32)],
        scratch_shapes=[pltpu.VMEM((3, nj, ML_TILE, ML_TILE), BF16)],
        compiler_params=_params("arbitrary", "arbitrary"),
        name="mlstm_pre",
    )(proj, conv_w, conv_b.reshape(1, ML_WIDTH), wq, wk, wv, wif, wif, wif, bif.reshape(1, ng))


def _dot_split(lhs_bf16, rhs):
    r1 = rhs.astype(BF16)
    e1 = rhs - r1.astype(F32)
    r2 = e1.astype(BF16)
    r3 = (e1 - r2.astype(F32)).astype(BF16)
    return _dot(lhs_bf16, r1) + _dot(lhs_bf16, r2) + _dot(lhs_bf16, r3)


def _ml_scan_kernel(*refs, T, state_in, state_out):
    refs = list(refs)
    pos = 0
    if state_in:
        m0_ref = refs[0]
        pos = 1
    q_ref, k_ref, v_ref, g_ref, op_ref, z_ref, xc_ref, gn_ref, sk_ref = refs[pos:pos + 9]
    pos += 9
    if state_in:
        c0_ref, n0_ref = refs[pos:pos + 2]
        pos += 2
    o_ref = refs[pos]
    pos += 1
    if state_out:
        cn_ref, nn_ref, mn_ref = refs[pos:pos + 3]
        pos += 3
    acc_ref, s_ref = refs[pos:pos + 2]

    b = pl.program_id(0)
    h = pl.program_id(1)
    n = T // CHUNK
    L = CHUNK
    scale = ML_HD ** -0.5

    def chunk(c):
        return slice(c * L, (c + 1) * L)

    ii = lax.broadcasted_iota(jnp.int32, (L, L), 0)
    jj = lax.broadcasted_iota(jnp.int32, (L, L), 1)
    neg = jnp.full((L, L), -jnp.inf, F32)
    zero = jnp.zeros((L, L), F32)
    tri = [jnp.where(jj <= ii, 1.0, 0.0).astype(BF16), jnp.where(jj >= ii, 1.0, 0.0).astype(BF16)]
    bias = [jnp.where(ii >= jj, zero, neg), jnp.where(ii <= jj, zero, neg)]

    g = g_ref[...]
    col = lax.broadcasted_iota(jnp.int32, g.shape, 1)

    def gate_column(j):
        return jnp.sum(jnp.where(col == j, g, 0.0), axis=1, keepdims=True)


    for i in range(n):
        for j in range(n):
            s_ref[i, j] = _dot_nt(q_ref[chunk(i), :], k_ref[chunk(j), :])

    G = []
    for d in range(2):
        order = list(range(n)) if d == 0 else list(reversed(range(n)))
        gi_all = gate_column(d * 2 * ML_HEADS + h)
        gf_all = _log_sigmoid(gate_column(d * 2 * ML_HEADS + ML_HEADS + h))
        m0 = jnp.full((1, 1), m0_ref[(b * 2 + d) * ML_HEADS + h], F32) if state_in else jnp.zeros((1, 1), F32)
        c_col, c_row, b_col, m_row = [None] * n, [None] * n, [None] * n, [None] * n
        offset = jnp.zeros((1, 1), F32)
        carry = m0
        for c in order:
            gi = gi_all[chunk(c)]
            gf = gf_all[chunk(c)]
            cum = _dot_split(tri[d], jnp.broadcast_to(gf, (L, L))) + offset
            offset = offset + jnp.sum(gf, axis=0, keepdims=True)
            cc = gi - cum
            c_col[c] = cc[:, :1]
            c_row[c] = jnp.transpose(cc)
            b_col[c] = cum[:, :1]
            m_row[c] = jnp.maximum(jnp.max(c_row[c] + bias[d], axis=1, keepdims=True), carry)
            carry = jnp.maximum(carry, jnp.max(c_col[c], axis=0, keepdims=True))
        G.append(dict(order=order, c_col=c_col, c_row=c_row, b_col=b_col, m_row=m_row, m0=m0,
                      m_end=carry, b_end=offset))

    done = set()
    for i in range(n):
        for d in range(2):
            e = G[d]
            js = list(range(0, i + 1)) if d == 0 else list(range(i, n))
            m_i = e["m_row"][i]
            parts = []
            tot = None
            for j in js:
                z = e["c_row"][j] - m_i
                if j == i:
                    z = z + bias[d]
                sw = s_ref[i, j] * (jnp.exp(z) * scale)
                tot = sw if tot is None else tot + sw
                parts.append(sw.astype(BF16))
            num = _dot(jnp.concatenate(parts, axis=1), v_ref[js[0] * L:(js[-1] + 1) * L, :])
            den = jnp.sum(tot, axis=1, keepdims=True)
            if state_in:
                qc = q_ref[chunk(i), :]
                w0 = jnp.exp(e["m0"] - m_i)
                num = num + _dot((qc.astype(F32) * w0).astype(BF16), c0_ref[d].astype(BF16))
                n0 = jnp.broadcast_to(n0_ref[d], (8, ML_HD)).astype(BF16)
                den = den + w0 * _dot_nt(qc, n0)[:, :1]
            hb = num / jnp.maximum(jnp.abs(den), jnp.exp(-(e["b_col"][i] + m_i)))
            if i not in done:
                done.add(i)
                acc_ref[chunk(i), :] = hb
            else:
                cell = _group_norm((acc_ref[chunk(i), :] + hb) * op_ref[chunk(i), :].astype(F32), gn_ref[...])
                mixed = cell + sk_ref[...] * xc_ref[chunk(i), :].astype(F32)
                o_ref[chunk(i), :] = (mixed * z_ref[chunk(i), :].astype(F32)).astype(BF16)

    if state_out:
        ones = jnp.ones((8, T), BF16)
        for d in range(2):
            e = G[d]
            kw = jnp.concatenate(
                [(k_ref[chunk(c), :].astype(F32) * (jnp.exp(e["c_col"][c] - e["m_end"]) * scale)).astype(BF16)
                 for c in range(n)], axis=0)
            C = _dot_tn(kw, v_ref[...])
            nvec = _dot(ones, kw)[:1]
            if state_in:
                w_end = jnp.exp(e["m0"] - e["m_end"])
                C = C + w_end * c0_ref[d]
                nvec = nvec + w_end * n0_ref[d]
            cn_ref[d] = C
            nn_ref[d] = nvec
            mn_ref[d] = jnp.broadcast_to(e["b_end"] + e["m_end"], mn_ref.shape[1:])


def _ml_scan(proj, q, k, v, xc, g, gn, skip, T, nb, row_blk0, state=None, want_state=False):
    H = ML_HEADS
    n = T // CHUNK
    tok = lambda b, h: (row_blk0 + b, h)
    head_vec = pl.BlockSpec((1, ML_HD), lambda b, h: (0, h))
    both = lambda *tail: pl.BlockSpec((None, 2, None, *tail), lambda b, h: (b, 0, h, 0, 0))
    in_specs, args = [], []
    if state is not None:
        c0, n0, m0 = state
        in_specs.append(pl.BlockSpec(memory_space=pltpu.SMEM))
        args.append(m0.reshape(nb * 2 * H))
    in_specs += [pl.BlockSpec((T, ML_HD), tok)] * 3
    in_specs += [pl.BlockSpec((T, g.shape[1]), lambda b, h: (row_blk0 + b, 0)),
                 pl.BlockSpec((T, ML_HD), lambda b, h: (row_blk0 + b, H + h)),
                 pl.BlockSpec((T, ML_HD), lambda b, h: (row_blk0 + b, 2 * H + h)),
                 pl.BlockSpec((T, ML_HD), tok), head_vec, head_vec]
    args += [q, k, v, g, proj, proj, xc, gn.reshape(1, ML_WIDTH), skip.reshape(1, ML_WIDTH)]
    if state is not None:
        in_specs += [both(ML_HD, ML_HD), both(1, ML_HD)]
        args += [c0, n0.reshape(nb, 2, H, 1, ML_HD)]
    out_specs = [pl.BlockSpec((T, ML_HD), lambda b, h: (b, h))]
    out_shape = [jax.ShapeDtypeStruct((nb * T, ML_WIDTH), BF16)]
    if want_state:
        out_specs += [both(ML_HD, ML_HD), both(1, ML_HD), both(1, 128)]
        out_shape += [jax.ShapeDtypeStruct((nb, 2, H, ML_HD, ML_HD), F32),
                      jax.ShapeDtypeStruct((nb, 2, H, 1, ML_HD), F32),
                      jax.ShapeDtypeStruct((nb, 2, H, 1, 128), F32)]
    scratch = [pltpu.VMEM((T, ML_HD), F32), pltpu.VMEM((n, n, CHUNK, CHUNK), F32)]
    return pl.pallas_call(
        functools.partial(_ml_scan_kernel, T=T, state_in=state is not None, state_out=want_state),
        grid=(nb, H),
        in_specs=in_specs, out_specs=out_specs, out_shape=out_shape, scratch_shapes=scratch,
        compiler_params=_params("arbitrary", "arbitrary"),
        name="mlstm_scan",
    )(*args)


def _att_rope(x, cos, sin):
    lane = lax.broadcasted_iota(jnp.int32, x.shape, 1)
    half = ATT_HD // 2
    rot = jnp.where((lane & (ATT_HD - 1)) < half, pltpu.roll(x, 128 - half, 1), pltpu.roll(x, half, 1))
    return x * cos + rot * sin


def _att_heads(qms, sinks, key_sets):
    scores = [[_dot_nt(k_both, qm) if bias is None else _dot_nt(k_both, qm) + bias
               for k_both, _, bias in key_sets] for qm in qms]
    ms = []
    for sink, per_set in zip(sinks, scores):
        m = sink
        for s in per_set:
            m = jnp.maximum(m, jnp.max(s, axis=0, keepdims=True))
        ms.append(m)
    probs = [[jnp.exp(s - m) for s in per_set] for m, per_set in zip(ms, scores)]
    outs = []
    for sink, m, per_set in zip(sinks, ms, probs):
        den = jnp.exp(sink - m)
        out = None
        for p, (_, v_t, _) in zip(per_set, key_sets):
            den = den + jnp.sum(p, axis=0, keepdims=True)
            o = _dot(v_t, p.astype(BF16))
            out = o if out is None else out + o
        outs.append(out * (1.0 / den))
    return outs


def _att_prepare_kv(k, v, kb_ref, vt_ref, kv, nblk):
    lane = lax.broadcasted_iota(jnp.int32, k.shape, 1)
    native = lane < ATT_HD if kv % 2 == 0 else lane >= ATT_HD
    kb_ref[kv] = jnp.where(native, k, pltpu.roll(k, ATT_HD, 1)).astype(BF16)
    r0 = (kv % 2) * ATT_HD
    blk = k.shape[0] // nblk
    for c in range(nblk):
        vt_ref[kv, c] = jnp.transpose(v[c * blk:(c + 1) * blk, :])[r0:r0 + ATT_HD, :].astype(BF16)


def _att_group(q_ref, z0_ref, z1_ref, o_ref, sink_ref, rows, kv, key_sets):
    lo = lax.broadcasted_iota(jnp.int32, (CHUNK, 128), 1) < ATT_HD
    zero = jnp.zeros((CHUNK, 128), BF16)
    qms, sinks = [], []
    for g in range(ATT_GROUP):
        col = (2 * kv + g // 2) * 128
        q = q_ref[rows, col:col + 128]
        qms.append(jnp.where(lo, q, zero) if g % 2 == 0 else jnp.where(lo, zero, q))
        sinks.append(jnp.full((1, CHUNK), sink_ref[kv * ATT_GROUP + g], F32))
    heads = _att_heads(qms, sinks, key_sets)
    for p in range(2):
        col = (2 * kv + p) * 128
        out = jnp.transpose(jnp.concatenate(heads[2 * p:2 * p + 2], axis=0))
        z_ref = z0_ref if col < ATT_WIDTH // 2 else z1_ref
        zc = col % (ATT_WIDTH // 2)
        zf = z_ref[rows, zc:zc + 128].astype(F32)
        o_ref[rows, col:col + 128] = (out * zf).astype(BF16)


def _att_ctx_kernel(sink_ref, q_ref, k_ref, v_ref, z0_ref, z1_ref, o_ref, qs_ref, kb_ref, vt_ref):
    T = q_ref.shape[0]
    qs_ref[...] = q_ref[...] * (ATT_HD ** -0.5)
    for kv in range(ATT_KV):
        slab = (kv // 2) * 128
        _att_prepare_kv(k_ref[:, slab:slab + 128].astype(F32), v_ref[:, slab:slab + 128].astype(F32),
                        kb_ref, vt_ref, kv, 1)
    for c in range(T // CHUNK):
        rows = slice(c * CHUNK, (c + 1) * CHUNK)
        for kv in range(ATT_KV):
            _att_group(qs_ref, z0_ref, z1_ref, o_ref, sink_ref, rows, kv, [(kb_ref[kv], vt_ref[kv, 0], None)])


def _att_ctx(proj, sink):
    T = SEQ
    half = ATT_WIDTH // 2
    return pl.pallas_call(
        _att_ctx_kernel,
        grid=(BATCH,),
        in_specs=[pl.BlockSpec(memory_space=pltpu.SMEM),
                  pl.BlockSpec((T, ATT_WIDTH), lambda b: (b, 0)),
                  pl.BlockSpec((T, ATT_KVW), lambda b: (b, ATT_WIDTH // ATT_KVW)),
                  pl.BlockSpec((T, ATT_KVW), lambda b: (b, ATT_WIDTH // ATT_KVW + 1)),
                  pl.BlockSpec((T, half), lambda b: (b, (ATT_WIDTH + 2 * ATT_KVW) // half)),
                  pl.BlockSpec((T, half), lambda b: (b, (ATT_WIDTH + 2 * ATT_KVW) // half + 1))],
        out_specs=pl.BlockSpec((T, ATT_WIDTH), lambda b: (b, 0)),
        out_shape=jax.ShapeDtypeStruct((N_PROMPT, ATT_WIDTH), BF16),
        scratch_shapes=[pltpu.VMEM((T, ATT_WIDTH), BF16), pltpu.VMEM((ATT_KV, T, 128), BF16),
                        pltpu.VMEM((ATT_KV, 1, ATT_HD, T), BF16)],
        compiler_params=_params("arbitrary"),
        name="att_ctx",
    )(sink, proj, proj, proj, proj, proj)


def _att_win_kernel(sink_ref, q_ref, k_ref, v_ref, z0_ref, z1_ref, kc_ref, vc_ref, cos_ref, sin_ref,
                    o_ref, qs_ref, kb_ref, vt_ref, kcb_ref, vct_ref):
    T = q_ref.shape[0]
    L = CHUNK
    n = T // L
    cos = cos_ref[...]
    sin = sin_ref[...]
    for c0 in range(0, ATT_WIDTH, 128):
        q = _att_rope(q_ref[:, c0:c0 + 128].astype(F32), cos, sin)
        qs_ref[:, c0:c0 + 128] = (q * (ATT_HD ** -0.5)).astype(BF16)
    for kv in range(ATT_KV):
        slab = (kv // 2) * 128
        _att_prepare_kv(_att_rope(k_ref[:, slab:slab + 128].astype(F32), cos, sin),
                        v_ref[:, slab:slab + 128].astype(F32), kb_ref, vt_ref, kv, n)
        _att_prepare_kv(kc_ref[:, slab:slab + 128], vc_ref[:, slab:slab + 128], kcb_ref, vct_ref, kv, 1)

    jj = lax.broadcasted_iota(jnp.int32, (L, L), 0)
    ii = lax.broadcasted_iota(jnp.int32, (L, L), 1)
    neg = jnp.full((L, L), -jnp.inf, F32)
    bias_prev = jnp.where(jj >= ii, 0.0, neg)
    bias_next = jnp.where(jj <= ii, 0.0, neg)

    def body(c, carry):
        rows = pl.ds(pl.multiple_of(c * L, L), L)
        c_prev = jnp.maximum(c - 1, 0)
        c_next = jnp.minimum(c + 1, n - 1)
        b_prev = jnp.where(c > 0, bias_prev, neg)
        b_next = jnp.where(c < n - 1, bias_next, neg)
        for kv in range(ATT_KV):
            def keys(cb):
                return kb_ref[kv, pl.ds(pl.multiple_of(cb * L, L), L), :]
            key_sets = [(keys(c_prev), vt_ref[kv, c_prev], b_prev),
                        (keys(c), vt_ref[kv, c], None),
                        (keys(c_next), vt_ref[kv, c_next], b_next),
                        (kcb_ref[kv], vct_ref[kv, 0], None)]
            _att_group(qs_ref, z0_ref, z1_ref, o_ref, sink_ref, rows, kv, key_sets)
        return carry

    lax.fori_loop(0, n, body, 0)


def _att_win(proj, sink, k_ctx, v_ctx, cos, sin):
    T = DEC_SEQ
    half = ATT_WIDTH // 2
    rb = N_PROMPT // T
    return pl.pallas_call(
        _att_win_kernel,
        grid=(DEC_BATCH,),
        in_specs=[pl.BlockSpec(memory_space=pltpu.SMEM),
                  pl.BlockSpec((T, ATT_WIDTH), lambda b: (rb + b, 0)),
                  pl.BlockSpec((T, ATT_KVW), lambda b: (rb + b, ATT_WIDTH // ATT_KVW)),
                  pl.BlockSpec((T, ATT_KVW), lambda b: (rb + b, ATT_WIDTH // ATT_KVW + 1)),
                  pl.BlockSpec((T, half), lambda b: (rb + b, (ATT_WIDTH + 2 * ATT_KVW) // half)),
                  pl.BlockSpec((T, half), lambda b: (rb + b, (ATT_WIDTH + 2 * ATT_KVW) // half + 1)),
                  pl.BlockSpec((None, PAST_LEN, ATT_KVW), lambda b: (b, 0, 0)),
                  pl.BlockSpec((None, PAST_LEN, ATT_KVW), lambda b: (b, 0, 0)),
                  pl.BlockSpec((T, 128), lambda b: (0, 0)),
                  pl.BlockSpec((T, 128), lambda b: (0, 0))],
        out_specs=pl.BlockSpec((T, ATT_WIDTH), lambda b: (b, 0)),
        out_shape=jax.ShapeDtypeStruct((N_LATENT, ATT_WIDTH), BF16),
        scratch_shapes=[pltpu.VMEM((T, ATT_WIDTH), BF16),
                        pltpu.VMEM((ATT_KV, T, 128), BF16),
                        pltpu.VMEM((ATT_KV, T // CHUNK, ATT_HD, CHUNK), BF16),
                        pltpu.VMEM((ATT_KV, PAST_LEN, 128), BF16),
                        pltpu.VMEM((ATT_KV, 1, ATT_HD, PAST_LEN), BF16)],
        compiler_params=_params("arbitrary"),
        name="att_win",
    )(sink, proj, proj, proj, proj, proj, k_ctx, v_ctx, cos, sin)


def _rope_tables(T, hd, reps):
    rows = T // GRID_W
    row = np.repeat(np.arange(rows, dtype=np.float64), GRID_W)
    col = np.tile(np.arange(GRID_W, dtype=np.float64), rows)
    nf = hd // 4
    inv = ROPE_BASE ** (-np.arange(nf, dtype=np.float64) / nf)
    ang = np.concatenate([row[:, None] * inv[None, :], col[:, None] * inv[None, :]], axis=-1)
    cos, sin = np.cos(ang), np.sin(ang)
    return (jnp.asarray(np.tile(np.concatenate([cos, cos], axis=-1), (1, reps)), F32),
            jnp.asarray(np.tile(np.concatenate([-sin, sin], axis=-1), (1, reps)), F32))


def _retention_layer(h, w_in, decay_f, decay_b, gn, state, rope):
    z0 = 2 * RET_QK + RET_WIDTH
    proj = _in_proj(h, w_in, 1024, (z0, z0))
    dec = jnp.stack([decay_f, decay_b]).astype(F32)
    a_p, new_state = _retention(proj, dec, gn, SEQ, BATCH, 0, want_state=True)
    (a_s,) = _retention(proj, dec, gn, DEC_SEQ, DEC_BATCH, N_PROMPT // DEC_SEQ, rope=rope, state=state)
    return a_p, a_s, new_state


def kernel(x_prompt, x_sample, c, c_ctx, state_l0_ret, state_l1_C, state_l1_n, state_l1_m, cache_l2_k, cache_l2_v, state_l3_ret, norm_l0, ada_w_l0, ada_b_l0, w_in_l0, w_out_l0, ret_decay_f_l0, ret_decay_b_l0, ret_gn_l0, norm_l1, ada_w_l1, ada_b_l1, w_in_l1, w_out_l1, conv_w_l1, conv_b_l1, wq_l1, wk_l1, wv_l1, wif_f_l1, bif_f_l1, wif_b_l1, bif_b_l1, gn_l1, skip_l1, norm_l2, ada_w_l2, ada_b_l2, w_in_l2, w_out_l2, sink_l2, norm_l3, ada_w_l3, ada_b_l3, w_in_l3, w_out_l3, ret_decay_f_l3, ret_decay_b_l3, ret_gn_l3, final_norm):
    ct = jnp.concatenate([c_ctx[:, None], c.T, jnp.zeros((D, MOD_ROWS - 1 - DEC_BATCH), F32)], axis=1)
    mods = [_adaln(ct, w, b) for w, b in ((ada_w_l0, ada_b_l0), (ada_w_l1, ada_b_l1),
                                          (ada_w_l2, ada_b_l2), (ada_w_l3, ada_b_l3))]
    rope_ret = _rope_tables(DEC_SEQ, RET_DK, 1)
    rope_att = _rope_tables(DEC_SEQ, ATT_HD, 2)

    x, h = _norm_mod(x_prompt.reshape(N_PROMPT, D), x_sample.reshape(N_LATENT, D), norm_l0, mods[0])

    a_p, a_s, new_l0_ret = _retention_layer(h, w_in_l0, ret_decay_f_l0, ret_decay_b_l0, ret_gn_l0, state_l0_ret,
                                            rope_ret)
    x, h = _out_proj(a_p, a_s, w_out_l0, x, mods[0], norm_l1, mods[1], False)

    proj = _in_proj(h, w_in_l1, 1024, (ML_WIDTH, 2 * ML_WIDTH))
    wif = jnp.concatenate([wif_f_l1, wif_b_l1], axis=1)
    bif = jnp.concatenate([bif_f_l1, bif_b_l1])
    q, k, v, xc, g = _ml_pre(proj, conv_w_l1, conv_b_l1, wq_l1, wk_l1, wv_l1, wif, bif)
    a_p, new_l1_C, n_new, m_new = _ml_scan(proj, q, k, v, xc, g, gn_l1, skip_l1, SEQ, BATCH, 0, want_state=True)
    (a_s,) = _ml_scan(proj, q, k, v, xc, g, gn_l1, skip_l1, DEC_SEQ, DEC_BATCH, N_PROMPT // DEC_SEQ,
                      state=(state_l1_C, state_l1_n, state_l1_m))
    new_l1_n = n_new.reshape(BATCH, 2, ML_HEADS, ML_HD)
    new_l1_m = m_new[:, :, :, 0, 0]
    x, h = _out_proj(a_p, a_s, w_out_l1, x, mods[1], norm_l2, mods[2], False)

    proj = _in_proj(h, w_in_l2, 1280, (ATT_WIDTH + 2 * ATT_KVW,) * 2)
    kv_new = proj[:N_PROMPT, ATT_WIDTH:ATT_WIDTH + 2 * ATT_KVW].astype(F32)
    new_l2_k = kv_new[:, :ATT_KVW].reshape(BATCH, SEQ, ATT_KV, ATT_HD)
    new_l2_v = kv_new[:, ATT_KVW:].reshape(BATCH, SEQ, ATT_KV, ATT_HD)
    a_p = _att_ctx(proj, sink_l2)
    a_s = _att_win(proj, sink_l2, cache_l2_k.reshape(DEC_BATCH, PAST_LEN, ATT_KVW),
                   cache_l2_v.reshape(DEC_BATCH, PAST_LEN, ATT_KVW), rope_att[0], rope_att[1])
    x, h = _out_proj(a_p, a_s, w_out_l2, x, mods[2], norm_l3, mods[3], False)

    a_p, a_s, new_l3_ret = _retention_layer(h, w_in_l3, ret_decay_f_l3, ret_decay_b_l3, ret_gn_l3, state_l3_ret,
                                            rope_ret)
    (y_p,) = _out_proj(a_p, a_s, w_out_l3, x, mods[3], final_norm, mods[3], True, rows=(0, N_PROMPT))
    (y_s,) = _out_proj(a_p, a_s, w_out_l3, x, mods[3], final_norm, mods[3], True, rows=(N_PROMPT, N_LATENT))

    y_prompt = y_p.reshape(BATCH, SEQ, D)
    y_sample = y_s.reshape(DEC_BATCH, DEC_SEQ, D)
    return (y_prompt, y_sample, new_l0_ret, new_l1_C, new_l1_n, new_l1_m, new_l2_k, new_l2_v, new_l3_ret)
```

```python
import functools

import jax
import jax.numpy as jnp
import numpy as np
from jax import lax
from jax.experimental import pallas as pl
from jax.experimental.pallas import tpu as pltpu

F32 = jnp.float32
BF16 = jnp.bfloat16

D = 1024
BATCH = 16
SEQ = 256
DEC_BATCH = 2
DEC_SEQ = 1024
PAST_LEN = 512
GRID_W = 64
CHUNK = 128
EPS = 1e-6
ROPE_BASE = 10000.0

N_PROMPT = BATCH * SEQ
N_LATENT = DEC_BATCH * DEC_SEQ
N_TOK = N_PROMPT + N_LATENT
MOD_ROWS = 8

RET_HEADS = 8
RET_DK = 128
RET_DV = 256
RET_QK = RET_HEADS * RET_DK
RET_WIDTH = RET_HEADS * RET_DV
RET_HB_PROMPT = 4
RET_HB_LATENT = 1

ML_HEADS = 4
ML_WIDTH = 2 * D
ML_HD = ML_WIDTH // ML_HEADS
ML_BLOCK = 4
ML_CONV = 5
ML_TILE = 256
ML_ROWS = 1024
ML_TN = 1024
ML_XT = ML_WIDTH // ML_TN

ATT_HEADS = 16
ATT_KV = 4
ATT_HD = 64
ATT_GROUP = ATT_HEADS // ATT_KV
ATT_WIDTH = ATT_HEADS * ATT_HD
ATT_KVW = ATT_KV * ATT_HD
WINDOW = 128

OUT_SUB = 256
IN_SUB = 4

VMEM_LIMIT = 56 * 1024 * 1024


def _params(*sem):
    return pltpu.CompilerParams(dimension_semantics=sem, vmem_limit_bytes=VMEM_LIMIT)


def _mod_row(row0):
    return jnp.maximum((row0 - N_PROMPT) // DEC_SEQ + 1, 0)


def _dot(a, b):
    return jnp.dot(a, b, preferred_element_type=F32)


def _dot_nt(a, b):
    return lax.dot_general(a, b, (((1,), (1,)), ((), ())), preferred_element_type=F32)


def _dot_tn(a, b):
    return lax.dot_general(a, b, (((0,), (0,)), ((), ())), preferred_element_type=F32)


def _silu(x):
    return x * jax.nn.sigmoid(x)


def _log_sigmoid(x):
    return jnp.minimum(x, 0.0) - jnp.log1p(jnp.exp(-jnp.abs(x)))


def _rms(x, g):
    return x * lax.rsqrt(jnp.mean(x * x, axis=-1, keepdims=True) + EPS) * g


def _group_norm(x, g):
    xc = x - jnp.mean(x, axis=-1, keepdims=True)
    return xc * lax.rsqrt(jnp.mean(xc * xc, axis=-1, keepdims=True) + EPS) * g


def _two_source_specs(tm, width, tile0=0):
    n_p = N_PROMPT // tm
    return (pl.BlockSpec((tm, width), lambda i: (jnp.minimum(tile0 + i, n_p - 1), 0)),
            pl.BlockSpec((tm, width), lambda i: (jnp.maximum(tile0 + i - n_p, 0), 0)))


def _pick(i, tm, p_ref, s_ref):
    return jnp.where(i < N_PROMPT // tm, p_ref[...], s_ref[...])


def _ada_kernel(ct_ref, w_ref, b_ref, o_ref):
    s = _silu(ct_ref[...])
    w = w_ref[...]
    rows = [jnp.sum(w * s[:, r:r + 1], axis=0, keepdims=True) for r in range(1 + DEC_BATCH)]
    rows.append(jnp.zeros((MOD_ROWS - len(rows), w.shape[1]), F32))
    o_ref[...] = jnp.concatenate(rows, axis=0) + b_ref[...]


def _adaln(ct, w, b):
    tn = 1024
    return pl.pallas_call(
        _ada_kernel,
        grid=(3 * D // tn,),
        in_specs=[pl.BlockSpec((D, MOD_ROWS), lambda j: (0, 0)),
                  pl.BlockSpec((D, tn), lambda j: (0, j)),
                  pl.BlockSpec((1, tn), lambda j: (0, j))],
        out_specs=pl.BlockSpec((MOD_ROWS, tn), lambda j: (0, j)),
        out_shape=jax.ShapeDtypeStruct((MOD_ROWS, 3 * D), F32),
        compiler_params=_params("arbitrary"),
        name="adaln",
    )(ct, w, b.reshape(1, 3 * D))


def _norm_mod_kernel(xp_ref, xs_ref, g_ref, mod_ref, x_ref, h_ref, *, tm):
    i = pl.program_id(0)
    m = mod_ref[pl.ds(_mod_row(i * tm), 1), :]
    x = _pick(i, tm, xp_ref, xs_ref)
    x_ref[...] = x
    y = _rms(x, g_ref[...])
    h_ref[...] = (y * (1.0 + m[:, D:2 * D]) + m[:, :D]).astype(BF16)


def _norm_mod(xp, xs, g, mod):
    tm = 512
    return pl.pallas_call(
        functools.partial(_norm_mod_kernel, tm=tm),
        grid=(N_TOK // tm,),
        in_specs=[*_two_source_specs(tm, D),
                  pl.BlockSpec((1, D), lambda i: (0, 0)),
                  pl.BlockSpec((MOD_ROWS, 3 * D), lambda i: (0, 0))],
        out_specs=[pl.BlockSpec((tm, D), lambda i: (i, 0))] * 2,
        out_shape=[jax.ShapeDtypeStruct((N_TOK, D), F32), jax.ShapeDtypeStruct((N_TOK, D), BF16)],
        compiler_params=_params("arbitrary"),
        name="norm_mod",
    )(xp, xs, g.reshape(1, D), mod)


def _in_proj_kernel(h_ref, w_ref, o_ref, wb_ref, *, tn, n, acts):
    j = pl.program_id(0)

    @pl.when(pl.program_id(1) == 0)
    def _():
        wb_ref[...] = w_ref[...].astype(BF16)

    sig0, silu0 = acts
    gated = (j + 1) * tn > min(sig0, silu0)

    @pl.when(jnp.logical_not(gated))
    def _():
        o_ref[...] = _dot(h_ref[...], wb_ref[...]).astype(BF16)

    @pl.when(gated)
    def _():
        sub = h_ref.shape[0] // IN_SUB
        for s in range(IN_SUB):
            rows = slice(s * sub, (s + 1) * sub)
            acc = _dot(h_ref[rows, :], wb_ref[...])
            col = j * tn + lax.broadcasted_iota(jnp.int32, acc.shape, 1)
            sig = jax.nn.sigmoid(acc)
            out = jnp.where(col >= silu0, acc * sig, jnp.where(col >= sig0, sig, acc))
            o_ref[rows, :] = out.astype(BF16)


def _in_proj(h, w, tn, acts):
    tm = 1024
    n = w.shape[1]
    return pl.pallas_call(
        functools.partial(_in_proj_kernel, tn=tn, n=n, acts=acts),
        grid=(n // tn, N_TOK // tm),
        in_specs=[pl.BlockSpec((tm, D), lambda j, i: (i, 0)),
                  pl.BlockSpec((D, tn), lambda j, i: (0, j))],
        out_specs=pl.BlockSpec((tm, tn), lambda j, i: (i, j)),
        out_shape=jax.ShapeDtypeStruct((N_TOK, n), BF16),
        scratch_shapes=[pltpu.VMEM((D, tn), BF16)],
        compiler_params=_params("arbitrary", "arbitrary"),
        name="in_proj",
    )(h, w)


def _out_proj_kernel(ap_ref, as_ref, w_ref, x_ref, mod_ref, g_ref, modn_ref, *rest, tm, tile0, final):
    wb_ref = rest[-1]

    @pl.when(pl.program_id(0) == 0)
    def _():
        wb_ref[...] = w_ref[...].astype(BF16)

    i = tile0 + pl.program_id(0)
    r = _mod_row(i * tm)
    gate = mod_ref[pl.ds(r, 1), :][:, 2 * D:]
    from_prompt = i < N_PROMPT // tm
    for s in range(tm // OUT_SUB):
        rows = slice(s * OUT_SUB, (s + 1) * OUT_SUB)
        a = jnp.where(from_prompt, ap_ref[rows, :], as_ref[rows, :])
        xn = x_ref[rows, :] + gate * _dot(a, wb_ref[...])
        y = _rms(xn, g_ref[...])
        if final:
            rest[0][rows, :] = y
        else:
            mn = modn_ref[pl.ds(r, 1), :]
            rest[0][rows, :] = xn
            rest[1][rows, :] = (y * (1.0 + mn[:, D:2 * D]) + mn[:, :D]).astype(BF16)


def _out_proj(a_p, a_s, w, x, mod, g_next, mod_next, final, rows=(0, N_TOK)):
    tm = 512
    kw = w.shape[0]
    tile0, nt = rows[0] // tm, rows[1] // tm
    row = pl.BlockSpec((tm, D), lambda i: (i, 0))
    if final:
        out_specs, out_shape = [row], [jax.ShapeDtypeStruct((rows[1], D), F32)]
    else:
        out_specs = [row, row]
        out_shape = [jax.ShapeDtypeStruct((rows[1], D), F32), jax.ShapeDtypeStruct((rows[1], D), BF16)]
    return pl.pallas_call(
        functools.partial(_out_proj_kernel, tm=tm, tile0=tile0, final=final),
        grid=(nt,),
        in_specs=[*_two_source_specs(tm, kw, tile0),
                  pl.BlockSpec((kw, D), lambda i: (0, 0)),
                  pl.BlockSpec((tm, D), lambda i: (tile0 + i, 0)),
                  pl.BlockSpec((MOD_ROWS, 3 * D), lambda i: (0, 0)),
                  pl.BlockSpec((1, D), lambda i: (0, 0)),
                  pl.BlockSpec((MOD_ROWS, 3 * D), lambda i: (0, 0))],
        out_specs=out_specs,
        out_shape=out_shape,
        scratch_shapes=[pltpu.VMEM((kw, D), BF16)],
        compiler_params=_params("arbitrary"),
        name="out_proj",
    )(a_p, a_s, w, x, mod, g_next.reshape(1, D), mod_next)


def _ret_kernel(*refs, T, HB, rope, state_in, state_out):
    refs = list(refs)
    dec_ref, q_ref, k_ref, v_ref, z_ref, gn_ref = refs[:6]
    pos = 6
    if rope:
        cos_ref, sin_ref = refs[pos:pos + 2]
        pos += 2
    if state_in:
        s0_ref = refs[pos]
        pos += 1
    o_ref = refs[pos]
    pos += 1
    if state_out:
        sn_ref = refs[pos]
        pos += 1
    kv_ref, sf_ref, sb_ref = refs[pos:pos + 3]
    pos += 3
    if rope:
        kr_ref = refs[pos]

    hg = pl.program_id(1)
    n = T // CHUNK
    L = CHUNK
    ii = lax.broadcasted_iota(jnp.int32, (L, L), 0).astype(F32)
    jj = lax.broadcasted_iota(jnp.int32, (L, L), 1).astype(F32)
    scale = RET_DK ** -0.5

    def chunk(c):
        return slice(c * L, (c + 1) * L)

    def rotate(x, c):
        return x * cos_ref[chunk(c), :] + pltpu.roll(x, RET_DK // 2, 1) * sin_ref[chunk(c), :]

    for hh in range(HB):
        head = hg * HB + hh
        qs = slice(hh * RET_DK, (hh + 1) * RET_DK)
        vs = slice(hh * RET_DV, (hh + 1) * RET_DV)
        lg_f = _log_sigmoid(jnp.full((1, RET_DV), dec_ref[0, head], F32))
        lg_b = _log_sigmoid(jnp.full((1, RET_DV), dec_ref[1, head], F32))
        lf, lb = lg_f[:, :L], lg_b[:, :L]
        k_dec_f = jnp.exp(lf * (L - 1.0 - ii)) * scale
        k_dec_b = jnp.exp(lb * ii) * scale
        q_dec_f = jnp.exp(lf * (ii + 1.0))
        q_dec_b = jnp.exp(lb * (L - ii))
        mask = (jnp.where(ii >= jj, jnp.exp(lf * jnp.maximum(ii - jj, 0.0)), 0.0)
                + jnp.where(jj >= ii, jnp.exp(lb * jnp.maximum(jj - ii, 0.0)), 0.0)) * scale
        g_f = jnp.exp(lg_f * float(L))
        g_b = jnp.exp(lg_b * float(L))

        for c in range(n):
            kc = k_ref[chunk(c), qs].astype(F32)
            if rope:
                kc = rotate(kc, c)
                kr_ref[chunk(c), qs] = kc.astype(BF16)
            kk = jnp.concatenate([kc * k_dec_f, kc * k_dec_b], axis=1).astype(BF16)
            kv_ref[hh, c] = _dot_tn(kk, v_ref[chunk(c), vs])
        S = s0_ref[0, hh] if state_in else None
        has_f = []
        for c in range(n):
            has_f.append(S is not None)
            if S is not None:
                sf_ref[hh, c] = S.astype(BF16)
            kvc = kv_ref[hh, c, :RET_DK, :]
            S = kvc if S is None else S * g_f + kvc
        if state_out:
            sn_ref[0, hh] = S
        S = s0_ref[1, hh] if state_in else None
        has_b = [False] * n
        for c in reversed(range(n)):
            has_b[c] = S is not None
            if S is not None:
                sb_ref[hh, c] = S.astype(BF16)
            kvc = kv_ref[hh, c, RET_DK:, :]
            S = kvc if S is None else S * g_b + kvc
        if state_out:
            sn_ref[1, hh] = S

        for c in range(n):
            qc = q_ref[chunk(c), qs]
            qf = qc.astype(F32)
            if rope:
                qf = rotate(qf, c)
                qc = qf.astype(BF16)
                kc = kr_ref[chunk(c), qs]
            else:
                kc = k_ref[chunk(c), qs]
            lhs = [(_dot_nt(qc, kc) * mask).astype(BF16)]
            rhs = [v_ref[chunk(c), vs]]
            if has_f[c]:
                lhs.append((qf * q_dec_f).astype(BF16))
                rhs.append(sf_ref[hh, c])
            if has_b[c]:
                lhs.append((qf * q_dec_b).astype(BF16))
                rhs.append(sb_ref[hh, c])
            o = _dot(jnp.concatenate(lhs, axis=1), jnp.concatenate(rhs, axis=0))
            zf = z_ref[chunk(c), vs].astype(F32)
            o_ref[chunk(c), vs] = (_group_norm(o, gn_ref[:, vs]) * zf).astype(BF16)


def _retention(proj, dec, gn, T, nb, row_blk0, rope=None, state=None, want_state=False):
    HB = RET_HB_PROMPT if T == SEQ else RET_HB_LATENT
    ng = RET_HEADS // HB
    n = T // CHUNK
    qw, vw = HB * RET_DK, HB * RET_DV
    in_specs = [pl.BlockSpec(memory_space=pltpu.SMEM),
                pl.BlockSpec((T, qw), lambda b, h: (row_blk0 + b, h)),
                pl.BlockSpec((T, qw), lambda b, h: (row_blk0 + b, ng + h)),
                pl.BlockSpec((T, vw), lambda b, h: (row_blk0 + b, ng + h)),
                pl.BlockSpec((T, vw), lambda b, h: (row_blk0 + b, 2 * ng + h)),
                pl.BlockSpec((1, vw), lambda b, h: (0, h))]
    args = [dec, proj, proj, proj, proj, gn.reshape(1, RET_WIDTH)]
    if rope is not None:
        in_specs += [pl.BlockSpec((T, RET_DK), lambda b, h: (0, 0))] * 2
        args += list(rope)
    if state is not None:
        in_specs.append(pl.BlockSpec((None, 2, HB, RET_DK, RET_DV), lambda b, h: (b, 0, h, 0, 0)))
        args.append(state)
    out_specs = [pl.BlockSpec((T, vw), lambda b, h: (b, h))]
    out_shape = [jax.ShapeDtypeStruct((nb * T, RET_WIDTH), BF16)]
    if want_state:
        out_specs.append(pl.BlockSpec((None, 2, HB, RET_DK, RET_DV), lambda b, h: (b, 0, h, 0, 0)))
        out_shape.append(jax.ShapeDtypeStruct((nb, 2, RET_HEADS, RET_DK, RET_DV), F32))
    scratch = [pltpu.VMEM((HB, n, 2 * RET_DK, RET_DV), F32),
               pltpu.VMEM((HB, n, RET_DK, RET_DV), BF16),
               pltpu.VMEM((HB, n, RET_DK, RET_DV), BF16)]
    if rope is not None:
        scratch.append(pltpu.VMEM((T, qw), BF16))
    return pl.pallas_call(
        functools.partial(_ret_kernel, T=T, HB=HB, rope=rope is not None, state_in=state is not None,
                          state_out=want_state),
        grid=(nb, ng),
        in_specs=in_specs, out_specs=out_specs, out_shape=out_shape, scratch_shapes=scratch,
        compiler_params=_params("arbitrary", "arbitrary"),
        name="retention",
    )(*args)


def _ml_in_kernel(h_ref, w_ref, cw_ref, cb_ref, bq_ref, bk_ref, bv_ref, wq_ref, wk_ref, wv_ref, bias_ref,
                  oz_ref, q_ref, k_ref, v_ref, xc_ref, g_ref, wb_ref, bd_ref):
    j = pl.program_id(0)
    i = pl.program_id(1)
    nsub = ML_TN // ML_TILE

    @pl.when(i == 0)
    def _():
        wb_ref[...] = w_ref[...].astype(BF16)

    @pl.when((i == 0) & (j < ML_XT))
    def _():
        row = lax.broadcasted_iota(jnp.int32, (ML_TILE, ML_TILE), 0)
        col = lax.broadcasted_iota(jnp.int32, (ML_TILE, ML_TILE), 1)
        shift = ML_BLOCK.bit_length() - 1
        same_block = jnp.right_shift(row, shift) == jnp.right_shift(col, shift)
        for which, b_ref in enumerate((bq_ref, bk_ref, bv_ref)):
            for t in range(nsub):
                w = b_ref[t * ML_TILE:(t + 1) * ML_TILE, :]
                rep = w[:, ML_BLOCK - 1:ML_BLOCK]
                for dd in range(ML_BLOCK - 1):
                    rep = jnp.where((col & (ML_BLOCK - 1)) == dd, w[:, dd:dd + 1], rep)
                bd_ref[which, t] = jnp.where(same_block, rep, 0.0).astype(BF16)

    @pl.when(j >= ML_XT)
    def _():
        sub = ML_ROWS // IN_SUB
        for s in range(IN_SUB):
            rows = slice(s * sub, (s + 1) * sub)
            acc = _dot(h_ref[rows, :], wb_ref[...])
            col = j * ML_TN + lax.broadcasted_iota(jnp.int32, acc.shape, 1)
            sig = jax.nn.sigmoid(acc)
            oz_ref[rows, :] = jnp.where(col >= 2 * ML_WIDTH, acc * sig, sig).astype(BF16)

    @pl.when(j < ML_XT)
    def _():
        acc = _dot(h_ref[...], wb_ref[...])
        T = jnp.where(i < N_PROMPT // ML_ROWS, SEQ, DEC_SEQ)
        pos = lax.broadcasted_iota(jnp.int32, (ML_ROWS, ML_TILE), 0) & (T - 1)
        pad = ML_CONV // 2
        g = jnp.where(j == 0, 1.0, 0.0) * bias_ref[...]
        for t in range(nsub):
            cols = slice(t * ML_TILE, (t + 1) * ML_TILE)
            x = acc[:, cols]
            conv = cb_ref[:, cols] + x * cw_ref[pad:pad + 1, cols]
            for tap in range(ML_CONV):
                delta = tap - pad
                if delta == 0:
                    continue
                shifted = pltpu.roll(x, (-delta) % ML_ROWS, 0)
                ok = (pos + delta >= 0) & (pos + delta < T)
                conv = conv + jnp.where(ok, shifted, 0.0) * cw_ref[tap:tap + 1, cols]
            xcb = _silu(conv).astype(BF16)
            qb = _dot(xcb, bd_ref[0, t]).astype(BF16)
            kb = _dot(xcb, bd_ref[1, t]).astype(BF16)
            vb = _dot(x.astype(BF16), bd_ref[2, t]).astype(BF16)
            q_ref[:, cols] = qb
            k_ref[:, cols] = kb
            v_ref[:, cols] = vb
            xc_ref[:, cols] = xcb
            g = g + (_dot(qb, wq_ref[cols, :].astype(BF16)) + _dot(kb, wk_ref[cols, :].astype(BF16))
                     + _dot(vb, wv_ref[cols, :].astype(BF16)))
        g_ref[...] = g


def _ml_in_proj(h, w, conv_w, conv_b, wq, wk, wv, wif, bif):
    ng = wif.shape[1]
    ni = N_TOK // ML_ROWS
    xt = ML_XT
    wq, wk, wv = (a.reshape(ML_WIDTH, ML_BLOCK) for a in (wq, wk, wv))
    xcol = lambda j, i: (0, jnp.minimum(j, xt - 1))
    xrow = lambda off: (lambda j, i: (off + jnp.minimum(j, xt - 1), 0))
    x_out = pl.BlockSpec((ML_ROWS, ML_TN), lambda j, i: (jnp.where(j < xt, i, ni - 1), jnp.minimum(j, xt - 1)))
    return pl.pallas_call(
        _ml_in_kernel,
        grid=(3 * ML_WIDTH // ML_TN, ni),
        in_specs=[pl.BlockSpec((ML_ROWS, D), lambda j, i: (i, 0)),
                  pl.BlockSpec((D, ML_TN), lambda j, i: (0, j)),
                  pl.BlockSpec((ML_CONV, ML_TN), xcol),
                  pl.BlockSpec((1, ML_TN), xcol),
                  pl.BlockSpec((ML_TN, ML_BLOCK), xrow(0)),
                  pl.BlockSpec((ML_TN, ML_BLOCK), xrow(0)),
                  pl.BlockSpec((ML_TN, ML_BLOCK), xrow(0)),
                  pl.BlockSpec((ML_TN, ng), xrow(0)),
                  pl.BlockSpec((ML_TN, ng), xrow(xt)),
                  pl.BlockSpec((ML_TN, ng), xrow(2 * xt)),
                  pl.BlockSpec((1, ng), lambda j, i: (0, 0))],
        out_specs=[pl.BlockSpec((ML_ROWS, ML_TN), lambda j, i: (jnp.where(j < xt, 0, i), jnp.maximum(j - xt, 0))),
                   x_out, x_out, x_out, x_out,
                   pl.BlockSpec((None, ML_ROWS, ng),
                                lambda j, i: (jnp.minimum(j, xt - 1), jnp.where(j < xt, i, ni - 1), 0))],
        out_shape=[jax.ShapeDtypeStruct((N_TOK, 2 * ML_WIDTH), BF16)]
        + [jax.ShapeDtypeStruct((N_TOK, ML_WIDTH), BF16)] * 4
        + [jax.ShapeDtypeStruct((xt, N_TOK, ng), F32)],
        scratch_shapes=[pltpu.VMEM((D, ML_TN), BF16), pltpu.VMEM((3, ML_TN // ML_TILE, ML_TILE, ML_TILE), BF16)],
        compiler_params=_params("arbitrary", "arbitrary"),
        name="mlstm_in_proj",
    )(h, w, conv_w, conv_b.reshape(1, ML_WIDTH), wq, wk, wv, wif, wif, wif, bif.reshape(1, ng))


def _dot_split(lhs_bf16, rhs):
    r1 = rhs.astype(BF16)
    e1 = rhs - r1.astype(F32)
    r2 = e1.astype(BF16)
    r3 = (e1 - r2.astype(F32)).astype(BF16)
    return _dot(lhs_bf16, r1) + _dot(lhs_bf16, r2) + _dot(lhs_bf16, r3)


def _ml_scan_kernel(*refs, T, state_in, state_out):
    refs = list(refs)
    pos = 0
    if state_in:
        m0_ref = refs[0]
        pos = 1
    q_ref, k_ref, v_ref, g_ref, op_ref, z_ref, xc_ref, gn_ref, sk_ref = refs[pos:pos + 9]
    pos += 9
    if state_in:
        c0_ref, n0_ref = refs[pos:pos + 2]
        pos += 2
    o_ref = refs[pos]
    pos += 1
    if state_out:
        cn_ref, nn_ref, mn_ref = refs[pos:pos + 3]
        pos += 3
    acc_ref, s_ref = refs[pos:pos + 2]

    b = pl.program_id(0)
    h = pl.program_id(1)
    n = T // CHUNK
    L = CHUNK
    scale = ML_HD ** -0.5

    def chunk(c):
        return slice(c * L, (c + 1) * L)

    ii = lax.broadcasted_iota(jnp.int32, (L, L), 0)
    jj = lax.broadcasted_iota(jnp.int32, (L, L), 1)
    neg = jnp.full((L, L), -jnp.inf, F32)
    zero = jnp.zeros((L, L), F32)
    tri = [jnp.where(jj <= ii, 1.0, 0.0).astype(BF16), jnp.where(jj >= ii, 1.0, 0.0).astype(BF16)]
    bias = [jnp.where(ii >= jj, zero, neg), jnp.where(ii <= jj, zero, neg)]

    g = g_ref[0]
    for part in range(1, ML_XT):
        g = g + g_ref[part]
    col = lax.broadcasted_iota(jnp.int32, g.shape, 1)

    def gate_column(j):
        return jnp.sum(jnp.where(col == j, g, 0.0), axis=1, keepdims=True)


    for i in range(n):
        for j in range(n):
            s_ref[i, j] = _dot_nt(q_ref[chunk(i), :], k_ref[chunk(j), :])

    G = []
    for d in range(2):
        order = list(range(n)) if d == 0 else list(reversed(range(n)))
        gi_all = gate_column(d * 2 * ML_HEADS + h)
        gf_all = _log_sigmoid(gate_column(d * 2 * ML_HEADS + ML_HEADS + h))
        m0 = jnp.full((1, 1), m0_ref[(b * 2 + d) * ML_HEADS + h], F32) if state_in else jnp.zeros((1, 1), F32)
        c_col, c_row, b_col, m_row = [None] * n, [None] * n, [None] * n, [None] * n
        offset = jnp.zeros((1, 1), F32)
        carry = m0
        for c in order:
            gi = gi_all[chunk(c)]
            gf = gf_all[chunk(c)]
            cum = _dot_split(tri[d], jnp.broadcast_to(gf, (L, L))) + offset
            offset = offset + jnp.sum(gf, axis=0, keepdims=True)
            cc = gi - cum
            c_col[c] = cc[:, :1]
            c_row[c] = jnp.transpose(cc)
            b_col[c] = cum[:, :1]
            m_row[c] = jnp.maximum(jnp.max(c_row[c] + bias[d], axis=1, keepdims=True), carry)
            carry = jnp.maximum(carry, jnp.max(c_col[c], axis=0, keepdims=True))
        G.append(dict(order=order, c_col=c_col, c_row=c_row, b_col=b_col, m_row=m_row, m0=m0,
                      m_end=carry, b_end=offset))

    done = set()
    for i in range(n):
        for d in range(2):
            e = G[d]
            js = list(range(0, i + 1)) if d == 0 else list(range(i, n))
            m_i = e["m_row"][i]
            parts = []
            tot = None
            for j in js:
                z = e["c_row"][j] - m_i
                if j == i:
                    z = z + bias[d]
                sw = s_ref[i, j] * (jnp.exp(z) * scale)
                tot = sw if tot is None else tot + sw
                parts.append(sw.astype(BF16))
            num = _dot(jnp.concatenate(parts, axis=1), v_ref[js[0] * L:(js[-1] + 1) * L, :])
            den = jnp.sum(tot, axis=1, keepdims=True)
            if state_in:
                qc = q_ref[chunk(i), :]
                w0 = jnp.exp(e["m0"] - m_i)
                num = num + _dot((qc.astype(F32) * w0).astype(BF16), c0_ref[d].astype(BF16))
                n0 = jnp.broadcast_to(n0_ref[d], (8, ML_HD)).astype(BF16)
                den = den + w0 * _dot_nt(qc, n0)[:, :1]
            hb = num / jnp.maximum(jnp.abs(den), jnp.exp(-(e["b_col"][i] + m_i)))
            if i not in done:
                done.add(i)
                acc_ref[chunk(i), :] = hb
            else:
                cell = _group_norm((acc_ref[chunk(i), :] + hb) * op_ref[chunk(i), :].astype(F32), gn_ref[...])
                mixed = cell + sk_ref[...] * xc_ref[chunk(i), :].astype(F32)
                o_ref[chunk(i), :] = (mixed * z_ref[chunk(i), :].astype(F32)).astype(BF16)

    if state_out:
        ones = jnp.ones((8, T), BF16)
        for d in range(2):
            e = G[d]
            kw = jnp.concatenate(
                [(k_ref[chunk(c), :].astype(F32) * (jnp.exp(e["c_col"][c] - e["m_end"]) * scale)).astype(BF16)
                 for c in range(n)], axis=0)
            C = _dot_tn(kw, v_ref[...])
            nvec = _dot(ones, kw)[:1]
            if state_in:
                w_end = jnp.exp(e["m0"] - e["m_end"])
                C = C + w_end * c0_ref[d]
                nvec = nvec + w_end * n0_ref[d]
            cn_ref[d] = C
            nn_ref[d] = nvec
            mn_ref[d] = jnp.broadcast_to(e["b_end"] + e["m_end"], mn_ref.shape[1:])


def _ml_scan(oz, q, k, v, xc, g, gn, skip, T, nb, row_blk0, state=None, want_state=False):
    H = ML_HEADS
    n = T // CHUNK
    tok = lambda b, h: (row_blk0 + b, h)
    head_vec = pl.BlockSpec((1, ML_HD), lambda b, h: (0, h))
    both = lambda *tail: pl.BlockSpec((None, 2, None, *tail), lambda b, h: (b, 0, h, 0, 0))
    in_specs, args = [], []
    if state is not None:
        c0, n0, m0 = state
        in_specs.append(pl.BlockSpec(memory_space=pltpu.SMEM))
        args.append(m0.reshape(nb * 2 * H))
    in_specs += [pl.BlockSpec((T, ML_HD), tok)] * 3
    in_specs += [pl.BlockSpec((ML_XT, T, g.shape[2]), lambda b, h: (0, row_blk0 + b, 0)),
                 pl.BlockSpec((T, ML_HD), tok),
                 pl.BlockSpec((T, ML_HD), lambda b, h: (row_blk0 + b, H + h)),
                 pl.BlockSpec((T, ML_HD), tok), head_vec, head_vec]
    args += [q, k, v, g, oz, oz, xc, gn.reshape(1, ML_WIDTH), skip.reshape(1, ML_WIDTH)]
    if state is not None:
        in_specs += [both(ML_HD, ML_HD), both(1, ML_HD)]
        args += [c0, n0.reshape(nb, 2, H, 1, ML_HD)]
    out_specs = [pl.BlockSpec((T, ML_HD), lambda b, h: (b, h))]
    out_shape = [jax.ShapeDtypeStruct((nb * T, ML_WIDTH), BF16)]
    if want_state:
        out_specs += [both(ML_HD, ML_HD), both(1, ML_HD), both(1, 128)]
        out_shape += [jax.ShapeDtypeStruct((nb, 2, H, ML_HD, ML_HD), F32),
                      jax.ShapeDtypeStruct((nb, 2, H, 1, ML_HD), F32),
                      jax.ShapeDtypeStruct((nb, 2, H, 1, 128), F32)]
    scratch = [pltpu.VMEM((T, ML_HD), F32), pltpu.VMEM((n, n, CHUNK, CHUNK), F32)]
    return pl.pallas_call(
        functools.partial(_ml_scan_kernel, T=T, state_in=state is not None, state_out=want_state),
        grid=(nb, H),
        in_specs=in_specs, out_specs=out_specs, out_shape=out_shape, scratch_shapes=scratch,
        compiler_params=_params("arbitrary", "arbitrary"),
        name="mlstm_scan",
    )(*args)


def _att_rope(x, cos, sin):
    lane = lax.broadcasted_iota(jnp.int32, x.shape, 1)
    half = ATT_HD // 2
    rot = jnp.where((lane & (ATT_HD - 1)) < half, pltpu.roll(x, 128 - half, 1), pltpu.roll(x, half, 1))
    return x * cos + rot * sin


def _att_heads(qms, sinks, key_sets):
    scores = [[_dot_nt(k_both, qm) if bias is None else _dot_nt(k_both, qm) + bias
               for k_both, _, bias in key_sets] for qm in qms]
    ms = []
    for sink, per_set in zip(sinks, scores):
        m = sink
        for s in per_set:
            m = jnp.maximum(m, jnp.max(s, axis=0, keepdims=True))
        ms.append(m)
    probs = [[jnp.exp(s - m) for s in per_set] for m, per_set in zip(ms, scores)]
    outs = []
    for sink, m, per_set in zip(sinks, ms, probs):
        den = jnp.exp(sink - m)
        out = None
        for p, (_, v_t, _) in zip(per_set, key_sets):
            den = den + jnp.sum(p, axis=0, keepdims=True)
            o = _dot(v_t, p.astype(BF16))
            out = o if out is None else out + o
        outs.append(out * (1.0 / den))
    return outs


def _att_prepare_kv(k, v, kb_ref, vt_ref, kv, nblk):
    lane = lax.broadcasted_iota(jnp.int32, k.shape, 1)
    native = lane < ATT_HD if kv % 2 == 0 else lane >= ATT_HD
    kb_ref[kv] = jnp.where(native, k, pltpu.roll(k, ATT_HD, 1)).astype(BF16)
    r0 = (kv % 2) * ATT_HD
    blk = k.shape[0] // nblk
    for c in range(nblk):
        vt_ref[kv, c] = jnp.transpose(v[c * blk:(c + 1) * blk, :])[r0:r0 + ATT_HD, :].astype(BF16)


def _att_group(q_ref, z0_ref, z1_ref, o_ref, sink_ref, rows, kv, key_sets):
    lo = lax.broadcasted_iota(jnp.int32, (CHUNK, 128), 1) < ATT_HD
    zero = jnp.zeros((CHUNK, 128), BF16)
    qms, sinks = [], []
    for g in range(ATT_GROUP):
        col = (2 * kv + g // 2) * 128
        q = q_ref[rows, col:col + 128]
        qms.append(jnp.where(lo, q, zero) if g % 2 == 0 else jnp.where(lo, zero, q))
        sinks.append(jnp.full((1, CHUNK), sink_ref[kv * ATT_GROUP + g], F32))
    heads = _att_heads(qms, sinks, key_sets)
    for p in range(2):
        col = (2 * kv + p) * 128
        out = jnp.transpose(jnp.concatenate(heads[2 * p:2 * p + 2], axis=0))
        z_ref = z0_ref if col < ATT_WIDTH // 2 else z1_ref
        zc = col % (ATT_WIDTH // 2)
        zf = z_ref[rows, zc:zc + 128].astype(F32)
        o_ref[rows, col:col + 128] = (out * zf).astype(BF16)


def _att_ctx_kernel(sink_ref, q_ref, k_ref, v_ref, z0_ref, z1_ref, o_ref, qs_ref, kb_ref, vt_ref):
    T = q_ref.shape[0]
    qs_ref[...] = q_ref[...] * (ATT_HD ** -0.5)
    for kv in range(ATT_KV):
        slab = (kv // 2) * 128
        _att_prepare_kv(k_ref[:, slab:slab + 128].astype(F32), v_ref[:, slab:slab + 128].astype(F32),
                        kb_ref, vt_ref, kv, 1)
    for c in range(T // CHUNK):
        rows = slice(c * CHUNK, (c + 1) * CHUNK)
        for kv in range(ATT_KV):
            _att_group(qs_ref, z0_ref, z1_ref, o_ref, sink_ref, rows, kv, [(kb_ref[kv], vt_ref[kv, 0], None)])


def _att_ctx(proj, sink):
    T = SEQ
    half = ATT_WIDTH // 2
    return pl.pallas_call(
        _att_ctx_kernel,
        grid=(BATCH,),
        in_specs=[pl.BlockSpec(memory_space=pltpu.SMEM),
                  pl.BlockSpec((T, ATT_WIDTH), lambda b: (b, 0)),
                  pl.BlockSpec((T, ATT_KVW), lambda b: (b, ATT_WIDTH // ATT_KVW)),
                  pl.BlockSpec((T, ATT_KVW), lambda b: (b, ATT_WIDTH // ATT_KVW + 1)),
                  pl.BlockSpec((T, half), lambda b: (b, (ATT_WIDTH + 2 * ATT_KVW) // half)),
                  pl.BlockSpec((T, half), lambda b: (b, (ATT_WIDTH + 2 * ATT_KVW) // half + 1))],
        out_specs=pl.BlockSpec((T, ATT_WIDTH), lambda b: (b, 0)),
        out_shape=jax.ShapeDtypeStruct((N_PROMPT, ATT_WIDTH), BF16),
        scratch_shapes=[pltpu.VMEM((T, ATT_WIDTH), BF16), pltpu.VMEM((ATT_KV, T, 128), BF16),
                        pltpu.VMEM((ATT_KV, 1, ATT_HD, T), BF16)],
        compiler_params=_params("arbitrary"),
        name="att_ctx",
    )(sink, proj, proj, proj, proj, proj)


def _att_win_kernel(sink_ref, q_ref, k_ref, v_ref, z0_ref, z1_ref, kc_ref, vc_ref, cos_ref, sin_ref,
                    o_ref, qs_ref, kb_ref, vt_ref, kcb_ref, vct_ref):
    T = q_ref.shape[0]
    L = CHUNK
    n = T // L
    cos = cos_ref[...]
    sin = sin_ref[...]
    for c0 in range(0, ATT_WIDTH, 128):
        q = _att_rope(q_ref[:, c0:c0 + 128].astype(F32), cos, sin)
        qs_ref[:, c0:c0 + 128] = (q * (ATT_HD ** -0.5)).astype(BF16)
    for kv in range(ATT_KV):
        slab = (kv // 2) * 128
        _att_prepare_kv(_att_rope(k_ref[:, slab:slab + 128].astype(F32), cos, sin),
                        v_ref[:, slab:slab + 128].astype(F32), kb_ref, vt_ref, kv, n)
        _att_prepare_kv(kc_ref[:, slab:slab + 128], vc_ref[:, slab:slab + 128], kcb_ref, vct_ref, kv, 1)

    jj = lax.broadcasted_iota(jnp.int32, (L, L), 0)
    ii = lax.broadcasted_iota(jnp.int32, (L, L), 1)
    neg = jnp.full((L, L), -jnp.inf, F32)
    bias_prev = jnp.where(jj >= ii, 0.0, neg)
    bias_next = jnp.where(jj <= ii, 0.0, neg)

    def body(c, carry):
        rows = pl.ds(pl.multiple_of(c * L, L), L)
        c_prev = jnp.maximum(c - 1, 0)
        c_next = jnp.minimum(c + 1, n - 1)
        b_prev = jnp.where(c > 0, bias_prev, neg)
        b_next = jnp.where(c < n - 1, bias_next, neg)
        for kv in range(ATT_KV):
            def keys(cb):
                return kb_ref[kv, pl.ds(pl.multiple_of(cb * L, L), L), :]
            key_sets = [(keys(c_prev), vt_ref[kv, c_prev], b_prev),
                        (keys(c), vt_ref[kv, c], None),
                        (keys(c_next), vt_ref[kv, c_next], b_next),
                        (kcb_ref[kv], vct_ref[kv, 0], None)]
            _att_group(qs_ref, z0_ref, z1_ref, o_ref, sink_ref, rows, kv, key_sets)
        return carry

    lax.fori_loop(0, n, body, 0)


def _att_win(proj, sink, k_ctx, v_ctx, cos, sin):
    T = DEC_SEQ
    half = ATT_WIDTH // 2
    rb = N_PROMPT // T
    return pl.pallas_call(
        _att_win_kernel,
        grid=(DEC_BATCH,),
        in_specs=[pl.BlockSpec(memory_space=pltpu.SMEM),
                  pl.BlockSpec((T, ATT_WIDTH), lambda b: (rb + b, 0)),
                  pl.BlockSpec((T, ATT_KVW), lambda b: (rb + b, ATT_WIDTH // ATT_KVW)),
                  pl.BlockSpec((T, ATT_KVW), lambda b: (rb + b, ATT_WIDTH // ATT_KVW + 1)),
                  pl.BlockSpec((T, half), lambda b: (rb + b, (ATT_WIDTH + 2 * ATT_KVW) // half)),
                  pl.BlockSpec((T, half), lambda b: (rb + b, (ATT_WIDTH + 2 * ATT_KVW) // half + 1)),
                  pl.BlockSpec((None, PAST_LEN, ATT_KVW), lambda b: (b, 0, 0)),
                  pl.BlockSpec((None, PAST_LEN, ATT_KVW), lambda b: (b, 0, 0)),
                  pl.BlockSpec((T, 128), lambda b: (0, 0)),
                  pl.BlockSpec((T, 128), lambda b: (0, 0))],
        out_specs=pl.BlockSpec((T, ATT_WIDTH), lambda b: (b, 0)),
        out_shape=jax.ShapeDtypeStruct((N_LATENT, ATT_WIDTH), BF16),
        scratch_shapes=[pltpu.VMEM((T, ATT_WIDTH), BF16),
                        pltpu.VMEM((ATT_KV, T, 128), BF16),
                        pltpu.VMEM((ATT_KV, T // CHUNK, ATT_HD, CHUNK), BF16),
                        pltpu.VMEM((ATT_KV, PAST_LEN, 128), BF16),
                        pltpu.VMEM((ATT_KV, 1, ATT_HD, PAST_LEN), BF16)],
        compiler_params=_params("arbitrary"),
        name="att_win",
    )(sink, proj, proj, proj, proj, proj, k_ctx, v_ctx, cos, sin)


def _rope_tables(T, hd, reps):
    rows = T // GRID_W
    row = np.repeat(np.arange(rows, dtype=np.float64), GRID_W)
    col = np.tile(np.arange(GRID_W, dtype=np.float64), rows)
    nf = hd // 4
    inv = ROPE_BASE ** (-np.arange(nf, dtype=np.float64) / nf)
    ang = np.concatenate([row[:, None] * inv[None, :], col[:, None] * inv[None, :]], axis=-1)
    cos, sin = np.cos(ang), np.sin(ang)
    return (jnp.asarray(np.tile(np.concatenate([cos, cos], axis=-1), (1, reps)), F32),
            jnp.asarray(np.tile(np.concatenate([-sin, sin], axis=-1), (1, reps)), F32))


def _retention_layer(h, w_in, decay_f, decay_b, gn, state, rope):
    z0 = 2 * RET_QK + RET_WIDTH
    proj = _in_proj(h, w_in, 1024, (z0, z0))
    dec = jnp.stack([decay_f, decay_b]).astype(F32)
    a_p, new_state = _retention(proj, dec, gn, SEQ, BATCH, 0, want_state=True)
    (a_s,) = _retention(proj, dec, gn, DEC_SEQ, DEC_BATCH, N_PROMPT // DEC_SEQ, rope=rope, state=state)
    return a_p, a_s, new_state


def kernel(x_prompt, x_sample, c, c_ctx, state_l0_ret, state_l1_C, state_l1_n, state_l1_m, cache_l2_k, cache_l2_v, state_l3_ret, norm_l0, ada_w_l0, ada_b_l0, w_in_l0, w_out_l0, ret_decay_f_l0, ret_decay_b_l0, ret_gn_l0, norm_l1, ada_w_l1, ada_b_l1, w_in_l1, w_out_l1, conv_w_l1, conv_b_l1, wq_l1, wk_l1, wv_l1, wif_f_l1, bif_f_l1, wif_b_l1, bif_b_l1, gn_l1, skip_l1, norm_l2, ada_w_l2, ada_b_l2, w_in_l2, w_out_l2, sink_l2, norm_l3, ada_w_l3, ada_b_l3, w_in_l3, w_out_l3, ret_decay_f_l3, ret_decay_b_l3, ret_gn_l3, final_norm):
    ct = jnp.concatenate([c_ctx[:, None], c.T, jnp.zeros((D, MOD_ROWS - 1 - DEC_BATCH), F32)], axis=1)
    mods = [_adaln(ct, w, b) for w, b in ((ada_w_l0, ada_b_l0), (ada_w_l1, ada_b_l1),
                                          (ada_w_l2, ada_b_l2), (ada_w_l3, ada_b_l3))]
    rope_ret = _rope_tables(DEC_SEQ, RET_DK, 1)
    rope_att = _rope_tables(DEC_SEQ, ATT_HD, 2)

    x, h = _norm_mod(x_prompt.reshape(N_PROMPT, D), x_sample.reshape(N_LATENT, D), norm_l0, mods[0])

    a_p, a_s, new_l0_ret = _retention_layer(h, w_in_l0, ret_decay_f_l0, ret_decay_b_l0, ret_gn_l0, state_l0_ret,
                                            rope_ret)
    x, h = _out_proj(a_p, a_s, w_out_l0, x, mods[0], norm_l1, mods[1], False)

    wif = jnp.concatenate([wif_f_l1, wif_b_l1], axis=1)
    bif = jnp.concatenate([bif_f_l1, bif_b_l1])
    oz, q, k, v, xc, g = _ml_in_proj(h, w_in_l1, conv_w_l1, conv_b_l1, wq_l1, wk_l1, wv_l1, wif, bif)
    a_p, new_l1_C, n_new, m_new = _ml_scan(oz, q, k, v, xc, g, gn_l1, skip_l1, SEQ, BATCH, 0, want_state=True)
    (a_s,) = _ml_scan(oz, q, k, v, xc, g, gn_l1, skip_l1, DEC_SEQ, DEC_BATCH, N_PROMPT // DEC_SEQ,
                      state=(state_l1_C, state_l1_n, state_l1_m))
    new_l1_n = n_new.reshape(BATCH, 2, ML_HEADS, ML_HD)
    new_l1_m = m_new[:, :, :, 0, 0]
    x, h = _out_proj(a_p, a_s, w_out_l1, x, mods[1], norm_l2, mods[2], False)

    proj = _in_proj(h, w_in_l2, 1280, (ATT_WIDTH + 2 * ATT_KVW,) * 2)
    kv_new = proj[:N_PROMPT, ATT_WIDTH:ATT_WIDTH + 2 * ATT_KVW].astype(F32)
    new_l2_k = kv_new[:, :ATT_KVW].reshape(BATCH, SEQ, ATT_KV, ATT_HD)
    new_l2_v = kv_new[:, ATT_KVW:].reshape(BATCH, SEQ, ATT_KV, ATT_HD)
    a_p = _att_ctx(proj, sink_l2)
    a_s = _att_win(proj, sink_l2, cache_l2_k.reshape(DEC_BATCH, PAST_LEN, ATT_KVW),
                   cache_l2_v.reshape(DEC_BATCH, PAST_LEN, ATT_KVW), rope_att[0], rope_att[1])
    x, h = _out_proj(a_p, a_s, w_out_l2, x, mods[2], norm_l3, mods[3], False)

    a_p, a_s, new_l3_ret = _retention_layer(h, w_in_l3, ret_decay_f_l3, ret_decay_b_l3, ret_gn_l3, state_l3_ret,
                                            rope_ret)
    (y_p,) = _out_proj(a_p, a_s, w_out_l3, x, mods[3], final_norm, mods[3], True, rows=(0, N_PROMPT))
    (y_s,) = _out_proj(a_p, a_s, w_out_l3, x, mods[3], final_norm, mods[3], True, rows=(N_PROMPT, N_LATENT))

    y_prompt = y_p.reshape(BATCH, SEQ, D)
    y_sample = y_s.reshape(DEC_BATCH, DEC_SEQ, D)
    return (y_prompt, y_sample, new_l0_ret, new_l1_C, new_l1_n, new_l1_m, new_l2_k, new_l2_v, new_l3_ret)
```

```python
import functools

import jax
import jax.numpy as jnp
import numpy as np
from jax import lax
from jax.experimental import pallas as pl
from jax.experimental.pallas import tpu as pltpu

F32 = jnp.float32
BF16 = jnp.bfloat16

D = 1024
BATCH = 16
SEQ = 256
DEC_BATCH = 2
DEC_SEQ = 1024
PAST_LEN = 512
GRID_W = 64
CHUNK = 128
EPS = 1e-6
ROPE_BASE = 10000.0

N_PROMPT = BATCH * SEQ
N_LATENT = DEC_BATCH * DEC_SEQ
N_TOK = N_PROMPT + N_LATENT
MOD_ROWS = 8

RET_HEADS = 8
RET_DK = 128
RET_DV = 256
RET_QK = RET_HEADS * RET_DK
RET_WIDTH = RET_HEADS * RET_DV
RET_HB_PROMPT = 4
RET_HB_LATENT = 1

ML_HEADS = 4
ML_WIDTH = 2 * D
ML_HD = ML_WIDTH // ML_HEADS
ML_BLOCK = 4
ML_CONV = 5
ML_TILE = 256
ML_ROWS = 1024
ML_TN = 1024
ML_XT = ML_WIDTH // ML_TN
ML_EDGE_ROWS = sorted({r for e in range(0, ML_ROWS + 1, SEQ) for r in (e - 8, e) if 0 <= r < ML_ROWS})

ATT_HEADS = 16
ATT_KV = 4
ATT_HD = 64
ATT_GROUP = ATT_HEADS // ATT_KV
ATT_WIDTH = ATT_HEADS * ATT_HD
ATT_KVW = ATT_KV * ATT_HD
WINDOW = 128

OUT_SUB = 256
IN_SUB = 4

VMEM_LIMIT = 56 * 1024 * 1024


def _params(*sem):
    return pltpu.CompilerParams(dimension_semantics=sem, vmem_limit_bytes=VMEM_LIMIT)


def _mod_row(row0):
    return jnp.maximum((row0 - N_PROMPT) // DEC_SEQ + 1, 0)


def _dot(a, b):
    return jnp.dot(a, b, preferred_element_type=F32)


def _dot_nt(a, b):
    return lax.dot_general(a, b, (((1,), (1,)), ((), ())), preferred_element_type=F32)


def _dot_tn(a, b):
    return lax.dot_general(a, b, (((0,), (0,)), ((), ())), preferred_element_type=F32)


def _silu(x):
    return x * jax.nn.sigmoid(x)


def _log_sigmoid(x):
    return jnp.minimum(x, 0.0) - jnp.log1p(jnp.exp(-jnp.abs(x)))


def _rms(x, g):
    return x * lax.rsqrt(jnp.mean(x * x, axis=-1, keepdims=True) + EPS) * g


def _group_norm(x, g):
    xc = x - jnp.mean(x, axis=-1, keepdims=True)
    return xc * lax.rsqrt(jnp.mean(xc * xc, axis=-1, keepdims=True) + EPS) * g


def _two_source_specs(tm, width, tile0=0):
    n_p = N_PROMPT // tm
    return (pl.BlockSpec((tm, width), lambda i: (jnp.minimum(tile0 + i, n_p - 1), 0)),
            pl.BlockSpec((tm, width), lambda i: (jnp.maximum(tile0 + i - n_p, 0), 0)))


def _pick(i, tm, p_ref, s_ref):
    return jnp.where(i < N_PROMPT // tm, p_ref[...], s_ref[...])


def _ada_kernel(ct_ref, w_ref, b_ref, o_ref):
    s = _silu(ct_ref[...])
    w = w_ref[...]
    rows = [jnp.sum(w * s[:, r:r + 1], axis=0, keepdims=True) for r in range(1 + DEC_BATCH)]
    rows.append(jnp.zeros((MOD_ROWS - len(rows), w.shape[1]), F32))
    o_ref[...] = jnp.concatenate(rows, axis=0) + b_ref[...]


def _adaln(ct, w, b):
    tn = 1024
    return pl.pallas_call(
        _ada_kernel,
        grid=(3 * D // tn,),
        in_specs=[pl.BlockSpec((D, MOD_ROWS), lambda j: (0, 0)),
                  pl.BlockSpec((D, tn), lambda j: (0, j)),
                  pl.BlockSpec((1, tn), lambda j: (0, j))],
        out_specs=pl.BlockSpec((MOD_ROWS, tn), lambda j: (0, j)),
        out_shape=jax.ShapeDtypeStruct((MOD_ROWS, 3 * D), F32),
        compiler_params=_params("arbitrary"),
        name="adaln",
    )(ct, w, b.reshape(1, 3 * D))


def _norm_mod_kernel(xp_ref, xs_ref, g_ref, mod_ref, x_ref, h_ref, *, tm):
    i = pl.program_id(0)
    m = mod_ref[pl.ds(_mod_row(i * tm), 1), :]
    x = _pick(i, tm, xp_ref, xs_ref)
    x_ref[...] = x
    y = _rms(x, g_ref[...])
    h_ref[...] = (y * (1.0 + m[:, D:2 * D]) + m[:, :D]).astype(BF16)


def _norm_mod(xp, xs, g, mod):
    tm = 512
    return pl.pallas_call(
        functools.partial(_norm_mod_kernel, tm=tm),
        grid=(N_TOK // tm,),
        in_specs=[*_two_source_specs(tm, D),
                  pl.BlockSpec((1, D), lambda i: (0, 0)),
                  pl.BlockSpec((MOD_ROWS, 3 * D), lambda i: (0, 0))],
        out_specs=[pl.BlockSpec((tm, D), lambda i: (i, 0))] * 2,
        out_shape=[jax.ShapeDtypeStruct((N_TOK, D), F32), jax.ShapeDtypeStruct((N_TOK, D), BF16)],
        compiler_params=_params("arbitrary"),
        name="norm_mod",
    )(xp, xs, g.reshape(1, D), mod)


def _in_proj_kernel(h_ref, w_ref, o_ref, wb_ref, *, tn, n, acts):
    j = pl.program_id(0)

    @pl.when(pl.program_id(1) == 0)
    def _():
        wb_ref[...] = w_ref[...].astype(BF16)

    sig0, silu0 = acts
    gated = (j + 1) * tn > min(sig0, silu0)

    @pl.when(jnp.logical_not(gated))
    def _():
        o_ref[...] = _dot(h_ref[...], wb_ref[...]).astype(BF16)

    @pl.when(gated)
    def _():
        sub = h_ref.shape[0] // IN_SUB
        for s in range(IN_SUB):
            rows = slice(s * sub, (s + 1) * sub)
            acc = _dot(h_ref[rows, :], wb_ref[...])
            col = j * tn + lax.broadcasted_iota(jnp.int32, acc.shape, 1)
            sig = jax.nn.sigmoid(acc)
            out = jnp.where(col >= silu0, acc * sig, jnp.where(col >= sig0, sig, acc))
            o_ref[rows, :] = out.astype(BF16)


def _in_proj(h, w, tn, acts):
    tm = 1024
    n = w.shape[1]
    return pl.pallas_call(
        functools.partial(_in_proj_kernel, tn=tn, n=n, acts=acts),
        grid=(n // tn, N_TOK // tm),
        in_specs=[pl.BlockSpec((tm, D), lambda j, i: (i, 0)),
                  pl.BlockSpec((D, tn), lambda j, i: (0, j))],
        out_specs=pl.BlockSpec((tm, tn), lambda j, i: (i, j)),
        out_shape=jax.ShapeDtypeStruct((N_TOK, n), BF16),
        scratch_shapes=[pltpu.VMEM((D, tn), BF16)],
        compiler_params=_params("arbitrary", "arbitrary"),
        name="in_proj",
    )(h, w)


def _out_proj_kernel(ap_ref, as_ref, w_ref, x_ref, mod_ref, g_ref, modn_ref, *rest, tm, tile0, final):
    wb_ref = rest[-1]

    @pl.when(pl.program_id(0) == 0)
    def _():
        wb_ref[...] = w_ref[...].astype(BF16)

    i = tile0 + pl.program_id(0)
    r = _mod_row(i * tm)
    gate = mod_ref[pl.ds(r, 1), :][:, 2 * D:]
    from_prompt = i < N_PROMPT // tm
    for s in range(tm // OUT_SUB):
        rows = slice(s * OUT_SUB, (s + 1) * OUT_SUB)
        a = jnp.where(from_prompt, ap_ref[rows, :], as_ref[rows, :])
        xn = x_ref[rows, :] + gate * _dot(a, wb_ref[...])
        y = _rms(xn, g_ref[...])
        if final:
            rest[0][rows, :] = y
        else:
            mn = modn_ref[pl.ds(r, 1), :]
            rest[0][rows, :] = xn
            rest[1][rows, :] = (y * (1.0 + mn[:, D:2 * D]) + mn[:, :D]).astype(BF16)


def _out_proj(a_p, a_s, w, x, mod, g_next, mod_next, final, rows=(0, N_TOK)):
    tm = 512
    kw = w.shape[0]
    tile0, nt = rows[0] // tm, rows[1] // tm
    row = pl.BlockSpec((tm, D), lambda i: (i, 0))
    if final:
        out_specs, out_shape = [row], [jax.ShapeDtypeStruct((rows[1], D), F32)]
    else:
        out_specs = [row, row]
        out_shape = [jax.ShapeDtypeStruct((rows[1], D), F32), jax.ShapeDtypeStruct((rows[1], D), BF16)]
    return pl.pallas_call(
        functools.partial(_out_proj_kernel, tm=tm, tile0=tile0, final=final),
        grid=(nt,),
        in_specs=[*_two_source_specs(tm, kw, tile0),
                  pl.BlockSpec((kw, D), lambda i: (0, 0)),
                  pl.BlockSpec((tm, D), lambda i: (tile0 + i, 0)),
                  pl.BlockSpec((MOD_ROWS, 3 * D), lambda i: (0, 0)),
                  pl.BlockSpec((1, D), lambda i: (0, 0)),
                  pl.BlockSpec((MOD_ROWS, 3 * D), lambda i: (0, 0))],
        out_specs=out_specs,
        out_shape=out_shape,
        scratch_shapes=[pltpu.VMEM((kw, D), BF16)],
        compiler_params=_params("arbitrary"),
        name="out_proj",
    )(a_p, a_s, w, x, mod, g_next.reshape(1, D), mod_next)


def _ret_kernel(*refs, T, HB, rope, state_in, state_out):
    refs = list(refs)
    dec_ref, q_ref, k_ref, v_ref, z_ref, gn_ref = refs[:6]
    pos = 6
    if rope:
        cos_ref, sin_ref = refs[pos:pos + 2]
        pos += 2
    if state_in:
        s0_ref = refs[pos]
        pos += 1
    o_ref = refs[pos]
    pos += 1
    if state_out:
        sn_ref = refs[pos]
        pos += 1
    kv_ref, sf_ref, sb_ref, tab_ref, gblk_ref = refs[pos:pos + 5]
    pos += 5
    if rope:
        kr_ref = refs[pos]

    hg = pl.program_id(1)
    n = T // CHUNK
    L = CHUNK
    ii = lax.broadcasted_iota(jnp.int32, (L, L), 0).astype(F32)
    jj = lax.broadcasted_iota(jnp.int32, (L, L), 1).astype(F32)
    scale = RET_DK ** -0.5

    def chunk(c):
        return slice(c * L, (c + 1) * L)

    def rotate(x, c):
        return x * cos_ref[chunk(c), :] + pltpu.roll(x, RET_DK // 2, 1) * sin_ref[chunk(c), :]

    @pl.when(pl.program_id(0) == 0)
    def _():
        for hh in range(HB):
            head = hg * HB + hh
            lg_f = _log_sigmoid(jnp.full((1, RET_DV), dec_ref[0, head], F32))
            lg_b = _log_sigmoid(jnp.full((1, RET_DV), dec_ref[1, head], F32))
            lf, lb = lg_f[:, :L], lg_b[:, :L]
            tab_ref[head, 0] = jnp.exp(lf * (L - 1.0 - ii)) * scale
            tab_ref[head, 1] = jnp.exp(lb * ii) * scale
            tab_ref[head, 2] = jnp.exp(lf * (ii + 1.0))
            tab_ref[head, 3] = jnp.exp(lb * (L - ii))
            tab_ref[head, 4] = (jnp.where(ii >= jj, jnp.exp(lf * jnp.maximum(ii - jj, 0.0)), 0.0)
                                + jnp.where(jj >= ii, jnp.exp(lb * jnp.maximum(jj - ii, 0.0)), 0.0)) * scale
            gblk_ref[head, 0] = jnp.exp(lg_f * float(L))
            gblk_ref[head, 1] = jnp.exp(lg_b * float(L))

    for hh in range(HB):
        head = hg * HB + hh
        qs = slice(hh * RET_DK, (hh + 1) * RET_DK)
        vs = slice(hh * RET_DV, (hh + 1) * RET_DV)
        k_dec_f, k_dec_b, q_dec_f, q_dec_b, mask = (tab_ref[head, t] for t in range(5))
        g_f = gblk_ref[head, 0]
        g_b = gblk_ref[head, 1]

        for c in range(n):
            kc = k_ref[chunk(c), qs].astype(F32)
            if rope:
                kc = rotate(kc, c)
                kr_ref[chunk(c), qs] = kc.astype(BF16)
            kk = jnp.concatenate([kc * k_dec_f, kc * k_dec_b], axis=1).astype(BF16)
            kv_ref[hh, c] = _dot_tn(kk, v_ref[chunk(c), vs])
        S = s0_ref[0, hh] if state_in else None
        has_f = []
        for c in range(n):
            has_f.append(S is not None)
            if S is not None:
                sf_ref[hh, c] = S.astype(BF16)
            kvc = kv_ref[hh, c, :RET_DK, :]
            S = kvc if S is None else S * g_f + kvc
        if state_out:
            sn_ref[0, hh] = S
        S = s0_ref[1, hh] if state_in else None
        has_b = [False] * n
        for c in reversed(range(n)):
            has_b[c] = S is not None
            if S is not None:
                sb_ref[hh, c] = S.astype(BF16)
            kvc = kv_ref[hh, c, RET_DK:, :]
            S = kvc if S is None else S * g_b + kvc
        if state_out:
            sn_ref[1, hh] = S

        for c in range(n):
            qc = q_ref[chunk(c), qs]
            qf = qc.astype(F32)
            if rope:
                qf = rotate(qf, c)
                qc = qf.astype(BF16)
                kc = kr_ref[chunk(c), qs]
            else:
                kc = k_ref[chunk(c), qs]
            lhs = [(_dot_nt(qc, kc) * mask).astype(BF16)]
            rhs = [v_ref[chunk(c), vs]]
            if has_f[c]:
                lhs.append((qf * q_dec_f).astype(BF16))
                rhs.append(sf_ref[hh, c])
            if has_b[c]:
                lhs.append((qf * q_dec_b).astype(BF16))
                rhs.append(sb_ref[hh, c])
            o = _dot(jnp.concatenate(lhs, axis=1), jnp.concatenate(rhs, axis=0))
            zf = z_ref[chunk(c), vs].astype(F32)
            o_ref[chunk(c), vs] = (_group_norm(o, gn_ref[:, vs]) * zf).astype(BF16)


def _retention(proj, dec, gn, T, nb, row_blk0, rope=None, state=None, want_state=False):
    HB = RET_HB_PROMPT if T == SEQ else RET_HB_LATENT
    ng = RET_HEADS // HB
    n = T // CHUNK
    qw, vw = HB * RET_DK, HB * RET_DV
    in_specs = [pl.BlockSpec(memory_space=pltpu.SMEM),
                pl.BlockSpec((T, qw), lambda b, h: (row_blk0 + b, h)),
                pl.BlockSpec((T, qw), lambda b, h: (row_blk0 + b, ng + h)),
                pl.BlockSpec((T, vw), lambda b, h: (row_blk0 + b, ng + h)),
                pl.BlockSpec((T, vw), lambda b, h: (row_blk0 + b, 2 * ng + h)),
                pl.BlockSpec((1, vw), lambda b, h: (0, h))]
    args = [dec, proj, proj, proj, proj, gn.reshape(1, RET_WIDTH)]
    if rope is not None:
        in_specs += [pl.BlockSpec((T, RET_DK), lambda b, h: (0, 0))] * 2
        args += list(rope)
    if state is not None:
        in_specs.append(pl.BlockSpec((None, 2, HB, RET_DK, RET_DV), lambda b, h: (b, 0, h, 0, 0)))
        args.append(state)
    out_specs = [pl.BlockSpec((T, vw), lambda b, h: (b, h))]
    out_shape = [jax.ShapeDtypeStruct((nb * T, RET_WIDTH), BF16)]
    if want_state:
        out_specs.append(pl.BlockSpec((None, 2, HB, RET_DK, RET_DV), lambda b, h: (b, 0, h, 0, 0)))
        out_shape.append(jax.ShapeDtypeStruct((nb, 2, RET_HEADS, RET_DK, RET_DV), F32))
    scratch = [pltpu.VMEM((HB, n, 2 * RET_DK, RET_DV), F32),
               pltpu.VMEM((HB, n, RET_DK, RET_DV), BF16),
               pltpu.VMEM((HB, n, RET_DK, RET_DV), BF16),
               pltpu.VMEM((RET_HEADS, 5, CHUNK, CHUNK), F32),
               pltpu.VMEM((RET_HEADS, 2, 1, RET_DV), F32)]
    if rope is not None:
        scratch.append(pltpu.VMEM((T, qw), BF16))
    return pl.pallas_call(
        functools.partial(_ret_kernel, T=T, HB=HB, rope=rope is not None, state_in=state is not None,
                          state_out=want_state),
        grid=(nb, ng),
        in_specs=in_specs, out_specs=out_specs, out_shape=out_shape, scratch_shapes=scratch,
        compiler_params=_params("arbitrary", "arbitrary"),
        name="retention",
    )(*args)


def _ml_in_kernel(h_ref, w_ref, cw_ref, cb_ref, bq_ref, bk_ref, bv_ref, wq_ref, wk_ref, wv_ref, bias_ref,
                  oz_ref, q_ref, k_ref, v_ref, xc_ref, g_ref, wb_ref, bd_ref, conv_ref):
    j = pl.program_id(0)
    i = pl.program_id(1)
    nsub = ML_TN // ML_TILE

    @pl.when(i == 0)
    def _():
        wb_ref[...] = w_ref[...].astype(BF16)

    @pl.when((i == 0) & (j < ML_XT))
    def _():
        row = lax.broadcasted_iota(jnp.int32, (ML_TILE, ML_TILE), 0)
        col = lax.broadcasted_iota(jnp.int32, (ML_TILE, ML_TILE), 1)
        shift = ML_BLOCK.bit_length() - 1
        same_block = jnp.right_shift(row, shift) == jnp.right_shift(col, shift)
        for which, b_ref in enumerate((bq_ref, bk_ref, bv_ref)):
            for t in range(nsub):
                w = b_ref[t * ML_TILE:(t + 1) * ML_TILE, :]
                rep = w[:, ML_BLOCK - 1:ML_BLOCK]
                for dd in range(ML_BLOCK - 1):
                    rep = jnp.where((col & (ML_BLOCK - 1)) == dd, w[:, dd:dd + 1], rep)
                bd_ref[which, t] = jnp.where(same_block, rep, 0.0).astype(BF16)

    @pl.when(j >= ML_XT)
    def _():
        sub = ML_ROWS // IN_SUB
        for s in range(IN_SUB):
            rows = slice(s * sub, (s + 1) * sub)
            acc = _dot(h_ref[rows, :], wb_ref[...])
            col = j * ML_TN + lax.broadcasted_iota(jnp.int32, acc.shape, 1)
            sig = jax.nn.sigmoid(acc)
            oz_ref[rows, :] = jnp.where(col >= 2 * ML_WIDTH, acc * sig, sig).astype(BF16)

    @pl.when(j < ML_XT)
    def _():
        acc = _dot(h_ref[...], wb_ref[...])
        T = jnp.where(i < N_PROMPT // ML_ROWS, SEQ, DEC_SEQ)
        pad = ML_CONV // 2
        taps = [tap for tap in range(ML_CONV) if tap != pad]
        sub8 = lax.broadcasted_iota(jnp.int32, (8, ML_TILE), 0)
        g = jnp.where(j == 0, 1.0, 0.0) * bias_ref[...]
        for t in range(nsub):
            cols = slice(t * ML_TILE, (t + 1) * ML_TILE)
            x = acc[:, cols]
            shifted = {tap: pltpu.roll(x, (pad - tap) % ML_ROWS, 0) for tap in taps}
            conv = cb_ref[:, cols] + x * cw_ref[pad:pad + 1, cols]
            for tap in taps:
                conv = conv + shifted[tap] * cw_ref[tap:tap + 1, cols]
            conv_ref[...] = conv
            for r0 in ML_EDGE_ROWS:
                rows = slice(r0, r0 + 8)
                pos = (r0 + sub8) & (T - 1)
                fixed = cb_ref[:, cols] + x[rows] * cw_ref[pad:pad + 1, cols]
                for tap in taps:
                    ok = (pos + (tap - pad) >= 0) & (pos + (tap - pad) < T)
                    fixed = fixed + jnp.where(ok, shifted[tap][rows], 0.0) * cw_ref[tap:tap + 1, cols]
                conv_ref[rows, :] = fixed
            xcb = _silu(conv_ref[...]).astype(BF16)
            qb = _dot(xcb, bd_ref[0, t]).astype(BF16)
            kb = _dot(xcb, bd_ref[1, t]).astype(BF16)
            vb = _dot(x.astype(BF16), bd_ref[2, t]).astype(BF16)
            q_ref[:, cols] = qb
            k_ref[:, cols] = kb
            v_ref[:, cols] = vb
            xc_ref[:, cols] = xcb
            g = g + (_dot(qb, wq_ref[cols, :].astype(BF16)) + _dot(kb, wk_ref[cols, :].astype(BF16))
                     + _dot(vb, wv_ref[cols, :].astype(BF16)))
        g_ref[...] = g


def _ml_in_proj(h, w, conv_w, conv_b, wq, wk, wv, wif, bif):
    ng = wif.shape[1]
    ni = N_TOK // ML_ROWS
    xt = ML_XT
    wq, wk, wv = (a.reshape(ML_WIDTH, ML_BLOCK) for a in (wq, wk, wv))
    xcol = lambda j, i: (0, jnp.minimum(j, xt - 1))
    xrow = lambda off: (lambda j, i: (off + jnp.minimum(j, xt - 1), 0))
    x_out = pl.BlockSpec((ML_ROWS, ML_TN), lambda j, i: (jnp.where(j < xt, i, ni - 1), jnp.minimum(j, xt - 1)))
    return pl.pallas_call(
        _ml_in_kernel,
        grid=(3 * ML_WIDTH // ML_TN, ni),
        in_specs=[pl.BlockSpec((ML_ROWS, D), lambda j, i: (i, 0)),
                  pl.BlockSpec((D, ML_TN), lambda j, i: (0, j)),
                  pl.BlockSpec((ML_CONV, ML_TN), xcol),
                  pl.BlockSpec((1, ML_TN), xcol),
                  pl.BlockSpec((ML_TN, ML_BLOCK), xrow(0)),
                  pl.BlockSpec((ML_TN, ML_BLOCK), xrow(0)),
                  pl.BlockSpec((ML_TN, ML_BLOCK), xrow(0)),
                  pl.BlockSpec((ML_TN, ng), xrow(0)),
                  pl.BlockSpec((ML_TN, ng), xrow(xt)),
                  pl.BlockSpec((ML_TN, ng), xrow(2 * xt)),
                  pl.BlockSpec((1, ng), lambda j, i: (0, 0))],
        out_specs=[pl.BlockSpec((ML_ROWS, ML_TN), lambda j, i: (jnp.where(j < xt, 0, i), jnp.maximum(j - xt, 0))),
                   x_out, x_out, x_out, x_out,
                   pl.BlockSpec((None, ML_ROWS, ng),
                                lambda j, i: (jnp.minimum(j, xt - 1), jnp.where(j < xt, i, ni - 1), 0))],
        out_shape=[jax.ShapeDtypeStruct((N_TOK, 2 * ML_WIDTH), BF16)]
        + [jax.ShapeDtypeStruct((N_TOK, ML_WIDTH), BF16)] * 4
        + [jax.ShapeDtypeStruct((xt, N_TOK, ng), F32)],
        scratch_shapes=[pltpu.VMEM((D, ML_TN), BF16), pltpu.VMEM((3, ML_TN // ML_TILE, ML_TILE, ML_TILE), BF16),
                        pltpu.VMEM((ML_ROWS, ML_TILE), F32)],
        compiler_params=_params("arbitrary", "arbitrary"),
        name="mlstm_in_proj",
    )(h, w, conv_w, conv_b.reshape(1, ML_WIDTH), wq, wk, wv, wif, wif, wif, bif.reshape(1, ng))


def _dot_split(lhs_bf16, rhs):
    r1 = rhs.astype(BF16)
    e1 = rhs - r1.astype(F32)
    r2 = e1.astype(BF16)
    r3 = (e1 - r2.astype(F32)).astype(BF16)
    return _dot(lhs_bf16, r1) + _dot(lhs_bf16, r2) + _dot(lhs_bf16, r3)


def _ml_scan_kernel(*refs, T, state_in, state_out):
    refs = list(refs)
    pos = 0
    if state_in:
        m0_ref = refs[0]
        pos = 1
    q_ref, k_ref, v_ref, g_ref, op_ref, z_ref, xc_ref, gn_ref, sk_ref = refs[pos:pos + 9]
    pos += 9
    if state_in:
        c0_ref, n0_ref = refs[pos:pos + 2]
        pos += 2
    o_ref = refs[pos]
    pos += 1
    if state_out:
        cn_ref, nn_ref, mn_ref = refs[pos:pos + 3]
        pos += 3
    acc_ref, s_ref = refs[pos:pos + 2]

    b = pl.program_id(0)
    h = pl.program_id(1)
    n = T // CHUNK
    L = CHUNK
    scale = ML_HD ** -0.5

    def chunk(c):
        return slice(c * L, (c + 1) * L)

    ii = lax.broadcasted_iota(jnp.int32, (L, L), 0)
    jj = lax.broadcasted_iota(jnp.int32, (L, L), 1)
    neg = jnp.full((L, L), -jnp.inf, F32)
    zero = jnp.zeros((L, L), F32)
    tri = [jnp.where(jj <= ii, 1.0, 0.0).astype(BF16), jnp.where(jj >= ii, 1.0, 0.0).astype(BF16)]
    bias = [jnp.where(ii >= jj, zero, neg), jnp.where(ii <= jj, zero, neg)]

    g = g_ref[0]
    for part in range(1, ML_XT):
        g = g + g_ref[part]
    col = lax.broadcasted_iota(jnp.int32, g.shape, 1)

    def gate_column(j):
        return jnp.sum(jnp.where(col == j, g, 0.0), axis=1, keepdims=True)


    for i in range(n):
        for j in range(n):
            s_ref[i, j] = _dot_nt(q_ref[chunk(i), :], k_ref[chunk(j), :])

    G = []
    for d in range(2):
        order = list(range(n)) if d == 0 else list(reversed(range(n)))
        gi_all = gate_column(d * 2 * ML_HEADS + h)
        gf_all = _log_sigmoid(gate_column(d * 2 * ML_HEADS + ML_HEADS + h))
        m0 = jnp.full((1, 1), m0_ref[(b * 2 + d) * ML_HEADS + h], F32) if state_in else jnp.zeros((1, 1), F32)
        c_col, c_row, b_col, m_row = [None] * n, [None] * n, [None] * n, [None] * n
        offset = jnp.zeros((1, 1), F32)
        carry = m0
        for c in order:
            gi = gi_all[chunk(c)]
            gf = gf_all[chunk(c)]
            cum = _dot_split(tri[d], jnp.broadcast_to(gf, (L, L))) + offset
            offset = offset + jnp.sum(gf, axis=0, keepdims=True)
            cc = gi - cum
            c_col[c] = cc[:, :1]
            c_row[c] = jnp.transpose(cc)
            b_col[c] = cum[:, :1]
            m_row[c] = jnp.maximum(jnp.max(c_row[c] + bias[d], axis=1, keepdims=True), carry)
            carry = jnp.maximum(carry, jnp.max(c_col[c], axis=0, keepdims=True))
        G.append(dict(order=order, c_col=c_col, c_row=c_row, b_col=b_col, m_row=m_row, m0=m0,
                      m_end=carry, b_end=offset))

    done = set()
    for i in range(n):
        for d in range(2):
            e = G[d]
            js = list(range(0, i + 1)) if d == 0 else list(range(i, n))
            m_i = e["m_row"][i]
            parts = []
            tot = None
            for j in js:
                z = e["c_row"][j] - m_i
                if j == i:
                    z = z + bias[d]
                sw = s_ref[i, j] * (jnp.exp(z) * scale)
                tot = sw if tot is None else tot + sw
                parts.append(sw.astype(BF16))
            num = _dot(jnp.concatenate(parts, axis=1), v_ref[js[0] * L:(js[-1] + 1) * L, :])
            den = jnp.sum(tot, axis=1, keepdims=True)
            if state_in:
                qc = q_ref[chunk(i), :]
                w0 = jnp.exp(e["m0"] - m_i)
                num = num + _dot((qc.astype(F32) * w0).astype(BF16), c0_ref[d].astype(BF16))
                n0 = jnp.broadcast_to(n0_ref[d], (8, ML_HD)).astype(BF16)
                den = den + w0 * _dot_nt(qc, n0)[:, :1]
            hb = num / jnp.maximum(jnp.abs(den), jnp.exp(-(e["b_col"][i] + m_i)))
            if i not in done:
                done.add(i)
                acc_ref[chunk(i), :] = hb
            else:
                cell = _group_norm((acc_ref[chunk(i), :] + hb) * op_ref[chunk(i), :].astype(F32), gn_ref[...])
                mixed = cell + sk_ref[...] * xc_ref[chunk(i), :].astype(F32)
                o_ref[chunk(i), :] = (mixed * z_ref[chunk(i), :].astype(F32)).astype(BF16)

    if state_out:
        ones = jnp.ones((8, T), BF16)
        for d in range(2):
            e = G[d]
            kw = jnp.concatenate(
                [(k_ref[chunk(c), :].astype(F32) * (jnp.exp(e["c_col"][c] - e["m_end"]) * scale)).astype(BF16)
                 for c in range(n)], axis=0)
            C = _dot_tn(kw, v_ref[...])
            nvec = _dot(ones, kw)[:1]
            if state_in:
                w_end = jnp.exp(e["m0"] - e["m_end"])
                C = C + w_end * c0_ref[d]
                nvec = nvec + w_end * n0_ref[d]
            cn_ref[d] = C
            nn_ref[d] = nvec
            mn_ref[d] = jnp.broadcast_to(e["b_end"] + e["m_end"], mn_ref.shape[1:])


def _ml_scan(oz, q, k, v, xc, g, gn, skip, T, nb, row_blk0, state=None, want_state=False):
    H = ML_HEADS
    n = T // CHUNK
    tok = lambda b, h: (row_blk0 + b, h)
    head_vec = pl.BlockSpec((1, ML_HD), lambda b, h: (0, h))
    both = lambda *tail: pl.BlockSpec((None, 2, None, *tail), lambda b, h: (b, 0, h, 0, 0))
    in_specs, args = [], []
    if state is not None:
        c0, n0, m0 = state
        in_specs.append(pl.BlockSpec(memory_space=pltpu.SMEM))
        args.append(m0.reshape(nb * 2 * H))
    in_specs += [pl.BlockSpec((T, ML_HD), tok)] * 3
    in_specs += [pl.BlockSpec((ML_XT, T, g.shape[2]), lambda b, h: (0, row_blk0 + b, 0)),
                 pl.BlockSpec((T, ML_HD), tok),
                 pl.BlockSpec((T, ML_HD), lambda b, h: (row_blk0 + b, H + h)),
                 pl.BlockSpec((T, ML_HD), tok), head_vec, head_vec]
    args += [q, k, v, g, oz, oz, xc, gn.reshape(1, ML_WIDTH), skip.reshape(1, ML_WIDTH)]
    if state is not None:
        in_specs += [both(ML_HD, ML_HD), both(1, ML_HD)]
        args += [c0, n0.reshape(nb, 2, H, 1, ML_HD)]
    out_specs = [pl.BlockSpec((T, ML_HD), lambda b, h: (b, h))]
    out_shape = [jax.ShapeDtypeStruct((nb * T, ML_WIDTH), BF16)]
    if want_state:
        out_specs += [both(ML_HD, ML_HD), both(1, ML_HD), both(1, 128)]
        out_shape += [jax.ShapeDtypeStruct((nb, 2, H, ML_HD, ML_HD), F32),
                      jax.ShapeDtypeStruct((nb, 2, H, 1, ML_HD), F32),
                      jax.ShapeDtypeStruct((nb, 2, H, 1, 128), F32)]
    scratch = [pltpu.VMEM((T, ML_HD), F32), pltpu.VMEM((n, n, CHUNK, CHUNK), F32)]
    return pl.pallas_call(
        functools.partial(_ml_scan_kernel, T=T, state_in=state is not None, state_out=want_state),
        grid=(nb, H),
        in_specs=in_specs, out_specs=out_specs, out_shape=out_shape, scratch_shapes=scratch,
        compiler_params=_params("arbitrary", "arbitrary"),
        name="mlstm_scan",
    )(*args)


def _att_rope(x, cos, sin):
    lane = lax.broadcasted_iota(jnp.int32, x.shape, 1)
    half = ATT_HD // 2
    rot = jnp.where((lane & (ATT_HD - 1)) < half, pltpu.roll(x, 128 - half, 1), pltpu.roll(x, half, 1))
    return x * cos + rot * sin


def _att_heads(qms, sinks, key_sets):
    scores = [[_dot_nt(k_both, qm) if bias is None else _dot_nt(k_both, qm) + bias
               for k_both, _, bias in key_sets] for qm in qms]
    ms = []
    for sink, per_set in zip(sinks, scores):
        m = sink
        for s in per_set:
            m = jnp.maximum(m, jnp.max(s, axis=0, keepdims=True))
        ms.append(m)
    probs = [[jnp.exp(s - m) for s in per_set] for m, per_set in zip(ms, scores)]
    outs = []
    for sink, m, per_set in zip(sinks, ms, probs):
        den = jnp.exp(sink - m)
        out = None
        for p, (_, v_t, _) in zip(per_set, key_sets):
            den = den + jnp.sum(p, axis=0, keepdims=True)
            o = _dot(v_t, p.astype(BF16))
            out = o if out is None else out + o
        outs.append(out * (1.0 / den))
    return outs


def _att_prepare_kv(k, v, kb_ref, vt_ref, kv, nblk):
    lane = lax.broadcasted_iota(jnp.int32, k.shape, 1)
    native = lane < ATT_HD if kv % 2 == 0 else lane >= ATT_HD
    kb_ref[kv] = jnp.where(native, k, pltpu.roll(k, ATT_HD, 1)).astype(BF16)
    r0 = (kv % 2) * ATT_HD
    blk = k.shape[0] // nblk
    for c in range(nblk):
        vt_ref[kv, c] = jnp.transpose(v[c * blk:(c + 1) * blk, :])[r0:r0 + ATT_HD, :].astype(BF16)


def _att_group(q_ref, z0_ref, z1_ref, o_ref, sink_ref, rows, kv, key_sets):
    lo = lax.broadcasted_iota(jnp.int32, (CHUNK, 128), 1) < ATT_HD
    zero = jnp.zeros((CHUNK, 128), BF16)
    qms, sinks = [], []
    for g in range(ATT_GROUP):
        col = (2 * kv + g // 2) * 128
        q = q_ref[rows, col:col + 128]
        qms.append(jnp.where(lo, q, zero) if g % 2 == 0 else jnp.where(lo, zero, q))
        sinks.append(jnp.full((1, CHUNK), sink_ref[kv * ATT_GROUP + g], F32))
    heads = _att_heads(qms, sinks, key_sets)
    for p in range(2):
        col = (2 * kv + p) * 128
        out = jnp.transpose(jnp.concatenate(heads[2 * p:2 * p + 2], axis=0))
        z_ref = z0_ref if col < ATT_WIDTH // 2 else z1_ref
        zc = col % (ATT_WIDTH // 2)
        zf = z_ref[rows, zc:zc + 128].astype(F32)
        o_ref[rows, col:col + 128] = (out * zf).astype(BF16)


def _att_ctx_kernel(sink_ref, q_ref, k_ref, v_ref, z0_ref, z1_ref, o_ref, qs_ref, kb_ref, vt_ref):
    T = q_ref.shape[0]
    qs_ref[...] = q_ref[...] * (ATT_HD ** -0.5)
    for kv in range(ATT_KV):
        slab = (kv // 2) * 128
        _att_prepare_kv(k_ref[:, slab:slab + 128].astype(F32), v_ref[:, slab:slab + 128].astype(F32),
                        kb_ref, vt_ref, kv, 1)
    for c in range(T // CHUNK):
        rows = slice(c * CHUNK, (c + 1) * CHUNK)
        for kv in range(ATT_KV):
            _att_group(qs_ref, z0_ref, z1_ref, o_ref, sink_ref, rows, kv, [(kb_ref[kv], vt_ref[kv, 0], None)])


def _att_ctx(proj, sink):
    T = SEQ
    half = ATT_WIDTH // 2
    return pl.pallas_call(
        _att_ctx_kernel,
        grid=(BATCH,),
        in_specs=[pl.BlockSpec(memory_space=pltpu.SMEM),
                  pl.BlockSpec((T, ATT_WIDTH), lambda b: (b, 0)),
                  pl.BlockSpec((T, ATT_KVW), lambda b: (b, ATT_WIDTH // ATT_KVW)),
                  pl.BlockSpec((T, ATT_KVW), lambda b: (b, ATT_WIDTH // ATT_KVW + 1)),
                  pl.BlockSpec((T, half), lambda b: (b, (ATT_WIDTH + 2 * ATT_KVW) // half)),
                  pl.BlockSpec((T, half), lambda b: (b, (ATT_WIDTH + 2 * ATT_KVW) // half + 1))],
        out_specs=pl.BlockSpec((T, ATT_WIDTH), lambda b: (b, 0)),
        out_shape=jax.ShapeDtypeStruct((N_PROMPT, ATT_WIDTH), BF16),
        scratch_shapes=[pltpu.VMEM((T, ATT_WIDTH), BF16), pltpu.VMEM((ATT_KV, T, 128), BF16),
                        pltpu.VMEM((ATT_KV, 1, ATT_HD, T), BF16)],
        compiler_params=_params("arbitrary"),
        name="att_ctx",
    )(sink, proj, proj, proj, proj, proj)


def _att_win_kernel(sink_ref, q_ref, k_ref, v_ref, z0_ref, z1_ref, kc_ref, vc_ref, cos_ref, sin_ref,
                    o_ref, qs_ref, kb_ref, vt_ref, kcb_ref, vct_ref):
    T = q_ref.shape[0]
    L = CHUNK
    n = T // L
    cos = cos_ref[...]
    sin = sin_ref[...]
    for c0 in range(0, ATT_WIDTH, 128):
        q = _att_rope(q_ref[:, c0:c0 + 128].astype(F32), cos, sin)
        qs_ref[:, c0:c0 + 128] = (q * (ATT_HD ** -0.5)).astype(BF16)
    for kv in range(ATT_KV):
        slab = (kv // 2) * 128
        _att_prepare_kv(_att_rope(k_ref[:, slab:slab + 128].astype(F32), cos, sin),
                        v_ref[:, slab:slab + 128].astype(F32), kb_ref, vt_ref, kv, n)
        _att_prepare_kv(kc_ref[:, slab:slab + 128], vc_ref[:, slab:slab + 128], kcb_ref, vct_ref, kv, 1)

    jj = lax.broadcasted_iota(jnp.int32, (L, L), 0)
    ii = lax.broadcasted_iota(jnp.int32, (L, L), 1)
    neg = jnp.full((L, L), -jnp.inf, F32)
    bias_prev = jnp.where(jj >= ii, 0.0, neg)
    bias_next = jnp.where(jj <= ii, 0.0, neg)

    def body(c, carry):
        rows = pl.ds(pl.multiple_of(c * L, L), L)
        c_prev = jnp.maximum(c - 1, 0)
        c_next = jnp.minimum(c + 1, n - 1)
        b_prev = jnp.where(c > 0, bias_prev, neg)
        b_next = jnp.where(c < n - 1, bias_next, neg)
        for kv in range(ATT_KV):
            def keys(cb):
                return kb_ref[kv, pl.ds(pl.multiple_of(cb * L, L), L), :]
            key_sets = [(keys(c_prev), vt_ref[kv, c_prev], b_prev),
                        (keys(c), vt_ref[kv, c], None),
                        (keys(c_next), vt_ref[kv, c_next], b_next),
                        (kcb_ref[kv], vct_ref[kv, 0], None)]
            _att_group(qs_ref, z0_ref, z1_ref, o_ref, sink_ref, rows, kv, key_sets)
        return carry

    lax.fori_loop(0, n, body, 0)


def _att_win(proj, sink, k_ctx, v_ctx, cos, sin):
    T = DEC_SEQ
    half = ATT_WIDTH // 2
    rb = N_PROMPT // T
    return pl.pallas_call(
        _att_win_kernel,
        grid=(DEC_BATCH,),
        in_specs=[pl.BlockSpec(memory_space=pltpu.SMEM),
                  pl.BlockSpec((T, ATT_WIDTH), lambda b: (rb + b, 0)),
                  pl.BlockSpec((T, ATT_KVW), lambda b: (rb + b, ATT_WIDTH // ATT_KVW)),
                  pl.BlockSpec((T, ATT_KVW), lambda b: (rb + b, ATT_WIDTH // ATT_KVW + 1)),
                  pl.BlockSpec((T, half), lambda b: (rb + b, (ATT_WIDTH + 2 * ATT_KVW) // half)),
                  pl.BlockSpec((T, half), lambda b: (rb + b, (ATT_WIDTH + 2 * ATT_KVW) // half + 1)),
                  pl.BlockSpec((None, PAST_LEN, ATT_KVW), lambda b: (b, 0, 0)),
                  pl.BlockSpec((None, PAST_LEN, ATT_KVW), lambda b: (b, 0, 0)),
                  pl.BlockSpec((T, 128), lambda b: (0, 0)),
                  pl.BlockSpec((T, 128), lambda b: (0, 0))],
        out_specs=pl.BlockSpec((T, ATT_WIDTH), lambda b: (b, 0)),
        out_shape=jax.ShapeDtypeStruct((N_LATENT, ATT_WIDTH), BF16),
        scratch_shapes=[pltpu.VMEM((T, ATT_WIDTH), BF16),
                        pltpu.VMEM((ATT_KV, T, 128), BF16),
                        pltpu.VMEM((ATT_KV, T // CHUNK, ATT_HD, CHUNK), BF16),
                        pltpu.VMEM((ATT_KV, PAST_LEN, 128), BF16),
                        pltpu.VMEM((ATT_KV, 1, ATT_HD, PAST_LEN), BF16)],
        compiler_params=_params("arbitrary"),
        name="att_win",
    )(sink, proj, proj, proj, proj, proj, k_ctx, v_ctx, cos, sin)


def _rope_tables(T, hd, reps):
    rows = T // GRID_W
    row = np.repeat(np.arange(rows, dtype=np.float64), GRID_W)
    col = np.tile(np.arange(GRID_W, dtype=np.float64), rows)
    nf = hd // 4
    inv = ROPE_BASE ** (-np.arange(nf, dtype=np.float64) / nf)
    ang = np.concatenate([row[:, None] * inv[None, :], col[:, None] * inv[None, :]], axis=-1)
    cos, sin = np.cos(ang), np.sin(ang)
    return (jnp.asarray(np.tile(np.concatenate([cos, cos], axis=-1), (1, reps)), F32),
            jnp.asarray(np.tile(np.concatenate([-sin, sin], axis=-1), (1, reps)), F32))


def _retention_layer(h, w_in, decay_f, decay_b, gn, state, rope):
    z0 = 2 * RET_QK + RET_WIDTH
    proj = _in_proj(h, w_in, 1024, (z0, z0))
    dec = jnp.stack([decay_f, decay_b]).astype(F32)
    a_p, new_state = _retention(proj, dec, gn, SEQ, BATCH, 0, want_state=True)
    (a_s,) = _retention(proj, dec, gn, DEC_SEQ, DEC_BATCH, N_PROMPT // DEC_SEQ, rope=rope, state=state)
    return a_p, a_s, new_state


def kernel(x_prompt, x_sample, c, c_ctx, state_l0_ret, state_l1_C, state_l1_n, state_l1_m, cache_l2_k, cache_l2_v, state_l3_ret, norm_l0, ada_w_l0, ada_b_l0, w_in_l0, w_out_l0, ret_decay_f_l0, ret_decay_b_l0, ret_gn_l0, norm_l1, ada_w_l1, ada_b_l1, w_in_l1, w_out_l1, conv_w_l1, conv_b_l1, wq_l1, wk_l1, wv_l1, wif_f_l1, bif_f_l1, wif_b_l1, bif_b_l1, gn_l1, skip_l1, norm_l2, ada_w_l2, ada_b_l2, w_in_l2, w_out_l2, sink_l2, norm_l3, ada_w_l3, ada_b_l3, w_in_l3, w_out_l3, ret_decay_f_l3, ret_decay_b_l3, ret_gn_l3, final_norm):
    ct = jnp.concatenate([c_ctx[:, None], c.T, jnp.zeros((D, MOD_ROWS - 1 - DEC_BATCH), F32)], axis=1)
    mods = [_adaln(ct, w, b) for w, b in ((ada_w_l0, ada_b_l0), (ada_w_l1, ada_b_l1),
                                          (ada_w_l2, ada_b_l2), (ada_w_l3, ada_b_l3))]
    rope_ret = _rope_tables(DEC_SEQ, RET_DK, 1)
    rope_att = _rope_tables(DEC_SEQ, ATT_HD, 2)

    x, h = _norm_mod(x_prompt.reshape(N_PROMPT, D), x_sample.reshape(N_LATENT, D), norm_l0, mods[0])

    a_p, a_s, new_l0_ret = _retention_layer(h, w_in_l0, ret_decay_f_l0, ret_decay_b_l0, ret_gn_l0, state_l0_ret,
                                            rope_ret)
    x, h = _out_proj(a_p, a_s, w_out_l0, x, mods[0], norm_l1, mods[1], False)

    wif = jnp.concatenate([wif_f_l1, wif_b_l1], axis=1)
    bif = jnp.concatenate([bif_f_l1, bif_b_l1])
    oz, q, k, v, xc, g = _ml_in_proj(h, w_in_l1, conv_w_l1, conv_b_l1, wq_l1, wk_l1, wv_l1, wif, bif)
    a_p, new_l1_C, n_new, m_new = _ml_scan(oz, q, k, v, xc, g, gn_l1, skip_l1, SEQ, BATCH, 0, want_state=True)
    (a_s,) = _ml_scan(oz, q, k, v, xc, g, gn_l1, skip_l1, DEC_SEQ, DEC_BATCH, N_PROMPT // DEC_SEQ,
                      state=(state_l1_C, state_l1_n, state_l1_m))
    new_l1_n = n_new.reshape(BATCH, 2, ML_HEADS, ML_HD)
    new_l1_m = m_new[:, :, :, 0, 0]
    x, h = _out_proj(a_p, a_s, w_out_l1, x, mods[1], norm_l2, mods[2], False)

    proj = _in_proj(h, w_in_l2, 1280, (ATT_WIDTH + 2 * ATT_KVW,) * 2)
    kv_new = proj[:N_PROMPT, ATT_WIDTH:ATT_WIDTH + 2 * ATT_KVW].astype(F32)
    new_l2_k = kv_new[:, :ATT_KVW].reshape(BATCH, SEQ, ATT_KV, ATT_HD)
    new_l2_v = kv_new[:, ATT_KVW:].reshape(BATCH, SEQ, ATT_KV, ATT_HD)
    a_p = _att_ctx(proj, sink_l2)
    a_s = _att_win(proj, sink_l2, cache_l2_k.reshape(DEC_BATCH, PAST_LEN, ATT_KVW),
                   cache_l2_v.reshape(DEC_BATCH, PAST_LEN, ATT_KVW), rope_att[0], rope_att[1])
    x, h = _out_proj(a_p, a_s, w_out_l2, x, mods[2], norm_l3, mods[3], False)

    a_p, a_s, new_l3_ret = _retention_layer(h, w_in_l3, ret_decay_f_l3, ret_decay_b_l3, ret_gn_l3, state_l3_ret,
                                            rope_ret)
    (y_p,) = _out_proj(a_p, a_s, w_out_l3, x, mods[3], final_norm, mods[3], True, rows=(0, N_PROMPT))
    (y_s,) = _out_proj(a_p, a_s, w_out_l3, x, mods[3], final_norm, mods[3], True, rows=(N_PROMPT, N_LATENT))

    y_prompt = y_p.reshape(BATCH, SEQ, D)
    y_sample = y_s.reshape(DEC_BATCH, DEC_SEQ, D)
    return (y_prompt, y_sample, new_l0_ret, new_l1_C, new_l1_n, new_l1_m, new_l2_k, new_l2_v, new_l3_ret)
```

```python
import functools

import jax
import jax.numpy as jnp
import numpy as np
from jax import lax
from jax.experimental import pallas as pl
from jax.experimental.pallas import tpu as pltpu

F32 = jnp.float32
BF16 = jnp.bfloat16

D = 1024
BATCH = 16
SEQ = 256
DEC_BATCH = 2
DEC_SEQ = 1024
PAST_LEN = 512
GRID_W = 64
CHUNK = 128
EPS = 1e-6
ROPE_BASE = 10000.0

N_PROMPT = BATCH * SEQ
N_LATENT = DEC_BATCH * DEC_SEQ
N_TOK = N_PROMPT + N_LATENT
MOD_ROWS = 8

RET_HEADS = 8
RET_DK = 128
RET_DV = 256
RET_QK = RET_HEADS * RET_DK
RET_WIDTH = RET_HEADS * RET_DV
RET_HB_PROMPT = 8
RET_HB_LATENT = 2
ML_HB_PROMPT = 1
ML_HB_LATENT = 1

ML_HEADS = 4
ML_WIDTH = 2 * D
ML_HD = ML_WIDTH // ML_HEADS
ML_BLOCK = 4
ML_CONV = 5
ML_TILE = 256
ML_ROWS = 1024
ML_TN = 1024
ML_XT = ML_WIDTH // ML_TN
ML_EDGE_ROWS = sorted({r for e in range(0, ML_ROWS + 1, SEQ) for r in (e - 8, e) if 0 <= r < ML_ROWS})

ATT_HEADS = 16
ATT_KV = 4
ATT_HD = 64
ATT_GROUP = ATT_HEADS // ATT_KV
ATT_WIDTH = ATT_HEADS * ATT_HD
ATT_KVW = ATT_KV * ATT_HD
WINDOW = 128

OUT_SUB = 256
IN_SUB = 4

VMEM_LIMIT = 56 * 1024 * 1024


def _params(*sem):
    return pltpu.CompilerParams(dimension_semantics=sem, vmem_limit_bytes=VMEM_LIMIT)


def _mod_row(row0):
    return jnp.maximum((row0 - N_PROMPT) // DEC_SEQ + 1, 0)


def _dot(a, b):
    return jnp.dot(a, b, preferred_element_type=F32)


def _dot_nt(a, b):
    return lax.dot_general(a, b, (((1,), (1,)), ((), ())), preferred_element_type=F32)


def _dot_tn(a, b):
    return lax.dot_general(a, b, (((0,), (0,)), ((), ())), preferred_element_type=F32)


def _silu(x):
    return x * jax.nn.sigmoid(x)


def _log_sigmoid(x):
    return jnp.minimum(x, 0.0) - jnp.log1p(jnp.exp(-jnp.abs(x)))


def _rms(x, g):
    return x * lax.rsqrt(jnp.mean(x * x, axis=-1, keepdims=True) + EPS) * g


def _group_norm(x, g):
    xc = x - jnp.mean(x, axis=-1, keepdims=True)
    return xc * lax.rsqrt(jnp.mean(xc * xc, axis=-1, keepdims=True) + EPS) * g


def _two_source_specs(tm, width, tile0=0):
    n_p = N_PROMPT // tm
    return (pl.BlockSpec((tm, width), lambda i: (jnp.minimum(tile0 + i, n_p - 1), 0)),
            pl.BlockSpec((tm, width), lambda i: (jnp.maximum(tile0 + i - n_p, 0), 0)))


def _pick(i, tm, p_ref, s_ref):
    return jnp.where(i < N_PROMPT // tm, p_ref[...], s_ref[...])


def _ada_kernel(ct_ref, w_ref, b_ref, o_ref):
    s = _silu(ct_ref[...])
    w = w_ref[...]
    rows = [jnp.sum(w * s[:, r:r + 1], axis=0, keepdims=True) for r in range(1 + DEC_BATCH)]
    rows.append(jnp.zeros((MOD_ROWS - len(rows), w.shape[1]), F32))
    o_ref[...] = jnp.concatenate(rows, axis=0) + b_ref[...]


def _adaln(ct, w, b):
    tn = 1024
    return pl.pallas_call(
        _ada_kernel,
        grid=(3 * D // tn,),
        in_specs=[pl.BlockSpec((D, MOD_ROWS), lambda j: (0, 0)),
                  pl.BlockSpec((D, tn), lambda j: (0, j)),
                  pl.BlockSpec((1, tn), lambda j: (0, j))],
        out_specs=pl.BlockSpec((MOD_ROWS, tn), lambda j: (0, j)),
        out_shape=jax.ShapeDtypeStruct((MOD_ROWS, 3 * D), F32),
        compiler_params=_params("arbitrary"),
        name="adaln",
    )(ct, w, b.reshape(1, 3 * D))


def _norm_mod_kernel(xp_ref, xs_ref, g_ref, mod_ref, x_ref, h_ref, *, tm):
    i = pl.program_id(0)
    m = mod_ref[pl.ds(_mod_row(i * tm), 1), :]
    x = _pick(i, tm, xp_ref, xs_ref)
    x_ref[...] = x
    y = _rms(x, g_ref[...])
    h_ref[...] = (y * (1.0 + m[:, D:2 * D]) + m[:, :D]).astype(BF16)


def _norm_mod(xp, xs, g, mod):
    tm = 512
    return pl.pallas_call(
        functools.partial(_norm_mod_kernel, tm=tm),
        grid=(N_TOK // tm,),
        in_specs=[*_two_source_specs(tm, D),
                  pl.BlockSpec((1, D), lambda i: (0, 0)),
                  pl.BlockSpec((MOD_ROWS, 3 * D), lambda i: (0, 0))],
        out_specs=[pl.BlockSpec((tm, D), lambda i: (i, 0))] * 2,
        out_shape=[jax.ShapeDtypeStruct((N_TOK, D), F32), jax.ShapeDtypeStruct((N_TOK, D), BF16)],
        compiler_params=_params("arbitrary"),
        name="norm_mod",
    )(xp, xs, g.reshape(1, D), mod)


def _in_proj_kernel(h_ref, w_ref, o_ref, wb_ref, *, tn, n, acts):
    j = pl.program_id(0)

    @pl.when(pl.program_id(1) == 0)
    def _():
        wb_ref[...] = w_ref[...].astype(BF16)

    sig0, silu0 = acts
    gated = (j + 1) * tn > min(sig0, silu0)

    @pl.when(jnp.logical_not(gated))
    def _():
        o_ref[...] = _dot(h_ref[...], wb_ref[...]).astype(BF16)

    @pl.when(gated)
    def _():
        sub = h_ref.shape[0] // IN_SUB
        for s in range(IN_SUB):
            rows = slice(s * sub, (s + 1) * sub)
            acc = _dot(h_ref[rows, :], wb_ref[...])
            col = j * tn + lax.broadcasted_iota(jnp.int32, acc.shape, 1)
            sig = jax.nn.sigmoid(acc)
            out = jnp.where(col >= silu0, acc * sig, jnp.where(col >= sig0, sig, acc))
            o_ref[rows, :] = out.astype(BF16)


def _in_proj(h, w, tn, acts):
    tm = 1024
    n = w.shape[1]
    return pl.pallas_call(
        functools.partial(_in_proj_kernel, tn=tn, n=n, acts=acts),
        grid=(n // tn, N_TOK // tm),
        in_specs=[pl.BlockSpec((tm, D), lambda j, i: (i, 0)),
                  pl.BlockSpec((D, tn), lambda j, i: (0, j))],
        out_specs=pl.BlockSpec((tm, tn), lambda j, i: (i, j)),
        out_shape=jax.ShapeDtypeStruct((N_TOK, n), BF16),
        scratch_shapes=[pltpu.VMEM((D, tn), BF16)],
        compiler_params=_params("arbitrary", "arbitrary"),
        name="in_proj",
    )(h, w)


def _out_proj_kernel(ap_ref, as_ref, w_ref, x_ref, mod_ref, g_ref, modn_ref, *rest, tm, tile0, final):
    wb_ref = rest[-1]

    @pl.when(pl.program_id(0) == 0)
    def _():
        wb_ref[...] = w_ref[...].astype(BF16)

    i = tile0 + pl.program_id(0)
    r = _mod_row(i * tm)
    gate = mod_ref[pl.ds(r, 1), :][:, 2 * D:]
    from_prompt = i < N_PROMPT // tm
    for s in range(tm // OUT_SUB):
        rows = slice(s * OUT_SUB, (s + 1) * OUT_SUB)
        a = jnp.where(from_prompt, ap_ref[rows, :], as_ref[rows, :])
        xn = x_ref[rows, :] + gate * _dot(a, wb_ref[...])
        y = _rms(xn, g_ref[...])
        if final:
            rest[0][rows, :] = y
        else:
            mn = modn_ref[pl.ds(r, 1), :]
            rest[0][rows, :] = xn
            rest[1][rows, :] = (y * (1.0 + mn[:, D:2 * D]) + mn[:, :D]).astype(BF16)


def _out_proj(a_p, a_s, w, x, mod, g_next, mod_next, final, rows=(0, N_TOK)):
    tm = 512
    kw = w.shape[0]
    tile0, nt = rows[0] // tm, rows[1] // tm
    row = pl.BlockSpec((tm, D), lambda i: (i, 0))
    if final:
        out_specs, out_shape = [row], [jax.ShapeDtypeStruct((rows[1], D), F32)]
    else:
        out_specs = [row, row]
        out_shape = [jax.ShapeDtypeStruct((rows[1], D), F32), jax.ShapeDtypeStruct((rows[1], D), BF16)]
    return pl.pallas_call(
        functools.partial(_out_proj_kernel, tm=tm, tile0=tile0, final=final),
        grid=(nt,),
        in_specs=[*_two_source_specs(tm, kw, tile0),
                  pl.BlockSpec((kw, D), lambda i: (0, 0)),
                  pl.BlockSpec((tm, D), lambda i: (tile0 + i, 0)),
                  pl.BlockSpec((MOD_ROWS, 3 * D), lambda i: (0, 0)),
                  pl.BlockSpec((1, D), lambda i: (0, 0)),
                  pl.BlockSpec((MOD_ROWS, 3 * D), lambda i: (0, 0))],
        out_specs=out_specs,
        out_shape=out_shape,
        scratch_shapes=[pltpu.VMEM((kw, D), BF16)],
        compiler_params=_params("arbitrary"),
        name="out_proj",
    )(a_p, a_s, w, x, mod, g_next.reshape(1, D), mod_next)


def _ret_kernel(*refs, T, HB, rope, state_in, state_out):
    refs = list(refs)
    dec_ref, q_ref, k_ref, v_ref, z_ref, gn_ref = refs[:6]
    pos = 6
    if rope:
        cos_ref, sin_ref = refs[pos:pos + 2]
        pos += 2
    if state_in:
        s0_ref = refs[pos]
        pos += 1
    o_ref = refs[pos]
    pos += 1
    if state_out:
        sn_ref = refs[pos]
        pos += 1
    kv_ref, sf_ref, sb_ref, tab_ref, gblk_ref = refs[pos:pos + 5]
    pos += 5
    if rope:
        kr_ref = refs[pos]

    hg = pl.program_id(1)
    n = T // CHUNK
    L = CHUNK
    ii = lax.broadcasted_iota(jnp.int32, (L, L), 0).astype(F32)
    jj = lax.broadcasted_iota(jnp.int32, (L, L), 1).astype(F32)
    scale = RET_DK ** -0.5

    def chunk(c):
        return slice(c * L, (c + 1) * L)

    def rotate(x, c):
        return x * cos_ref[chunk(c), :] + pltpu.roll(x, RET_DK // 2, 1) * sin_ref[chunk(c), :]

    @pl.when(pl.program_id(0) == 0)
    def _():
        for hh in range(HB):
            head = hg * HB + hh
            lg_f = _log_sigmoid(jnp.full((1, RET_DV), dec_ref[0, head], F32))
            lg_b = _log_sigmoid(jnp.full((1, RET_DV), dec_ref[1, head], F32))
            lf, lb = lg_f[:, :L], lg_b[:, :L]
            tab_ref[head, 0] = jnp.exp(lf * (L - 1.0 - ii)) * scale
            tab_ref[head, 1] = jnp.exp(lb * ii) * scale
            tab_ref[head, 2] = jnp.exp(lf * (ii + 1.0))
            tab_ref[head, 3] = jnp.exp(lb * (L - ii))
            tab_ref[head, 4] = (jnp.where(ii >= jj, jnp.exp(lf * jnp.maximum(ii - jj, 0.0)), 0.0)
                                + jnp.where(jj >= ii, jnp.exp(lb * jnp.maximum(jj - ii, 0.0)), 0.0)) * scale
            gblk_ref[head, 0] = jnp.exp(lg_f * float(L))
            gblk_ref[head, 1] = jnp.exp(lg_b * float(L))

    for hh in range(HB):
        head = hg * HB + hh
        qs = slice(hh * RET_DK, (hh + 1) * RET_DK)
        vs = slice(hh * RET_DV, (hh + 1) * RET_DV)
        k_dec_f, k_dec_b, q_dec_f, q_dec_b, mask = (tab_ref[head, t] for t in range(5))
        g_f = gblk_ref[head, 0]
        g_b = gblk_ref[head, 1]

        for c in range(n):
            kc = k_ref[chunk(c), qs].astype(F32)
            if rope:
                kc = rotate(kc, c)
                kr_ref[chunk(c), qs] = kc.astype(BF16)
            kk = jnp.concatenate([kc * k_dec_f, kc * k_dec_b], axis=1).astype(BF16)
            kv_ref[hh, c] = _dot_tn(kk, v_ref[chunk(c), vs])
        S = s0_ref[0, hh] if state_in else None
        has_f = []
        for c in range(n):
            has_f.append(S is not None)
            if S is not None:
                sf_ref[hh, c] = S.astype(BF16)
            kvc = kv_ref[hh, c, :RET_DK, :]
            S = kvc if S is None else S * g_f + kvc
        if state_out:
            sn_ref[0, hh] = S
        S = s0_ref[1, hh] if state_in else None
        has_b = [False] * n
        for c in reversed(range(n)):
            has_b[c] = S is not None
            if S is not None:
                sb_ref[hh, c] = S.astype(BF16)
            kvc = kv_ref[hh, c, RET_DK:, :]
            S = kvc if S is None else S * g_b + kvc
        if state_out:
            sn_ref[1, hh] = S

        for c in range(n):
            qc = q_ref[chunk(c), qs]
            qf = qc.astype(F32)
            if rope:
                qf = rotate(qf, c)
                qc = qf.astype(BF16)
                kc = kr_ref[chunk(c), qs]
            else:
                kc = k_ref[chunk(c), qs]
            lhs = [(_dot_nt(qc, kc) * mask).astype(BF16)]
            rhs = [v_ref[chunk(c), vs]]
            if has_f[c]:
                lhs.append((qf * q_dec_f).astype(BF16))
                rhs.append(sf_ref[hh, c])
            if has_b[c]:
                lhs.append((qf * q_dec_b).astype(BF16))
                rhs.append(sb_ref[hh, c])
            o = _dot(jnp.concatenate(lhs, axis=1), jnp.concatenate(rhs, axis=0))
            zf = z_ref[chunk(c), vs].astype(F32)
            o_ref[chunk(c), vs] = (_group_norm(o, gn_ref[:, vs]) * zf).astype(BF16)


def _retention(proj, dec, gn, T, nb, row_blk0, rope=None, state=None, want_state=False):
    HB = RET_HB_PROMPT if T == SEQ else RET_HB_LATENT
    ng = RET_HEADS // HB
    n = T // CHUNK
    qw, vw = HB * RET_DK, HB * RET_DV
    in_specs = [pl.BlockSpec(memory_space=pltpu.SMEM),
                pl.BlockSpec((T, qw), lambda b, h: (row_blk0 + b, h)),
                pl.BlockSpec((T, qw), lambda b, h: (row_blk0 + b, ng + h)),
                pl.BlockSpec((T, vw), lambda b, h: (row_blk0 + b, ng + h)),
                pl.BlockSpec((T, vw), lambda b, h: (row_blk0 + b, 2 * ng + h)),
                pl.BlockSpec((1, vw), lambda b, h: (0, h))]
    args = [dec, proj, proj, proj, proj, gn.reshape(1, RET_WIDTH)]
    if rope is not None:
        in_specs += [pl.BlockSpec((T, RET_DK), lambda b, h: (0, 0))] * 2
        args += list(rope)
    if state is not None:
        in_specs.append(pl.BlockSpec((None, 2, HB, RET_DK, RET_DV), lambda b, h: (b, 0, h, 0, 0)))
        args.append(state)
    out_specs = [pl.BlockSpec((T, vw), lambda b, h: (b, h))]
    out_shape = [jax.ShapeDtypeStruct((nb * T, RET_WIDTH), BF16)]
    if want_state:
        out_specs.append(pl.BlockSpec((None, 2, HB, RET_DK, RET_DV), lambda b, h: (b, 0, h, 0, 0)))
        out_shape.append(jax.ShapeDtypeStruct((nb, 2, RET_HEADS, RET_DK, RET_DV), F32))
    scratch = [pltpu.VMEM((HB, n, 2 * RET_DK, RET_DV), F32),
               pltpu.VMEM((HB, n, RET_DK, RET_DV), BF16),
               pltpu.VMEM((HB, n, RET_DK, RET_DV), BF16),
               pltpu.VMEM((RET_HEADS, 5, CHUNK, CHUNK), F32),
               pltpu.VMEM((RET_HEADS, 2, 1, RET_DV), F32)]
    if rope is not None:
        scratch.append(pltpu.VMEM((T, qw), BF16))
    return pl.pallas_call(
        functools.partial(_ret_kernel, T=T, HB=HB, rope=rope is not None, state_in=state is not None,
                          state_out=want_state),
        grid=(nb, ng),
        in_specs=in_specs, out_specs=out_specs, out_shape=out_shape, scratch_shapes=scratch,
        compiler_params=_params("arbitrary", "arbitrary"),
        name="retention",
    )(*args)


def _ml_in_kernel(h_ref, w_ref, cw_ref, cb_ref, bq_ref, bk_ref, bv_ref, wq_ref, wk_ref, wv_ref, bias_ref,
                  oz_ref, q_ref, k_ref, v_ref, xc_ref, g_ref, wb_ref, bd_ref, conv_ref):
    j = pl.program_id(0)
    i = pl.program_id(1)
    nsub = ML_TN // ML_TILE

    @pl.when(i == 0)
    def _():
        wb_ref[...] = w_ref[...].astype(BF16)

    @pl.when((i == 0) & (j < ML_XT))
    def _():
        row = lax.broadcasted_iota(jnp.int32, (ML_TILE, ML_TILE), 0)
        col = lax.broadcasted_iota(jnp.int32, (ML_TILE, ML_TILE), 1)
        shift = ML_BLOCK.bit_length() - 1
        same_block = jnp.right_shift(row, shift) == jnp.right_shift(col, shift)
        for which, b_ref in enumerate((bq_ref, bk_ref, bv_ref)):
            for t in range(nsub):
                w = b_ref[t * ML_TILE:(t + 1) * ML_TILE, :]
                rep = w[:, ML_BLOCK - 1:ML_BLOCK]
                for dd in range(ML_BLOCK - 1):
                    rep = jnp.where((col & (ML_BLOCK - 1)) == dd, w[:, dd:dd + 1], rep)
                bd_ref[which, t] = jnp.where(same_block, rep, 0.0).astype(BF16)

    @pl.when(j >= ML_XT)
    def _():
        sub = ML_ROWS // IN_SUB
        for s in range(IN_SUB):
            rows = slice(s * sub, (s + 1) * sub)
            acc = _dot(h_ref[rows, :], wb_ref[...])
            col = j * ML_TN + lax.broadcasted_iota(jnp.int32, acc.shape, 1)
            sig = jax.nn.sigmoid(acc)
            oz_ref[rows, :] = jnp.where(col >= 2 * ML_WIDTH, acc * sig, sig).astype(BF16)

    @pl.when(j < ML_XT)
    def _():
        acc = _dot(h_ref[...], wb_ref[...])
        T = jnp.where(i < N_PROMPT // ML_ROWS, SEQ, DEC_SEQ)
        pad = ML_CONV // 2
        taps = [tap for tap in range(ML_CONV) if tap != pad]
        sub8 = lax.broadcasted_iota(jnp.int32, (8, ML_TILE), 0)
        g = jnp.where(j == 0, 1.0, 0.0) * bias_ref[...]
        for t in range(nsub):
            cols = slice(t * ML_TILE, (t + 1) * ML_TILE)
            x = acc[:, cols]
            shifted = {tap: pltpu.roll(x, (pad - tap) % ML_ROWS, 0) for tap in taps}
            conv = cb_ref[:, cols] + x * cw_ref[pad:pad + 1, cols]
            for tap in taps:
                conv = conv + shifted[tap] * cw_ref[tap:tap + 1, cols]
            conv_ref[...] = conv
            for r0 in ML_EDGE_ROWS:
                rows = slice(r0, r0 + 8)
                pos = (r0 + sub8) & (T - 1)
                fixed = cb_ref[:, cols] + x[rows] * cw_ref[pad:pad + 1, cols]
                for tap in taps:
                    ok = (pos + (tap - pad) >= 0) & (pos + (tap - pad) < T)
                    fixed = fixed + jnp.where(ok, shifted[tap][rows], 0.0) * cw_ref[tap:tap + 1, cols]
                conv_ref[rows, :] = fixed
            xcb = _silu(conv_ref[...]).astype(BF16)
            qb = _dot(xcb, bd_ref[0, t]).astype(BF16)
            kb = _dot(xcb, bd_ref[1, t]).astype(BF16)
            vb = _dot(x.astype(BF16), bd_ref[2, t]).astype(BF16)
            q_ref[:, cols] = qb
            k_ref[:, cols] = kb
            v_ref[:, cols] = vb
            xc_ref[:, cols] = xcb
            g = g + (_dot(qb, wq_ref[cols, :].astype(BF16)) + _dot(kb, wk_ref[cols, :].astype(BF16))
                     + _dot(vb, wv_ref[cols, :].astype(BF16)))
        g_ref[...] = g


def _ml_in_proj(h, w, conv_w, conv_b, wq, wk, wv, wif, bif):
    ng = wif.shape[1]
    ni = N_TOK // ML_ROWS
    xt = ML_XT
    wq, wk, wv = (a.reshape(ML_WIDTH, ML_BLOCK) for a in (wq, wk, wv))
    xcol = lambda j, i: (0, jnp.minimum(j, xt - 1))
    xrow = lambda off: (lambda j, i: (off + jnp.minimum(j, xt - 1), 0))
    x_out = pl.BlockSpec((ML_ROWS, ML_TN), lambda j, i: (jnp.where(j < xt, i, ni - 1), jnp.minimum(j, xt - 1)))
    return pl.pallas_call(
        _ml_in_kernel,
        grid=(3 * ML_WIDTH // ML_TN, ni),
        in_specs=[pl.BlockSpec((ML_ROWS, D), lambda j, i: (i, 0)),
                  pl.BlockSpec((D, ML_TN), lambda j, i: (0, j)),
                  pl.BlockSpec((ML_CONV, ML_TN), xcol),
                  pl.BlockSpec((1, ML_TN), xcol),
                  pl.BlockSpec((ML_TN, ML_BLOCK), xrow(0)),
                  pl.BlockSpec((ML_TN, ML_BLOCK), xrow(0)),
                  pl.BlockSpec((ML_TN, ML_BLOCK), xrow(0)),
                  pl.BlockSpec((ML_TN, ng), xrow(0)),
                  pl.BlockSpec((ML_TN, ng), xrow(xt)),
                  pl.BlockSpec((ML_TN, ng), xrow(2 * xt)),
                  pl.BlockSpec((1, ng), lambda j, i: (0, 0))],
        out_specs=[pl.BlockSpec((ML_ROWS, ML_TN), lambda j, i: (jnp.where(j < xt, 0, i), jnp.maximum(j - xt, 0))),
                   x_out, x_out, x_out, x_out,
                   pl.BlockSpec((None, ML_ROWS, ng),
                                lambda j, i: (jnp.minimum(j, xt - 1), jnp.where(j < xt, i, ni - 1), 0))],
        out_shape=[jax.ShapeDtypeStruct((N_TOK, 2 * ML_WIDTH), BF16)]
        + [jax.ShapeDtypeStruct((N_TOK, ML_WIDTH), BF16)] * 4
        + [jax.ShapeDtypeStruct((xt, N_TOK, ng), F32)],
        scratch_shapes=[pltpu.VMEM((D, ML_TN), BF16), pltpu.VMEM((3, ML_TN // ML_TILE, ML_TILE, ML_TILE), BF16),
                        pltpu.VMEM((ML_ROWS, ML_TILE), F32)],
        compiler_params=_params("arbitrary", "arbitrary"),
        name="mlstm_in_proj",
    )(h, w, conv_w, conv_b.reshape(1, ML_WIDTH), wq, wk, wv, wif, wif, wif, bif.reshape(1, ng))


def _dot_split(lhs_bf16, rhs):
    r1 = rhs.astype(BF16)
    e1 = rhs - r1.astype(F32)
    r2 = e1.astype(BF16)
    r3 = (e1 - r2.astype(F32)).astype(BF16)
    return _dot(lhs_bf16, r1) + _dot(lhs_bf16, r2) + _dot(lhs_bf16, r3)


def _ml_scan_kernel(*refs, T, HB, state_in, state_out):
    refs = list(refs)
    pos = 0
    m0_ref = c0_ref = n0_ref = cn_ref = nn_ref = mn_ref = None
    if state_in:
        m0_ref = refs[0]
        pos = 1
    q_ref, k_ref, v_ref, g_ref, op_ref, z_ref, xc_ref, gn_ref, sk_ref = refs[pos:pos + 9]
    pos += 9
    if state_in:
        c0_ref, n0_ref = refs[pos:pos + 2]
        pos += 2
    o_ref = refs[pos]
    pos += 1
    if state_out:
        cn_ref, nn_ref, mn_ref = refs[pos:pos + 3]
        pos += 3
    acc_ref, s_ref = refs[pos:pos + 2]

    for hh in range(HB):
        cols = slice(hh * ML_HD, (hh + 1) * ML_HD)

        def head_cols(ref):
            return ref.at[:, cols]

        def head_state(ref):
            return None if ref is None else ref.at[:, hh]

        _ml_scan_head(pl.program_id(0), pl.program_id(1) * HB + hh, m0_ref,
                      head_cols(q_ref), head_cols(k_ref), head_cols(v_ref), g_ref, head_cols(op_ref),
                      head_cols(z_ref), head_cols(xc_ref), head_cols(gn_ref), head_cols(sk_ref),
                      head_state(c0_ref), head_state(n0_ref), head_cols(o_ref),
                      head_state(cn_ref), head_state(nn_ref), head_state(mn_ref), acc_ref.at[hh], s_ref.at[hh],
                      T=T, state_in=state_in, state_out=state_out)


def _ml_scan_head(b, h, m0_ref, q_ref, k_ref, v_ref, g_ref, op_ref, z_ref, xc_ref, gn_ref, sk_ref,
                  c0_ref, n0_ref, o_ref, cn_ref, nn_ref, mn_ref, acc_ref, s_ref, *, T, state_in, state_out):
    n = T // CHUNK
    L = CHUNK
    scale = ML_HD ** -0.5

    def chunk(c):
        return slice(c * L, (c + 1) * L)

    ii = lax.broadcasted_iota(jnp.int32, (L, L), 0)
    jj = lax.broadcasted_iota(jnp.int32, (L, L), 1)
    neg = jnp.full((L, L), -jnp.inf, F32)
    zero = jnp.zeros((L, L), F32)
    tri = [jnp.where(jj <= ii, 1.0, 0.0).astype(BF16), jnp.where(jj >= ii, 1.0, 0.0).astype(BF16)]
    bias = [jnp.where(ii >= jj, zero, neg), jnp.where(ii <= jj, zero, neg)]

    g = g_ref[0]
    for part in range(1, ML_XT):
        g = g + g_ref[part]
    col = lax.broadcasted_iota(jnp.int32, g.shape, 1)

    def gate_column(j):
        return jnp.sum(jnp.where(col == j, g, 0.0), axis=1, keepdims=True)


    for i in range(n):
        for j in range(n):
            s_ref[i, j] = _dot_nt(q_ref[chunk(i), :], k_ref[chunk(j), :])

    G = []
    for d in range(2):
        order = list(range(n)) if d == 0 else list(reversed(range(n)))
        gi_all = gate_column(d * 2 * ML_HEADS + h)
        gf_all = _log_sigmoid(gate_column(d * 2 * ML_HEADS + ML_HEADS + h))
        m0 = jnp.full((1, 1), m0_ref[(b * 2 + d) * ML_HEADS + h], F32) if state_in else jnp.zeros((1, 1), F32)
        c_col, c_row, b_col, m_row = [None] * n, [None] * n, [None] * n, [None] * n
        offset = jnp.zeros((1, 1), F32)
        carry = m0
        for c in order:
            gi = gi_all[chunk(c)]
            gf = gf_all[chunk(c)]
            cum = _dot_split(tri[d], jnp.broadcast_to(gf, (L, L))) + offset
            offset = offset + jnp.sum(gf, axis=0, keepdims=True)
            cc = gi - cum
            c_col[c] = cc[:, :1]
            c_row[c] = jnp.transpose(cc)
            b_col[c] = cum[:, :1]
            m_row[c] = jnp.maximum(jnp.max(c_row[c] + bias[d], axis=1, keepdims=True), carry)
            carry = jnp.maximum(carry, jnp.max(c_col[c], axis=0, keepdims=True))
        G.append(dict(order=order, c_col=c_col, c_row=c_row, b_col=b_col, m_row=m_row, m0=m0,
                      m_end=carry, b_end=offset))

    done = set()
    for i in range(n):
        for d in range(2):
            e = G[d]
            js = list(range(0, i + 1)) if d == 0 else list(range(i, n))
            m_i = e["m_row"][i]
            parts = []
            tot = None
            for j in js:
                z = e["c_row"][j] - m_i
                if j == i:
                    z = z + bias[d]
                sw = s_ref[i, j] * (jnp.exp(z) * scale)
                tot = sw if tot is None else tot + sw
                parts.append(sw.astype(BF16))
            num = _dot(jnp.concatenate(parts, axis=1), v_ref[js[0] * L:(js[-1] + 1) * L, :])
            den = jnp.sum(tot, axis=1, keepdims=True)
            if state_in:
                qc = q_ref[chunk(i), :]
                w0 = jnp.exp(e["m0"] - m_i)
                num = num + _dot((qc.astype(F32) * w0).astype(BF16), c0_ref[d].astype(BF16))
                n0 = jnp.broadcast_to(n0_ref[d], (8, ML_HD)).astype(BF16)
                den = den + w0 * _dot_nt(qc, n0)[:, :1]
            hb = num / jnp.maximum(jnp.abs(den), jnp.exp(-(e["b_col"][i] + m_i)))
            if i not in done:
                done.add(i)
                acc_ref[chunk(i), :] = hb
            else:
                cell = _group_norm((acc_ref[chunk(i), :] + hb) * op_ref[chunk(i), :].astype(F32), gn_ref[...])
                mixed = cell + sk_ref[...] * xc_ref[chunk(i), :].astype(F32)
                o_ref[chunk(i), :] = (mixed * z_ref[chunk(i), :].astype(F32)).astype(BF16)

    if state_out:
        ones = jnp.ones((8, T), BF16)
        for d in range(2):
            e = G[d]
            kw = jnp.concatenate(
                [(k_ref[chunk(c), :].astype(F32) * (jnp.exp(e["c_col"][c] - e["m_end"]) * scale)).astype(BF16)
                 for c in range(n)], axis=0)
            C = _dot_tn(kw, v_ref[...])
            nvec = _dot(ones, kw)[:1]
            if state_in:
                w_end = jnp.exp(e["m0"] - e["m_end"])
                C = C + w_end * c0_ref[d]
                nvec = nvec + w_end * n0_ref[d]
            cn_ref[d] = C
            nn_ref[d] = nvec
            mn_ref[d] = jnp.broadcast_to(e["b_end"] + e["m_end"], mn_ref.shape[1:])


def _ml_scan(oz, q, k, v, xc, g, gn, skip, T, nb, row_blk0, state=None, want_state=False):
    H = ML_HEADS
    HB = ML_HB_PROMPT if T == SEQ else ML_HB_LATENT
    ng = H // HB
    hw = HB * ML_HD
    n = T // CHUNK
    tok = lambda b, h: (row_blk0 + b, h)
    head_vec = pl.BlockSpec((1, hw), lambda b, h: (0, h))
    both = lambda *tail: pl.BlockSpec((None, 2, HB, *tail), lambda b, h: (b, 0, h, 0, 0))
    in_specs, args = [], []
    if state is not None:
        c0, n0, m0 = state
        in_specs.append(pl.BlockSpec(memory_space=pltpu.SMEM))
        args.append(m0.reshape(nb * 2 * H))
    in_specs += [pl.BlockSpec((T, hw), tok)] * 3
    in_specs += [pl.BlockSpec((ML_XT, T, g.shape[2]), lambda b, h: (0, row_blk0 + b, 0)),
                 pl.BlockSpec((T, hw), tok),
                 pl.BlockSpec((T, hw), lambda b, h: (row_blk0 + b, ng + h)),
                 pl.BlockSpec((T, hw), tok), head_vec, head_vec]
    args += [q, k, v, g, oz, oz, xc, gn.reshape(1, ML_WIDTH), skip.reshape(1, ML_WIDTH)]
    if state is not None:
        in_specs += [both(ML_HD, ML_HD), both(1, ML_HD)]
        args += [c0, n0.reshape(nb, 2, H, 1, ML_HD)]
    out_specs = [pl.BlockSpec((T, hw), lambda b, h: (b, h))]
    out_shape = [jax.ShapeDtypeStruct((nb * T, ML_WIDTH), BF16)]
    if want_state:
        out_specs += [both(ML_HD, ML_HD), both(1, ML_HD), both(1, 128)]
        out_shape += [jax.ShapeDtypeStruct((nb, 2, H, ML_HD, ML_HD), F32),
                      jax.ShapeDtypeStruct((nb, 2, H, 1, ML_HD), F32),
                      jax.ShapeDtypeStruct((nb, 2, H, 1, 128), F32)]
    scratch = [pltpu.VMEM((HB, T, ML_HD), F32), pltpu.VMEM((HB, n, n, CHUNK, CHUNK), F32)]
    return pl.pallas_call(
        functools.partial(_ml_scan_kernel, T=T, HB=HB, state_in=state is not None, state_out=want_state),
        grid=(nb, ng),
        in_specs=in_specs, out_specs=out_specs, out_shape=out_shape, scratch_shapes=scratch,
        compiler_params=_params("arbitrary", "arbitrary"),
        name="mlstm_scan",
    )(*args)


def _att_rope(x, cos, sin):
    lane = lax.broadcasted_iota(jnp.int32, x.shape, 1)
    half = ATT_HD // 2
    rot = jnp.where((lane & (ATT_HD - 1)) < half, pltpu.roll(x, 128 - half, 1), pltpu.roll(x, half, 1))
    return x * cos + rot * sin


def _att_heads(qms, sinks, key_sets):
    scores = [[_dot_nt(k_both, qm) if bias is None else _dot_nt(k_both, qm) + bias
               for k_both, _, bias in key_sets] for qm in qms]
    ms = []
    for sink, per_set in zip(sinks, scores):
        m = sink
        for s in per_set:
            m = jnp.maximum(m, jnp.max(s, axis=0, keepdims=True))
        ms.append(m)
    probs = [[jnp.exp(s - m) for s in per_set] for m, per_set in zip(ms, scores)]
    outs = []
    for sink, m, per_set in zip(sinks, ms, probs):
        den = jnp.exp(sink - m)
        out = None
        for p, (_, v_t, _) in zip(per_set, key_sets):
            den = den + jnp.sum(p, axis=0, keepdims=True)
            o = _dot(v_t, p.astype(BF16))
            out = o if out is None else out + o
        outs.append(out * (1.0 / den))
    return outs


def _att_prepare_kv(k, v, kb_ref, vt_ref, kv, nblk):
    lane = lax.broadcasted_iota(jnp.int32, k.shape, 1)
    native = lane < ATT_HD if kv % 2 == 0 else lane >= ATT_HD
    kb_ref[kv] = jnp.where(native, k, pltpu.roll(k, ATT_HD, 1)).astype(BF16)
    r0 = (kv % 2) * ATT_HD
    blk = k.shape[0] // nblk
    for c in range(nblk):
        vt_ref[kv, c] = jnp.transpose(v[c * blk:(c + 1) * blk, :])[r0:r0 + ATT_HD, :].astype(BF16)


def _att_group(q_ref, z0_ref, z1_ref, o_ref, sink_ref, rows, kv, key_sets):
    lo = lax.broadcasted_iota(jnp.int32, (CHUNK, 128), 1) < ATT_HD
    zero = jnp.zeros((CHUNK, 128), BF16)
    qms, sinks = [], []
    for g in range(ATT_GROUP):
        col = (2 * kv + g // 2) * 128
        q = q_ref[rows, col:col + 128]
        qms.append(jnp.where(lo, q, zero) if g % 2 == 0 else jnp.where(lo, zero, q))
        sinks.append(jnp.full((1, CHUNK), sink_ref[kv * ATT_GROUP + g], F32))
    heads = _att_heads(qms, sinks, key_sets)
    for p in range(2):
        col = (2 * kv + p) * 128
        out = jnp.transpose(jnp.concatenate(heads[2 * p:2 * p + 2], axis=0))
        z_ref = z0_ref if col < ATT_WIDTH // 2 else z1_ref
        zc = col % (ATT_WIDTH // 2)
        zf = z_ref[rows, zc:zc + 128].astype(F32)
        o_ref[rows, col:col + 128] = (out * zf).astype(BF16)


def _att_ctx_kernel(sink_ref, q_ref, k_ref, v_ref, z0_ref, z1_ref, o_ref, qs_ref, kb_ref, vt_ref):
    T = q_ref.shape[0]
    qs_ref[...] = q_ref[...] * (ATT_HD ** -0.5)
    for kv in range(ATT_KV):
        slab = (kv // 2) * 128
        _att_prepare_kv(k_ref[:, slab:slab + 128].astype(F32), v_ref[:, slab:slab + 128].astype(F32),
                        kb_ref, vt_ref, kv, 1)
    for c in range(T // CHUNK):
        rows = slice(c * CHUNK, (c + 1) * CHUNK)
        for kv in range(ATT_KV):
            _att_group(qs_ref, z0_ref, z1_ref, o_ref, sink_ref, rows, kv, [(kb_ref[kv], vt_ref[kv, 0], None)])


def _att_ctx(proj, sink):
    T = SEQ
    half = ATT_WIDTH // 2
    return pl.pallas_call(
        _att_ctx_kernel,
        grid=(BATCH,),
        in_specs=[pl.BlockSpec(memory_space=pltpu.SMEM),
                  pl.BlockSpec((T, ATT_WIDTH), lambda b: (b, 0)),
                  pl.BlockSpec((T, ATT_KVW), lambda b: (b, ATT_WIDTH // ATT_KVW)),
                  pl.BlockSpec((T, ATT_KVW), lambda b: (b, ATT_WIDTH // ATT_KVW + 1)),
                  pl.BlockSpec((T, half), lambda b: (b, (ATT_WIDTH + 2 * ATT_KVW) // half)),
                  pl.BlockSpec((T, half), lambda b: (b, (ATT_WIDTH + 2 * ATT_KVW) // half + 1))],
        out_specs=pl.BlockSpec((T, ATT_WIDTH), lambda b: (b, 0)),
        out_shape=jax.ShapeDtypeStruct((N_PROMPT, ATT_WIDTH), BF16),
        scratch_shapes=[pltpu.VMEM((T, ATT_WIDTH), BF16), pltpu.VMEM((ATT_KV, T, 128), BF16),
                        pltpu.VMEM((ATT_KV, 1, ATT_HD, T), BF16)],
        compiler_params=_params("arbitrary"),
        name="att_ctx",
    )(sink, proj, proj, proj, proj, proj)


def _att_win_kernel(sink_ref, q_ref, k_ref, v_ref, z0_ref, z1_ref, kc_ref, vc_ref, cos_ref, sin_ref,
                    o_ref, qs_ref, kb_ref, vt_ref, kcb_ref, vct_ref):
    T = q_ref.shape[0]
    L = CHUNK
    n = T // L
    cos = cos_ref[...]
    sin = sin_ref[...]
    for c0 in range(0, ATT_WIDTH, 128):
        q = _att_rope(q_ref[:, c0:c0 + 128].astype(F32), cos, sin)
        qs_ref[:, c0:c0 + 128] = (q * (ATT_HD ** -0.5)).astype(BF16)
    for kv in range(ATT_KV):
        slab = (kv // 2) * 128
        _att_prepare_kv(_att_rope(k_ref[:, slab:slab + 128].astype(F32), cos, sin),
                        v_ref[:, slab:slab + 128].astype(F32), kb_ref, vt_ref, kv, n)
        _att_prepare_kv(kc_ref[:, slab:slab + 128], vc_ref[:, slab:slab + 128], kcb_ref, vct_ref, kv, 1)

    jj = lax.broadcasted_iota(jnp.int32, (L, L), 0)
    ii = lax.broadcasted_iota(jnp.int32, (L, L), 1)
    neg = jnp.full((L, L), -jnp.inf, F32)
    bias_prev = jnp.where(jj >= ii, 0.0, neg)
    bias_next = jnp.where(jj <= ii, 0.0, neg)

    def body(c, carry):
        rows = pl.ds(pl.multiple_of(c * L, L), L)
        c_prev = jnp.maximum(c - 1, 0)
        c_next = jnp.minimum(c + 1, n - 1)
        b_prev = jnp.where(c > 0, bias_prev, neg)
        b_next = jnp.where(c < n - 1, bias_next, neg)
        for kv in range(ATT_KV):
            def keys(cb):
                return kb_ref[kv, pl.ds(pl.multiple_of(cb * L, L), L), :]
            key_sets = [(keys(c_prev), vt_ref[kv, c_prev], b_prev),
                        (keys(c), vt_ref[kv, c], None),
                        (keys(c_next), vt_ref[kv, c_next], b_next),
                        (kcb_ref[kv], vct_ref[kv, 0], None)]
            _att_group(qs_ref, z0_ref, z1_ref, o_ref, sink_ref, rows, kv, key_sets)
        return carry

    lax.fori_loop(0, n, body, 0)


def _att_win(proj, sink, k_ctx, v_ctx, cos, sin):
    T = DEC_SEQ
    half = ATT_WIDTH // 2
    rb = N_PROMPT // T
    return pl.pallas_call(
        _att_win_kernel,
        grid=(DEC_BATCH,),
        in_specs=[pl.BlockSpec(memory_space=pltpu.SMEM),
                  pl.BlockSpec((T, ATT_WIDTH), lambda b: (rb + b, 0)),
                  pl.BlockSpec((T, ATT_KVW), lambda b: (rb + b, ATT_WIDTH // ATT_KVW)),
                  pl.BlockSpec((T, ATT_KVW), lambda b: (rb + b, ATT_WIDTH // ATT_KVW + 1)),
                  pl.BlockSpec((T, half), lambda b: (rb + b, (ATT_WIDTH + 2 * ATT_KVW) // half)),
                  pl.BlockSpec((T, half), lambda b: (rb + b, (ATT_WIDTH + 2 * ATT_KVW) // half + 1)),
                  pl.BlockSpec((None, PAST_LEN, ATT_KVW), lambda b: (b, 0, 0)),
                  pl.BlockSpec((None, PAST_LEN, ATT_KVW), lambda b: (b, 0, 0)),
                  pl.BlockSpec((T, 128), lambda b: (0, 0)),
                  pl.BlockSpec((T, 128), lambda b: (0, 0))],
        out_specs=pl.BlockSpec((T, ATT_WIDTH), lambda b: (b, 0)),
        out_shape=jax.ShapeDtypeStruct((N_LATENT, ATT_WIDTH), BF16),
        scratch_shapes=[pltpu.VMEM((T, ATT_WIDTH), BF16),
                        pltpu.VMEM((ATT_KV, T, 128), BF16),
                        pltpu.VMEM((ATT_KV, T // CHUNK, ATT_HD, CHUNK), BF16),
                        pltpu.VMEM((ATT_KV, PAST_LEN, 128), BF16),
                        pltpu.VMEM((ATT_KV, 1, ATT_HD, PAST_LEN), BF16)],
        compiler_params=_params("arbitrary"),
        name="att_win",
    )(sink, proj, proj, proj, proj, proj, k_ctx, v_ctx, cos, sin)


def _rope_tables(T, hd, reps):
    rows = T // GRID_W
    row = np.repeat(np.arange(rows, dtype=np.float64), GRID_W)
    col = np.tile(np.arange(GRID_W, dtype=np.float64), rows)
    nf = hd // 4
    inv = ROPE_BASE ** (-np.arange(nf, dtype=np.float64) / nf)
    ang = np.concatenate([row[:, None] * inv[None, :], col[:, None] * inv[None, :]], axis=-1)
    cos, sin = np.cos(ang), np.sin(ang)
    return (jnp.asarray(np.tile(np.concatenate([cos, cos], axis=-1), (1, reps)), F32),
            jnp.asarray(np.tile(np.concatenate([-sin, sin], axis=-1), (1, reps)), F32))


def _retention_layer(h, w_in, decay_f, decay_b, gn, state, rope):
    z0 = 2 * RET_QK + RET_WIDTH
    proj = _in_proj(h, w_in, 1024, (z0, z0))
    dec = jnp.stack([decay_f, decay_b]).astype(F32)
    a_p, new_state = _retention(proj, dec, gn, SEQ, BATCH, 0, want_state=True)
    (a_s,) = _retention(proj, dec, gn, DEC_SEQ, DEC_BATCH, N_PROMPT // DEC_SEQ, rope=rope, state=state)
    return a_p, a_s, new_state


def kernel(x_prompt, x_sample, c, c_ctx, state_l0_ret, state_l1_C, state_l1_n, state_l1_m, cache_l2_k, cache_l2_v, state_l3_ret, norm_l0, ada_w_l0, ada_b_l0, w_in_l0, w_out_l0, ret_decay_f_l0, ret_decay_b_l0, ret_gn_l0, norm_l1, ada_w_l1, ada_b_l1, w_in_l1, w_out_l1, conv_w_l1, conv_b_l1, wq_l1, wk_l1, wv_l1, wif_f_l1, bif_f_l1, wif_b_l1, bif_b_l1, gn_l1, skip_l1, norm_l2, ada_w_l2, ada_b_l2, w_in_l2, w_out_l2, sink_l2, norm_l3, ada_w_l3, ada_b_l3, w_in_l3, w_out_l3, ret_decay_f_l3, ret_decay_b_l3, ret_gn_l3, final_norm):
    ct = jnp.concatenate([c_ctx[:, None], c.T, jnp.zeros((D, MOD_ROWS - 1 - DEC_BATCH), F32)], axis=1)
    mods = [_adaln(ct, w, b) for w, b in ((ada_w_l0, ada_b_l0), (ada_w_l1, ada_b_l1),
                                          (ada_w_l2, ada_b_l2), (ada_w_l3, ada_b_l3))]
    rope_ret = _rope_tables(DEC_SEQ, RET_DK, 1)
    rope_att = _rope_tables(DEC_SEQ, ATT_HD, 2)

    x, h = _norm_mod(x_prompt.reshape(N_PROMPT, D), x_sample.reshape(N_LATENT, D), norm_l0, mods[0])

    a_p, a_s, new_l0_ret = _retention_layer(h, w_in_l0, ret_decay_f_l0, ret_decay_b_l0, ret_gn_l0, state_l0_ret,
                                            rope_ret)
    x, h = _out_proj(a_p, a_s, w_out_l0, x, mods[0], norm_l1, mods[1], False)

    wif = jnp.concatenate([wif_f_l1, wif_b_l1], axis=1)
    bif = jnp.concatenate([bif_f_l1, bif_b_l1])
    oz, q, k, v, xc, g = _ml_in_proj(h, w_in_l1, conv_w_l1, conv_b_l1, wq_l1, wk_l1, wv_l1, wif, bif)
    a_p, new_l1_C, n_new, m_new = _ml_scan(oz, q, k, v, xc, g, gn_l1, skip_l1, SEQ, BATCH, 0, want_state=True)
    (a_s,) = _ml_scan(oz, q, k, v, xc, g, gn_l1, skip_l1, DEC_SEQ, DEC_BATCH, N_PROMPT // DEC_SEQ,
                      state=(state_l1_C, state_l1_n, state_l1_m))
    new_l1_n = n_new.reshape(BATCH, 2, ML_HEADS, ML_HD)
    new_l1_m = m_new[:, :, :, 0, 0]
    x, h = _out_proj(a_p, a_s, w_out_l1, x, mods[1], norm_l2, mods[2], False)

    proj = _in_proj(h, w_in_l2, 1280, (ATT_WIDTH + 2 * ATT_KVW,) * 2)
    kv_new = proj[:N_PROMPT, ATT_WIDTH:ATT_WIDTH + 2 * ATT_KVW].astype(F32)
    new_l2_k = kv_new[:, :ATT_KVW].reshape(BATCH, SEQ, ATT_KV, ATT_HD)
    new_l2_v = kv_new[:, ATT_KVW:].reshape(BATCH, SEQ, ATT_KV, ATT_HD)
    a_p = _att_ctx(proj, sink_l2)
    a_s = _att_win(proj, sink_l2, cache_l2_k.reshape(DEC_BATCH, PAST_LEN, ATT_KVW),
                   cache_l2_v.reshape(DEC_BATCH, PAST_LEN, ATT_KVW), rope_att[0], rope_att[1])
    x, h = _out_proj(a_p, a_s, w_out_l2, x, mods[2], norm_l3, mods[3], False)

    a_p, a_s, new_l3_ret = _retention_layer(h, w_in_l3, ret_decay_f_l3, ret_decay_b_l3, ret_gn_l3, state_l3_ret,
                                            rope_ret)
    (y_p,) = _out_proj(a_p, a_s, w_out_l3, x, mods[3], final_norm, mods[3], True, rows=(0, N_PROMPT))
    (y_s,) = _out_proj(a_p, a_s, w_out_l3, x, mods[3], final_norm, mods[3], True, rows=(N_PROMPT, N_LATENT))

    y_prompt = y_p.reshape(BATCH, SEQ, D)
    y_sample = y_s.reshape(DEC_BATCH, DEC_SEQ, D)
    return (y_prompt, y_sample, new_l0_ret, new_l1_C, new_l1_n, new_l1_m, new_l2_k, new_l2_v, new_l3_ret)
```

```python
import functools

import jax
import jax.numpy as jnp
import numpy as np
from jax import lax
from jax.experimental import pallas as pl
from jax.experimental.pallas import tpu as pltpu

F32 = jnp.float32
BF16 = jnp.bfloat16

D = 1024
BATCH = 16
SEQ = 256
DEC_BATCH = 2
DEC_SEQ = 1024
PAST_LEN = 512
GRID_W = 64
CHUNK = 128
EPS = 1e-6
ROPE_BASE = 10000.0

N_PROMPT = BATCH * SEQ
N_LATENT = DEC_BATCH * DEC_SEQ
N_TOK = N_PROMPT + N_LATENT
MOD_ROWS = 8

RET_HEADS = 8
RET_DK = 128
RET_DV = 256
RET_QK = RET_HEADS * RET_DK
RET_WIDTH = RET_HEADS * RET_DV
RET_HB_PROMPT = 8
RET_HB_LATENT = 4
ML_HB_PROMPT = 1
ML_HB_LATENT = 1

ML_HEADS = 4
ML_WIDTH = 2 * D
ML_HD = ML_WIDTH // ML_HEADS
ML_BLOCK = 4
ML_CONV = 5
ML_TILE = 256
ML_ROWS = 1024
ML_TN = 1024
ML_XT = ML_WIDTH // ML_TN
ML_EDGE_ROWS = sorted({r for e in range(0, ML_ROWS + 1, SEQ) for r in (e - 8, e) if 0 <= r < ML_ROWS})

ATT_HEADS = 16
ATT_KV = 4
ATT_HD = 64
ATT_GROUP = ATT_HEADS // ATT_KV
ATT_WIDTH = ATT_HEADS * ATT_HD
ATT_KVW = ATT_KV * ATT_HD
WINDOW = 128

OUT_SUB = 256
IN_SUB = 4

VMEM_LIMIT = 56 * 1024 * 1024


def _params(*sem):
    return pltpu.CompilerParams(dimension_semantics=sem, vmem_limit_bytes=VMEM_LIMIT)


def _mod_row(row0):
    return jnp.maximum((row0 - N_PROMPT) // DEC_SEQ + 1, 0)


def _dot(a, b):
    return jnp.dot(a, b, preferred_element_type=F32)


def _dot_nt(a, b):
    return lax.dot_general(a, b, (((1,), (1,)), ((), ())), preferred_element_type=F32)


def _dot_tn(a, b):
    return lax.dot_general(a, b, (((0,), (0,)), ((), ())), preferred_element_type=F32)


def _silu(x):
    return x * jax.nn.sigmoid(x)


def _log_sigmoid(x):
    return jnp.minimum(x, 0.0) - jnp.log1p(jnp.exp(-jnp.abs(x)))


def _rms(x, g):
    return x * lax.rsqrt(jnp.mean(x * x, axis=-1, keepdims=True) + EPS) * g


def _group_norm(x, g):
    xc = x - jnp.mean(x, axis=-1, keepdims=True)
    return xc * lax.rsqrt(jnp.mean(xc * xc, axis=-1, keepdims=True) + EPS) * g


def _two_source_specs(tm, width, tile0=0):
    n_p = N_PROMPT // tm
    return (pl.BlockSpec((tm, width), lambda i: (jnp.minimum(tile0 + i, n_p - 1), 0)),
            pl.BlockSpec((tm, width), lambda i: (jnp.maximum(tile0 + i - n_p, 0), 0)))


def _ada_kernel(ct_ref, *refs, nl, nt):
    o_ref = refs[-1]
    layer = pl.program_id(0) // nt
    s = _silu(ct_ref[...])
    for l in range(nl):
        @pl.when(layer == l)
        def _(w_ref=refs[l], b_ref=refs[nl + l]):
            w = w_ref[...]
            rows = [jnp.sum(w * s[:, r:r + 1], axis=0, keepdims=True) for r in range(1 + DEC_BATCH)]
            rows.append(jnp.zeros((MOD_ROWS - len(rows), w.shape[1]), F32))
            o_ref[...] = jnp.concatenate(rows, axis=0) + b_ref[...]


def _adaln(ct, ws, bs):
    tn = 1024
    nl, nt = len(ws), 3 * D // tn
    own = lambda l: (lambda j: (0, jnp.clip(j - l * nt, 0, nt - 1)))
    return pl.pallas_call(
        functools.partial(_ada_kernel, nl=nl, nt=nt),
        grid=(nl * nt,),
        in_specs=[pl.BlockSpec((D, MOD_ROWS), lambda j: (0, 0))]
        + [pl.BlockSpec((D, tn), own(l)) for l in range(nl)]
        + [pl.BlockSpec((1, tn), own(l)) for l in range(nl)],
        out_specs=pl.BlockSpec((None, MOD_ROWS, tn), lambda j: (j // nt, 0, j % nt)),
        out_shape=jax.ShapeDtypeStruct((nl, MOD_ROWS, 3 * D), F32),
        compiler_params=_params("arbitrary"),
        name="adaln",
    )(ct, *ws, *[b.reshape(1, 3 * D) for b in bs])


def _in_proj_kernel(*refs, tm, tn, acts, from_x):
    j = pl.program_id(0)
    i = pl.program_id(1)
    w_ref, o_ref, wb_ref = refs[-3:]
    sub = tm // IN_SUB

    @pl.when(i == 0)
    def _():
        wb_ref[...] = w_ref[...].astype(BF16)

    if from_x:
        xp_ref, xs_ref, g_ref, mod_ref = refs[:4]
        m = mod_ref[pl.ds(_mod_row(i * tm), 1), :]

        def lhs(rows):
            y = _rms(jnp.where(i < N_PROMPT // tm, xp_ref[rows, :], xs_ref[rows, :]), g_ref[...])
            return (y * (1.0 + m[:, D:2 * D]) + m[:, :D]).astype(BF16)
    else:
        def lhs(rows):
            return refs[0][rows, :]

    sig0, silu0 = acts
    gated = (j + 1) * tn > min(sig0, silu0)

    @pl.when(jnp.logical_not(gated))
    def _():
        if from_x:
            for s in range(IN_SUB):
                rows = slice(s * sub, (s + 1) * sub)
                o_ref[rows, :] = _dot(lhs(rows), wb_ref[...]).astype(BF16)
        else:
            o_ref[...] = _dot(refs[0][...], wb_ref[...]).astype(BF16)

    @pl.when(gated)
    def _():
        for s in range(IN_SUB):
            rows = slice(s * sub, (s + 1) * sub)
            acc = _dot(lhs(rows), wb_ref[...])
            col = j * tn + lax.broadcasted_iota(jnp.int32, acc.shape, 1)
            sig = jax.nn.sigmoid(acc)
            out = jnp.where(col >= silu0, acc * sig, jnp.where(col >= sig0, sig, acc))
            o_ref[rows, :] = out.astype(BF16)


def _in_proj(h, w, tn, acts):
    tm = 1024
    n = w.shape[1]
    from_x = isinstance(h, tuple)
    if from_x:
        xp, xs, g, mod = h
        n_p = N_PROMPT // tm
        lhs_specs = [pl.BlockSpec((tm, D), lambda j, i: (jnp.minimum(i, n_p - 1), 0)),
                     pl.BlockSpec((tm, D), lambda j, i: (jnp.maximum(i - n_p, 0), 0)),
                     pl.BlockSpec((1, D), lambda j, i: (0, 0)),
                     pl.BlockSpec((MOD_ROWS, 3 * D), lambda j, i: (0, 0))]
        lhs_args = [xp, xs, g.reshape(1, D), mod]
    else:
        lhs_specs = [pl.BlockSpec((tm, D), lambda j, i: (i, 0))]
        lhs_args = [h]
    return pl.pallas_call(
        functools.partial(_in_proj_kernel, tm=tm, tn=tn, acts=acts, from_x=from_x),
        grid=(n // tn, N_TOK // tm),
        in_specs=lhs_specs + [pl.BlockSpec((D, tn), lambda j, i: (0, j))],
        out_specs=pl.BlockSpec((tm, tn), lambda j, i: (i, j)),
        out_shape=jax.ShapeDtypeStruct((N_TOK, n), BF16),
        scratch_shapes=[pltpu.VMEM((D, tn), BF16)],
        compiler_params=_params("arbitrary", "arbitrary"),
        name="in_proj",
    )(*lhs_args, w)


def _out_proj_kernel(ap_ref, as_ref, w_ref, mod_ref, g_ref, modn_ref, *rest, tm, tile0, final, nx):
    x_refs, rest = rest[:nx], rest[nx:]
    wb_ref = rest[-1]

    @pl.when(pl.program_id(0) == 0)
    def _():
        wb_ref[...] = w_ref[...].astype(BF16)

    i = tile0 + pl.program_id(0)
    r = _mod_row(i * tm)
    gate = mod_ref[pl.ds(r, 1), :][:, 2 * D:]
    from_prompt = i < N_PROMPT // tm
    for s in range(tm // OUT_SUB):
        rows = slice(s * OUT_SUB, (s + 1) * OUT_SUB)
        a = jnp.where(from_prompt, ap_ref[rows, :], as_ref[rows, :])
        x = x_refs[0][rows, :] if nx == 1 else jnp.where(from_prompt, x_refs[0][rows, :], x_refs[1][rows, :])
        xn = x + gate * _dot(a, wb_ref[...])
        y = _rms(xn, g_ref[...])
        if final:
            rest[0][rows, :] = y
        else:
            mn = modn_ref[pl.ds(r, 1), :]
            rest[0][rows, :] = xn
            rest[1][rows, :] = (y * (1.0 + mn[:, D:2 * D]) + mn[:, :D]).astype(BF16)


def _out_proj(a_p, a_s, w, x, mod, g_next, mod_next, final, rows=(0, N_TOK)):
    tm = 512
    kw = w.shape[0]
    tile0, nt = rows[0] // tm, rows[1] // tm
    row = pl.BlockSpec((tm, D), lambda i: (i, 0))
    if final:
        out_specs, out_shape = [row], [jax.ShapeDtypeStruct((rows[1], D), F32)]
    else:
        out_specs = [row, row]
        out_shape = [jax.ShapeDtypeStruct((rows[1], D), F32), jax.ShapeDtypeStruct((rows[1], D), BF16)]
    if isinstance(x, tuple):
        x_specs, xs = list(_two_source_specs(tm, D, tile0)), list(x)
    else:
        x_specs, xs = [pl.BlockSpec((tm, D), lambda i: (tile0 + i, 0))], [x]
    return pl.pallas_call(
        functools.partial(_out_proj_kernel, tm=tm, tile0=tile0, final=final, nx=len(xs)),
        grid=(nt,),
        in_specs=[*_two_source_specs(tm, kw, tile0),
                  pl.BlockSpec((kw, D), lambda i: (0, 0)),
                  pl.BlockSpec((MOD_ROWS, 3 * D), lambda i: (0, 0)),
                  pl.BlockSpec((1, D), lambda i: (0, 0)),
                  pl.BlockSpec((MOD_ROWS, 3 * D), lambda i: (0, 0)),
                  *x_specs],
        out_specs=out_specs,
        out_shape=out_shape,
        scratch_shapes=[pltpu.VMEM((kw, D), BF16)],
        compiler_params=_params("arbitrary"),
        name="out_proj",
    )(a_p, a_s, w, mod, g_next.reshape(1, D), mod_next, *xs)


def _ret_kernel(*refs, T, HB, rope, state_in, state_out):
    refs = list(refs)
    dec_ref, q_ref, k_ref, v_ref, z_ref, gn_ref = refs[:6]
    pos = 6
    if rope:
        cos_ref, sin_ref = refs[pos:pos + 2]
        pos += 2
    if state_in:
        s0_ref = refs[pos]
        pos += 1
    o_ref = refs[pos]
    pos += 1
    if state_out:
        sn_ref = refs[pos]
        pos += 1
    kv_ref, sf_ref, sb_ref, tab_ref, gblk_ref = refs[pos:pos + 5]
    pos += 5
    if rope:
        kr_ref = refs[pos]

    hg = pl.program_id(1)
    n = T // CHUNK
    L = CHUNK
    ii = lax.broadcasted_iota(jnp.int32, (L, L), 0).astype(F32)
    jj = lax.broadcasted_iota(jnp.int32, (L, L), 1).astype(F32)
    scale = RET_DK ** -0.5

    def chunk(c):
        return slice(c * L, (c + 1) * L)

    def rotate(x, c):
        return x * cos_ref[chunk(c), :] + pltpu.roll(x, RET_DK // 2, 1) * sin_ref[chunk(c), :]

    @pl.when(pl.program_id(0) == 0)
    def _():
        for hh in range(HB):
            head = hg * HB + hh
            lg_f = _log_sigmoid(jnp.full((1, RET_DV), dec_ref[0, head], F32))
            lg_b = _log_sigmoid(jnp.full((1, RET_DV), dec_ref[1, head], F32))
            lf, lb = lg_f[:, :L], lg_b[:, :L]
            tab_ref[head, 0] = jnp.exp(lf * (L - 1.0 - ii)) * scale
            tab_ref[head, 1] = jnp.exp(lb * ii) * scale
            tab_ref[head, 2] = jnp.exp(lf * (ii + 1.0))
            tab_ref[head, 3] = jnp.exp(lb * (L - ii))
            tab_ref[head, 4] = (jnp.where(ii >= jj, jnp.exp(lf * jnp.maximum(ii - jj, 0.0)), 0.0)
                                + jnp.where(jj >= ii, jnp.exp(lb * jnp.maximum(jj - ii, 0.0)), 0.0)) * scale
            gblk_ref[head, 0] = jnp.exp(lg_f * float(L))
            gblk_ref[head, 1] = jnp.exp(lg_b * float(L))

    for hh in range(HB):
        head = hg * HB + hh
        qs = slice(hh * RET_DK, (hh + 1) * RET_DK)
        vs = slice(hh * RET_DV, (hh + 1) * RET_DV)
        k_dec_f, k_dec_b, q_dec_f, q_dec_b, mask = (tab_ref[head, t] for t in range(5))
        g_f = gblk_ref[head, 0]
        g_b = gblk_ref[head, 1]

        for c in range(n):
            kc = k_ref[chunk(c), qs].astype(F32)
            if rope:
                kc = rotate(kc, c)
                kr_ref[chunk(c), qs] = kc.astype(BF16)
            kk = jnp.concatenate([kc * k_dec_f, kc * k_dec_b], axis=1).astype(BF16)
            kv_ref[hh, c] = _dot_tn(kk, v_ref[chunk(c), vs])
        S = s0_ref[0, hh] if state_in else None
        has_f = []
        for c in range(n):
            has_f.append(S is not None)
            if S is not None:
                sf_ref[hh, c] = S.astype(BF16)
            kvc = kv_ref[hh, c, :RET_DK, :]
            S = kvc if S is None else S * g_f + kvc
        if state_out:
            sn_ref[0, hh] = S
        S = s0_ref[1, hh] if state_in else None
        has_b = [False] * n
        for c in reversed(range(n)):
            has_b[c] = S is not None
            if S is not None:
                sb_ref[hh, c] = S.astype(BF16)
            kvc = kv_ref[hh, c, RET_DK:, :]
            S = kvc if S is None else S * g_b + kvc
        if state_out:
            sn_ref[1, hh] = S

        for c in range(n):
            qc = q_ref[chunk(c), qs]
            qf = qc.astype(F32)
            if rope:
                qf = rotate(qf, c)
                qc = qf.astype(BF16)
                kc = kr_ref[chunk(c), qs]
            else:
                kc = k_ref[chunk(c), qs]
            lhs = [(_dot_nt(qc, kc) * mask).astype(BF16)]
            rhs = [v_ref[chunk(c), vs]]
            if has_f[c]:
                lhs.append((qf * q_dec_f).astype(BF16))
                rhs.append(sf_ref[hh, c])
            if has_b[c]:
                lhs.append((qf * q_dec_b).astype(BF16))
                rhs.append(sb_ref[hh, c])
            o = _dot(jnp.concatenate(lhs, axis=1), jnp.concatenate(rhs, axis=0))
            zf = z_ref[chunk(c), vs].astype(F32)
            o_ref[chunk(c), vs] = (_group_norm(o, gn_ref[:, vs]) * zf).astype(BF16)


def _retention(proj, dec, gn, T, nb, row_blk0, rope=None, state=None, want_state=False):
    HB = RET_HB_PROMPT if T == SEQ else RET_HB_LATENT
    ng = RET_HEADS // HB
    n = T // CHUNK
    qw, vw = HB * RET_DK, HB * RET_DV
    in_specs = [pl.BlockSpec(memory_space=pltpu.SMEM),
                pl.BlockSpec((T, qw), lambda b, h: (row_blk0 + b, h)),
                pl.BlockSpec((T, qw), lambda b, h: (row_blk0 + b, ng + h)),
                pl.BlockSpec((T, vw), lambda b, h: (row_blk0 + b, ng + h)),
                pl.BlockSpec((T, vw), lambda b, h: (row_blk0 + b, 2 * ng + h)),
                pl.BlockSpec((1, vw), lambda b, h: (0, h))]
    args = [dec, proj, proj, proj, proj, gn.reshape(1, RET_WIDTH)]
    if rope is not None:
        in_specs += [pl.BlockSpec((T, RET_DK), lambda b, h: (0, 0))] * 2
        args += list(rope)
    if state is not None:
        in_specs.append(pl.BlockSpec((None, 2, HB, RET_DK, RET_DV), lambda b, h: (b, 0, h, 0, 0)))
        args.append(state)
    out_specs = [pl.BlockSpec((T, vw), lambda b, h: (b, h))]
    out_shape = [jax.ShapeDtypeStruct((nb * T, RET_WIDTH), BF16)]
    if want_state:
        out_specs.append(pl.BlockSpec((None, 2, HB, RET_DK, RET_DV), lambda b, h: (b, 0, h, 0, 0)))
        out_shape.append(jax.ShapeDtypeStruct((nb, 2, RET_HEADS, RET_DK, RET_DV), F32))
    scratch = [pltpu.VMEM((HB, n, 2 * RET_DK, RET_DV), F32),
               pltpu.VMEM((HB, n, RET_DK, RET_DV), BF16),
               pltpu.VMEM((HB, n, RET_DK, RET_DV), BF16),
               pltpu.VMEM((RET_HEADS, 5, CHUNK, CHUNK), F32),
               pltpu.VMEM((RET_HEADS, 2, 1, RET_DV), F32)]
    if rope is not None:
        scratch.append(pltpu.VMEM((T, qw), BF16))
    return pl.pallas_call(
        functools.partial(_ret_kernel, T=T, HB=HB, rope=rope is not None, state_in=state is not None,
                          state_out=want_state),
        grid=(nb, ng),
        in_specs=in_specs, out_specs=out_specs, out_shape=out_shape, scratch_shapes=scratch,
        compiler_params=_params("arbitrary", "arbitrary"),
        name="retention",
    )(*args)


def _ml_in_kernel(h_ref, w_ref, cw_ref, cb_ref, bq_ref, bk_ref, bv_ref, wq_ref, wk_ref, wv_ref, bias_ref,
                  oz_ref, q_ref, k_ref, v_ref, xc_ref, g_ref, wb_ref, bd_ref, conv_ref):
    j = pl.program_id(0)
    i = pl.program_id(1)
    nsub = ML_TN // ML_TILE

    @pl.when(i == 0)
    def _():
        wb_ref[...] = w_ref[...].astype(BF16)

    @pl.when((i == 0) & (j < ML_XT))
    def _():
        row = lax.broadcasted_iota(jnp.int32, (ML_TILE, ML_TILE), 0)
        col = lax.broadcasted_iota(jnp.int32, (ML_TILE, ML_TILE), 1)
        shift = ML_BLOCK.bit_length() - 1
        same_block = jnp.right_shift(row, shift) == jnp.right_shift(col, shift)
        for which, b_ref in enumerate((bq_ref, bk_ref, bv_ref)):
            for t in range(nsub):
                w = b_ref[t * ML_TILE:(t + 1) * ML_TILE, :]
                rep = w[:, ML_BLOCK - 1:ML_BLOCK]
                for dd in range(ML_BLOCK - 1):
                    rep = jnp.where((col & (ML_BLOCK - 1)) == dd, w[:, dd:dd + 1], rep)
                bd_ref[which, t] = jnp.where(same_block, rep, 0.0).astype(BF16)

    @pl.when(j >= ML_XT)
    def _():
        sub = ML_ROWS // IN_SUB
        for s in range(IN_SUB):
            rows = slice(s * sub, (s + 1) * sub)
            acc = _dot(h_ref[rows, :], wb_ref[...])
            col = j * ML_TN + lax.broadcasted_iota(jnp.int32, acc.shape, 1)
            sig = jax.nn.sigmoid(acc)
            oz_ref[rows, :] = jnp.where(col >= 2 * ML_WIDTH, acc * sig, sig).astype(BF16)

    @pl.when(j < ML_XT)
    def _():
        acc = _dot(h_ref[...], wb_ref[...])
        T = jnp.where(i < N_PROMPT // ML_ROWS, SEQ, DEC_SEQ)
        pad = ML_CONV // 2
        taps = [tap for tap in range(ML_CONV) if tap != pad]
        sub8 = lax.broadcasted_iota(jnp.int32, (8, ML_TILE), 0)
        g = jnp.where(j == 0, 1.0, 0.0) * bias_ref[...]
        for t in range(nsub):
            cols = slice(t * ML_TILE, (t + 1) * ML_TILE)
            x = acc[:, cols]
            shifted = {tap: pltpu.roll(x, (pad - tap) % ML_ROWS, 0) for tap in taps}
            conv = cb_ref[:, cols] + x * cw_ref[pad:pad + 1, cols]
            for tap in taps:
                conv = conv + shifted[tap] * cw_ref[tap:tap + 1, cols]
            conv_ref[...] = conv
            for r0 in ML_EDGE_ROWS:
                rows = slice(r0, r0 + 8)
                pos = (r0 + sub8) & (T - 1)
                fixed = cb_ref[:, cols] + x[rows] * cw_ref[pad:pad + 1, cols]
                for tap in taps:
                    ok = (pos + (tap - pad) >= 0) & (pos + (tap - pad) < T)
                    fixed = fixed + jnp.where(ok, shifted[tap][rows], 0.0) * cw_ref[tap:tap + 1, cols]
                conv_ref[rows, :] = fixed
            xcb = _silu(conv_ref[...]).astype(BF16)
            qb = _dot(xcb, bd_ref[0, t]).astype(BF16)
            kb = _dot(xcb, bd_ref[1, t]).astype(BF16)
            vb = _dot(x.astype(BF16), bd_ref[2, t]).astype(BF16)
            q_ref[:, cols] = qb
            k_ref[:, cols] = kb
            v_ref[:, cols] = vb
            xc_ref[:, cols] = xcb
            g = g + (_dot(qb, wq_ref[cols, :].astype(BF16)) + _dot(kb, wk_ref[cols, :].astype(BF16))
                     + _dot(vb, wv_ref[cols, :].astype(BF16)))
        g_ref[...] = g


def _ml_in_proj(h, w, conv_w, conv_b, wq, wk, wv, wif, bif):
    ng = wif.shape[1]
    ni = N_TOK // ML_ROWS
    xt = ML_XT
    wq, wk, wv = (a.reshape(ML_WIDTH, ML_BLOCK) for a in (wq, wk, wv))
    xcol = lambda j, i: (0, jnp.minimum(j, xt - 1))
    xrow = lambda off: (lambda j, i: (off + jnp.minimum(j, xt - 1), 0))
    x_out = pl.BlockSpec((ML_ROWS, ML_TN), lambda j, i: (jnp.where(j < xt, i, ni - 1), jnp.minimum(j, xt - 1)))
    return pl.pallas_call(
        _ml_in_kernel,
        grid=(3 * ML_WIDTH // ML_TN, ni),
        in_specs=[pl.BlockSpec((ML_ROWS, D), lambda j, i: (i, 0)),
                  pl.BlockSpec((D, ML_TN), lambda j, i: (0, j)),
                  pl.BlockSpec((ML_CONV, ML_TN), xcol),
                  pl.BlockSpec((1, ML_TN), xcol),
                  pl.BlockSpec((ML_TN, ML_BLOCK), xrow(0)),
                  pl.BlockSpec((ML_TN, ML_BLOCK), xrow(0)),
                  pl.BlockSpec((ML_TN, ML_BLOCK), xrow(0)),
                  pl.BlockSpec((ML_TN, ng), xrow(0)),
                  pl.BlockSpec((ML_TN, ng), xrow(xt)),
                  pl.BlockSpec((ML_TN, ng), xrow(2 * xt)),
                  pl.BlockSpec((1, ng), lambda j, i: (0, 0))],
        out_specs=[pl.BlockSpec((ML_ROWS, ML_TN), lambda j, i: (jnp.where(j < xt, 0, i), jnp.maximum(j - xt, 0))),
                   x_out, x_out, x_out, x_out,
                   pl.BlockSpec((None, ML_ROWS, ng),
                                lambda j, i: (jnp.minimum(j, xt - 1), jnp.where(j < xt, i, ni - 1), 0))],
        out_shape=[jax.ShapeDtypeStruct((N_TOK, 2 * ML_WIDTH), BF16)]
        + [jax.ShapeDtypeStruct((N_TOK, ML_WIDTH), BF16)] * 4
        + [jax.ShapeDtypeStruct((xt, N_TOK, ng), F32)],
        scratch_shapes=[pltpu.VMEM((D, ML_TN), BF16), pltpu.VMEM((3, ML_TN // ML_TILE, ML_TILE, ML_TILE), BF16),
                        pltpu.VMEM((ML_ROWS, ML_TILE), F32)],
        compiler_params=_params("arbitrary", "arbitrary"),
        name="mlstm_in_proj",
    )(h, w, conv_w, conv_b.reshape(1, ML_WIDTH), wq, wk, wv, wif, wif, wif, bif.reshape(1, ng))


def _dot_split(lhs_bf16, rhs):
    r1 = rhs.astype(BF16)
    e1 = rhs - r1.astype(F32)
    r2 = e1.astype(BF16)
    r3 = (e1 - r2.astype(F32)).astype(BF16)
    return _dot(lhs_bf16, r1) + _dot(lhs_bf16, r2) + _dot(lhs_bf16, r3)


def _ml_scan_kernel(*refs, T, HB, state_in, state_out):
    refs = list(refs)
    pos = 0
    m0_ref = c0_ref = n0_ref = cn_ref = nn_ref = mn_ref = None
    if state_in:
        m0_ref = refs[0]
        pos = 1
    q_ref, k_ref, v_ref, g_ref, op_ref, z_ref, xc_ref, gn_ref, sk_ref = refs[pos:pos + 9]
    pos += 9
    if state_in:
        c0_ref, n0_ref = refs[pos:pos + 2]
        pos += 2
    o_ref = refs[pos]
    pos += 1
    if state_out:
        cn_ref, nn_ref, mn_ref = refs[pos:pos + 3]
        pos += 3
    acc_ref, s_ref = refs[pos:pos + 2]

    for hh in range(HB):
        cols = slice(hh * ML_HD, (hh + 1) * ML_HD)

        def head_cols(ref):
            return ref.at[:, cols]

        def head_state(ref):
            return None if ref is None else ref.at[:, hh]

        _ml_scan_head(pl.program_id(0), pl.program_id(1) * HB + hh, m0_ref,
                      head_cols(q_ref), head_cols(k_ref), head_cols(v_ref), g_ref, head_cols(op_ref),
                      head_cols(z_ref), head_cols(xc_ref), head_cols(gn_ref), head_cols(sk_ref),
                      head_state(c0_ref), head_state(n0_ref), head_cols(o_ref),
                      head_state(cn_ref), head_state(nn_ref), head_state(mn_ref), acc_ref.at[hh], s_ref.at[hh],
                      T=T, state_in=state_in, state_out=state_out)


def _ml_scan_head(b, h, m0_ref, q_ref, k_ref, v_ref, g_ref, op_ref, z_ref, xc_ref, gn_ref, sk_ref,
                  c0_ref, n0_ref, o_ref, cn_ref, nn_ref, mn_ref, acc_ref, s_ref, *, T, state_in, state_out):
    n = T // CHUNK
    L = CHUNK
    scale = ML_HD ** -0.5

    def chunk(c):
        return slice(c * L, (c + 1) * L)

    ii = lax.broadcasted_iota(jnp.int32, (L, L), 0)
    jj = lax.broadcasted_iota(jnp.int32, (L, L), 1)
    neg = jnp.full((L, L), -jnp.inf, F32)
    zero = jnp.zeros((L, L), F32)
    tri = [jnp.where(jj <= ii, 1.0, 0.0).astype(BF16), jnp.where(jj >= ii, 1.0, 0.0).astype(BF16)]
    bias = [jnp.where(ii >= jj, zero, neg), jnp.where(ii <= jj, zero, neg)]

    g = g_ref[0]
    for part in range(1, ML_XT):
        g = g + g_ref[part]
    col = lax.broadcasted_iota(jnp.int32, g.shape, 1)
    g = jnp.where((col & (2 * ML_HEADS - 1)) >= ML_HEADS, _log_sigmoid(g), g)

    def gate_column(j):
        return jnp.sum(jnp.where(col == j, g, 0.0), axis=1, keepdims=True)


    for i in range(n):
        for j in range(n):
            s_ref[i, j] = _dot_nt(q_ref[chunk(i), :], k_ref[chunk(j), :])

    G = []
    for d in range(2):
        order = list(range(n)) if d == 0 else list(reversed(range(n)))
        gi_all = gate_column(d * 2 * ML_HEADS + h)
        gf_all = gate_column(d * 2 * ML_HEADS + ML_HEADS + h)
        m0 = jnp.full((1, 1), m0_ref[(b * 2 + d) * ML_HEADS + h], F32) if state_in else jnp.zeros((1, 1), F32)
        c_col, c_row, b_col, m_row = [None] * n, [None] * n, [None] * n, [None] * n
        offset = jnp.zeros((1, 1), F32)
        carry = m0
        for c in order:
            gi = gi_all[chunk(c)]
            gf = gf_all[chunk(c)]
            cum = _dot_split(tri[d], jnp.broadcast_to(gf, (L, L))) + offset
            offset = offset + jnp.sum(gf, axis=0, keepdims=True)
            cc = gi - cum
            c_col[c] = cc[:, :1]
            c_row[c] = jnp.transpose(cc)
            b_col[c] = cum[:, :1]
            m_row[c] = jnp.maximum(jnp.max(c_row[c] + bias[d], axis=1, keepdims=True), carry)
            carry = jnp.maximum(carry, jnp.max(c_col[c], axis=0, keepdims=True))
        G.append(dict(order=order, c_col=c_col, c_row=c_row, b_col=b_col, m_row=m_row, m0=m0,
                      m_end=carry, b_end=offset))

    done = set()
    for i in range(n):
        for d in range(2):
            e = G[d]
            js = list(range(0, i + 1)) if d == 0 else list(range(i, n))
            m_i = e["m_row"][i]
            parts = []
            tot = None
            for j in js:
                z = e["c_row"][j] - m_i
                if j == i:
                    z = z + bias[d]
                sw = s_ref[i, j] * (jnp.exp(z) * scale)
                tot = sw if tot is None else tot + sw
                parts.append(sw.astype(BF16))
            num = _dot(jnp.concatenate(parts, axis=1), v_ref[js[0] * L:(js[-1] + 1) * L, :])
            den = jnp.sum(tot, axis=1, keepdims=True)
            if state_in:
                qc = q_ref[chunk(i), :]
                w0 = jnp.exp(e["m0"] - m_i)
                num = num + _dot((qc.astype(F32) * w0).astype(BF16), c0_ref[d].astype(BF16))
                n0 = jnp.broadcast_to(n0_ref[d], (8, ML_HD)).astype(BF16)
                den = den + w0 * _dot_nt(qc, n0)[:, :1]
            hb = num / jnp.maximum(jnp.abs(den), jnp.exp(-(e["b_col"][i] + m_i)))
            if i not in done:
                done.add(i)
                acc_ref[chunk(i), :] = hb
            else:
                cell = _group_norm((acc_ref[chunk(i), :] + hb) * op_ref[chunk(i), :].astype(F32), gn_ref[...])
                mixed = cell + sk_ref[...] * xc_ref[chunk(i), :].astype(F32)
                o_ref[chunk(i), :] = (mixed * z_ref[chunk(i), :].astype(F32)).astype(BF16)

    if state_out:
        ones = jnp.ones((8, T), BF16)
        for d in range(2):
            e = G[d]
            kw = jnp.concatenate(
                [(k_ref[chunk(c), :].astype(F32) * (jnp.exp(e["c_col"][c] - e["m_end"]) * scale)).astype(BF16)
                 for c in range(n)], axis=0)
            C = _dot_tn(kw, v_ref[...])
            nvec = _dot(ones, kw)[:1]
            if state_in:
                w_end = jnp.exp(e["m0"] - e["m_end"])
                C = C + w_end * c0_ref[d]
                nvec = nvec + w_end * n0_ref[d]
            cn_ref[d] = C
            nn_ref[d] = nvec
            mn_ref[d] = jnp.broadcast_to(e["b_end"] + e["m_end"], mn_ref.shape[1:])


def _ml_scan(oz, q, k, v, xc, g, gn, skip, T, nb, row_blk0, state=None, want_state=False):
    H = ML_HEADS
    HB = ML_HB_PROMPT if T == SEQ else ML_HB_LATENT
    ng = H // HB
    hw = HB * ML_HD
    n = T // CHUNK
    tok = lambda b, h: (row_blk0 + b, h)
    head_vec = pl.BlockSpec((1, hw), lambda b, h: (0, h))
    both = lambda *tail: pl.BlockSpec((None, 2, HB, *tail), lambda b, h: (b, 0, h, 0, 0))
    in_specs, args = [], []
    if state is not None:
        c0, n0, m0 = state
        in_specs.append(pl.BlockSpec(memory_space=pltpu.SMEM))
        args.append(m0.reshape(nb * 2 * H))
    in_specs += [pl.BlockSpec((T, hw), tok)] * 3
    in_specs += [pl.BlockSpec((ML_XT, T, g.shape[2]), lambda b, h: (0, row_blk0 + b, 0)),
                 pl.BlockSpec((T, hw), tok),
                 pl.BlockSpec((T, hw), lambda b, h: (row_blk0 + b, ng + h)),
                 pl.BlockSpec((T, hw), tok), head_vec, head_vec]
    args += [q, k, v, g, oz, oz, xc, gn.reshape(1, ML_WIDTH), skip.reshape(1, ML_WIDTH)]
    if state is not None:
        in_specs += [both(ML_HD, ML_HD), both(1, ML_HD)]
        args += [c0, n0.reshape(nb, 2, H, 1, ML_HD)]
    out_specs = [pl.BlockSpec((T, hw), lambda b, h: (b, h))]
    out_shape = [jax.ShapeDtypeStruct((nb * T, ML_WIDTH), BF16)]
    if want_state:
        out_specs += [both(ML_HD, ML_HD), both(1, ML_HD), both(1, 128)]
        out_shape += [jax.ShapeDtypeStruct((nb, 2, H, ML_HD, ML_HD), F32),
                      jax.ShapeDtypeStruct((nb, 2, H, 1, ML_HD), F32),
                      jax.ShapeDtypeStruct((nb, 2, H, 1, 128), F32)]
    scratch = [pltpu.VMEM((HB, T, ML_HD), F32), pltpu.VMEM((HB, n, n, CHUNK, CHUNK), F32)]
    return pl.pallas_call(
        functools.partial(_ml_scan_kernel, T=T, HB=HB, state_in=state is not None, state_out=want_state),
        grid=(nb, ng),
        in_specs=in_specs, out_specs=out_specs, out_shape=out_shape, scratch_shapes=scratch,
        compiler_params=_params("arbitrary", "arbitrary"),
        name="mlstm_scan",
    )(*args)


def _att_rope(x, cos, sin):
    lane = lax.broadcasted_iota(jnp.int32, x.shape, 1)
    half = ATT_HD // 2
    rot = jnp.where((lane & (ATT_HD - 1)) < half, pltpu.roll(x, 128 - half, 1), pltpu.roll(x, half, 1))
    return x * cos + rot * sin


def _att_heads(qms, sinks, key_sets):
    scores = [[_dot_nt(k_both, qm) if bias is None else _dot_nt(k_both, qm) + bias
               for k_both, _, bias in key_sets] for qm in qms]
    ms = []
    for sink, per_set in zip(sinks, scores):
        m = sink
        for s in per_set:
            m = jnp.maximum(m, jnp.max(s, axis=0, keepdims=True))
        ms.append(m)
    probs = [[jnp.exp(s - m) for s in per_set] for m, per_set in zip(ms, scores)]
    outs = []
    for sink, m, per_set in zip(sinks, ms, probs):
        den = jnp.exp(sink - m)
        out = None
        for p, (_, v_t, _) in zip(per_set, key_sets):
            den = den + jnp.sum(p, axis=0, keepdims=True)
            o = _dot(v_t, p.astype(BF16))
            out = o if out is None else out + o
        outs.append(out * (1.0 / den))
    return outs


def _att_prepare_kv(k, v, kb_ref, vt_ref, kv, nblk):
    lane = lax.broadcasted_iota(jnp.int32, k.shape, 1)
    native = lane < ATT_HD if kv % 2 == 0 else lane >= ATT_HD
    kb_ref[kv] = jnp.where(native, k, pltpu.roll(k, ATT_HD, 1)).astype(BF16)
    r0 = (kv % 2) * ATT_HD
    blk = k.shape[0] // nblk
    for c in range(nblk):
        vt_ref[kv, c] = jnp.transpose(v[c * blk:(c + 1) * blk, :])[r0:r0 + ATT_HD, :].astype(BF16)


def _att_group(q_ref, z0_ref, z1_ref, o_ref, sink_ref, rows, kv, key_sets):
    lo = lax.broadcasted_iota(jnp.int32, (CHUNK, 128), 1) < ATT_HD
    zero = jnp.zeros((CHUNK, 128), BF16)
    qms, sinks = [], []
    for g in range(ATT_GROUP):
        col = (2 * kv + g // 2) * 128
        q = q_ref[rows, col:col + 128]
        qms.append(jnp.where(lo, q, zero) if g % 2 == 0 else jnp.where(lo, zero, q))
        sinks.append(jnp.full((1, CHUNK), sink_ref[kv * ATT_GROUP + g], F32))
    heads = _att_heads(qms, sinks, key_sets)
    for p in range(2):
        col = (2 * kv + p) * 128
        out = jnp.transpose(jnp.concatenate(heads[2 * p:2 * p + 2], axis=0))
        z_ref = z0_ref if col < ATT_WIDTH // 2 else z1_ref
        zc = col % (ATT_WIDTH // 2)
        zf = z_ref[rows, zc:zc + 128].astype(F32)
        o_ref[rows, col:col + 128] = (out * zf).astype(BF16)


def _att_ctx_kernel(sink_ref, q_ref, k_ref, v_ref, z0_ref, z1_ref, o_ref, qs_ref, kb_ref, vt_ref):
    T = q_ref.shape[0]
    qs_ref[...] = q_ref[...] * (ATT_HD ** -0.5)
    for kv in range(ATT_KV):
        slab = (kv // 2) * 128
        _att_prepare_kv(k_ref[:, slab:slab + 128].astype(F32), v_ref[:, slab:slab + 128].astype(F32),
                        kb_ref, vt_ref, kv, 1)
    for c in range(T // CHUNK):
        rows = slice(c * CHUNK, (c + 1) * CHUNK)
        for kv in range(ATT_KV):
            _att_group(qs_ref, z0_ref, z1_ref, o_ref, sink_ref, rows, kv, [(kb_ref[kv], vt_ref[kv, 0], None)])


def _att_ctx(proj, sink):
    T = SEQ
    half = ATT_WIDTH // 2
    return pl.pallas_call(
        _att_ctx_kernel,
        grid=(BATCH,),
        in_specs=[pl.BlockSpec(memory_space=pltpu.SMEM),
                  pl.BlockSpec((T, ATT_WIDTH), lambda b: (b, 0)),
                  pl.BlockSpec((T, ATT_KVW), lambda b: (b, ATT_WIDTH // ATT_KVW)),
                  pl.BlockSpec((T, ATT_KVW), lambda b: (b, ATT_WIDTH // ATT_KVW + 1)),
                  pl.BlockSpec((T, half), lambda b: (b, (ATT_WIDTH + 2 * ATT_KVW) // half)),
                  pl.BlockSpec((T, half), lambda b: (b, (ATT_WIDTH + 2 * ATT_KVW) // half + 1))],
        out_specs=pl.BlockSpec((T, ATT_WIDTH), lambda b: (b, 0)),
        out_shape=jax.ShapeDtypeStruct((N_PROMPT, ATT_WIDTH), BF16),
        scratch_shapes=[pltpu.VMEM((T, ATT_WIDTH), BF16), pltpu.VMEM((ATT_KV, T, 128), BF16),
                        pltpu.VMEM((ATT_KV, 1, ATT_HD, T), BF16)],
        compiler_params=_params("arbitrary"),
        name="att_ctx",
    )(sink, proj, proj, proj, proj, proj)


def _att_win_kernel(sink_ref, q_ref, k_ref, v_ref, z0_ref, z1_ref, kc_ref, vc_ref, cos_ref, sin_ref,
                    o_ref, qs_ref, kb_ref, vt_ref, kcb_ref, vct_ref):
    T = q_ref.shape[0]
    L = CHUNK
    n = T // L
    cos = cos_ref[...]
    sin = sin_ref[...]
    for c0 in range(0, ATT_WIDTH, 128):
        q = _att_rope(q_ref[:, c0:c0 + 128].astype(F32), cos, sin)
        qs_ref[:, c0:c0 + 128] = (q * (ATT_HD ** -0.5)).astype(BF16)
    for kv in range(ATT_KV):
        slab = (kv // 2) * 128
        _att_prepare_kv(_att_rope(k_ref[:, slab:slab + 128].astype(F32), cos, sin),
                        v_ref[:, slab:slab + 128].astype(F32), kb_ref, vt_ref, kv, n)
        _att_prepare_kv(kc_ref[:, slab:slab + 128], vc_ref[:, slab:slab + 128], kcb_ref, vct_ref, kv, 1)

    jj = lax.broadcasted_iota(jnp.int32, (L, L), 0)
    ii = lax.broadcasted_iota(jnp.int32, (L, L), 1)
    neg = jnp.full((L, L), -jnp.inf, F32)
    bias_prev = jnp.where(jj >= ii, 0.0, neg)
    bias_next = jnp.where(jj <= ii, 0.0, neg)

    def body(c, carry):
        rows = pl.ds(pl.multiple_of(c * L, L), L)
        c_prev = jnp.maximum(c - 1, 0)
        c_next = jnp.minimum(c + 1, n - 1)
        b_prev = jnp.where(c > 0, bias_prev, neg)
        b_next = jnp.where(c < n - 1, bias_next, neg)
        for kv in range(ATT_KV):
            def keys(cb):
                return kb_ref[kv, pl.ds(pl.multiple_of(cb * L, L), L), :]
            key_sets = [(keys(c_prev), vt_ref[kv, c_prev], b_prev),
                        (keys(c), vt_ref[kv, c], None),
                        (keys(c_next), vt_ref[kv, c_next], b_next),
                        (kcb_ref[kv], vct_ref[kv, 0], None)]
            _att_group(qs_ref, z0_ref, z1_ref, o_ref, sink_ref, rows, kv, key_sets)
        return carry

    lax.fori_loop(0, n, body, 0)


def _att_win(proj, sink, k_ctx, v_ctx, cos, sin):
    T = DEC_SEQ
    half = ATT_WIDTH // 2
    rb = N_PROMPT // T
    return pl.pallas_call(
        _att_win_kernel,
        grid=(DEC_BATCH,),
        in_specs=[pl.BlockSpec(memory_space=pltpu.SMEM),
                  pl.BlockSpec((T, ATT_WIDTH), lambda b: (rb + b, 0)),
                  pl.BlockSpec((T, ATT_KVW), lambda b: (rb + b, ATT_WIDTH // ATT_KVW)),
                  pl.BlockSpec((T, ATT_KVW), lambda b: (rb + b, ATT_WIDTH // ATT_KVW + 1)),
                  pl.BlockSpec((T, half), lambda b: (rb + b, (ATT_WIDTH + 2 * ATT_KVW) // half)),
                  pl.BlockSpec((T, half), lambda b: (rb + b, (ATT_WIDTH + 2 * ATT_KVW) // half + 1)),
                  pl.BlockSpec((None, PAST_LEN, ATT_KVW), lambda b: (b, 0, 0)),
                  pl.BlockSpec((None, PAST_LEN, ATT_KVW), lambda b: (b, 0, 0)),
                  pl.BlockSpec((T, 128), lambda b: (0, 0)),
                  pl.BlockSpec((T, 128), lambda b: (0, 0))],
        out_specs=pl.BlockSpec((T, ATT_WIDTH), lambda b: (b, 0)),
        out_shape=jax.ShapeDtypeStruct((N_LATENT, ATT_WIDTH), BF16),
        scratch_shapes=[pltpu.VMEM((T, ATT_WIDTH), BF16),
                        pltpu.VMEM((ATT_KV, T, 128), BF16),
                        pltpu.VMEM((ATT_KV, T // CHUNK, ATT_HD, CHUNK), BF16),
                        pltpu.VMEM((ATT_KV, PAST_LEN, 128), BF16),
                        pltpu.VMEM((ATT_KV, 1, ATT_HD, PAST_LEN), BF16)],
        compiler_params=_params("arbitrary"),
        name="att_win",
    )(sink, proj, proj, proj, proj, proj, k_ctx, v_ctx, cos, sin)


def _rope_tables(T, hd, reps):
    rows = T // GRID_W
    row = np.repeat(np.arange(rows, dtype=np.float64), GRID_W)
    col = np.tile(np.arange(GRID_W, dtype=np.float64), rows)
    nf = hd // 4
    inv = ROPE_BASE ** (-np.arange(nf, dtype=np.float64) / nf)
    ang = np.concatenate([row[:, None] * inv[None, :], col[:, None] * inv[None, :]], axis=-1)
    cos, sin = np.cos(ang), np.sin(ang)
    return (jnp.asarray(np.tile(np.concatenate([cos, cos], axis=-1), (1, reps)), F32),
            jnp.asarray(np.tile(np.concatenate([-sin, sin], axis=-1), (1, reps)), F32))


def _retention_layer(h, w_in, decay_f, decay_b, gn, state, rope):
    z0 = 2 * RET_QK + RET_WIDTH
    proj = _in_proj(h, w_in, 1024, (z0, z0))
    dec = jnp.stack([decay_f, decay_b]).astype(F32)
    a_p, new_state = _retention(proj, dec, gn, SEQ, BATCH, 0, want_state=True)
    (a_s,) = _retention(proj, dec, gn, DEC_SEQ, DEC_BATCH, N_PROMPT // DEC_SEQ, rope=rope, state=state)
    return a_p, a_s, new_state


def kernel(x_prompt, x_sample, c, c_ctx, state_l0_ret, state_l1_C, state_l1_n, state_l1_m, cache_l2_k, cache_l2_v, state_l3_ret, norm_l0, ada_w_l0, ada_b_l0, w_in_l0, w_out_l0, ret_decay_f_l0, ret_decay_b_l0, ret_gn_l0, norm_l1, ada_w_l1, ada_b_l1, w_in_l1, w_out_l1, conv_w_l1, conv_b_l1, wq_l1, wk_l1, wv_l1, wif_f_l1, bif_f_l1, wif_b_l1, bif_b_l1, gn_l1, skip_l1, norm_l2, ada_w_l2, ada_b_l2, w_in_l2, w_out_l2, sink_l2, norm_l3, ada_w_l3, ada_b_l3, w_in_l3, w_out_l3, ret_decay_f_l3, ret_decay_b_l3, ret_gn_l3, final_norm):
    ct = jnp.concatenate([c_ctx[:, None], c.T, jnp.zeros((D, MOD_ROWS - 1 - DEC_BATCH), F32)], axis=1)
    mod_all = _adaln(ct, (ada_w_l0, ada_w_l1, ada_w_l2, ada_w_l3), (ada_b_l0, ada_b_l1, ada_b_l2, ada_b_l3))
    mods = [mod_all[l] for l in range(mod_all.shape[0])]
    rope_ret = _rope_tables(DEC_SEQ, RET_DK, 1)
    rope_att = _rope_tables(DEC_SEQ, ATT_HD, 2)

    x0 = (x_prompt.reshape(N_PROMPT, D), x_sample.reshape(N_LATENT, D))

    a_p, a_s, new_l0_ret = _retention_layer((*x0, norm_l0, mods[0]), w_in_l0, ret_decay_f_l0, ret_decay_b_l0,
                                            ret_gn_l0, state_l0_ret, rope_ret)
    x, h = _out_proj(a_p, a_s, w_out_l0, x0, mods[0], norm_l1, mods[1], False)

    wif = jnp.concatenate([wif_f_l1, wif_b_l1], axis=1)
    bif = jnp.concatenate([bif_f_l1, bif_b_l1])
    oz, q, k, v, xc, g = _ml_in_proj(h, w_in_l1, conv_w_l1, conv_b_l1, wq_l1, wk_l1, wv_l1, wif, bif)
    a_p, new_l1_C, n_new, m_new = _ml_scan(oz, q, k, v, xc, g, gn_l1, skip_l1, SEQ, BATCH, 0, want_state=True)
    (a_s,) = _ml_scan(oz, q, k, v, xc, g, gn_l1, skip_l1, DEC_SEQ, DEC_BATCH, N_PROMPT // DEC_SEQ,
                      state=(state_l1_C, state_l1_n, state_l1_m))
    new_l1_n = n_new.reshape(BATCH, 2, ML_HEADS, ML_HD)
    new_l1_m = m_new[:, :, :, 0, 0]
    x, h = _out_proj(a_p, a_s, w_out_l1, x, mods[1], norm_l2, mods[2], False)

    proj = _in_proj(h, w_in_l2, 1280, (ATT_WIDTH + 2 * ATT_KVW,) * 2)
    kv_new = proj[:N_PROMPT, ATT_WIDTH:ATT_WIDTH + 2 * ATT_KVW].astype(F32)
    new_l2_k = kv_new[:, :ATT_KVW].reshape(BATCH, SEQ, ATT_KV, ATT_HD)
    new_l2_v = kv_new[:, ATT_KVW:].reshape(BATCH, SEQ, ATT_KV, ATT_HD)
    a_p = _att_ctx(proj, sink_l2)
    a_s = _att_win(proj, sink_l2, cache_l2_k.reshape(DEC_BATCH, PAST_LEN, ATT_KVW),
                   cache_l2_v.reshape(DEC_BATCH, PAST_LEN, ATT_KVW), rope_att[0], rope_att[1])
    x, h = _out_proj(a_p, a_s, w_out_l2, x, mods[2], norm_l3, mods[3], False)

    a_p, a_s, new_l3_ret = _retention_layer(h, w_in_l3, ret_decay_f_l3, ret_decay_b_l3, ret_gn_l3, state_l3_ret,
                                            rope_ret)
    (y_p,) = _out_proj(a_p, a_s, w_out_l3, x, mods[3], final_norm, mods[3], True, rows=(0, N_PROMPT))
    (y_s,) = _out_proj(a_p, a_s, w_out_l3, x, mods[3], final_norm, mods[3], True, rows=(N_PROMPT, N_LATENT))

    y_prompt = y_p.reshape(BATCH, SEQ, D)
    y_sample = y_s.reshape(DEC_BATCH, DEC_SEQ, D)
    return (y_prompt, y_sample, new_l0_ret, new_l1_C, new_l1_n, new_l1_m, new_l2_k, new_l2_v, new_l3_ret)
```

```python
import functools

import jax
import jax.numpy as jnp
import numpy as np
from jax import lax
from jax.experimental import pallas as pl
from jax.experimental.pallas import tpu as pltpu

F32 = jnp.float32
BF16 = jnp.bfloat16

D = 1024
BATCH = 16
SEQ = 256
DEC_BATCH = 2
DEC_SEQ = 1024
PAST_LEN = 512
GRID_W = 64
CHUNK = 128
EPS = 1e-6
ROPE_BASE = 10000.0

N_PROMPT = BATCH * SEQ
N_LATENT = DEC_BATCH * DEC_SEQ
N_TOK = N_PROMPT + N_LATENT
MOD_ROWS = 8

RET_HEADS = 8
RET_DK = 128
RET_DV = 256
RET_QK = RET_HEADS * RET_DK
RET_WIDTH = RET_HEADS * RET_DV
RET_HB_PROMPT = 8
RET_HB_LATENT = 4
ML_HB_PROMPT = 1
ML_HB_LATENT = 1

ML_HEADS = 4
ML_WIDTH = 2 * D
ML_HD = ML_WIDTH // ML_HEADS
ML_BLOCK = 4
ML_CONV = 5
ML_TILE = 256
ML_ROWS = 1024
ML_TN = 1024
ML_XT = ML_WIDTH // ML_TN
ML_EDGE_ROWS = sorted({r for e in range(0, ML_ROWS + 1, SEQ) for r in (e - 8, e) if 0 <= r < ML_ROWS})

ATT_HEADS = 16
ATT_KV = 4
ATT_HD = 64
ATT_GROUP = ATT_HEADS // ATT_KV
ATT_WIDTH = ATT_HEADS * ATT_HD
ATT_KVW = ATT_KV * ATT_HD
WINDOW = 128

OUT_SUB = 256
IN_SUB = 4

VMEM_LIMIT = 56 * 1024 * 1024


def _params(*sem):
    return pltpu.CompilerParams(dimension_semantics=sem, vmem_limit_bytes=VMEM_LIMIT)


def _mod_row(row0):
    return jnp.maximum((row0 - N_PROMPT) // DEC_SEQ + 1, 0)


def _dot(a, b):
    return jnp.dot(a, b, preferred_element_type=F32)


def _dot_nt(a, b):
    return lax.dot_general(a, b, (((1,), (1,)), ((), ())), preferred_element_type=F32)


def _dot_tn(a, b):
    return lax.dot_general(a, b, (((0,), (0,)), ((), ())), preferred_element_type=F32)


def _silu(x):
    return x * jax.nn.sigmoid(x)


def _log_sigmoid(x):
    return jnp.minimum(x, 0.0) - jnp.log1p(jnp.exp(-jnp.abs(x)))


def _rms(x, g):
    return x * lax.rsqrt(jnp.mean(x * x, axis=-1, keepdims=True) + EPS) * g


def _group_norm(x, g):
    xc = x - jnp.mean(x, axis=-1, keepdims=True)
    return xc * lax.rsqrt(jnp.mean(xc * xc, axis=-1, keepdims=True) + EPS) * g


def _two_source_specs(tm, width, tile0=0):
    n_p = N_PROMPT // tm
    return (pl.BlockSpec((tm, width), lambda i: (jnp.minimum(tile0 + i, n_p - 1), 0)),
            pl.BlockSpec((tm, width), lambda i: (jnp.maximum(tile0 + i - n_p, 0), 0)))


def _ada_kernel(ct_ref, *refs, nl, nt):
    o_ref = refs[-1]
    layer = pl.program_id(0) // nt
    s = _silu(ct_ref[...])
    for l in range(nl):
        @pl.when(layer == l)
        def _(w_ref=refs[l], b_ref=refs[nl + l]):
            w = w_ref[...]
            rows = [jnp.sum(w * s[:, r:r + 1], axis=0, keepdims=True) for r in range(1 + DEC_BATCH)]
            rows.append(jnp.zeros((MOD_ROWS - len(rows), w.shape[1]), F32))
            o_ref[...] = jnp.concatenate(rows, axis=0) + b_ref[...]


def _adaln(ct, ws, bs):
    tn = 1024
    nl, nt = len(ws), 3 * D // tn
    own = lambda l: (lambda j: (0, jnp.clip(j - l * nt, 0, nt - 1)))
    return pl.pallas_call(
        functools.partial(_ada_kernel, nl=nl, nt=nt),
        grid=(nl * nt,),
        in_specs=[pl.BlockSpec((D, MOD_ROWS), lambda j: (0, 0))]
        + [pl.BlockSpec((D, tn), own(l)) for l in range(nl)]
        + [pl.BlockSpec((1, tn), own(l)) for l in range(nl)],
        out_specs=pl.BlockSpec((None, MOD_ROWS, tn), lambda j: (j // nt, 0, j % nt)),
        out_shape=jax.ShapeDtypeStruct((nl, MOD_ROWS, 3 * D), F32),
        compiler_params=_params("arbitrary"),
        name="adaln",
    )(ct, *ws, *[b.reshape(1, 3 * D) for b in bs])


def _in_proj_kernel(*refs, tm, tn, acts, from_x):
    j = pl.program_id(0)
    i = pl.program_id(1)
    w_ref, o_ref, wb_ref = refs[-3:]
    sub = tm // IN_SUB

    @pl.when(i == 0)
    def _():
        wb_ref[...] = w_ref[...].astype(BF16)

    if from_x:
        xp_ref, xs_ref, g_ref, mod_ref = refs[:4]
        m = mod_ref[pl.ds(_mod_row(i * tm), 1), :]

        def lhs(rows):
            y = _rms(jnp.where(i < N_PROMPT // tm, xp_ref[rows, :], xs_ref[rows, :]), g_ref[...])
            return (y * (1.0 + m[:, D:2 * D]) + m[:, :D]).astype(BF16)
    else:
        def lhs(rows):
            return refs[0][rows, :]

    sig0, silu0 = acts
    gated = (j + 1) * tn > min(sig0, silu0)

    @pl.when(jnp.logical_not(gated))
    def _():
        if from_x:
            for s in range(IN_SUB):
                rows = slice(s * sub, (s + 1) * sub)
                o_ref[rows, :] = _dot(lhs(rows), wb_ref[...]).astype(BF16)
        else:
            o_ref[...] = _dot(refs[0][...], wb_ref[...]).astype(BF16)

    @pl.when(gated)
    def _():
        for s in range(IN_SUB):
            rows = slice(s * sub, (s + 1) * sub)
            acc = _dot(lhs(rows), wb_ref[...])
            col = j * tn + lax.broadcasted_iota(jnp.int32, acc.shape, 1)
            sig = jax.nn.sigmoid(acc)
            out = jnp.where(col >= silu0, acc * sig, jnp.where(col >= sig0, sig, acc))
            o_ref[rows, :] = out.astype(BF16)


def _in_proj(h, w, tn, acts):
    tm = 1024
    n = w.shape[1]
    from_x = isinstance(h, tuple)
    if from_x:
        xp, xs, g, mod = h
        n_p = N_PROMPT // tm
        lhs_specs = [pl.BlockSpec((tm, D), lambda j, i: (jnp.minimum(i, n_p - 1), 0)),
                     pl.BlockSpec((tm, D), lambda j, i: (jnp.maximum(i - n_p, 0), 0)),
                     pl.BlockSpec((1, D), lambda j, i: (0, 0)),
                     pl.BlockSpec((MOD_ROWS, 3 * D), lambda j, i: (0, 0))]
        lhs_args = [xp, xs, g.reshape(1, D), mod]
    else:
        lhs_specs = [pl.BlockSpec((tm, D), lambda j, i: (i, 0))]
        lhs_args = [h]
    return pl.pallas_call(
        functools.partial(_in_proj_kernel, tm=tm, tn=tn, acts=acts, from_x=from_x),
        grid=(n // tn, N_TOK // tm),
        in_specs=lhs_specs + [pl.BlockSpec((D, tn), lambda j, i: (0, j))],
        out_specs=pl.BlockSpec((tm, tn), lambda j, i: (i, j)),
        out_shape=jax.ShapeDtypeStruct((N_TOK, n), BF16),
        scratch_shapes=[pltpu.VMEM((D, tn), BF16)],
        compiler_params=_params("arbitrary", "arbitrary"),
        name="in_proj",
    )(*lhs_args, w)


def _out_proj_kernel(ap_ref, as_ref, w_ref, mod_ref, g_ref, modn_ref, *rest, tm, tile0, final, nx):
    x_refs, rest = rest[:nx], rest[nx:]
    wb_ref = rest[-1]

    @pl.when(pl.program_id(0) == 0)
    def _():
        wb_ref[...] = w_ref[...].astype(BF16)

    i = tile0 + pl.program_id(0)
    r = _mod_row(i * tm)
    gate = mod_ref[pl.ds(r, 1), :][:, 2 * D:]
    from_prompt = i < N_PROMPT // tm
    for s in range(tm // OUT_SUB):
        rows = slice(s * OUT_SUB, (s + 1) * OUT_SUB)
        a = jnp.where(from_prompt, ap_ref[rows, :], as_ref[rows, :])
        x = x_refs[0][rows, :] if nx == 1 else jnp.where(from_prompt, x_refs[0][rows, :], x_refs[1][rows, :])
        xn = x + gate * _dot(a, wb_ref[...])
        y = _rms(xn, g_ref[...])
        if final:
            rest[0][rows, :] = y
        else:
            mn = modn_ref[pl.ds(r, 1), :]
            rest[0][rows, :] = xn
            rest[1][rows, :] = (y * (1.0 + mn[:, D:2 * D]) + mn[:, :D]).astype(BF16)


def _out_proj(a_p, a_s, w, x, mod, g_next, mod_next, final, rows=(0, N_TOK)):
    tm = 512
    kw = w.shape[0]
    tile0, nt = rows[0] // tm, rows[1] // tm
    row = pl.BlockSpec((tm, D), lambda i: (i, 0))
    if final:
        out_specs, out_shape = [row], [jax.ShapeDtypeStruct((rows[1], D), F32)]
    else:
        out_specs = [row, row]
        out_shape = [jax.ShapeDtypeStruct((rows[1], D), F32), jax.ShapeDtypeStruct((rows[1], D), BF16)]
    if isinstance(x, tuple):
        x_specs, xs = list(_two_source_specs(tm, D, tile0)), list(x)
    else:
        x_specs, xs = [pl.BlockSpec((tm, D), lambda i: (tile0 + i, 0))], [x]
    return pl.pallas_call(
        functools.partial(_out_proj_kernel, tm=tm, tile0=tile0, final=final, nx=len(xs)),
        grid=(nt,),
        in_specs=[*_two_source_specs(tm, kw, tile0),
                  pl.BlockSpec((kw, D), lambda i: (0, 0)),
                  pl.BlockSpec((MOD_ROWS, 3 * D), lambda i: (0, 0)),
                  pl.BlockSpec((1, D), lambda i: (0, 0)),
                  pl.BlockSpec((MOD_ROWS, 3 * D), lambda i: (0, 0)),
                  *x_specs],
        out_specs=out_specs,
        out_shape=out_shape,
        scratch_shapes=[pltpu.VMEM((kw, D), BF16)],
        compiler_params=_params("arbitrary"),
        name="out_proj",
    )(a_p, a_s, w, mod, g_next.reshape(1, D), mod_next, *xs)


def _ret_kernel(*refs, T, HB, rope, state_in, state_out):
    refs = list(refs)
    dec_ref, q_ref, k_ref, v_ref, z_ref, gn_ref = refs[:6]
    pos = 6
    if rope:
        cos_ref, sin_ref = refs[pos:pos + 2]
        pos += 2
    if state_in:
        s0_ref = refs[pos]
        pos += 1
    o_ref = refs[pos]
    pos += 1
    if state_out:
        sn_ref = refs[pos]
        pos += 1
    kv_ref, sf_ref, sb_ref, tab_ref, gblk_ref = refs[pos:pos + 5]
    pos += 5
    if rope:
        kr_ref = refs[pos]

    hg = pl.program_id(1)
    n = T // CHUNK
    L = CHUNK
    ii = lax.broadcasted_iota(jnp.int32, (L, L), 0).astype(F32)
    jj = lax.broadcasted_iota(jnp.int32, (L, L), 1).astype(F32)
    scale = RET_DK ** -0.5

    def chunk(c):
        return slice(c * L, (c + 1) * L)

    def rotate(x, c):
        return x * cos_ref[chunk(c), :] + pltpu.roll(x, RET_DK // 2, 1) * sin_ref[chunk(c), :]

    @pl.when(pl.program_id(0) == 0)
    def _():
        for hh in range(HB):
            head = hg * HB + hh
            lg_f = _log_sigmoid(jnp.full((1, RET_DV), dec_ref[0, head], F32))
            lg_b = _log_sigmoid(jnp.full((1, RET_DV), dec_ref[1, head], F32))
            lf, lb = lg_f[:, :L], lg_b[:, :L]
            tab_ref[head, 0] = jnp.exp(lf * (L - 1.0 - ii)) * scale
            tab_ref[head, 1] = jnp.exp(lb * ii) * scale
            tab_ref[head, 2] = jnp.exp(lf * (ii + 1.0))
            tab_ref[head, 3] = jnp.exp(lb * (L - ii))
            tab_ref[head, 4] = (jnp.where(ii >= jj, jnp.exp(lf * jnp.maximum(ii - jj, 0.0)), 0.0)
                                + jnp.where(jj >= ii, jnp.exp(lb * jnp.maximum(jj - ii, 0.0)), 0.0)) * scale
            gblk_ref[head, 0] = jnp.exp(lg_f * float(L))
            gblk_ref[head, 1] = jnp.exp(lg_b * float(L))

    for hh in range(HB):
        head = hg * HB + hh
        qs = slice(hh * RET_DK, (hh + 1) * RET_DK)
        vs = slice(hh * RET_DV, (hh + 1) * RET_DV)
        k_dec_f, k_dec_b, q_dec_f, q_dec_b, mask = (tab_ref[head, t] for t in range(5))
        g_f = gblk_ref[head, 0]
        g_b = gblk_ref[head, 1]

        for c in range(n):
            kc = k_ref[chunk(c), qs].astype(F32)
            if rope:
                kc = rotate(kc, c)
                kr_ref[chunk(c), qs] = kc.astype(BF16)
            kk = jnp.concatenate([kc * k_dec_f, kc * k_dec_b], axis=1).astype(BF16)
            kv_ref[hh, c] = _dot_tn(kk, v_ref[chunk(c), vs])
        S = s0_ref[0, hh] if state_in else None
        has_f = []
        for c in range(n):
            has_f.append(S is not None)
            if S is not None:
                sf_ref[hh, c] = S.astype(BF16)
            kvc = kv_ref[hh, c, :RET_DK, :]
            S = kvc if S is None else S * g_f + kvc
        if state_out:
            sn_ref[0, hh] = S
        S = s0_ref[1, hh] if state_in else None
        has_b = [False] * n
        for c in reversed(range(n)):
            has_b[c] = S is not None
            if S is not None:
                sb_ref[hh, c] = S.astype(BF16)
            kvc = kv_ref[hh, c, RET_DK:, :]
            S = kvc if S is None else S * g_b + kvc
        if state_out:
            sn_ref[1, hh] = S

        for c in range(n):
            qc = q_ref[chunk(c), qs]
            qf = qc.astype(F32)
            if rope:
                qf = rotate(qf, c)
                qc = qf.astype(BF16)
                kc = kr_ref[chunk(c), qs]
            else:
                kc = k_ref[chunk(c), qs]
            lhs = [(_dot_nt(qc, kc) * mask).astype(BF16)]
            rhs = [v_ref[chunk(c), vs]]
            if has_f[c]:
                lhs.append((qf * q_dec_f).astype(BF16))
                rhs.append(sf_ref[hh, c])
            if has_b[c]:
                lhs.append((qf * q_dec_b).astype(BF16))
                rhs.append(sb_ref[hh, c])
            o = _dot(jnp.concatenate(lhs, axis=1), jnp.concatenate(rhs, axis=0))
            zf = z_ref[chunk(c), vs].astype(F32)
            o_ref[chunk(c), vs] = (_group_norm(o, gn_ref[:, vs]) * zf).astype(BF16)


def _retention(proj, dec, gn, T, nb, row_blk0, rope=None, state=None, want_state=False):
    HB = RET_HB_PROMPT if T == SEQ else RET_HB_LATENT
    ng = RET_HEADS // HB
    n = T // CHUNK
    qw, vw = HB * RET_DK, HB * RET_DV
    in_specs = [pl.BlockSpec(memory_space=pltpu.SMEM),
                pl.BlockSpec((T, qw), lambda b, h: (row_blk0 + b, h)),
                pl.BlockSpec((T, qw), lambda b, h: (row_blk0 + b, ng + h)),
                pl.BlockSpec((T, vw), lambda b, h: (row_blk0 + b, ng + h)),
                pl.BlockSpec((T, vw), lambda b, h: (row_blk0 + b, 2 * ng + h)),
                pl.BlockSpec((1, vw), lambda b, h: (0, h))]
    args = [dec, proj, proj, proj, proj, gn.reshape(1, RET_WIDTH)]
    if rope is not None:
        in_specs += [pl.BlockSpec((T, RET_DK), lambda b, h: (0, 0))] * 2
        args += list(rope)
    if state is not None:
        in_specs.append(pl.BlockSpec((None, 2, HB, RET_DK, RET_DV), lambda b, h: (b, 0, h, 0, 0)))
        args.append(state)
    out_specs = [pl.BlockSpec((T, vw), lambda b, h: (b, h))]
    out_shape = [jax.ShapeDtypeStruct((nb * T, RET_WIDTH), BF16)]
    if want_state:
        out_specs.append(pl.BlockSpec((None, 2, HB, RET_DK, RET_DV), lambda b, h: (b, 0, h, 0, 0)))
        out_shape.append(jax.ShapeDtypeStruct((nb, 2, RET_HEADS, RET_DK, RET_DV), F32))
    scratch = [pltpu.VMEM((HB, n, 2 * RET_DK, RET_DV), F32),
               pltpu.VMEM((HB, n, RET_DK, RET_DV), BF16),
               pltpu.VMEM((HB, n, RET_DK, RET_DV), BF16),
               pltpu.VMEM((RET_HEADS, 5, CHUNK, CHUNK), F32),
               pltpu.VMEM((RET_HEADS, 2, 1, RET_DV), F32)]
    if rope is not None:
        scratch.append(pltpu.VMEM((T, qw), BF16))
    return pl.pallas_call(
        functools.partial(_ret_kernel, T=T, HB=HB, rope=rope is not None, state_in=state is not None,
                          state_out=want_state),
        grid=(nb, ng),
        in_specs=in_specs, out_specs=out_specs, out_shape=out_shape, scratch_shapes=scratch,
        compiler_params=_params("arbitrary", "arbitrary"),
        name="retention",
    )(*args)


def _ret_full_kernel(dec_ref, q_ref, k_ref, v_ref, z_ref, gn_ref, o_ref, sn_ref, mask_ref, kdec_ref, *, T):
    scale = RET_DK ** -0.5

    @pl.when(pl.program_id(0) == 0)
    def _():
        ii = lax.broadcasted_iota(jnp.int32, (T, T), 0).astype(F32)
        jj = lax.broadcasted_iota(jnp.int32, (T, T), 1).astype(F32)
        row = lax.broadcasted_iota(jnp.int32, (T, RET_DK), 0).astype(F32)
        for head in range(RET_HEADS):
            lf = _log_sigmoid(jnp.full((1, T), dec_ref[0, head], F32))
            lb = _log_sigmoid(jnp.full((1, T), dec_ref[1, head], F32))
            mask_ref[head] = (jnp.where(ii >= jj, jnp.exp(lf * jnp.maximum(ii - jj, 0.0)), 0.0)
                              + jnp.where(jj >= ii, jnp.exp(lb * jnp.maximum(jj - ii, 0.0)), 0.0)) * scale
            kdec_ref[head, 0] = jnp.exp(lf[:, :RET_DK] * (T - 1.0 - row)) * scale
            kdec_ref[head, 1] = jnp.exp(lb[:, :RET_DK] * row) * scale

    for head in range(RET_HEADS):
        qs = slice(head * RET_DK, (head + 1) * RET_DK)
        vs = slice(head * RET_DV, (head + 1) * RET_DV)
        k = k_ref[:, qs]
        v = v_ref[:, vs]
        a = (_dot_nt(q_ref[:, qs], k) * mask_ref[head]).astype(BF16)
        o = _dot(a, v)
        o_ref[:, vs] = (_group_norm(o, gn_ref[:, vs]) * z_ref[:, vs].astype(F32)).astype(BF16)
        kf = k.astype(F32)
        kk = jnp.concatenate([kf * kdec_ref[head, 0], kf * kdec_ref[head, 1]], axis=1).astype(BF16)
        kv = _dot_tn(kk, v)
        sn_ref[0, head] = kv[:RET_DK]
        sn_ref[1, head] = kv[RET_DK:]


def _retention_full(proj, dec, gn, T, nb):
    return pl.pallas_call(
        functools.partial(_ret_full_kernel, T=T),
        grid=(nb,),
        in_specs=[pl.BlockSpec(memory_space=pltpu.SMEM),
                  pl.BlockSpec((T, RET_QK), lambda b: (b, 0)),
                  pl.BlockSpec((T, RET_QK), lambda b: (b, 1)),
                  pl.BlockSpec((T, RET_WIDTH), lambda b: (b, 1)),
                  pl.BlockSpec((T, RET_WIDTH), lambda b: (b, 2)),
                  pl.BlockSpec((1, RET_WIDTH), lambda b: (0, 0))],
        out_specs=[pl.BlockSpec((T, RET_WIDTH), lambda b: (b, 0)),
                   pl.BlockSpec((None, 2, RET_HEADS, RET_DK, RET_DV), lambda b: (b, 0, 0, 0, 0))],
        out_shape=[jax.ShapeDtypeStruct((nb * T, RET_WIDTH), BF16),
                   jax.ShapeDtypeStruct((nb, 2, RET_HEADS, RET_DK, RET_DV), F32)],
        scratch_shapes=[pltpu.VMEM((RET_HEADS, T, T), F32), pltpu.VMEM((RET_HEADS, 2, T, RET_DK), F32)],
        compiler_params=_params("arbitrary"),
        name="retention_full",
    )(dec, proj, proj, proj, proj, gn.reshape(1, RET_WIDTH))


def _ml_in_kernel(h_ref, w_ref, cw_ref, cb_ref, bq_ref, bk_ref, bv_ref, wq_ref, wk_ref, wv_ref, bias_ref,
                  oz_ref, q_ref, k_ref, v_ref, xc_ref, g_ref, wb_ref, bd_ref, conv_ref):
    j = pl.program_id(0)
    i = pl.program_id(1)
    nsub = ML_TN // ML_TILE

    @pl.when(i == 0)
    def _():
        wb_ref[...] = w_ref[...].astype(BF16)

    @pl.when((i == 0) & (j < ML_XT))
    def _():
        row = lax.broadcasted_iota(jnp.int32, (ML_TILE, ML_TILE), 0)
        col = lax.broadcasted_iota(jnp.int32, (ML_TILE, ML_TILE), 1)
        shift = ML_BLOCK.bit_length() - 1
        same_block = jnp.right_shift(row, shift) == jnp.right_shift(col, shift)
        for which, b_ref in enumerate((bq_ref, bk_ref, bv_ref)):
            for t in range(nsub):
                w = b_ref[t * ML_TILE:(t + 1) * ML_TILE, :]
                rep = w[:, ML_BLOCK - 1:ML_BLOCK]
                for dd in range(ML_BLOCK - 1):
                    rep = jnp.where((col & (ML_BLOCK - 1)) == dd, w[:, dd:dd + 1], rep)
                bd_ref[which, t] = jnp.where(same_block, rep, 0.0).astype(BF16)

    @pl.when(j >= ML_XT)
    def _():
        sub = ML_ROWS // IN_SUB
        for s in range(IN_SUB):
            rows = slice(s * sub, (s + 1) * sub)
            acc = _dot(h_ref[rows, :], wb_ref[...])
            col = j * ML_TN + lax.broadcasted_iota(jnp.int32, acc.shape, 1)
            sig = jax.nn.sigmoid(acc)
            oz_ref[rows, :] = jnp.where(col >= 2 * ML_WIDTH, acc * sig, sig).astype(BF16)

    @pl.when(j < ML_XT)
    def _():
        acc = _dot(h_ref[...], wb_ref[...])
        T = jnp.where(i < N_PROMPT // ML_ROWS, SEQ, DEC_SEQ)
        pad = ML_CONV // 2
        taps = [tap for tap in range(ML_CONV) if tap != pad]
        sub8 = lax.broadcasted_iota(jnp.int32, (8, ML_TILE), 0)
        g = jnp.where(j == 0, 1.0, 0.0) * bias_ref[...]
        for t in range(nsub):
            cols = slice(t * ML_TILE, (t + 1) * ML_TILE)
            x = acc[:, cols]
            shifted = {tap: pltpu.roll(x, (pad - tap) % ML_ROWS, 0) for tap in taps}
            conv = cb_ref[:, cols] + x * cw_ref[pad:pad + 1, cols]
            for tap in taps:
                conv = conv + shifted[tap] * cw_ref[tap:tap + 1, cols]
            conv_ref[...] = conv
            for r0 in ML_EDGE_ROWS:
                rows = slice(r0, r0 + 8)
                pos = (r0 + sub8) & (T - 1)
                fixed = cb_ref[:, cols] + x[rows] * cw_ref[pad:pad + 1, cols]
                for tap in taps:
                    ok = (pos + (tap - pad) >= 0) & (pos + (tap - pad) < T)
                    fixed = fixed + jnp.where(ok, shifted[tap][rows], 0.0) * cw_ref[tap:tap + 1, cols]
                conv_ref[rows, :] = fixed
            xcb = _silu(conv_ref[...]).astype(BF16)
            qb = _dot(xcb, bd_ref[0, t]).astype(BF16)
            kb = _dot(xcb, bd_ref[1, t]).astype(BF16)
            vb = _dot(x.astype(BF16), bd_ref[2, t]).astype(BF16)
            q_ref[:, cols] = qb
            k_ref[:, cols] = kb
            v_ref[:, cols] = vb
            xc_ref[:, cols] = xcb
            g = g + (_dot(qb, wq_ref[cols, :].astype(BF16)) + _dot(kb, wk_ref[cols, :].astype(BF16))
                     + _dot(vb, wv_ref[cols, :].astype(BF16)))
        g_ref[...] = g


def _ml_in_proj(h, w, conv_w, conv_b, wq, wk, wv, wif, bif):
    ng = wif.shape[1]
    ni = N_TOK // ML_ROWS
    xt = ML_XT
    wq, wk, wv = (a.reshape(ML_WIDTH, ML_BLOCK) for a in (wq, wk, wv))
    xcol = lambda j, i: (0, jnp.minimum(j, xt - 1))
    xrow = lambda off: (lambda j, i: (off + jnp.minimum(j, xt - 1), 0))
    x_out = pl.BlockSpec((ML_ROWS, ML_TN), lambda j, i: (jnp.where(j < xt, i, ni - 1), jnp.minimum(j, xt - 1)))
    return pl.pallas_call(
        _ml_in_kernel,
        grid=(3 * ML_WIDTH // ML_TN, ni),
        in_specs=[pl.BlockSpec((ML_ROWS, D), lambda j, i: (i, 0)),
                  pl.BlockSpec((D, ML_TN), lambda j, i: (0, j)),
                  pl.BlockSpec((ML_CONV, ML_TN), xcol),
                  pl.BlockSpec((1, ML_TN), xcol),
                  pl.BlockSpec((ML_TN, ML_BLOCK), xrow(0)),
                  pl.BlockSpec((ML_TN, ML_BLOCK), xrow(0)),
                  pl.BlockSpec((ML_TN, ML_BLOCK), xrow(0)),
                  pl.BlockSpec((ML_TN, ng), xrow(0)),
                  pl.BlockSpec((ML_TN, ng), xrow(xt)),
                  pl.BlockSpec((ML_TN, ng), xrow(2 * xt)),
                  pl.BlockSpec((1, ng), lambda j, i: (0, 0))],
        out_specs=[pl.BlockSpec((ML_ROWS, ML_TN), lambda j, i: (jnp.where(j < xt, 0, i), jnp.maximum(j - xt, 0))),
                   x_out, x_out, x_out, x_out,
                   pl.BlockSpec((None, ML_ROWS, ng),
                                lambda j, i: (jnp.minimum(j, xt - 1), jnp.where(j < xt, i, ni - 1), 0))],
        out_shape=[jax.ShapeDtypeStruct((N_TOK, 2 * ML_WIDTH), BF16)]
        + [jax.ShapeDtypeStruct((N_TOK, ML_WIDTH), BF16)] * 4
        + [jax.ShapeDtypeStruct((xt, N_TOK, ng), F32)],
        scratch_shapes=[pltpu.VMEM((D, ML_TN), BF16), pltpu.VMEM((3, ML_TN // ML_TILE, ML_TILE, ML_TILE), BF16),
                        pltpu.VMEM((ML_ROWS, ML_TILE), F32)],
        compiler_params=_params("arbitrary", "arbitrary"),
        name="mlstm_in_proj",
    )(h, w, conv_w, conv_b.reshape(1, ML_WIDTH), wq, wk, wv, wif, wif, wif, bif.reshape(1, ng))


def _dot_split(lhs_bf16, rhs):
    r1 = rhs.astype(BF16)
    e1 = rhs - r1.astype(F32)
    r2 = e1.astype(BF16)
    r3 = (e1 - r2.astype(F32)).astype(BF16)
    return _dot(lhs_bf16, r1) + _dot(lhs_bf16, r2) + _dot(lhs_bf16, r3)


def _ml_scan_kernel(*refs, T, HB, state_in, state_out):
    refs = list(refs)
    pos = 0
    m0_ref = c0_ref = n0_ref = cn_ref = nn_ref = mn_ref = None
    if state_in:
        m0_ref = refs[0]
        pos = 1
    q_ref, k_ref, v_ref, g_ref, op_ref, z_ref, xc_ref, gn_ref, sk_ref = refs[pos:pos + 9]
    pos += 9
    if state_in:
        c0_ref, n0_ref = refs[pos:pos + 2]
        pos += 2
    o_ref = refs[pos]
    pos += 1
    if state_out:
        cn_ref, nn_ref, mn_ref = refs[pos:pos + 3]
        pos += 3
    acc_ref, s_ref = refs[pos:pos + 2]

    for hh in range(HB):
        cols = slice(hh * ML_HD, (hh + 1) * ML_HD)

        def head_cols(ref):
            return ref.at[:, cols]

        def head_state(ref):
            return None if ref is None else ref.at[:, hh]

        _ml_scan_head(pl.program_id(0), pl.program_id(1) * HB + hh, m0_ref,
                      head_cols(q_ref), head_cols(k_ref), head_cols(v_ref), g_ref, head_cols(op_ref),
                      head_cols(z_ref), head_cols(xc_ref), head_cols(gn_ref), head_cols(sk_ref),
                      head_state(c0_ref), head_state(n0_ref), head_cols(o_ref),
                      head_state(cn_ref), head_state(nn_ref), head_state(mn_ref), acc_ref.at[hh], s_ref.at[hh],
                      T=T, state_in=state_in, state_out=state_out)


def _ml_scan_head(b, h, m0_ref, q_ref, k_ref, v_ref, g_ref, op_ref, z_ref, xc_ref, gn_ref, sk_ref,
                  c0_ref, n0_ref, o_ref, cn_ref, nn_ref, mn_ref, acc_ref, s_ref, *, T, state_in, state_out):
    n = T // CHUNK
    L = CHUNK
    scale = ML_HD ** -0.5

    def chunk(c):
        return slice(c * L, (c + 1) * L)

    ii = lax.broadcasted_iota(jnp.int32, (L, L), 0)
    jj = lax.broadcasted_iota(jnp.int32, (L, L), 1)
    neg = jnp.full((L, L), -jnp.inf, F32)
    zero = jnp.zeros((L, L), F32)
    tri = [jnp.where(jj <= ii, 1.0, 0.0).astype(BF16), jnp.where(jj >= ii, 1.0, 0.0).astype(BF16)]
    bias = [jnp.where(ii >= jj, zero, neg), jnp.where(ii <= jj, zero, neg)]

    g = g_ref[0]
    for part in range(1, ML_XT):
        g = g + g_ref[part]
    col = lax.broadcasted_iota(jnp.int32, g.shape, 1)
    g = jnp.where((col & (2 * ML_HEADS - 1)) >= ML_HEADS, _log_sigmoid(g), g)

    def gate_column(j):
        return jnp.sum(jnp.where(col == j, g, 0.0), axis=1, keepdims=True)


    for i in range(n):
        for j in range(n):
            s_ref[i, j] = _dot_nt(q_ref[chunk(i), :], k_ref[chunk(j), :])

    G = []
    for d in range(2):
        order = list(range(n)) if d == 0 else list(reversed(range(n)))
        gi_all = gate_column(d * 2 * ML_HEADS + h)
        gf_all = gate_column(d * 2 * ML_HEADS + ML_HEADS + h)
        m0 = jnp.full((1, 1), m0_ref[(b * 2 + d) * ML_HEADS + h], F32) if state_in else jnp.zeros((1, 1), F32)
        c_col, c_row, b_col, m_row = [None] * n, [None] * n, [None] * n, [None] * n
        offset = jnp.zeros((1, 1), F32)
        carry = m0
        for c in order:
            gi = gi_all[chunk(c)]
            gf = gf_all[chunk(c)]
            cum = _dot_split(tri[d], jnp.broadcast_to(gf, (L, L))) + offset
            offset = offset + jnp.sum(gf, axis=0, keepdims=True)
            cc = gi - cum
            c_col[c] = cc[:, :1]
            c_row[c] = jnp.transpose(cc)
            b_col[c] = cum[:, :1]
            m_row[c] = jnp.maximum(jnp.max(c_row[c] + bias[d], axis=1, keepdims=True), carry)
            carry = jnp.maximum(carry, jnp.max(c_col[c], axis=0, keepdims=True))
        G.append(dict(order=order, c_col=c_col, c_row=c_row, b_col=b_col, m_row=m_row, m0=m0,
                      m_end=carry, b_end=offset))

    done = set()
    for i in range(n):
        for d in range(2):
            e = G[d]
            js = list(range(0, i + 1)) if d == 0 else list(range(i, n))
            m_i = e["m_row"][i]
            parts = []
            tot = None
            for j in js:
                z = e["c_row"][j] - m_i
                if j == i:
                    z = z + bias[d]
                sw = s_ref[i, j] * (jnp.exp(z) * scale)
                tot = sw if tot is None else tot + sw
                parts.append(sw.astype(BF16))
            num = _dot(jnp.concatenate(parts, axis=1), v_ref[js[0] * L:(js[-1] + 1) * L, :])
            den = jnp.sum(tot, axis=1, keepdims=True)
            if state_in:
                qc = q_ref[chunk(i), :]
                w0 = jnp.exp(e["m0"] - m_i)
                num = num + _dot((qc.astype(F32) * w0).astype(BF16), c0_ref[d].astype(BF16))
                n0 = jnp.broadcast_to(n0_ref[d], (8, ML_HD)).astype(BF16)
                den = den + w0 * _dot_nt(qc, n0)[:, :1]
            hb = num / jnp.maximum(jnp.abs(den), jnp.exp(-(e["b_col"][i] + m_i)))
            if i not in done:
                done.add(i)
                acc_ref[chunk(i), :] = hb
            else:
                cell = _group_norm((acc_ref[chunk(i), :] + hb) * op_ref[chunk(i), :].astype(F32), gn_ref[...])
                mixed = cell + sk_ref[...] * xc_ref[chunk(i), :].astype(F32)
                o_ref[chunk(i), :] = (mixed * z_ref[chunk(i), :].astype(F32)).astype(BF16)

    if state_out:
        ones = jnp.ones((8, T), BF16)
        for d in range(2):
            e = G[d]
            kw = jnp.concatenate(
                [(k_ref[chunk(c), :].astype(F32) * (jnp.exp(e["c_col"][c] - e["m_end"]) * scale)).astype(BF16)
                 for c in range(n)], axis=0)
            C = _dot_tn(kw, v_ref[...])
            nvec = _dot(ones, kw)[:1]
            if state_in:
                w_end = jnp.exp(e["m0"] - e["m_end"])
                C = C + w_end * c0_ref[d]
                nvec = nvec + w_end * n0_ref[d]
            cn_ref[d] = C
            nn_ref[d] = nvec
            mn_ref[d] = jnp.broadcast_to(e["b_end"] + e["m_end"], mn_ref.shape[1:])


def _ml_scan(oz, q, k, v, xc, g, gn, skip, T, nb, row_blk0, state=None, want_state=False):
    H = ML_HEADS
    HB = ML_HB_PROMPT if T == SEQ else ML_HB_LATENT
    ng = H // HB
    hw = HB * ML_HD
    n = T // CHUNK
    tok = lambda b, h: (row_blk0 + b, h)
    head_vec = pl.BlockSpec((1, hw), lambda b, h: (0, h))
    both = lambda *tail: pl.BlockSpec((None, 2, HB, *tail), lambda b, h: (b, 0, h, 0, 0))
    in_specs, args = [], []
    if state is not None:
        c0, n0, m0 = state
        in_specs.append(pl.BlockSpec(memory_space=pltpu.SMEM))
        args.append(m0.reshape(nb * 2 * H))
    in_specs += [pl.BlockSpec((T, hw), tok)] * 3
    in_specs += [pl.BlockSpec((ML_XT, T, g.shape[2]), lambda b, h: (0, row_blk0 + b, 0)),
                 pl.BlockSpec((T, hw), tok),
                 pl.BlockSpec((T, hw), lambda b, h: (row_blk0 + b, ng + h)),
                 pl.BlockSpec((T, hw), tok), head_vec, head_vec]
    args += [q, k, v, g, oz, oz, xc, gn.reshape(1, ML_WIDTH), skip.reshape(1, ML_WIDTH)]
    if state is not None:
        in_specs += [both(ML_HD, ML_HD), both(1, ML_HD)]
        args += [c0, n0.reshape(nb, 2, H, 1, ML_HD)]
    out_specs = [pl.BlockSpec((T, hw), lambda b, h: (b, h))]
    out_shape = [jax.ShapeDtypeStruct((nb * T, ML_WIDTH), BF16)]
    if want_state:
        out_specs += [both(ML_HD, ML_HD), both(1, ML_HD), both(1, 128)]
        out_shape += [jax.ShapeDtypeStruct((nb, 2, H, ML_HD, ML_HD), F32),
                      jax.ShapeDtypeStruct((nb, 2, H, 1, ML_HD), F32),
                      jax.ShapeDtypeStruct((nb, 2, H, 1, 128), F32)]
    scratch = [pltpu.VMEM((HB, T, ML_HD), F32), pltpu.VMEM((HB, n, n, CHUNK, CHUNK), F32)]
    return pl.pallas_call(
        functools.partial(_ml_scan_kernel, T=T, HB=HB, state_in=state is not None, state_out=want_state),
        grid=(nb, ng),
        in_specs=in_specs, out_specs=out_specs, out_shape=out_shape, scratch_shapes=scratch,
        compiler_params=_params("arbitrary", "arbitrary"),
        name="mlstm_scan",
    )(*args)


def _att_rope(x, cos, sin):
    lane = lax.broadcasted_iota(jnp.int32, x.shape, 1)
    half = ATT_HD // 2
    rot = jnp.where((lane & (ATT_HD - 1)) < half, pltpu.roll(x, 128 - half, 1), pltpu.roll(x, half, 1))
    return x * cos + rot * sin


def _att_heads(qms, sinks, key_sets):
    scores = [[_dot_nt(k_both, qm) if bias is None else _dot_nt(k_both, qm) + bias
               for k_both, _, bias in key_sets] for qm in qms]
    ms = []
    for sink, per_set in zip(sinks, scores):
        m = sink
        for s in per_set:
            m = jnp.maximum(m, jnp.max(s, axis=0, keepdims=True))
        ms.append(m)
    probs = [[jnp.exp(s - m) for s in per_set] for m, per_set in zip(ms, scores)]
    outs = []
    for sink, m, per_set in zip(sinks, ms, probs):
        den = jnp.exp(sink - m)
        out = None
        for p, (_, v_t, _) in zip(per_set, key_sets):
            den = den + jnp.sum(p, axis=0, keepdims=True)
            o = _dot(v_t, p.astype(BF16))
            out = o if out is None else out + o
        outs.append(out * (1.0 / den))
    return outs


def _att_prepare_kv(k, v, kb_ref, vt_ref, kv, nblk):
    lane = lax.broadcasted_iota(jnp.int32, k.shape, 1)
    native = lane < ATT_HD if kv % 2 == 0 else lane >= ATT_HD
    kb_ref[kv] = jnp.where(native, k, pltpu.roll(k, ATT_HD, 1)).astype(BF16)
    r0 = (kv % 2) * ATT_HD
    blk = k.shape[0] // nblk
    for c in range(nblk):
        vt_ref[kv, c] = jnp.transpose(v[c * blk:(c + 1) * blk, :])[r0:r0 + ATT_HD, :].astype(BF16)


def _att_group(q_ref, z0_ref, z1_ref, o_ref, sink_ref, rows, kv, key_sets):
    lo = lax.broadcasted_iota(jnp.int32, (CHUNK, 128), 1) < ATT_HD
    zero = jnp.zeros((CHUNK, 128), BF16)
    qms, sinks = [], []
    for g in range(ATT_GROUP):
        col = (2 * kv + g // 2) * 128
        q = q_ref[rows, col:col + 128]
        qms.append(jnp.where(lo, q, zero) if g % 2 == 0 else jnp.where(lo, zero, q))
        sinks.append(jnp.full((1, CHUNK), sink_ref[kv * ATT_GROUP + g], F32))
    heads = _att_heads(qms, sinks, key_sets)
    for p in range(2):
        col = (2 * kv + p) * 128
        out = jnp.transpose(jnp.concatenate(heads[2 * p:2 * p + 2], axis=0))
        z_ref = z0_ref if col < ATT_WIDTH // 2 else z1_ref
        zc = col % (ATT_WIDTH // 2)
        zf = z_ref[rows, zc:zc + 128].astype(F32)
        o_ref[rows, col:col + 128] = (out * zf).astype(BF16)


def _att_ctx_kernel(sink_ref, q_ref, k_ref, v_ref, z0_ref, z1_ref, o_ref, qs_ref, kb_ref, vt_ref):
    T = q_ref.shape[0]
    qs_ref[...] = q_ref[...] * (ATT_HD ** -0.5)
    for kv in range(ATT_KV):
        slab = (kv // 2) * 128
        _att_prepare_kv(k_ref[:, slab:slab + 128].astype(F32), v_ref[:, slab:slab + 128].astype(F32),
                        kb_ref, vt_ref, kv, 1)
    for c in range(T // CHUNK):
        rows = slice(c * CHUNK, (c + 1) * CHUNK)
        for kv in range(ATT_KV):
            _att_group(qs_ref, z0_ref, z1_ref, o_ref, sink_ref, rows, kv, [(kb_ref[kv], vt_ref[kv, 0], None)])


def _att_ctx(proj, sink):
    T = SEQ
    half = ATT_WIDTH // 2
    return pl.pallas_call(
        _att_ctx_kernel,
        grid=(BATCH,),
        in_specs=[pl.BlockSpec(memory_space=pltpu.SMEM),
                  pl.BlockSpec((T, ATT_WIDTH), lambda b: (b, 0)),
                  pl.BlockSpec((T, ATT_KVW), lambda b: (b, ATT_WIDTH // ATT_KVW)),
                  pl.BlockSpec((T, ATT_KVW), lambda b: (b, ATT_WIDTH // ATT_KVW + 1)),
                  pl.BlockSpec((T, half), lambda b: (b, (ATT_WIDTH + 2 * ATT_KVW) // half)),
                  pl.BlockSpec((T, half), lambda b: (b, (ATT_WIDTH + 2 * ATT_KVW) // half + 1))],
        out_specs=pl.BlockSpec((T, ATT_WIDTH), lambda b: (b, 0)),
        out_shape=jax.ShapeDtypeStruct((N_PROMPT, ATT_WIDTH), BF16),
        scratch_shapes=[pltpu.VMEM((T, ATT_WIDTH), BF16), pltpu.VMEM((ATT_KV, T, 128), BF16),
                        pltpu.VMEM((ATT_KV, 1, ATT_HD, T), BF16)],
        compiler_params=_params("arbitrary"),
        name="att_ctx",
    )(sink, proj, proj, proj, proj, proj)


def _att_win_kernel(sink_ref, q_ref, k_ref, v_ref, z0_ref, z1_ref, kc_ref, vc_ref, cos_ref, sin_ref,
                    o_ref, qs_ref, kb_ref, vt_ref, kcb_ref, vct_ref):
    T = q_ref.shape[0]
    L = CHUNK
    n = T // L
    cos = cos_ref[...]
    sin = sin_ref[...]
    for c0 in range(0, ATT_WIDTH, 128):
        q = _att_rope(q_ref[:, c0:c0 + 128].astype(F32), cos, sin)
        qs_ref[:, c0:c0 + 128] = (q * (ATT_HD ** -0.5)).astype(BF16)
    for kv in range(ATT_KV):
        slab = (kv // 2) * 128
        _att_prepare_kv(_att_rope(k_ref[:, slab:slab + 128].astype(F32), cos, sin),
                        v_ref[:, slab:slab + 128].astype(F32), kb_ref, vt_ref, kv, n)
        _att_prepare_kv(kc_ref[:, slab:slab + 128], vc_ref[:, slab:slab + 128], kcb_ref, vct_ref, kv, 1)

    jj = lax.broadcasted_iota(jnp.int32, (L, L), 0)
    ii = lax.broadcasted_iota(jnp.int32, (L, L), 1)
    neg = jnp.full((L, L), -jnp.inf, F32)
    bias_prev = jnp.where(jj >= ii, 0.0, neg)
    bias_next = jnp.where(jj <= ii, 0.0, neg)

    def body(c, carry):
        rows = pl.ds(pl.multiple_of(c * L, L), L)
        c_prev = jnp.maximum(c - 1, 0)
        c_next = jnp.minimum(c + 1, n - 1)
        b_prev = jnp.where(c > 0, bias_prev, neg)
        b_next = jnp.where(c < n - 1, bias_next, neg)
        for kv in range(ATT_KV):
            def keys(cb):
                return kb_ref[kv, pl.ds(pl.multiple_of(cb * L, L), L), :]
            key_sets = [(keys(c_prev), vt_ref[kv, c_prev], b_prev),
                        (keys(c), vt_ref[kv, c], None),
                        (keys(c_next), vt_ref[kv, c_next], b_next),
                        (kcb_ref[kv], vct_ref[kv, 0], None)]
            _att_group(qs_ref, z0_ref, z1_ref, o_ref, sink_ref, rows, kv, key_sets)
        return carry

    lax.fori_loop(0, n, body, 0)


def _att_win(proj, sink, k_ctx, v_ctx, cos, sin):
    T = DEC_SEQ
    half = ATT_WIDTH // 2
    rb = N_PROMPT // T
    return pl.pallas_call(
        _att_win_kernel,
        grid=(DEC_BATCH,),
        in_specs=[pl.BlockSpec(memory_space=pltpu.SMEM),
                  pl.BlockSpec((T, ATT_WIDTH), lambda b: (rb + b, 0)),
                  pl.BlockSpec((T, ATT_KVW), lambda b: (rb + b, ATT_WIDTH // ATT_KVW)),
                  pl.BlockSpec((T, ATT_KVW), lambda b: (rb + b, ATT_WIDTH // ATT_KVW + 1)),
                  pl.BlockSpec((T, half), lambda b: (rb + b, (ATT_WIDTH + 2 * ATT_KVW) // half)),
                  pl.BlockSpec((T, half), lambda b: (rb + b, (ATT_WIDTH + 2 * ATT_KVW) // half + 1)),
                  pl.BlockSpec((None, PAST_LEN, ATT_KVW), lambda b: (b, 0, 0)),
                  pl.BlockSpec((None, PAST_LEN, ATT_KVW), lambda b: (b, 0, 0)),
                  pl.BlockSpec((T, 128), lambda b: (0, 0)),
                  pl.BlockSpec((T, 128), lambda b: (0, 0))],
        out_specs=pl.BlockSpec((T, ATT_WIDTH), lambda b: (b, 0)),
        out_shape=jax.ShapeDtypeStruct((N_LATENT, ATT_WIDTH), BF16),
        scratch_shapes=[pltpu.VMEM((T, ATT_WIDTH), BF16),
                        pltpu.VMEM((ATT_KV, T, 128), BF16),
                        pltpu.VMEM((ATT_KV, T // CHUNK, ATT_HD, CHUNK), BF16),
                        pltpu.VMEM((ATT_KV, PAST_LEN, 128), BF16),
                        pltpu.VMEM((ATT_KV, 1, ATT_HD, PAST_LEN), BF16)],
        compiler_params=_params("arbitrary"),
        name="att_win",
    )(sink, proj, proj, proj, proj, proj, k_ctx, v_ctx, cos, sin)


def _rope_tables(T, hd, reps):
    rows = T // GRID_W
    row = np.repeat(np.arange(rows, dtype=np.float64), GRID_W)
    col = np.tile(np.arange(GRID_W, dtype=np.float64), rows)
    nf = hd // 4
    inv = ROPE_BASE ** (-np.arange(nf, dtype=np.float64) / nf)
    ang = np.concatenate([row[:, None] * inv[None, :], col[:, None] * inv[None, :]], axis=-1)
    cos, sin = np.cos(ang), np.sin(ang)
    return (jnp.asarray(np.tile(np.concatenate([cos, cos], axis=-1), (1, reps)), F32),
            jnp.asarray(np.tile(np.concatenate([-sin, sin], axis=-1), (1, reps)), F32))


def _retention_layer(h, w_in, decay_f, decay_b, gn, state, rope):
    z0 = 2 * RET_QK + RET_WIDTH
    proj = _in_proj(h, w_in, 1024, (z0, z0))
    dec = jnp.stack([decay_f, decay_b]).astype(F32)
    a_p, new_state = _retention_full(proj, dec, gn, SEQ, BATCH)
    (a_s,) = _retention(proj, dec, gn, DEC_SEQ, DEC_BATCH, N_PROMPT // DEC_SEQ, rope=rope, state=state)
    return a_p, a_s, new_state


def kernel(x_prompt, x_sample, c, c_ctx, state_l0_ret, state_l1_C, state_l1_n, state_l1_m, cache_l2_k, cache_l2_v, state_l3_ret, norm_l0, ada_w_l0, ada_b_l0, w_in_l0, w_out_l0, ret_decay_f_l0, ret_decay_b_l0, ret_gn_l0, norm_l1, ada_w_l1, ada_b_l1, w_in_l1, w_out_l1, conv_w_l1, conv_b_l1, wq_l1, wk_l1, wv_l1, wif_f_l1, bif_f_l1, wif_b_l1, bif_b_l1, gn_l1, skip_l1, norm_l2, ada_w_l2, ada_b_l2, w_in_l2, w_out_l2, sink_l2, norm_l3, ada_w_l3, ada_b_l3, w_in_l3, w_out_l3, ret_decay_f_l3, ret_decay_b_l3, ret_gn_l3, final_norm):
    ct = jnp.concatenate([c_ctx[:, None], c.T, jnp.zeros((D, MOD_ROWS - 1 - DEC_BATCH), F32)], axis=1)
    mod_all = _adaln(ct, (ada_w_l0, ada_w_l1, ada_w_l2, ada_w_l3), (ada_b_l0, ada_b_l1, ada_b_l2, ada_b_l3))
    mods = [mod_all[l] for l in range(mod_all.shape[0])]
    rope_ret = _rope_tables(DEC_SEQ, RET_DK, 1)
    rope_att = _rope_tables(DEC_SEQ, ATT_HD, 2)

    x0 = (x_prompt.reshape(N_PROMPT, D), x_sample.reshape(N_LATENT, D))

    a_p, a_s, new_l0_ret = _retention_layer((*x0, norm_l0, mods[0]), w_in_l0, ret_decay_f_l0, ret_decay_b_l0,
                                            ret_gn_l0, state_l0_ret, rope_ret)
    x, h = _out_proj(a_p, a_s, w_out_l0, x0, mods[0], norm_l1, mods[1], False)

    wif = jnp.concatenate([wif_f_l1, wif_b_l1], axis=1)
    bif = jnp.concatenate([bif_f_l1, bif_b_l1])
    oz, q, k, v, xc, g = _ml_in_proj(h, w_in_l1, conv_w_l1, conv_b_l1, wq_l1, wk_l1, wv_l1, wif, bif)
    a_p, new_l1_C, n_new, m_new = _ml_scan(oz, q, k, v, xc, g, gn_l1, skip_l1, SEQ, BATCH, 0, want_state=True)
    (a_s,) = _ml_scan(oz, q, k, v, xc, g, gn_l1, skip_l1, DEC_SEQ, DEC_BATCH, N_PROMPT // DEC_SEQ,
                      state=(state_l1_C, state_l1_n, state_l1_m))
    new_l1_n = n_new.reshape(BATCH, 2, ML_HEADS, ML_HD)
    new_l1_m = m_new[:, :, :, 0, 0]
    x, h = _out_proj(a_p, a_s, w_out_l1, x, mods[1], norm_l2, mods[2], False)

    proj = _in_proj(h, w_in_l2, 1280, (ATT_WIDTH + 2 * ATT_KVW,) * 2)
    kv_new = proj[:N_PROMPT, ATT_WIDTH:ATT_WIDTH + 2 * ATT_KVW].astype(F32)
    new_l2_k = kv_new[:, :ATT_KVW].reshape(BATCH, SEQ, ATT_KV, ATT_HD)
    new_l2_v = kv_new[:, ATT_KVW:].reshape(BATCH, SEQ, ATT_KV, ATT_HD)
    a_p = _att_ctx(proj, sink_l2)
    a_s = _att_win(proj, sink_l2, cache_l2_k.reshape(DEC_BATCH, PAST_LEN, ATT_KVW),
                   cache_l2_v.reshape(DEC_BATCH, PAST_LEN, ATT_KVW), rope_att[0], rope_att[1])
    x, h = _out_proj(a_p, a_s, w_out_l2, x, mods[2], norm_l3, mods[3], False)

    a_p, a_s, new_l3_ret = _retention_layer(h, w_in_l3, ret_decay_f_l3, ret_decay_b_l3, ret_gn_l3, state_l3_ret,
                                            rope_ret)
    (y_p,) = _out_proj(a_p, a_s, w_out_l3, x, mods[3], final_norm, mods[3], True, rows=(0, N_PROMPT))
    (y_s,) = _out_proj(a_p, a_s, w_out_l3, x, mods[3], final_norm, mods[3], True, rows=(N_PROMPT, N_LATENT))

    y_prompt = y_p.reshape(BATCH, SEQ, D)
    y_sample = y_s.reshape(DEC_BATCH, DEC_SEQ, D)
    return (y_prompt, y_sample, new_l0_ret, new_l1_C, new_l1_n, new_l1_m, new_l2_k, new_l2_v, new_l3_ret)
```

```python
import functools

import jax
import jax.numpy as jnp
import numpy as np
from jax import lax
from jax.experimental import pallas as pl
from jax.experimental.pallas import tpu as pltpu

F32 = jnp.float32
BF16 = jnp.bfloat16

D = 1024
BATCH = 16
SEQ = 256
DEC_BATCH = 2
DEC_SEQ = 1024
PAST_LEN = 512
GRID_W = 64
CHUNK = 128
EPS = 1e-6
ROPE_BASE = 10000.0

N_PROMPT = BATCH * SEQ
N_LATENT = DEC_BATCH * DEC_SEQ
N_TOK = N_PROMPT + N_LATENT
MOD_ROWS = 8

RET_HEADS = 8
RET_DK = 128
RET_DV = 256
RET_QK = RET_HEADS * RET_DK
RET_WIDTH = RET_HEADS * RET_DV
RET_HB_PROMPT = 8
RET_HB_LATENT = 4
ML_HB_PROMPT = 1
ML_HB_LATENT = 1

ML_HEADS = 4
ML_WIDTH = 2 * D
ML_HD = ML_WIDTH // ML_HEADS
ML_BLOCK = 4
ML_CONV = 5
ML_TILE = 256
ML_ROWS = 1024
ML_TN = 1024
ML_XT = ML_WIDTH // ML_TN
ML_EDGE_ROWS = sorted({r for e in range(0, ML_ROWS + 1, SEQ) for r in (e - 8, e) if 0 <= r < ML_ROWS})

ATT_HEADS = 16
ATT_KV = 4
ATT_HD = 64
ATT_GROUP = ATT_HEADS // ATT_KV
ATT_WIDTH = ATT_HEADS * ATT_HD
ATT_KVW = ATT_KV * ATT_HD
WINDOW = 128

OUT_SUB = 256
IN_SUB = 4

VMEM_LIMIT = 56 * 1024 * 1024


def _params(*sem):
    return pltpu.CompilerParams(dimension_semantics=sem, vmem_limit_bytes=VMEM_LIMIT)


def _mod_row(row0):
    return jnp.maximum((row0 - N_PROMPT) // DEC_SEQ + 1, 0)


def _dot(a, b):
    return jnp.dot(a, b, preferred_element_type=F32)


def _dot_nt(a, b):
    return lax.dot_general(a, b, (((1,), (1,)), ((), ())), preferred_element_type=F32)


def _dot_tn(a, b):
    return lax.dot_general(a, b, (((0,), (0,)), ((), ())), preferred_element_type=F32)


def _silu(x):
    return x * jax.nn.sigmoid(x)


def _log_sigmoid(x):
    return jnp.minimum(x, 0.0) - jnp.log1p(jnp.exp(-jnp.abs(x)))


def _rms(x, g):
    return x * lax.rsqrt(jnp.mean(x * x, axis=-1, keepdims=True) + EPS) * g


def _group_norm(x, g):
    xc = x - jnp.mean(x, axis=-1, keepdims=True)
    return xc * lax.rsqrt(jnp.mean(xc * xc, axis=-1, keepdims=True) + EPS) * g


def _two_source_specs(tm, width, tile0=0):
    n_p = N_PROMPT // tm
    return (pl.BlockSpec((tm, width), lambda i: (jnp.minimum(tile0 + i, n_p - 1), 0)),
            pl.BlockSpec((tm, width), lambda i: (jnp.maximum(tile0 + i - n_p, 0), 0)))


def _ada_kernel(ct_ref, *refs, nl, nt):
    o_ref = refs[-1]
    layer = pl.program_id(0) // nt
    s = _silu(ct_ref[...])
    for l in range(nl):
        @pl.when(layer == l)
        def _(w_ref=refs[l], b_ref=refs[nl + l]):
            w = w_ref[...]
            rows = [jnp.sum(w * s[:, r:r + 1], axis=0, keepdims=True) for r in range(1 + DEC_BATCH)]
            rows.append(jnp.zeros((MOD_ROWS - len(rows), w.shape[1]), F32))
            o_ref[...] = jnp.concatenate(rows, axis=0) + b_ref[...]


def _adaln(ct, ws, bs):
    tn = 1024
    nl, nt = len(ws), 3 * D // tn
    own = lambda l: (lambda j: (0, jnp.clip(j - l * nt, 0, nt - 1)))
    return pl.pallas_call(
        functools.partial(_ada_kernel, nl=nl, nt=nt),
        grid=(nl * nt,),
        in_specs=[pl.BlockSpec((D, MOD_ROWS), lambda j: (0, 0))]
        + [pl.BlockSpec((D, tn), own(l)) for l in range(nl)]
        + [pl.BlockSpec((1, tn), own(l)) for l in range(nl)],
        out_specs=pl.BlockSpec((None, MOD_ROWS, tn), lambda j: (j // nt, 0, j % nt)),
        out_shape=jax.ShapeDtypeStruct((nl, MOD_ROWS, 3 * D), F32),
        compiler_params=_params("arbitrary"),
        name="adaln",
    )(ct, *ws, *[b.reshape(1, 3 * D) for b in bs])


def _in_proj_kernel(*refs, tm, tn, acts, from_x):
    j = pl.program_id(0)
    i = pl.program_id(1)
    w_ref, o_ref, wb_ref = refs[-3:]
    sub = tm // IN_SUB

    @pl.when(i == 0)
    def _():
        wb_ref[...] = w_ref[...].astype(BF16)

    if from_x:
        xp_ref, xs_ref, g_ref, mod_ref = refs[:4]
        m = mod_ref[pl.ds(_mod_row(i * tm), 1), :]

        def lhs(rows):
            y = _rms(jnp.where(i < N_PROMPT // tm, xp_ref[rows, :], xs_ref[rows, :]), g_ref[...])
            return (y * (1.0 + m[:, D:2 * D]) + m[:, :D]).astype(BF16)
    else:
        def lhs(rows):
            return refs[0][rows, :]

    sig0, silu0 = acts
    gated = (j + 1) * tn > min(sig0, silu0)

    @pl.when(jnp.logical_not(gated))
    def _():
        if from_x:
            for s in range(IN_SUB):
                rows = slice(s * sub, (s + 1) * sub)
                o_ref[rows, :] = _dot(lhs(rows), wb_ref[...]).astype(BF16)
        else:
            o_ref[...] = _dot(refs[0][...], wb_ref[...]).astype(BF16)

    @pl.when(gated)
    def _():
        for s in range(IN_SUB):
            rows = slice(s * sub, (s + 1) * sub)
            acc = _dot(lhs(rows), wb_ref[...])
            col = j * tn + lax.broadcasted_iota(jnp.int32, acc.shape, 1)
            sig = jax.nn.sigmoid(acc)
            out = jnp.where(col >= silu0, acc * sig, jnp.where(col >= sig0, sig, acc))
            o_ref[rows, :] = out.astype(BF16)


def _in_proj(h, w, tn, acts):
    tm = 1024
    n = w.shape[1]
    from_x = isinstance(h, tuple)
    if from_x:
        xp, xs, g, mod = h
        n_p = N_PROMPT // tm
        lhs_specs = [pl.BlockSpec((tm, D), lambda j, i: (jnp.minimum(i, n_p - 1), 0)),
                     pl.BlockSpec((tm, D), lambda j, i: (jnp.maximum(i - n_p, 0), 0)),
                     pl.BlockSpec((1, D), lambda j, i: (0, 0)),
                     pl.BlockSpec((MOD_ROWS, 3 * D), lambda j, i: (0, 0))]
        lhs_args = [xp, xs, g.reshape(1, D), mod]
    else:
        lhs_specs = [pl.BlockSpec((tm, D), lambda j, i: (i, 0))]
        lhs_args = [h]
    return pl.pallas_call(
        functools.partial(_in_proj_kernel, tm=tm, tn=tn, acts=acts, from_x=from_x),
        grid=(n // tn, N_TOK // tm),
        in_specs=lhs_specs + [pl.BlockSpec((D, tn), lambda j, i: (0, j))],
        out_specs=pl.BlockSpec((tm, tn), lambda j, i: (i, j)),
        out_shape=jax.ShapeDtypeStruct((N_TOK, n), BF16),
        scratch_shapes=[pltpu.VMEM((D, tn), BF16)],
        compiler_params=_params("arbitrary", "arbitrary"),
        name="in_proj",
    )(*lhs_args, w)


def _out_proj_kernel(ap_ref, as_ref, w_ref, mod_ref, g_ref, modn_ref, *rest, tm, tile0, final, nx):
    x_refs, rest = rest[:nx], rest[nx:]
    wb_ref = rest[-1]

    @pl.when(pl.program_id(0) == 0)
    def _():
        wb_ref[...] = w_ref[...].astype(BF16)

    i = tile0 + pl.program_id(0)
    r = _mod_row(i * tm)
    gate = mod_ref[pl.ds(r, 1), :][:, 2 * D:]
    from_prompt = i < N_PROMPT // tm
    for s in range(tm // OUT_SUB):
        rows = slice(s * OUT_SUB, (s + 1) * OUT_SUB)
        a = jnp.where(from_prompt, ap_ref[rows, :], as_ref[rows, :])
        x = x_refs[0][rows, :] if nx == 1 else jnp.where(from_prompt, x_refs[0][rows, :], x_refs[1][rows, :])
        xn = x + gate * _dot(a, wb_ref[...])
        y = _rms(xn, g_ref[...])
        if final:
            rest[0][rows, :] = y
        else:
            mn = modn_ref[pl.ds(r, 1), :]
            rest[0][rows, :] = xn
            rest[1][rows, :] = (y * (1.0 + mn[:, D:2 * D]) + mn[:, :D]).astype(BF16)


def _out_proj(a_p, a_s, w, x, mod, g_next, mod_next, final, rows=(0, N_TOK)):
    tm = 512
    kw = w.shape[0]
    tile0, nt = rows[0] // tm, rows[1] // tm
    row = pl.BlockSpec((tm, D), lambda i: (i, 0))
    if final:
        out_specs, out_shape = [row], [jax.ShapeDtypeStruct((rows[1], D), F32)]
    else:
        out_specs = [row, row]
        out_shape = [jax.ShapeDtypeStruct((rows[1], D), F32), jax.ShapeDtypeStruct((rows[1], D), BF16)]
    if isinstance(x, tuple):
        x_specs, xs = list(_two_source_specs(tm, D, tile0)), list(x)
    else:
        x_specs, xs = [pl.BlockSpec((tm, D), lambda i: (tile0 + i, 0))], [x]
    return pl.pallas_call(
        functools.partial(_out_proj_kernel, tm=tm, tile0=tile0, final=final, nx=len(xs)),
        grid=(nt,),
        in_specs=[*_two_source_specs(tm, kw, tile0),
                  pl.BlockSpec((kw, D), lambda i: (0, 0)),
                  pl.BlockSpec((MOD_ROWS, 3 * D), lambda i: (0, 0)),
                  pl.BlockSpec((1, D), lambda i: (0, 0)),
                  pl.BlockSpec((MOD_ROWS, 3 * D), lambda i: (0, 0)),
                  *x_specs],
        out_specs=out_specs,
        out_shape=out_shape,
        scratch_shapes=[pltpu.VMEM((kw, D), BF16)],
        compiler_params=_params("arbitrary"),
        name="out_proj",
    )(a_p, a_s, w, mod, g_next.reshape(1, D), mod_next, *xs)


def _ret_kernel(*refs, T, HB, rope, state_in, state_out):
    refs = list(refs)
    dec_ref, q_ref, k_ref, v_ref, z_ref, gn_ref = refs[:6]
    pos = 6
    if rope:
        cos_ref, sin_ref = refs[pos:pos + 2]
        pos += 2
    if state_in:
        s0_ref = refs[pos]
        pos += 1
    o_ref = refs[pos]
    pos += 1
    if state_out:
        sn_ref = refs[pos]
        pos += 1
    kv_ref, sf_ref, sb_ref, tab_ref, gblk_ref = refs[pos:pos + 5]
    pos += 5
    if rope:
        kr_ref = refs[pos]

    hg = pl.program_id(1)
    n = T // CHUNK
    L = CHUNK
    ii = lax.broadcasted_iota(jnp.int32, (L, L), 0).astype(F32)
    jj = lax.broadcasted_iota(jnp.int32, (L, L), 1).astype(F32)
    scale = RET_DK ** -0.5

    def chunk(c):
        return slice(c * L, (c + 1) * L)

    def rotate(x, c):
        return x * cos_ref[chunk(c), :] + pltpu.roll(x, RET_DK // 2, 1) * sin_ref[chunk(c), :]

    @pl.when(pl.program_id(0) == 0)
    def _():
        for hh in range(HB):
            head = hg * HB + hh
            lg_f = _log_sigmoid(jnp.full((1, RET_DV), dec_ref[0, head], F32))
            lg_b = _log_sigmoid(jnp.full((1, RET_DV), dec_ref[1, head], F32))
            lf, lb = lg_f[:, :L], lg_b[:, :L]
            tab_ref[head, 0] = jnp.exp(lf * (L - 1.0 - ii)) * scale
            tab_ref[head, 1] = jnp.exp(lb * ii) * scale
            tab_ref[head, 2] = jnp.exp(lf * (ii + 1.0))
            tab_ref[head, 3] = jnp.exp(lb * (L - ii))
            tab_ref[head, 4] = (jnp.where(ii >= jj, jnp.exp(lf * jnp.maximum(ii - jj, 0.0)), 0.0)
                                + jnp.where(jj >= ii, jnp.exp(lb * jnp.maximum(jj - ii, 0.0)), 0.0)) * scale
            gblk_ref[head, 0] = jnp.exp(lg_f * float(L))
            gblk_ref[head, 1] = jnp.exp(lg_b * float(L))

    for hh in range(HB):
        head = hg * HB + hh
        qs = slice(hh * RET_DK, (hh + 1) * RET_DK)
        vs = slice(hh * RET_DV, (hh + 1) * RET_DV)
        k_dec_f, k_dec_b, q_dec_f, q_dec_b, mask = (tab_ref[head, t] for t in range(5))
        g_f = gblk_ref[head, 0]
        g_b = gblk_ref[head, 1]

        for c in range(n):
            kc = k_ref[chunk(c), qs].astype(F32)
            if rope:
                kc = rotate(kc, c)
                kr_ref[chunk(c), qs] = kc.astype(BF16)
            kk = jnp.concatenate([kc * k_dec_f, kc * k_dec_b], axis=1).astype(BF16)
            kv_ref[hh, c] = _dot_tn(kk, v_ref[chunk(c), vs])
        S = s0_ref[0, hh] if state_in else None
        has_f = []
        for c in range(n):
            has_f.append(S is not None)
            if S is not None:
                sf_ref[hh, c] = S.astype(BF16)
            kvc = kv_ref[hh, c, :RET_DK, :]
            S = kvc if S is None else S * g_f + kvc
        if state_out:
            sn_ref[0, hh] = S
        S = s0_ref[1, hh] if state_in else None
        has_b = [False] * n
        for c in reversed(range(n)):
            has_b[c] = S is not None
            if S is not None:
                sb_ref[hh, c] = S.astype(BF16)
            kvc = kv_ref[hh, c, RET_DK:, :]
            S = kvc if S is None else S * g_b + kvc
        if state_out:
            sn_ref[1, hh] = S

        for c in range(n):
            qc = q_ref[chunk(c), qs]
            qf = qc.astype(F32)
            if rope:
                qf = rotate(qf, c)
                qc = qf.astype(BF16)
                kc = kr_ref[chunk(c), qs]
            else:
                kc = k_ref[chunk(c), qs]
            lhs = [(_dot_nt(qc, kc) * mask).astype(BF16)]
            rhs = [v_ref[chunk(c), vs]]
            if has_f[c]:
                lhs.append((qf * q_dec_f).astype(BF16))
                rhs.append(sf_ref[hh, c])
            if has_b[c]:
                lhs.append((qf * q_dec_b).astype(BF16))
                rhs.append(sb_ref[hh, c])
            o = _dot(jnp.concatenate(lhs, axis=1), jnp.concatenate(rhs, axis=0))
            zf = z_ref[chunk(c), vs].astype(F32)
            o_ref[chunk(c), vs] = (_group_norm(o, gn_ref[:, vs]) * zf).astype(BF16)


def _retention(proj, dec, gn, T, nb, row_blk0, rope=None, state=None, want_state=False):
    HB = RET_HB_PROMPT if T == SEQ else RET_HB_LATENT
    ng = RET_HEADS // HB
    n = T // CHUNK
    qw, vw = HB * RET_DK, HB * RET_DV
    in_specs = [pl.BlockSpec(memory_space=pltpu.SMEM),
                pl.BlockSpec((T, qw), lambda b, h: (row_blk0 + b, h)),
                pl.BlockSpec((T, qw), lambda b, h: (row_blk0 + b, ng + h)),
                pl.BlockSpec((T, vw), lambda b, h: (row_blk0 + b, ng + h)),
                pl.BlockSpec((T, vw), lambda b, h: (row_blk0 + b, 2 * ng + h)),
                pl.BlockSpec((1, vw), lambda b, h: (0, h))]
    args = [dec, proj, proj, proj, proj, gn.reshape(1, RET_WIDTH)]
    if rope is not None:
        in_specs += [pl.BlockSpec((T, RET_DK), lambda b, h: (0, 0))] * 2
        args += list(rope)
    if state is not None:
        in_specs.append(pl.BlockSpec((None, 2, HB, RET_DK, RET_DV), lambda b, h: (b, 0, h, 0, 0)))
        args.append(state)
    out_specs = [pl.BlockSpec((T, vw), lambda b, h: (b, h))]
    out_shape = [jax.ShapeDtypeStruct((nb * T, RET_WIDTH), BF16)]
    if want_state:
        out_specs.append(pl.BlockSpec((None, 2, HB, RET_DK, RET_DV), lambda b, h: (b, 0, h, 0, 0)))
        out_shape.append(jax.ShapeDtypeStruct((nb, 2, RET_HEADS, RET_DK, RET_DV), F32))
    scratch = [pltpu.VMEM((HB, n, 2 * RET_DK, RET_DV), F32),
               pltpu.VMEM((HB, n, RET_DK, RET_DV), BF16),
               pltpu.VMEM((HB, n, RET_DK, RET_DV), BF16),
               pltpu.VMEM((RET_HEADS, 5, CHUNK, CHUNK), F32),
               pltpu.VMEM((RET_HEADS, 2, 1, RET_DV), F32)]
    if rope is not None:
        scratch.append(pltpu.VMEM((T, qw), BF16))
    return pl.pallas_call(
        functools.partial(_ret_kernel, T=T, HB=HB, rope=rope is not None, state_in=state is not None,
                          state_out=want_state),
        grid=(nb, ng),
        in_specs=in_specs, out_specs=out_specs, out_shape=out_shape, scratch_shapes=scratch,
        compiler_params=_params("arbitrary", "arbitrary"),
        name="retention",
    )(*args)


def _ret_full_kernel(dec_ref, q_ref, k_ref, v_ref, z_ref, gn_ref, o_ref, sn_ref, mask_ref, kdec_ref, *, T):
    scale = RET_DK ** -0.5

    @pl.when(pl.program_id(0) == 0)
    def _():
        ii = lax.broadcasted_iota(jnp.int32, (T, T), 0).astype(F32)
        jj = lax.broadcasted_iota(jnp.int32, (T, T), 1).astype(F32)
        row = lax.broadcasted_iota(jnp.int32, (T, RET_DK), 0).astype(F32)
        for head in range(RET_HEADS):
            lf = _log_sigmoid(jnp.full((1, T), dec_ref[0, head], F32))
            lb = _log_sigmoid(jnp.full((1, T), dec_ref[1, head], F32))
            mask_ref[head] = (jnp.where(ii >= jj, jnp.exp(lf * jnp.maximum(ii - jj, 0.0)), 0.0)
                              + jnp.where(jj >= ii, jnp.exp(lb * jnp.maximum(jj - ii, 0.0)), 0.0)) * scale
            kdec_ref[head, 0] = jnp.exp(lf[:, :RET_DK] * (T - 1.0 - row)) * scale
            kdec_ref[head, 1] = jnp.exp(lb[:, :RET_DK] * row) * scale

    for head in range(RET_HEADS):
        qs = slice(head * RET_DK, (head + 1) * RET_DK)
        vs = slice(head * RET_DV, (head + 1) * RET_DV)
        k = k_ref[:, qs]
        v = v_ref[:, vs]
        a = (_dot_nt(q_ref[:, qs], k) * mask_ref[head]).astype(BF16)
        o = _dot(a, v)
        o_ref[:, vs] = (_group_norm(o, gn_ref[:, vs]) * z_ref[:, vs].astype(F32)).astype(BF16)
        kf = k.astype(F32)
        kk = jnp.concatenate([kf * kdec_ref[head, 0], kf * kdec_ref[head, 1]], axis=1).astype(BF16)
        kv = _dot_tn(kk, v)
        sn_ref[0, head] = kv[:RET_DK]
        sn_ref[1, head] = kv[RET_DK:]


def _retention_full(proj, dec, gn, T, nb):
    return pl.pallas_call(
        functools.partial(_ret_full_kernel, T=T),
        grid=(nb,),
        in_specs=[pl.BlockSpec(memory_space=pltpu.SMEM),
                  pl.BlockSpec((T, RET_QK), lambda b: (b, 0)),
                  pl.BlockSpec((T, RET_QK), lambda b: (b, 1)),
                  pl.BlockSpec((T, RET_WIDTH), lambda b: (b, 1)),
                  pl.BlockSpec((T, RET_WIDTH), lambda b: (b, 2)),
                  pl.BlockSpec((1, RET_WIDTH), lambda b: (0, 0))],
        out_specs=[pl.BlockSpec((T, RET_WIDTH), lambda b: (b, 0)),
                   pl.BlockSpec((None, 2, RET_HEADS, RET_DK, RET_DV), lambda b: (b, 0, 0, 0, 0))],
        out_shape=[jax.ShapeDtypeStruct((nb * T, RET_WIDTH), BF16),
                   jax.ShapeDtypeStruct((nb, 2, RET_HEADS, RET_DK, RET_DV), F32)],
        scratch_shapes=[pltpu.VMEM((RET_HEADS, T, T), F32), pltpu.VMEM((RET_HEADS, 2, T, RET_DK), F32)],
        compiler_params=_params("arbitrary"),
        name="retention_full",
    )(dec, proj, proj, proj, proj, gn.reshape(1, RET_WIDTH))


def _ml_in_kernel(h_ref, w_ref, cw_ref, cb_ref, bq_ref, bk_ref, bv_ref, wq_ref, wk_ref, wv_ref, bias_ref,
                  oz_ref, q_ref, k_ref, v_ref, xc_ref, g_ref, wb_ref, bd_ref, conv_ref):
    j = pl.program_id(0)
    i = pl.program_id(1)
    nsub = ML_TN // ML_TILE

    @pl.when(i == 0)
    def _():
        wb_ref[...] = w_ref[...].astype(BF16)

    @pl.when((i == 0) & (j < ML_XT))
    def _():
        row = lax.broadcasted_iota(jnp.int32, (ML_TILE, ML_TILE), 0)
        col = lax.broadcasted_iota(jnp.int32, (ML_TILE, ML_TILE), 1)
        shift = ML_BLOCK.bit_length() - 1
        same_block = jnp.right_shift(row, shift) == jnp.right_shift(col, shift)
        for which, b_ref in enumerate((bq_ref, bk_ref, bv_ref)):
            for t in range(nsub):
                w = b_ref[t * ML_TILE:(t + 1) * ML_TILE, :]
                rep = w[:, ML_BLOCK - 1:ML_BLOCK]
                for dd in range(ML_BLOCK - 1):
                    rep = jnp.where((col & (ML_BLOCK - 1)) == dd, w[:, dd:dd + 1], rep)
                bd_ref[which, t] = jnp.where(same_block, rep, 0.0).astype(BF16)

    @pl.when(j >= ML_XT)
    def _():
        sub = ML_ROWS // IN_SUB
        for s in range(IN_SUB):
            rows = slice(s * sub, (s + 1) * sub)
            acc = _dot(h_ref[rows, :], wb_ref[...])
            col = j * ML_TN + lax.broadcasted_iota(jnp.int32, acc.shape, 1)
            sig = jax.nn.sigmoid(acc)
            oz_ref[rows, :] = jnp.where(col >= 2 * ML_WIDTH, acc * sig, sig).astype(BF16)

    @pl.when(j < ML_XT)
    def _():
        acc = _dot(h_ref[...], wb_ref[...])
        T = jnp.where(i < N_PROMPT // ML_ROWS, SEQ, DEC_SEQ)
        pad = ML_CONV // 2
        taps = [tap for tap in range(ML_CONV) if tap != pad]
        sub8 = lax.broadcasted_iota(jnp.int32, (8, ML_TILE), 0)
        g = jnp.where(j == 0, 1.0, 0.0) * bias_ref[...]
        for t in range(nsub):
            cols = slice(t * ML_TILE, (t + 1) * ML_TILE)
            x = acc[:, cols]
            shifted = {tap: pltpu.roll(x, (pad - tap) % ML_ROWS, 0) for tap in taps}
            conv = cb_ref[:, cols] + x * cw_ref[pad:pad + 1, cols]
            for tap in taps:
                conv = conv + shifted[tap] * cw_ref[tap:tap + 1, cols]
            conv_ref[...] = conv
            for r0 in ML_EDGE_ROWS:
                rows = slice(r0, r0 + 8)
                pos = (r0 + sub8) & (T - 1)
                fixed = cb_ref[:, cols] + x[rows] * cw_ref[pad:pad + 1, cols]
                for tap in taps:
                    ok = (pos + (tap - pad) >= 0) & (pos + (tap - pad) < T)
                    fixed = fixed + jnp.where(ok, shifted[tap][rows], 0.0) * cw_ref[tap:tap + 1, cols]
                conv_ref[rows, :] = fixed
            xcb = _silu(conv_ref[...]).astype(BF16)
            qb = _dot(xcb, bd_ref[0, t]).astype(BF16)
            kb = _dot(xcb, bd_ref[1, t]).astype(BF16)
            vb = _dot(x.astype(BF16), bd_ref[2, t]).astype(BF16)
            q_ref[:, cols] = qb
            k_ref[:, cols] = kb
            v_ref[:, cols] = vb
            xc_ref[:, cols] = xcb
            g = g + (_dot(qb, wq_ref[cols, :].astype(BF16)) + _dot(kb, wk_ref[cols, :].astype(BF16))
                     + _dot(vb, wv_ref[cols, :].astype(BF16)))
        g_ref[...] = g


def _ml_in_proj(h, w, conv_w, conv_b, wq, wk, wv, wif, bif):
    ng = wif.shape[1]
    ni = N_TOK // ML_ROWS
    xt = ML_XT
    wq, wk, wv = (a.reshape(ML_WIDTH, ML_BLOCK) for a in (wq, wk, wv))
    xcol = lambda j, i: (0, jnp.minimum(j, xt - 1))
    xrow = lambda off: (lambda j, i: (off + jnp.minimum(j, xt - 1), 0))
    x_out = pl.BlockSpec((ML_ROWS, ML_TN), lambda j, i: (jnp.where(j < xt, i, ni - 1), jnp.minimum(j, xt - 1)))
    return pl.pallas_call(
        _ml_in_kernel,
        grid=(3 * ML_WIDTH // ML_TN, ni),
        in_specs=[pl.BlockSpec((ML_ROWS, D), lambda j, i: (i, 0)),
                  pl.BlockSpec((D, ML_TN), lambda j, i: (0, j)),
                  pl.BlockSpec((ML_CONV, ML_TN), xcol),
                  pl.BlockSpec((1, ML_TN), xcol),
                  pl.BlockSpec((ML_TN, ML_BLOCK), xrow(0)),
                  pl.BlockSpec((ML_TN, ML_BLOCK), xrow(0)),
                  pl.BlockSpec((ML_TN, ML_BLOCK), xrow(0)),
                  pl.BlockSpec((ML_TN, ng), xrow(0)),
                  pl.BlockSpec((ML_TN, ng), xrow(xt)),
                  pl.BlockSpec((ML_TN, ng), xrow(2 * xt)),
                  pl.BlockSpec((1, ng), lambda j, i: (0, 0))],
        out_specs=[pl.BlockSpec((ML_ROWS, ML_TN), lambda j, i: (jnp.where(j < xt, 0, i), jnp.maximum(j - xt, 0))),
                   x_out, x_out, x_out, x_out,
                   pl.BlockSpec((None, ML_ROWS, ng),
                                lambda j, i: (jnp.minimum(j, xt - 1), jnp.where(j < xt, i, ni - 1), 0))],
        out_shape=[jax.ShapeDtypeStruct((N_TOK, 2 * ML_WIDTH), BF16)]
        + [jax.ShapeDtypeStruct((N_TOK, ML_WIDTH), BF16)] * 4
        + [jax.ShapeDtypeStruct((xt, N_TOK, ng), F32)],
        scratch_shapes=[pltpu.VMEM((D, ML_TN), BF16), pltpu.VMEM((3, ML_TN // ML_TILE, ML_TILE, ML_TILE), BF16),
                        pltpu.VMEM((ML_ROWS, ML_TILE), F32)],
        compiler_params=_params("arbitrary", "arbitrary"),
        name="mlstm_in_proj",
    )(h, w, conv_w, conv_b.reshape(1, ML_WIDTH), wq, wk, wv, wif, wif, wif, bif.reshape(1, ng))


def _dot_split(lhs_bf16, rhs):
    r1 = rhs.astype(BF16)
    e1 = rhs - r1.astype(F32)
    r2 = e1.astype(BF16)
    r3 = (e1 - r2.astype(F32)).astype(BF16)
    return _dot(lhs_bf16, r1) + _dot(lhs_bf16, r2) + _dot(lhs_bf16, r3)


def _ml_scan_kernel(*refs, T, HB, state_in, state_out):
    refs = list(refs)
    pos = 0
    m0_ref = c0_ref = n0_ref = cn_ref = nn_ref = mn_ref = None
    if state_in:
        m0_ref = refs[0]
        pos = 1
    q_ref, k_ref, v_ref, g_ref, op_ref, z_ref, xc_ref, gn_ref, sk_ref = refs[pos:pos + 9]
    pos += 9
    if state_in:
        c0_ref, n0_ref = refs[pos:pos + 2]
        pos += 2
    o_ref = refs[pos]
    pos += 1
    if state_out:
        cn_ref, nn_ref, mn_ref = refs[pos:pos + 3]
        pos += 3
    acc_ref, s_ref = refs[pos:pos + 2]

    for hh in range(HB):
        cols = slice(hh * ML_HD, (hh + 1) * ML_HD)

        def head_cols(ref):
            return ref.at[:, cols]

        def head_state(ref):
            return None if ref is None else ref.at[:, hh]

        _ml_scan_head(pl.program_id(0), pl.program_id(1) * HB + hh, m0_ref,
                      head_cols(q_ref), head_cols(k_ref), head_cols(v_ref), g_ref, head_cols(op_ref),
                      head_cols(z_ref), head_cols(xc_ref), head_cols(gn_ref), head_cols(sk_ref),
                      head_state(c0_ref), head_state(n0_ref), head_cols(o_ref),
                      head_state(cn_ref), head_state(nn_ref), head_state(mn_ref), acc_ref.at[hh], s_ref.at[hh],
                      T=T, state_in=state_in, state_out=state_out)


def _ml_scan_head(b, h, m0_ref, q_ref, k_ref, v_ref, g_ref, op_ref, z_ref, xc_ref, gn_ref, sk_ref,
                  c0_ref, n0_ref, o_ref, cn_ref, nn_ref, mn_ref, acc_ref, s_ref, *, T, state_in, state_out):
    n = T // CHUNK
    L = CHUNK
    scale = ML_HD ** -0.5

    def chunk(c):
        return slice(c * L, (c + 1) * L)

    ii = lax.broadcasted_iota(jnp.int32, (L, L), 0)
    jj = lax.broadcasted_iota(jnp.int32, (L, L), 1)
    neg = jnp.full((L, L), -jnp.inf, F32)
    zero = jnp.zeros((L, L), F32)
    tri = [jnp.where(jj <= ii, 1.0, 0.0).astype(BF16), jnp.where(jj >= ii, 1.0, 0.0).astype(BF16)]
    bias = [jnp.where(ii >= jj, zero, neg), jnp.where(ii <= jj, zero, neg)]

    g = g_ref[0]
    for part in range(1, ML_XT):
        g = g + g_ref[part]
    col = lax.broadcasted_iota(jnp.int32, g.shape, 1)
    g = jnp.where((col & (2 * ML_HEADS - 1)) >= ML_HEADS, _log_sigmoid(g), g)

    def gate_column(j):
        return jnp.sum(jnp.where(col == j, g, 0.0), axis=1, keepdims=True)


    for i in range(n):
        for j in range(n):
            s_ref[i, j] = _dot_nt(q_ref[chunk(i), :], k_ref[chunk(j), :])

    G = []
    for d in range(2):
        order = list(range(n)) if d == 0 else list(reversed(range(n)))
        gi_all = gate_column(d * 2 * ML_HEADS + h)
        gf_all = gate_column(d * 2 * ML_HEADS + ML_HEADS + h)
        m0 = jnp.full((1, 1), m0_ref[(b * 2 + d) * ML_HEADS + h], F32) if state_in else jnp.zeros((1, 1), F32)
        c_col, c_row, b_col, m_row = [None] * n, [None] * n, [None] * n, [None] * n
        offset = jnp.zeros((1, 1), F32)
        carry = m0
        for c in order:
            gi = gi_all[chunk(c)]
            gf = gf_all[chunk(c)]
            cum = _dot_split(tri[d], jnp.broadcast_to(gf, (L, L))) + offset
            offset = offset + jnp.sum(gf, axis=0, keepdims=True)
            cc = gi - cum
            c_col[c] = cc[:, :1]
            c_row[c] = jnp.transpose(cc)
            b_col[c] = cum[:, :1]
            m_row[c] = jnp.maximum(jnp.max(c_row[c] + bias[d], axis=1, keepdims=True), carry)
            carry = jnp.maximum(carry, jnp.max(c_col[c], axis=0, keepdims=True))
        G.append(dict(order=order, c_col=c_col, c_row=c_row, b_col=b_col, m_row=m_row, m0=m0,
                      m_end=carry, b_end=offset))

    done = set()
    for i in range(n):
        for d in range(2):
            e = G[d]
            js = list(range(0, i + 1)) if d == 0 else list(range(i, n))
            m_i = e["m_row"][i]
            parts = []
            tot = None
            for j in js:
                z = e["c_row"][j] - m_i
                if j == i:
                    z = z + bias[d]
                sw = s_ref[i, j] * (jnp.exp(z) * scale)
                tot = sw if tot is None else tot + sw
                parts.append(sw.astype(BF16))
            num = _dot(jnp.concatenate(parts, axis=1), v_ref[js[0] * L:(js[-1] + 1) * L, :])
            den = jnp.sum(tot, axis=1, keepdims=True)
            if state_in:
                qc = q_ref[chunk(i), :]
                w0 = jnp.exp(e["m0"] - m_i)
                num = num + _dot((qc.astype(F32) * w0).astype(BF16), c0_ref[d].astype(BF16))
                n0 = jnp.broadcast_to(n0_ref[d], (8, ML_HD)).astype(BF16)
                den = den + w0 * _dot_nt(qc, n0)[:, :1]
            hb = num / jnp.maximum(jnp.abs(den), jnp.exp(-(e["b_col"][i] + m_i)))
            if i not in done:
                done.add(i)
                acc_ref[chunk(i), :] = hb
            else:
                cell = _group_norm((acc_ref[chunk(i), :] + hb) * op_ref[chunk(i), :].astype(F32), gn_ref[...])
                mixed = cell + sk_ref[...] * xc_ref[chunk(i), :].astype(F32)
                o_ref[chunk(i), :] = (mixed * z_ref[chunk(i), :].astype(F32)).astype(BF16)

    if state_out:
        ones = jnp.ones((8, T), BF16)
        for d in range(2):
            e = G[d]
            kw = jnp.concatenate(
                [(k_ref[chunk(c), :].astype(F32) * (jnp.exp(e["c_col"][c] - e["m_end"]) * scale)).astype(BF16)
                 for c in range(n)], axis=0)
            C = _dot_tn(kw, v_ref[...])
            nvec = _dot(ones, kw)[:1]
            if state_in:
                w_end = jnp.exp(e["m0"] - e["m_end"])
                C = C + w_end * c0_ref[d]
                nvec = nvec + w_end * n0_ref[d]
            cn_ref[d] = C
            nn_ref[d] = nvec
            mn_ref[d] = jnp.broadcast_to(e["b_end"] + e["m_end"], mn_ref.shape[1:])


def _ml_scan(oz, q, k, v, xc, g, gn, skip, T, nb, row_blk0, state=None, want_state=False):
    H = ML_HEADS
    HB = ML_HB_PROMPT if T == SEQ else ML_HB_LATENT
    ng = H // HB
    hw = HB * ML_HD
    n = T // CHUNK
    tok = lambda b, h: (row_blk0 + b, h)
    head_vec = pl.BlockSpec((1, hw), lambda b, h: (0, h))
    both = lambda *tail: pl.BlockSpec((None, 2, HB, *tail), lambda b, h: (b, 0, h, 0, 0))
    in_specs, args = [], []
    if state is not None:
        c0, n0, m0 = state
        in_specs.append(pl.BlockSpec(memory_space=pltpu.SMEM))
        args.append(m0.reshape(nb * 2 * H))
    in_specs += [pl.BlockSpec((T, hw), tok)] * 3
    in_specs += [pl.BlockSpec((ML_XT, T, g.shape[2]), lambda b, h: (0, row_blk0 + b, 0)),
                 pl.BlockSpec((T, hw), tok),
                 pl.BlockSpec((T, hw), lambda b, h: (row_blk0 + b, ng + h)),
                 pl.BlockSpec((T, hw), tok), head_vec, head_vec]
    args += [q, k, v, g, oz, oz, xc, gn.reshape(1, ML_WIDTH), skip.reshape(1, ML_WIDTH)]
    if state is not None:
        in_specs += [both(ML_HD, ML_HD), both(1, ML_HD)]
        args += [c0, n0.reshape(nb, 2, H, 1, ML_HD)]
    out_specs = [pl.BlockSpec((T, hw), lambda b, h: (b, h))]
    out_shape = [jax.ShapeDtypeStruct((nb * T, ML_WIDTH), BF16)]
    if want_state:
        out_specs += [both(ML_HD, ML_HD), both(1, ML_HD), both(1, 128)]
        out_shape += [jax.ShapeDtypeStruct((nb, 2, H, ML_HD, ML_HD), F32),
                      jax.ShapeDtypeStruct((nb, 2, H, 1, ML_HD), F32),
                      jax.ShapeDtypeStruct((nb, 2, H, 1, 128), F32)]
    scratch = [pltpu.VMEM((HB, T, ML_HD), F32), pltpu.VMEM((HB, n, n, CHUNK, CHUNK), F32)]
    return pl.pallas_call(
        functools.partial(_ml_scan_kernel, T=T, HB=HB, state_in=state is not None, state_out=want_state),
        grid=(nb, ng),
        in_specs=in_specs, out_specs=out_specs, out_shape=out_shape, scratch_shapes=scratch,
        compiler_params=_params("arbitrary", "arbitrary"),
        name="mlstm_scan",
    )(*args)


def _att_rope(x, cos, sin):
    lane = lax.broadcasted_iota(jnp.int32, x.shape, 1)
    half = ATT_HD // 2
    rot = jnp.where((lane & (ATT_HD - 1)) < half, pltpu.roll(x, 128 - half, 1), pltpu.roll(x, half, 1))
    return x * cos + rot * sin


def _att_heads(qms, sinks, key_sets):
    scores = [[_dot_nt(k_both, qm) if bias is None else _dot_nt(k_both, qm) + bias
               for k_both, _, bias in key_sets] for qm in qms]
    ms = []
    for sink, per_set in zip(sinks, scores):
        m = sink
        for s in per_set:
            m = jnp.maximum(m, jnp.max(s, axis=0, keepdims=True))
        ms.append(m)
    probs = [[jnp.exp(s - m) for s in per_set] for m, per_set in zip(ms, scores)]
    outs = []
    for sink, m, per_set in zip(sinks, ms, probs):
        den = jnp.exp(sink - m)
        out = None
        for p, (_, v_t, _) in zip(per_set, key_sets):
            den = den + jnp.sum(p, axis=0, keepdims=True)
            o = _dot(v_t, p.astype(BF16))
            out = o if out is None else out + o
        outs.append(out * (1.0 / den))
    return outs


def _att_prepare_kv(k, v, kb_ref, vt_ref, kv, nblk):
    lane = lax.broadcasted_iota(jnp.int32, k.shape, 1)
    native = lane < ATT_HD if kv % 2 == 0 else lane >= ATT_HD
    kb_ref[kv] = jnp.where(native, k, pltpu.roll(k, ATT_HD, 1)).astype(BF16)
    r0 = (kv % 2) * ATT_HD
    blk = k.shape[0] // nblk
    for c in range(nblk):
        vt_ref[kv, c] = jnp.transpose(v[c * blk:(c + 1) * blk, :])[r0:r0 + ATT_HD, :].astype(BF16)


def _att_group(q_ref, z0_ref, z1_ref, o_ref, sink_ref, rows, kv, key_sets):
    lo = lax.broadcasted_iota(jnp.int32, (CHUNK, 128), 1) < ATT_HD
    zero = jnp.zeros((CHUNK, 128), BF16)
    qms, sinks = [], []
    for g in range(ATT_GROUP):
        col = (2 * kv + g // 2) * 128
        q = q_ref[rows, col:col + 128]
        qms.append(jnp.where(lo, q, zero) if g % 2 == 0 else jnp.where(lo, zero, q))
        sinks.append(jnp.full((1, CHUNK), sink_ref[kv * ATT_GROUP + g], F32))
    heads = _att_heads(qms, sinks, key_sets)
    for p in range(2):
        col = (2 * kv + p) * 128
        out = jnp.transpose(jnp.concatenate(heads[2 * p:2 * p + 2], axis=0))
        z_ref = z0_ref if col < ATT_WIDTH // 2 else z1_ref
        zc = col % (ATT_WIDTH // 2)
        zf = z_ref[rows, zc:zc + 128].astype(F32)
        o_ref[rows, col:col + 128] = (out * zf).astype(BF16)


def _att_ctx_kernel(sink_ref, q_ref, k_ref, v_ref, z0_ref, z1_ref, o_ref, qs_ref, kb_ref, vt_ref):
    T = q_ref.shape[0]
    qs_ref[...] = q_ref[...] * (ATT_HD ** -0.5)
    for kv in range(ATT_KV):
        slab = (kv // 2) * 128
        _att_prepare_kv(k_ref[:, slab:slab + 128].astype(F32), v_ref[:, slab:slab + 128].astype(F32),
                        kb_ref, vt_ref, kv, 1)
    for c in range(T // CHUNK):
        rows = slice(c * CHUNK, (c + 1) * CHUNK)
        for kv in range(ATT_KV):
            _att_group(qs_ref, z0_ref, z1_ref, o_ref, sink_ref, rows, kv, [(kb_ref[kv], vt_ref[kv, 0], None)])


def _att_ctx(proj, sink):
    T = SEQ
    half = ATT_WIDTH // 2
    return pl.pallas_call(
        _att_ctx_kernel,
        grid=(BATCH,),
        in_specs=[pl.BlockSpec(memory_space=pltpu.SMEM),
                  pl.BlockSpec((T, ATT_WIDTH), lambda b: (b, 0)),
                  pl.BlockSpec((T, ATT_KVW), lambda b: (b, ATT_WIDTH // ATT_KVW)),
                  pl.BlockSpec((T, ATT_KVW), lambda b: (b, ATT_WIDTH // ATT_KVW + 1)),
                  pl.BlockSpec((T, half), lambda b: (b, (ATT_WIDTH + 2 * ATT_KVW) // half)),
                  pl.BlockSpec((T, half), lambda b: (b, (ATT_WIDTH + 2 * ATT_KVW) // half + 1))],
        out_specs=pl.BlockSpec((T, ATT_WIDTH), lambda b: (b, 0)),
        out_shape=jax.ShapeDtypeStruct((N_PROMPT, ATT_WIDTH), BF16),
        scratch_shapes=[pltpu.VMEM((T, ATT_WIDTH), BF16), pltpu.VMEM((ATT_KV, T, 128), BF16),
                        pltpu.VMEM((ATT_KV, 1, ATT_HD, T), BF16)],
        compiler_params=_params("arbitrary"),
        name="att_ctx",
    )(sink, proj, proj, proj, proj, proj)


def _att_win_kernel(sink_ref, q_ref, k_ref, v_ref, z0_ref, z1_ref, kc_ref, vc_ref, cos_ref, sin_ref,
                    o_ref, qs_ref, kb_ref, vt_ref, kcb_ref, vct_ref):
    T = q_ref.shape[0]
    L = CHUNK
    n = T // L
    cos = cos_ref[...]
    sin = sin_ref[...]
    for c0 in range(0, ATT_WIDTH, 128):
        q = _att_rope(q_ref[:, c0:c0 + 128].astype(F32), cos, sin)
        qs_ref[:, c0:c0 + 128] = (q * (ATT_HD ** -0.5)).astype(BF16)
    for kv in range(ATT_KV):
        slab = (kv // 2) * 128
        _att_prepare_kv(_att_rope(k_ref[:, slab:slab + 128].astype(F32), cos, sin),
                        v_ref[:, slab:slab + 128].astype(F32), kb_ref, vt_ref, kv, n)
        _att_prepare_kv(kc_ref[:, slab:slab + 128], vc_ref[:, slab:slab + 128], kcb_ref, vct_ref, kv, 1)

    jj = lax.broadcasted_iota(jnp.int32, (L, L), 0)
    ii = lax.broadcasted_iota(jnp.int32, (L, L), 1)
    neg = jnp.full((L, L), -jnp.inf, F32)
    bias_prev = jnp.where(jj >= ii, 0.0, neg)
    bias_next = jnp.where(jj <= ii, 0.0, neg)

    def body(c, carry):
        rows = pl.ds(pl.multiple_of(c * L, L), L)
        c_prev = jnp.maximum(c - 1, 0)
        c_next = jnp.minimum(c + 1, n - 1)
        b_prev = jnp.where(c > 0, bias_prev, neg)
        b_next = jnp.where(c < n - 1, bias_next, neg)
        for kv in range(ATT_KV):
            def keys(cb):
                return kb_ref[kv, pl.ds(pl.multiple_of(cb * L, L), L), :]
            key_sets = [(keys(c_prev), vt_ref[kv, c_prev], b_prev),
                        (keys(c), vt_ref[kv, c], None),
                        (keys(c_next), vt_ref[kv, c_next], b_next),
                        (kcb_ref[kv], vct_ref[kv, 0], None)]
            _att_group(qs_ref, z0_ref, z1_ref, o_ref, sink_ref, rows, kv, key_sets)
        return carry

    lax.fori_loop(0, n, body, 0)


def _att_win(proj, sink, k_ctx, v_ctx, cos, sin):
    T = DEC_SEQ
    half = ATT_WIDTH // 2
    rb = N_PROMPT // T
    return pl.pallas_call(
        _att_win_kernel,
        grid=(DEC_BATCH,),
        in_specs=[pl.BlockSpec(memory_space=pltpu.SMEM),
                  pl.BlockSpec((T, ATT_WIDTH), lambda b: (rb + b, 0)),
                  pl.BlockSpec((T, ATT_KVW), lambda b: (rb + b, ATT_WIDTH // ATT_KVW)),
                  pl.BlockSpec((T, ATT_KVW), lambda b: (rb + b, ATT_WIDTH // ATT_KVW + 1)),
                  pl.BlockSpec((T, half), lambda b: (rb + b, (ATT_WIDTH + 2 * ATT_KVW) // half)),
                  pl.BlockSpec((T, half), lambda b: (rb + b, (ATT_WIDTH + 2 * ATT_KVW) // half + 1)),
                  pl.BlockSpec((None, PAST_LEN, ATT_KVW), lambda b: (b, 0, 0)),
                  pl.BlockSpec((None, PAST_LEN, ATT_KVW), lambda b: (b, 0, 0)),
                  pl.BlockSpec((T, 128), lambda b: (0, 0)),
                  pl.BlockSpec((T, 128), lambda b: (0, 0))],
        out_specs=pl.BlockSpec((T, ATT_WIDTH), lambda b: (b, 0)),
        out_shape=jax.ShapeDtypeStruct((N_LATENT, ATT_WIDTH), BF16),
        scratch_shapes=[pltpu.VMEM((T, ATT_WIDTH), BF16),
                        pltpu.VMEM((ATT_KV, T, 128), BF16),
                        pltpu.VMEM((ATT_KV, T // CHUNK, ATT_HD, CHUNK), BF16),
                        pltpu.VMEM((ATT_KV, PAST_LEN, 128), BF16),
                        pltpu.VMEM((ATT_KV, 1, ATT_HD, PAST_LEN), BF16)],
        compiler_params=_params("arbitrary"),
        name="att_win",
    )(sink, proj, proj, proj, proj, proj, k_ctx, v_ctx, cos, sin)


def _rope_tables(T, hd, reps):
    rows = T // GRID_W
    row = np.repeat(np.arange(rows, dtype=np.float64), GRID_W)
    col = np.tile(np.arange(GRID_W, dtype=np.float64), rows)
    nf = hd // 4
    inv = ROPE_BASE ** (-np.arange(nf, dtype=np.float64) / nf)
    ang = np.concatenate([row[:, None] * inv[None, :], col[:, None] * inv[None, :]], axis=-1)
    cos, sin = np.cos(ang), np.sin(ang)
    return (jnp.asarray(np.tile(np.concatenate([cos, cos], axis=-1), (1, reps)), F32),
            jnp.asarray(np.tile(np.concatenate([-sin, sin], axis=-1), (1, reps)), F32))


def _retention_layer(h, w_in, decay_f, decay_b, gn, state, rope):
    z0 = 2 * RET_QK + RET_WIDTH
    proj = _in_proj(h, w_in, 2048, (z0, z0))
    dec = jnp.stack([decay_f, decay_b]).astype(F32)
    a_p, new_state = _retention_full(proj, dec, gn, SEQ, BATCH)
    (a_s,) = _retention(proj, dec, gn, DEC_SEQ, DEC_BATCH, N_PROMPT // DEC_SEQ, rope=rope, state=state)
    return a_p, a_s, new_state


def kernel(x_prompt, x_sample, c, c_ctx, state_l0_ret, state_l1_C, state_l1_n, state_l1_m, cache_l2_k, cache_l2_v, state_l3_ret, norm_l0, ada_w_l0, ada_b_l0, w_in_l0, w_out_l0, ret_decay_f_l0, ret_decay_b_l0, ret_gn_l0, norm_l1, ada_w_l1, ada_b_l1, w_in_l1, w_out_l1, conv_w_l1, conv_b_l1, wq_l1, wk_l1, wv_l1, wif_f_l1, bif_f_l1, wif_b_l1, bif_b_l1, gn_l1, skip_l1, norm_l2, ada_w_l2, ada_b_l2, w_in_l2, w_out_l2, sink_l2, norm_l3, ada_w_l3, ada_b_l3, w_in_l3, w_out_l3, ret_decay_f_l3, ret_decay_b_l3, ret_gn_l3, final_norm):
    ct = jnp.concatenate([c_ctx[:, None], c.T, jnp.zeros((D, MOD_ROWS - 1 - DEC_BATCH), F32)], axis=1)
    mod_all = _adaln(ct, (ada_w_l0, ada_w_l1, ada_w_l2, ada_w_l3), (ada_b_l0, ada_b_l1, ada_b_l2, ada_b_l3))
    mods = [mod_all[l] for l in range(mod_all.shape[0])]
    rope_ret = _rope_tables(DEC_SEQ, RET_DK, 1)
    rope_att = _rope_tables(DEC_SEQ, ATT_HD, 2)

    x0 = (x_prompt.reshape(N_PROMPT, D), x_sample.reshape(N_LATENT, D))

    a_p, a_s, new_l0_ret = _retention_layer((*x0, norm_l0, mods[0]), w_in_l0, ret_decay_f_l0, ret_decay_b_l0,
                                            ret_gn_l0, state_l0_ret, rope_ret)
    x, h = _out_proj(a_p, a_s, w_out_l0, x0, mods[0], norm_l1, mods[1], False)

    wif = jnp.concatenate([wif_f_l1, wif_b_l1], axis=1)
    bif = jnp.concatenate([bif_f_l1, bif_b_l1])
    oz, q, k, v, xc, g = _ml_in_proj(h, w_in_l1, conv_w_l1, conv_b_l1, wq_l1, wk_l1, wv_l1, wif, bif)
    a_p, new_l1_C, n_new, m_new = _ml_scan(oz, q, k, v, xc, g, gn_l1, skip_l1, SEQ, BATCH, 0, want_state=True)
    (a_s,) = _ml_scan(oz, q, k, v, xc, g, gn_l1, skip_l1, DEC_SEQ, DEC_BATCH, N_PROMPT // DEC_SEQ,
                      state=(state_l1_C, state_l1_n, state_l1_m))
    new_l1_n = n_new.reshape(BATCH, 2, ML_HEADS, ML_HD)
    new_l1_m = m_new[:, :, :, 0, 0]
    x, h = _out_proj(a_p, a_s, w_out_l1, x, mods[1], norm_l2, mods[2], False)

    proj = _in_proj(h, w_in_l2, 1280, (ATT_WIDTH + 2 * ATT_KVW,) * 2)
    kv_new = proj[:N_PROMPT, ATT_WIDTH:ATT_WIDTH + 2 * ATT_KVW].astype(F32)
    new_l2_k = kv_new[:, :ATT_KVW].reshape(BATCH, SEQ, ATT_KV, ATT_HD)
    new_l2_v = kv_new[:, ATT_KVW:].reshape(BATCH, SEQ, ATT_KV, ATT_HD)
    a_p = _att_ctx(proj, sink_l2)
    a_s = _att_win(proj, sink_l2, cache_l2_k.reshape(DEC_BATCH, PAST_LEN, ATT_KVW),
                   cache_l2_v.reshape(DEC_BATCH, PAST_LEN, ATT_KVW), rope_att[0], rope_att[1])
    x, h = _out_proj(a_p, a_s, w_out_l2, x, mods[2], norm_l3, mods[3], False)

    a_p, a_s, new_l3_ret = _retention_layer(h, w_in_l3, ret_decay_f_l3, ret_decay_b_l3, ret_gn_l3, state_l3_ret,
                                            rope_ret)
    (y_p,) = _out_proj(a_p, a_s, w_out_l3, x, mods[3], final_norm, mods[3], True, rows=(0, N_PROMPT))
    (y_s,) = _out_proj(a_p, a_s, w_out_l3, x, mods[3], final_norm, mods[3], True, rows=(N_PROMPT, N_LATENT))

    y_prompt = y_p.reshape(BATCH, SEQ, D)
    y_sample = y_s.reshape(DEC_BATCH, DEC_SEQ, D)
    return (y_prompt, y_sample, new_l0_ret, new_l1_C, new_l1_n, new_l1_m, new_l2_k, new_l2_v, new_l3_ret)
```

```python
import functools

import jax
import jax.numpy as jnp
import numpy as np
from jax import lax
from jax.experimental import pallas as pl
from jax.experimental.pallas import tpu as pltpu

F32 = jnp.float32
BF16 = jnp.bfloat16

D = 1024
BATCH = 16
SEQ = 256
DEC_BATCH = 2
DEC_SEQ = 1024
PAST_LEN = 512
GRID_W = 64
CHUNK = 128
EPS = 1e-6
ROPE_BASE = 10000.0

N_PROMPT = BATCH * SEQ
N_LATENT = DEC_BATCH * DEC_SEQ
N_TOK = N_PROMPT + N_LATENT
MOD_ROWS = 8

RET_HEADS = 8
RET_DK = 128
RET_DV = 256
RET_QK = RET_HEADS * RET_DK
RET_WIDTH = RET_HEADS * RET_DV
RET_HB_PROMPT = 8
RET_HB_LATENT = 4
ML_HB_PROMPT = 1
ML_HB_LATENT = 1

ML_HEADS = 4
ML_WIDTH = 2 * D
ML_HD = ML_WIDTH // ML_HEADS
ML_BLOCK = 4
ML_CONV = 5
ML_TILE = 256
ML_ROWS = 1024
ML_TN = 1024
ML_XT = ML_WIDTH // ML_TN
ML_EDGE_ROWS = sorted({r for e in range(0, ML_ROWS + 1, SEQ) for r in (e - 8, e) if 0 <= r < ML_ROWS})

ATT_HEADS = 16
ATT_KV = 4
ATT_HD = 64
ATT_GROUP = ATT_HEADS // ATT_KV
ATT_WIDTH = ATT_HEADS * ATT_HD
ATT_KVW = ATT_KV * ATT_HD
WINDOW = 128

OUT_SUB = 256
IN_SUB = 4

VMEM_LIMIT = 56 * 1024 * 1024


def _params(*sem):
    return pltpu.CompilerParams(dimension_semantics=sem, vmem_limit_bytes=VMEM_LIMIT)


def _mod_row(row0):
    return jnp.maximum((row0 - N_PROMPT) // DEC_SEQ + 1, 0)


def _dot(a, b):
    return jnp.dot(a, b, preferred_element_type=F32)


def _dot_nt(a, b):
    return lax.dot_general(a, b, (((1,), (1,)), ((), ())), preferred_element_type=F32)


def _dot_tn(a, b):
    return lax.dot_general(a, b, (((0,), (0,)), ((), ())), preferred_element_type=F32)


def _silu(x):
    return x * jax.nn.sigmoid(x)


def _log_sigmoid(x):
    return jnp.minimum(x, 0.0) - jnp.log1p(jnp.exp(-jnp.abs(x)))


def _rms(x, g):
    return x * lax.rsqrt(jnp.mean(x * x, axis=-1, keepdims=True) + EPS) * g


def _group_norm(x, g):
    xc = x - jnp.mean(x, axis=-1, keepdims=True)
    return xc * lax.rsqrt(jnp.mean(xc * xc, axis=-1, keepdims=True) + EPS) * g


def _two_source_specs(tm, width, tile0=0):
    n_p = N_PROMPT // tm
    return (pl.BlockSpec((tm, width), lambda i: (jnp.minimum(tile0 + i, n_p - 1), 0)),
            pl.BlockSpec((tm, width), lambda i: (jnp.maximum(tile0 + i - n_p, 0), 0)))


def _ada_kernel(ct_ref, *refs, nl, nt):
    o_ref = refs[-1]
    layer = pl.program_id(0) // nt
    s = _silu(ct_ref[...])
    for l in range(nl):
        @pl.when(layer == l)
        def _(w_ref=refs[l], b_ref=refs[nl + l]):
            w = w_ref[...]
            rows = [jnp.sum(w * s[:, r:r + 1], axis=0, keepdims=True) for r in range(1 + DEC_BATCH)]
            rows.append(jnp.zeros((MOD_ROWS - len(rows), w.shape[1]), F32))
            o_ref[...] = jnp.concatenate(rows, axis=0) + b_ref[...]


def _adaln(ct, ws, bs):
    tn = 1024
    nl, nt = len(ws), 3 * D // tn
    own = lambda l: (lambda j: (0, jnp.clip(j - l * nt, 0, nt - 1)))
    return pl.pallas_call(
        functools.partial(_ada_kernel, nl=nl, nt=nt),
        grid=(nl * nt,),
        in_specs=[pl.BlockSpec((D, MOD_ROWS), lambda j: (0, 0))]
        + [pl.BlockSpec((D, tn), own(l)) for l in range(nl)]
        + [pl.BlockSpec((1, tn), own(l)) for l in range(nl)],
        out_specs=pl.BlockSpec((None, MOD_ROWS, tn), lambda j: (j // nt, 0, j % nt)),
        out_shape=jax.ShapeDtypeStruct((nl, MOD_ROWS, 3 * D), F32),
        compiler_params=_params("arbitrary"),
        name="adaln",
    )(ct, *ws, *[b.reshape(1, 3 * D) for b in bs])


def _in_proj_kernel(*refs, tm, tn, acts, from_x):
    j = pl.program_id(0)
    i = pl.program_id(1)
    w_ref, o_ref, wb_ref = refs[-3:]
    sub = tm // IN_SUB

    @pl.when(i == 0)
    def _():
        wb_ref[...] = w_ref[...].astype(BF16)

    if from_x:
        xp_ref, xs_ref, g_ref, mod_ref = refs[:4]
        m = mod_ref[pl.ds(_mod_row(i * tm), 1), :]

        def lhs(rows):
            y = _rms(jnp.where(i < N_PROMPT // tm, xp_ref[rows, :], xs_ref[rows, :]), g_ref[...])
            return (y * (1.0 + m[:, D:2 * D]) + m[:, :D]).astype(BF16)
    else:
        def lhs(rows):
            return refs[0][rows, :]

    sig0, silu0 = acts
    gated = (j + 1) * tn > min(sig0, silu0)

    @pl.when(jnp.logical_not(gated))
    def _():
        if from_x:
            for s in range(IN_SUB):
                rows = slice(s * sub, (s + 1) * sub)
                o_ref[rows, :] = _dot(lhs(rows), wb_ref[...]).astype(BF16)
        else:
            o_ref[...] = _dot(refs[0][...], wb_ref[...]).astype(BF16)

    @pl.when(gated)
    def _():
        for s in range(IN_SUB):
            rows = slice(s * sub, (s + 1) * sub)
            acc = _dot(lhs(rows), wb_ref[...])
            col = j * tn + lax.broadcasted_iota(jnp.int32, acc.shape, 1)
            sig = jax.nn.sigmoid(acc)
            out = jnp.where(col >= silu0, acc * sig, jnp.where(col >= sig0, sig, acc))
            o_ref[rows, :] = out.astype(BF16)


def _in_proj(h, w, tn, acts):
    tm = 1024
    n = w.shape[1]
    from_x = isinstance(h, tuple)
    if from_x:
        xp, xs, g, mod = h
        n_p = N_PROMPT // tm
        lhs_specs = [pl.BlockSpec((tm, D), lambda j, i: (jnp.minimum(i, n_p - 1), 0)),
                     pl.BlockSpec((tm, D), lambda j, i: (jnp.maximum(i - n_p, 0), 0)),
                     pl.BlockSpec((1, D), lambda j, i: (0, 0)),
                     pl.BlockSpec((MOD_ROWS, 3 * D), lambda j, i: (0, 0))]
        lhs_args = [xp, xs, g.reshape(1, D), mod]
    else:
        lhs_specs = [pl.BlockSpec((tm, D), lambda j, i: (i, 0))]
        lhs_args = [h]
    return pl.pallas_call(
        functools.partial(_in_proj_kernel, tm=tm, tn=tn, acts=acts, from_x=from_x),
        grid=(n // tn, N_TOK // tm),
        in_specs=lhs_specs + [pl.BlockSpec((D, tn), lambda j, i: (0, j))],
        out_specs=pl.BlockSpec((tm, tn), lambda j, i: (i, j)),
        out_shape=jax.ShapeDtypeStruct((N_TOK, n), BF16),
        scratch_shapes=[pltpu.VMEM((D, tn), BF16)],
        compiler_params=_params("arbitrary", "arbitrary"),
        name="in_proj",
    )(*lhs_args, w)


def _out_proj_kernel(ap_ref, as_ref, w_ref, mod_ref, g_ref, modn_ref, *rest, tm, tile0, final, nx):
    x_refs, rest = rest[:nx], rest[nx:]
    wb_ref = rest[-1]

    @pl.when(pl.program_id(0) == 0)
    def _():
        wb_ref[...] = w_ref[...].astype(BF16)

    i = tile0 + pl.program_id(0)
    r = _mod_row(i * tm)
    gate = mod_ref[pl.ds(r, 1), :][:, 2 * D:]
    from_prompt = i < N_PROMPT // tm
    for s in range(tm // OUT_SUB):
        rows = slice(s * OUT_SUB, (s + 1) * OUT_SUB)
        a = jnp.where(from_prompt, ap_ref[rows, :], as_ref[rows, :])
        x = x_refs[0][rows, :] if nx == 1 else jnp.where(from_prompt, x_refs[0][rows, :], x_refs[1][rows, :])
        xn = x + gate * _dot(a, wb_ref[...])
        y = _rms(xn, g_ref[...])
        if final:
            rest[0][rows, :] = y
        else:
            mn = modn_ref[pl.ds(r, 1), :]
            rest[0][rows, :] = xn
            rest[1][rows, :] = (y * (1.0 + mn[:, D:2 * D]) + mn[:, :D]).astype(BF16)


def _out_proj(a_p, a_s, w, x, mod, g_next, mod_next, final, rows=(0, N_TOK)):
    tm = 512
    kw = w.shape[0]
    tile0, nt = rows[0] // tm, rows[1] // tm
    row = pl.BlockSpec((tm, D), lambda i: (i, 0))
    if final:
        out_specs, out_shape = [row], [jax.ShapeDtypeStruct((rows[1], D), F32)]
    else:
        out_specs = [row, row]
        out_shape = [jax.ShapeDtypeStruct((rows[1], D), F32), jax.ShapeDtypeStruct((rows[1], D), BF16)]
    if isinstance(x, tuple):
        x_specs, xs = list(_two_source_specs(tm, D, tile0)), list(x)
    else:
        x_specs, xs = [pl.BlockSpec((tm, D), lambda i: (tile0 + i, 0))], [x]
    return pl.pallas_call(
        functools.partial(_out_proj_kernel, tm=tm, tile0=tile0, final=final, nx=len(xs)),
        grid=(nt,),
        in_specs=[*_two_source_specs(tm, kw, tile0),
                  pl.BlockSpec((kw, D), lambda i: (0, 0)),
                  pl.BlockSpec((MOD_ROWS, 3 * D), lambda i: (0, 0)),
                  pl.BlockSpec((1, D), lambda i: (0, 0)),
                  pl.BlockSpec((MOD_ROWS, 3 * D), lambda i: (0, 0)),
                  *x_specs],
        out_specs=out_specs,
        out_shape=out_shape,
        scratch_shapes=[pltpu.VMEM((kw, D), BF16)],
        compiler_params=_params("arbitrary"),
        name="out_proj",
    )(a_p, a_s, w, mod, g_next.reshape(1, D), mod_next, *xs)


def _ret_kernel(*refs, T, HB, rope, state_in, state_out):
    refs = list(refs)
    dec_ref, q_ref, k_ref, v_ref, z_ref, gn_ref = refs[:6]
    pos = 6
    if rope:
        cos_ref, sin_ref = refs[pos:pos + 2]
        pos += 2
    if state_in:
        s0_ref = refs[pos]
        pos += 1
    o_ref = refs[pos]
    pos += 1
    if state_out:
        sn_ref = refs[pos]
        pos += 1
    kv_ref, sf_ref, sb_ref, tab_ref, gblk_ref = refs[pos:pos + 5]
    pos += 5
    if rope:
        kr_ref = refs[pos]

    hg = pl.program_id(1)
    n = T // CHUNK
    L = CHUNK
    ii = lax.broadcasted_iota(jnp.int32, (L, L), 0).astype(F32)
    jj = lax.broadcasted_iota(jnp.int32, (L, L), 1).astype(F32)
    scale = RET_DK ** -0.5

    def chunk(c):
        return slice(c * L, (c + 1) * L)

    def rotate(x, c):
        return x * cos_ref[chunk(c), :] + pltpu.roll(x, RET_DK // 2, 1) * sin_ref[chunk(c), :]

    @pl.when(pl.program_id(0) == 0)
    def _():
        for hh in range(HB):
            head = hg * HB + hh
            lg_f = _log_sigmoid(jnp.full((1, RET_DV), dec_ref[0, head], F32))
            lg_b = _log_sigmoid(jnp.full((1, RET_DV), dec_ref[1, head], F32))
            lf, lb = lg_f[:, :L], lg_b[:, :L]
            tab_ref[head, 0] = jnp.exp(lf * (L - 1.0 - ii)) * scale
            tab_ref[head, 1] = jnp.exp(lb * ii) * scale
            tab_ref[head, 2] = jnp.exp(lf * (ii + 1.0))
            tab_ref[head, 3] = jnp.exp(lb * (L - ii))
            tab_ref[head, 4] = (jnp.where(ii >= jj, jnp.exp(lf * jnp.maximum(ii - jj, 0.0)), 0.0)
                                + jnp.where(jj >= ii, jnp.exp(lb * jnp.maximum(jj - ii, 0.0)), 0.0)) * scale
            gblk_ref[head, 0] = jnp.exp(lg_f * float(L))
            gblk_ref[head, 1] = jnp.exp(lg_b * float(L))

    for hh in range(HB):
        head = hg * HB + hh
        qs = slice(hh * RET_DK, (hh + 1) * RET_DK)
        vs = slice(hh * RET_DV, (hh + 1) * RET_DV)
        k_dec_f, k_dec_b, q_dec_f, q_dec_b, mask = (tab_ref[head, t] for t in range(5))
        g_f = gblk_ref[head, 0]
        g_b = gblk_ref[head, 1]

        for c in range(n):
            kc = k_ref[chunk(c), qs].astype(F32)
            if rope:
                kc = rotate(kc, c)
                kr_ref[chunk(c), qs] = kc.astype(BF16)
            kk = jnp.concatenate([kc * k_dec_f, kc * k_dec_b], axis=1).astype(BF16)
            kv_ref[hh, c] = _dot_tn(kk, v_ref[chunk(c), vs])
        S = s0_ref[0, hh] if state_in else None
        has_f = []
        for c in range(n):
            has_f.append(S is not None)
            if S is not None:
                sf_ref[hh, c] = S.astype(BF16)
            kvc = kv_ref[hh, c, :RET_DK, :]
            S = kvc if S is None else S * g_f + kvc
        if state_out:
            sn_ref[0, hh] = S
        S = s0_ref[1, hh] if state_in else None
        has_b = [False] * n
        for c in reversed(range(n)):
            has_b[c] = S is not None
            if S is not None:
                sb_ref[hh, c] = S.astype(BF16)
            kvc = kv_ref[hh, c, RET_DK:, :]
            S = kvc if S is None else S * g_b + kvc
        if state_out:
            sn_ref[1, hh] = S

        for c in range(n):
            qc = q_ref[chunk(c), qs]
            qf = qc.astype(F32)
            if rope:
                qf = rotate(qf, c)
                qc = qf.astype(BF16)
                kc = kr_ref[chunk(c), qs]
            else:
                kc = k_ref[chunk(c), qs]
            lhs = [(_dot_nt(qc, kc) * mask).astype(BF16)]
            rhs = [v_ref[chunk(c), vs]]
            if has_f[c]:
                lhs.append((qf * q_dec_f).astype(BF16))
                rhs.append(sf_ref[hh, c])
            if has_b[c]:
                lhs.append((qf * q_dec_b).astype(BF16))
                rhs.append(sb_ref[hh, c])
            o = _dot(jnp.concatenate(lhs, axis=1), jnp.concatenate(rhs, axis=0))
            zf = z_ref[chunk(c), vs].astype(F32)
            o_ref[chunk(c), vs] = (_group_norm(o, gn_ref[:, vs]) * zf).astype(BF16)


def _retention(proj, dec, gn, T, nb, row_blk0, rope=None, state=None, want_state=False):
    HB = RET_HB_PROMPT if T == SEQ else RET_HB_LATENT
    ng = RET_HEADS // HB
    n = T // CHUNK
    qw, vw = HB * RET_DK, HB * RET_DV
    in_specs = [pl.BlockSpec(memory_space=pltpu.SMEM),
                pl.BlockSpec((T, qw), lambda b, h: (row_blk0 + b, h)),
                pl.BlockSpec((T, qw), lambda b, h: (row_blk0 + b, ng + h)),
                pl.BlockSpec((T, vw), lambda b, h: (row_blk0 + b, ng + h)),
                pl.BlockSpec((T, vw), lambda b, h: (row_blk0 + b, 2 * ng + h)),
                pl.BlockSpec((1, vw), lambda b, h: (0, h))]
    args = [dec, proj, proj, proj, proj, gn.reshape(1, RET_WIDTH)]
    if rope is not None:
        in_specs += [pl.BlockSpec((T, RET_DK), lambda b, h: (0, 0))] * 2
        args += list(rope)
    if state is not None:
        in_specs.append(pl.BlockSpec((None, 2, HB, RET_DK, RET_DV), lambda b, h: (b, 0, h, 0, 0)))
        args.append(state)
    out_specs = [pl.BlockSpec((T, vw), lambda b, h: (b, h))]
    out_shape = [jax.ShapeDtypeStruct((nb * T, RET_WIDTH), BF16)]
    if want_state:
        out_specs.append(pl.BlockSpec((None, 2, HB, RET_DK, RET_DV), lambda b, h: (b, 0, h, 0, 0)))
        out_shape.append(jax.ShapeDtypeStruct((nb, 2, RET_HEADS, RET_DK, RET_DV), F32))
    scratch = [pltpu.VMEM((HB, n, 2 * RET_DK, RET_DV), F32),
               pltpu.VMEM((HB, n, RET_DK, RET_DV), BF16),
               pltpu.VMEM((HB, n, RET_DK, RET_DV), BF16),
               pltpu.VMEM((RET_HEADS, 5, CHUNK, CHUNK), F32),
               pltpu.VMEM((RET_HEADS, 2, 1, RET_DV), F32)]
    if rope is not None:
        scratch.append(pltpu.VMEM((T, qw), BF16))
    return pl.pallas_call(
        functools.partial(_ret_kernel, T=T, HB=HB, rope=rope is not None, state_in=state is not None,
                          state_out=want_state),
        grid=(nb, ng),
        in_specs=in_specs, out_specs=out_specs, out_shape=out_shape, scratch_shapes=scratch,
        compiler_params=_params("arbitrary", "arbitrary"),
        name="retention",
    )(*args)


def _ret_full_kernel(dec_ref, q_ref, k_ref, v_ref, z_ref, gn_ref, o_ref, sn_ref, mask_ref, kdec_ref, *, T):
    scale = RET_DK ** -0.5

    @pl.when(pl.program_id(0) == 0)
    def _():
        ii = lax.broadcasted_iota(jnp.int32, (T, T), 0).astype(F32)
        jj = lax.broadcasted_iota(jnp.int32, (T, T), 1).astype(F32)
        row = lax.broadcasted_iota(jnp.int32, (T, RET_DK), 0).astype(F32)
        for head in range(RET_HEADS):
            lf = _log_sigmoid(jnp.full((1, T), dec_ref[0, head], F32))
            lb = _log_sigmoid(jnp.full((1, T), dec_ref[1, head], F32))
            mask_ref[head] = (jnp.where(ii >= jj, jnp.exp(lf * jnp.maximum(ii - jj, 0.0)), 0.0)
                              + jnp.where(jj >= ii, jnp.exp(lb * jnp.maximum(jj - ii, 0.0)), 0.0)) * scale
            kdec_ref[head, 0] = jnp.exp(lf[:, :RET_DK] * (T - 1.0 - row)) * scale
            kdec_ref[head, 1] = jnp.exp(lb[:, :RET_DK] * row) * scale

    for head in range(RET_HEADS):
        qs = slice(head * RET_DK, (head + 1) * RET_DK)
        vs = slice(head * RET_DV, (head + 1) * RET_DV)
        k = k_ref[:, qs]
        v = v_ref[:, vs]
        a = (_dot_nt(q_ref[:, qs], k) * mask_ref[head]).astype(BF16)
        o = _dot(a, v)
        o_ref[:, vs] = (_group_norm(o, gn_ref[:, vs]) * z_ref[:, vs].astype(F32)).astype(BF16)
        kf = k.astype(F32)
        kk = jnp.concatenate([kf * kdec_ref[head, 0], kf * kdec_ref[head, 1]], axis=1).astype(BF16)
        kv = _dot_tn(kk, v)
        sn_ref[0, head] = kv[:RET_DK]
        sn_ref[1, head] = kv[RET_DK:]


def _retention_full(proj, dec, gn, T, nb):
    return pl.pallas_call(
        functools.partial(_ret_full_kernel, T=T),
        grid=(nb,),
        in_specs=[pl.BlockSpec(memory_space=pltpu.SMEM),
                  pl.BlockSpec((T, RET_QK), lambda b: (b, 0)),
                  pl.BlockSpec((T, RET_QK), lambda b: (b, 1)),
                  pl.BlockSpec((T, RET_WIDTH), lambda b: (b, 1)),
                  pl.BlockSpec((T, RET_WIDTH), lambda b: (b, 2)),
                  pl.BlockSpec((1, RET_WIDTH), lambda b: (0, 0))],
        out_specs=[pl.BlockSpec((T, RET_WIDTH), lambda b: (b, 0)),
                   pl.BlockSpec((None, 2, RET_HEADS, RET_DK, RET_DV), lambda b: (b, 0, 0, 0, 0))],
        out_shape=[jax.ShapeDtypeStruct((nb * T, RET_WIDTH), BF16),
                   jax.ShapeDtypeStruct((nb, 2, RET_HEADS, RET_DK, RET_DV), F32)],
        scratch_shapes=[pltpu.VMEM((RET_HEADS, T, T), F32), pltpu.VMEM((RET_HEADS, 2, T, RET_DK), F32)],
        compiler_params=_params("arbitrary"),
        name="retention_full",
    )(dec, proj, proj, proj, proj, gn.reshape(1, RET_WIDTH))


def _ml_in_kernel(h_ref, w_ref, cw_ref, cb_ref, bq_ref, bk_ref, bv_ref, wq_ref, wk_ref, wv_ref, bias_ref,
                  oz_ref, q_ref, k_ref, v_ref, xc_ref, g_ref, wb_ref, bd_ref, conv_ref):
    j = pl.program_id(0)
    i = pl.program_id(1)
    nsub = ML_TN // ML_TILE

    @pl.when(i == 0)
    def _():
        wb_ref[...] = w_ref[...].astype(BF16)

    @pl.when((i == 0) & (j < ML_XT))
    def _():
        row = lax.broadcasted_iota(jnp.int32, (ML_TILE, ML_TILE), 0)
        col = lax.broadcasted_iota(jnp.int32, (ML_TILE, ML_TILE), 1)
        shift = ML_BLOCK.bit_length() - 1
        same_block = jnp.right_shift(row, shift) == jnp.right_shift(col, shift)
        for which, b_ref in enumerate((bq_ref, bk_ref, bv_ref)):
            for t in range(nsub):
                w = b_ref[t * ML_TILE:(t + 1) * ML_TILE, :]
                rep = w[:, ML_BLOCK - 1:ML_BLOCK]
                for dd in range(ML_BLOCK - 1):
                    rep = jnp.where((col & (ML_BLOCK - 1)) == dd, w[:, dd:dd + 1], rep)
                bd_ref[which, t] = jnp.where(same_block, rep, 0.0).astype(BF16)

    @pl.when(j >= ML_XT)
    def _():
        sub = ML_ROWS // IN_SUB
        for s in range(IN_SUB):
            rows = slice(s * sub, (s + 1) * sub)
            acc = _dot(h_ref[rows, :], wb_ref[...])
            col = j * ML_TN + lax.broadcasted_iota(jnp.int32, acc.shape, 1)
            sig = jax.nn.sigmoid(acc)
            oz_ref[rows, :] = jnp.where(col >= 2 * ML_WIDTH, acc * sig, sig).astype(BF16)

    @pl.when(j < ML_XT)
    def _():
        acc = _dot(h_ref[...], wb_ref[...])
        T = jnp.where(i < N_PROMPT // ML_ROWS, SEQ, DEC_SEQ)
        pad = ML_CONV // 2
        taps = [tap for tap in range(ML_CONV) if tap != pad]
        sub8 = lax.broadcasted_iota(jnp.int32, (8, ML_TILE), 0)
        g = jnp.where(j == 0, 1.0, 0.0) * bias_ref[...]
        for t in range(nsub):
            cols = slice(t * ML_TILE, (t + 1) * ML_TILE)
            x = acc[:, cols]
            shifted = {tap: pltpu.roll(x, (pad - tap) % ML_ROWS, 0) for tap in taps}
            conv = cb_ref[:, cols] + x * cw_ref[pad:pad + 1, cols]
            for tap in taps:
                conv = conv + shifted[tap] * cw_ref[tap:tap + 1, cols]
            conv_ref[...] = conv
            for r0 in ML_EDGE_ROWS:
                rows = slice(r0, r0 + 8)
                pos = (r0 + sub8) & (T - 1)
                fixed = cb_ref[:, cols] + x[rows] * cw_ref[pad:pad + 1, cols]
                for tap in taps:
                    ok = (pos + (tap - pad) >= 0) & (pos + (tap - pad) < T)
                    fixed = fixed + jnp.where(ok, shifted[tap][rows], 0.0) * cw_ref[tap:tap + 1, cols]
                conv_ref[rows, :] = fixed
            xcb = _silu(conv_ref[...]).astype(BF16)
            qb = _dot(xcb, bd_ref[0, t]).astype(BF16)
            kb = _dot(xcb, bd_ref[1, t]).astype(BF16)
            vb = _dot(x.astype(BF16), bd_ref[2, t]).astype(BF16)
            q_ref[:, cols] = qb
            k_ref[:, cols] = kb
            v_ref[:, cols] = vb
            xc_ref[:, cols] = xcb
            g = g + (_dot(qb, wq_ref[cols, :].astype(BF16)) + _dot(kb, wk_ref[cols, :].astype(BF16))
                     + _dot(vb, wv_ref[cols, :].astype(BF16)))
        g_ref[...] = g


def _ml_in_proj(h, w, conv_w, conv_b, wq, wk, wv, wif, bif):
    ng = wif.shape[1]
    ni = N_TOK // ML_ROWS
    xt = ML_XT
    wq, wk, wv = (a.reshape(ML_WIDTH, ML_BLOCK) for a in (wq, wk, wv))
    xcol = lambda j, i: (0, jnp.minimum(j, xt - 1))
    xrow = lambda off: (lambda j, i: (off + jnp.minimum(j, xt - 1), 0))
    x_out = pl.BlockSpec((ML_ROWS, ML_TN), lambda j, i: (jnp.where(j < xt, i, ni - 1), jnp.minimum(j, xt - 1)))
    return pl.pallas_call(
        _ml_in_kernel,
        grid=(3 * ML_WIDTH // ML_TN, ni),
        in_specs=[pl.BlockSpec((ML_ROWS, D), lambda j, i: (i, 0)),
                  pl.BlockSpec((D, ML_TN), lambda j, i: (0, j)),
                  pl.BlockSpec((ML_CONV, ML_TN), xcol),
                  pl.BlockSpec((1, ML_TN), xcol),
                  pl.BlockSpec((ML_TN, ML_BLOCK), xrow(0)),
                  pl.BlockSpec((ML_TN, ML_BLOCK), xrow(0)),
                  pl.BlockSpec((ML_TN, ML_BLOCK), xrow(0)),
                  pl.BlockSpec((ML_TN, ng), xrow(0)),
                  pl.BlockSpec((ML_TN, ng), xrow(xt)),
                  pl.BlockSpec((ML_TN, ng), xrow(2 * xt)),
                  pl.BlockSpec((1, ng), lambda j, i: (0, 0))],
        out_specs=[pl.BlockSpec((ML_ROWS, ML_TN), lambda j, i: (jnp.where(j < xt, 0, i), jnp.maximum(j - xt, 0))),
                   x_out, x_out, x_out, x_out,
                   pl.BlockSpec((None, ML_ROWS, ng),
                                lambda j, i: (jnp.minimum(j, xt - 1), jnp.where(j < xt, i, ni - 1), 0))],
        out_shape=[jax.ShapeDtypeStruct((N_TOK, 2 * ML_WIDTH), BF16)]
        + [jax.ShapeDtypeStruct((N_TOK, ML_WIDTH), BF16)] * 4
        + [jax.ShapeDtypeStruct((xt, N_TOK, ng), F32)],
        scratch_shapes=[pltpu.VMEM((D, ML_TN), BF16), pltpu.VMEM((3, ML_TN // ML_TILE, ML_TILE, ML_TILE), BF16),
                        pltpu.VMEM((ML_ROWS, ML_TILE), F32)],
        compiler_params=_params("arbitrary", "arbitrary"),
        name="mlstm_in_proj",
    )(h, w, conv_w, conv_b.reshape(1, ML_WIDTH), wq, wk, wv, wif, wif, wif, bif.reshape(1, ng))


def _dot_split(lhs_bf16, rhs):
    r1 = rhs.astype(BF16)
    e1 = rhs - r1.astype(F32)
    r2 = e1.astype(BF16)
    r3 = (e1 - r2.astype(F32)).astype(BF16)
    return _dot(lhs_bf16, r1) + _dot(lhs_bf16, r2) + _dot(lhs_bf16, r3)


def _ml_scan_kernel(*refs, T, HB, state_in, state_out):
    refs = list(refs)
    pos = 0
    m0_ref = c0_ref = n0_ref = cn_ref = nn_ref = mn_ref = None
    if state_in:
        m0_ref = refs[0]
        pos = 1
    q_ref, k_ref, v_ref, g_ref, op_ref, z_ref, xc_ref, gn_ref, sk_ref = refs[pos:pos + 9]
    pos += 9
    if state_in:
        c0_ref, n0_ref = refs[pos:pos + 2]
        pos += 2
    o_ref = refs[pos]
    pos += 1
    if state_out:
        cn_ref, nn_ref, mn_ref = refs[pos:pos + 3]
        pos += 3
    acc_ref, s_ref = refs[pos:pos + 2]

    for hh in range(HB):
        cols = slice(hh * ML_HD, (hh + 1) * ML_HD)

        def head_cols(ref):
            return ref.at[:, cols]

        def head_state(ref):
            return None if ref is None else ref.at[:, hh]

        _ml_scan_head(pl.program_id(0), pl.program_id(1) * HB + hh, m0_ref,
                      head_cols(q_ref), head_cols(k_ref), head_cols(v_ref), g_ref, head_cols(op_ref),
                      head_cols(z_ref), head_cols(xc_ref), head_cols(gn_ref), head_cols(sk_ref),
                      head_state(c0_ref), head_state(n0_ref), head_cols(o_ref),
                      head_state(cn_ref), head_state(nn_ref), head_state(mn_ref), acc_ref.at[hh], s_ref.at[hh],
                      T=T, state_in=state_in, state_out=state_out)


def _ml_scan_head(b, h, m0_ref, q_ref, k_ref, v_ref, g_ref, op_ref, z_ref, xc_ref, gn_ref, sk_ref,
                  c0_ref, n0_ref, o_ref, cn_ref, nn_ref, mn_ref, acc_ref, s_ref, *, T, state_in, state_out):
    n = T // CHUNK
    L = CHUNK
    scale = ML_HD ** -0.5

    def chunk(c):
        return slice(c * L, (c + 1) * L)

    ii = lax.broadcasted_iota(jnp.int32, (L, L), 0)
    jj = lax.broadcasted_iota(jnp.int32, (L, L), 1)
    neg = jnp.full((L, L), -jnp.inf, F32)
    zero = jnp.zeros((L, L), F32)
    tri = [jnp.where(jj <= ii, 1.0, 0.0).astype(BF16), jnp.where(jj >= ii, 1.0, 0.0).astype(BF16)]
    bias = [jnp.where(ii >= jj, zero, neg), jnp.where(ii <= jj, zero, neg)]

    g = g_ref[0]
    for part in range(1, ML_XT):
        g = g + g_ref[part]
    col = lax.broadcasted_iota(jnp.int32, g.shape, 1)
    g = jnp.where((col & (2 * ML_HEADS - 1)) >= ML_HEADS, _log_sigmoid(g), g)

    def gate_column(j):
        return jnp.sum(jnp.where(col == j, g, 0.0), axis=1, keepdims=True)


    for i in range(n):
        for j in range(n):
            s_ref[i, j] = _dot_nt(q_ref[chunk(i), :], k_ref[chunk(j), :])

    G = []
    for d in range(2):
        order = list(range(n)) if d == 0 else list(reversed(range(n)))
        gi_all = gate_column(d * 2 * ML_HEADS + h)
        gf_all = gate_column(d * 2 * ML_HEADS + ML_HEADS + h)
        m0 = jnp.full((1, 1), m0_ref[(b * 2 + d) * ML_HEADS + h], F32) if state_in else jnp.zeros((1, 1), F32)
        c_col, c_row, b_col, m_row = [None] * n, [None] * n, [None] * n, [None] * n
        offset = jnp.zeros((1, 1), F32)
        carry = m0
        for c in order:
            gi = gi_all[chunk(c)]
            gf = gf_all[chunk(c)]
            cum = _dot_split(tri[d], jnp.broadcast_to(gf, (L, L))) + offset
            offset = offset + jnp.sum(gf, axis=0, keepdims=True)
            cc = gi - cum
            c_col[c] = cc[:, :1]
            c_row[c] = jnp.transpose(cc)
            b_col[c] = cum[:, :1]
            m_row[c] = jnp.maximum(jnp.max(c_row[c] + bias[d], axis=1, keepdims=True), carry)
            carry = jnp.maximum(carry, jnp.max(c_col[c], axis=0, keepdims=True))
        G.append(dict(order=order, c_col=c_col, c_row=c_row, b_col=b_col, m_row=m_row, m0=m0,
                      m_end=carry, b_end=offset))

    done = set()
    for i in range(n):
        for d in range(2):
            e = G[d]
            js = list(range(0, i + 1)) if d == 0 else list(range(i, n))
            m_i = e["m_row"][i]
            parts = []
            tot = None
            for j in js:
                z = e["c_row"][j] - m_i
                if j == i:
                    z = z + bias[d]
                sw = s_ref[i, j] * (jnp.exp(z) * scale)
                tot = sw if tot is None else tot + sw
                parts.append(sw.astype(BF16))
            num = _dot(jnp.concatenate(parts, axis=1), v_ref[js[0] * L:(js[-1] + 1) * L, :])
            den = jnp.sum(tot, axis=1, keepdims=True)
            if state_in:
                qc = q_ref[chunk(i), :]
                w0 = jnp.exp(e["m0"] - m_i)
                num = num + _dot((qc.astype(F32) * w0).astype(BF16), c0_ref[d].astype(BF16))
                n0 = jnp.broadcast_to(n0_ref[d], (8, ML_HD)).astype(BF16)
                den = den + w0 * _dot_nt(qc, n0)[:, :1]
            hb = num / jnp.maximum(jnp.abs(den), jnp.exp(-(e["b_col"][i] + m_i)))
            if i not in done:
                done.add(i)
                acc_ref[chunk(i), :] = hb
            else:
                cell = _group_norm((acc_ref[chunk(i), :] + hb) * op_ref[chunk(i), :].astype(F32), gn_ref[...])
                mixed = cell + sk_ref[...] * xc_ref[chunk(i), :].astype(F32)
                o_ref[chunk(i), :] = (mixed * z_ref[chunk(i), :].astype(F32)).astype(BF16)

    if state_out:
        ones = jnp.ones((8, T), BF16)
        for d in range(2):
            e = G[d]
            kw = jnp.concatenate(
                [(k_ref[chunk(c), :].astype(F32) * (jnp.exp(e["c_col"][c] - e["m_end"]) * scale)).astype(BF16)
                 for c in range(n)], axis=0)
            C = _dot_tn(kw, v_ref[...])
            nvec = _dot(ones, kw)[:1]
            if state_in:
                w_end = jnp.exp(e["m0"] - e["m_end"])
                C = C + w_end * c0_ref[d]
                nvec = nvec + w_end * n0_ref[d]
            cn_ref[d] = C
            nn_ref[d] = nvec
            mn_ref[d] = jnp.broadcast_to(e["b_end"] + e["m_end"], mn_ref.shape[1:])


def _ml_scan(oz, q, k, v, xc, g, gn, skip, T, nb, row_blk0, state=None, want_state=False):
    H = ML_HEADS
    HB = ML_HB_PROMPT if T == SEQ else ML_HB_LATENT
    ng = H // HB
    hw = HB * ML_HD
    n = T // CHUNK
    tok = lambda b, h: (row_blk0 + b, h)
    head_vec = pl.BlockSpec((1, hw), lambda b, h: (0, h))
    both = lambda *tail: pl.BlockSpec((None, 2, HB, *tail), lambda b, h: (b, 0, h, 0, 0))
    in_specs, args = [], []
    if state is not None:
        c0, n0, m0 = state
        in_specs.append(pl.BlockSpec(memory_space=pltpu.SMEM))
        args.append(m0.reshape(nb * 2 * H))
    in_specs += [pl.BlockSpec((T, hw), tok)] * 3
    in_specs += [pl.BlockSpec((ML_XT, T, g.shape[2]), lambda b, h: (0, row_blk0 + b, 0)),
                 pl.BlockSpec((T, hw), tok),
                 pl.BlockSpec((T, hw), lambda b, h: (row_blk0 + b, ng + h)),
                 pl.BlockSpec((T, hw), tok), head_vec, head_vec]
    args += [q, k, v, g, oz, oz, xc, gn.reshape(1, ML_WIDTH), skip.reshape(1, ML_WIDTH)]
    if state is not None:
        in_specs += [both(ML_HD, ML_HD), both(1, ML_HD)]
        args += [c0, n0.reshape(nb, 2, H, 1, ML_HD)]
    out_specs = [pl.BlockSpec((T, hw), lambda b, h: (b, h))]
    out_shape = [jax.ShapeDtypeStruct((nb * T, ML_WIDTH), BF16)]
    if want_state:
        out_specs += [both(ML_HD, ML_HD), both(1, ML_HD), both(1, 128)]
        out_shape += [jax.ShapeDtypeStruct((nb, 2, H, ML_HD, ML_HD), F32),
                      jax.ShapeDtypeStruct((nb, 2, H, 1, ML_HD), F32),
                      jax.ShapeDtypeStruct((nb, 2, H, 1, 128), F32)]
    scratch = [pltpu.VMEM((HB, T, ML_HD), F32), pltpu.VMEM((HB, n, n, CHUNK, CHUNK), F32)]
    return pl.pallas_call(
        functools.partial(_ml_scan_kernel, T=T, HB=HB, state_in=state is not None, state_out=want_state),
        grid=(nb, ng),
        in_specs=in_specs, out_specs=out_specs, out_shape=out_shape, scratch_shapes=scratch,
        compiler_params=_params("arbitrary", "arbitrary"),
        name="mlstm_scan",
    )(*args)


def _att_rope(x, cos, sin):
    lane = lax.broadcasted_iota(jnp.int32, x.shape, 1)
    half = ATT_HD // 2
    rot = jnp.where((lane & (ATT_HD - 1)) < half, pltpu.roll(x, 128 - half, 1), pltpu.roll(x, half, 1))
    return x * cos + rot * sin


def _att_heads(qms, sinks, key_sets):
    scores = [[_dot_nt(k_both, qm) if bias is None else _dot_nt(k_both, qm) + bias
               for k_both, _, bias in key_sets] for qm in qms]
    ms = []
    for sink, per_set in zip(sinks, scores):
        m = sink
        for s in per_set:
            m = jnp.maximum(m, jnp.max(s, axis=0, keepdims=True))
        ms.append(m)
    probs = [[jnp.exp(s - m) for s in per_set] for m, per_set in zip(ms, scores)]
    outs = []
    for sink, m, per_set in zip(sinks, ms, probs):
        den = jnp.exp(sink - m)
        out = None
        for p, (_, v_t, _) in zip(per_set, key_sets):
            den = den + jnp.sum(p, axis=0, keepdims=True)
            o = _dot(v_t, p.astype(BF16))
            out = o if out is None else out + o
        outs.append(out * (1.0 / den))
    return outs


def _att_prepare_kv(k, v, kb_ref, vt_ref, kv, nblk):
    lane = lax.broadcasted_iota(jnp.int32, k.shape, 1)
    native = lane < ATT_HD if kv % 2 == 0 else lane >= ATT_HD
    kb_ref[kv] = jnp.where(native, k, pltpu.roll(k, ATT_HD, 1)).astype(BF16)
    r0 = (kv % 2) * ATT_HD
    blk = k.shape[0] // nblk
    for c in range(nblk):
        vt_ref[kv, c] = jnp.transpose(v[c * blk:(c + 1) * blk, :])[r0:r0 + ATT_HD, :].astype(BF16)


def _att_group(q_ref, z0_ref, z1_ref, o_ref, sink_ref, rows, kv, key_sets):
    lo = lax.broadcasted_iota(jnp.int32, (CHUNK, 128), 1) < ATT_HD
    zero = jnp.zeros((CHUNK, 128), BF16)
    qms, sinks = [], []
    for g in range(ATT_GROUP):
        col = (2 * kv + g // 2) * 128
        q = q_ref[rows, col:col + 128]
        qms.append(jnp.where(lo, q, zero) if g % 2 == 0 else jnp.where(lo, zero, q))
        sinks.append(jnp.full((1, CHUNK), sink_ref[kv * ATT_GROUP + g], F32))
    heads = _att_heads(qms, sinks, key_sets)
    for p in range(2):
        col = (2 * kv + p) * 128
        out = jnp.transpose(jnp.concatenate(heads[2 * p:2 * p + 2], axis=0))
        z_ref = z0_ref if col < ATT_WIDTH // 2 else z1_ref
        zc = col % (ATT_WIDTH // 2)
        zf = z_ref[rows, zc:zc + 128].astype(F32)
        o_ref[rows, col:col + 128] = (out * zf).astype(BF16)


def _att_ctx_kernel(sink_ref, q_ref, k_ref, v_ref, z0_ref, z1_ref, o_ref, kt_ref, vto_ref, qs_ref, kb_ref, vt_ref):
    T = q_ref.shape[0]
    qs_ref[...] = q_ref[...] * (ATT_HD ** -0.5)
    for kv in range(ATT_KV):
        slab = (kv // 2) * 128
        k = k_ref[:, slab:slab + 128].astype(F32)
        v = v_ref[:, slab:slab + 128].astype(F32)
        _att_prepare_kv(k, v, kb_ref, vt_ref, kv, 1)
        r0 = (kv % 2) * ATT_HD
        kt_ref[kv] = jnp.transpose(k)[r0:r0 + ATT_HD, :]
        vto_ref[kv] = jnp.transpose(v)[r0:r0 + ATT_HD, :]
    for c in range(T // CHUNK):
        rows = slice(c * CHUNK, (c + 1) * CHUNK)
        for kv in range(ATT_KV):
            _att_group(qs_ref, z0_ref, z1_ref, o_ref, sink_ref, rows, kv, [(kb_ref[kv], vt_ref[kv, 0], None)])


def _att_ctx(proj, sink):
    T = SEQ
    half = ATT_WIDTH // 2
    return pl.pallas_call(
        _att_ctx_kernel,
        grid=(BATCH,),
        in_specs=[pl.BlockSpec(memory_space=pltpu.SMEM),
                  pl.BlockSpec((T, ATT_WIDTH), lambda b: (b, 0)),
                  pl.BlockSpec((T, ATT_KVW), lambda b: (b, ATT_WIDTH // ATT_KVW)),
                  pl.BlockSpec((T, ATT_KVW), lambda b: (b, ATT_WIDTH // ATT_KVW + 1)),
                  pl.BlockSpec((T, half), lambda b: (b, (ATT_WIDTH + 2 * ATT_KVW) // half)),
                  pl.BlockSpec((T, half), lambda b: (b, (ATT_WIDTH + 2 * ATT_KVW) // half + 1))],
        out_specs=[pl.BlockSpec((T, ATT_WIDTH), lambda b: (b, 0)),
                   pl.BlockSpec((None, ATT_KV, ATT_HD, T), lambda b: (b, 0, 0, 0)),
                   pl.BlockSpec((None, ATT_KV, ATT_HD, T), lambda b: (b, 0, 0, 0))],
        out_shape=[jax.ShapeDtypeStruct((N_PROMPT, ATT_WIDTH), BF16),
                   jax.ShapeDtypeStruct((BATCH, ATT_KV, ATT_HD, T), F32),
                   jax.ShapeDtypeStruct((BATCH, ATT_KV, ATT_HD, T), F32)],
        scratch_shapes=[pltpu.VMEM((T, ATT_WIDTH), BF16), pltpu.VMEM((ATT_KV, T, 128), BF16),
                        pltpu.VMEM((ATT_KV, 1, ATT_HD, T), BF16)],
        compiler_params=_params("arbitrary"),
        name="att_ctx",
    )(sink, proj, proj, proj, proj, proj)


def _att_win_kernel(sink_ref, q_ref, k_ref, v_ref, z0_ref, z1_ref, kc_ref, vc_ref, cos_ref, sin_ref,
                    o_ref, qs_ref, kb_ref, vt_ref, kcb_ref, vct_ref):
    T = q_ref.shape[0]
    L = CHUNK
    n = T // L
    cos = cos_ref[...]
    sin = sin_ref[...]
    for c0 in range(0, ATT_WIDTH, 128):
        q = _att_rope(q_ref[:, c0:c0 + 128].astype(F32), cos, sin)
        qs_ref[:, c0:c0 + 128] = (q * (ATT_HD ** -0.5)).astype(BF16)
    for kv in range(ATT_KV):
        slab = (kv // 2) * 128
        _att_prepare_kv(_att_rope(k_ref[:, slab:slab + 128].astype(F32), cos, sin),
                        v_ref[:, slab:slab + 128].astype(F32), kb_ref, vt_ref, kv, n)
        _att_prepare_kv(kc_ref[:, slab:slab + 128], vc_ref[:, slab:slab + 128], kcb_ref, vct_ref, kv, 1)

    jj = lax.broadcasted_iota(jnp.int32, (L, L), 0)
    ii = lax.broadcasted_iota(jnp.int32, (L, L), 1)
    neg = jnp.full((L, L), -jnp.inf, F32)
    bias_prev = jnp.where(jj >= ii, 0.0, neg)
    bias_next = jnp.where(jj <= ii, 0.0, neg)

    def body(c, carry):
        rows = pl.ds(pl.multiple_of(c * L, L), L)
        c_prev = jnp.maximum(c - 1, 0)
        c_next = jnp.minimum(c + 1, n - 1)
        b_prev = jnp.where(c > 0, bias_prev, neg)
        b_next = jnp.where(c < n - 1, bias_next, neg)
        for kv in range(ATT_KV):
            def keys(cb):
                return kb_ref[kv, pl.ds(pl.multiple_of(cb * L, L), L), :]
            key_sets = [(keys(c_prev), vt_ref[kv, c_prev], b_prev),
                        (keys(c), vt_ref[kv, c], None),
                        (keys(c_next), vt_ref[kv, c_next], b_next),
                        (kcb_ref[kv], vct_ref[kv, 0], None)]
            _att_group(qs_ref, z0_ref, z1_ref, o_ref, sink_ref, rows, kv, key_sets)
        return carry

    lax.fori_loop(0, n, body, 0)


def _att_win(proj, sink, k_ctx, v_ctx, cos, sin):
    T = DEC_SEQ
    half = ATT_WIDTH // 2
    rb = N_PROMPT // T
    return pl.pallas_call(
        _att_win_kernel,
        grid=(DEC_BATCH,),
        in_specs=[pl.BlockSpec(memory_space=pltpu.SMEM),
                  pl.BlockSpec((T, ATT_WIDTH), lambda b: (rb + b, 0)),
                  pl.BlockSpec((T, ATT_KVW), lambda b: (rb + b, ATT_WIDTH // ATT_KVW)),
                  pl.BlockSpec((T, ATT_KVW), lambda b: (rb + b, ATT_WIDTH // ATT_KVW + 1)),
                  pl.BlockSpec((T, half), lambda b: (rb + b, (ATT_WIDTH + 2 * ATT_KVW) // half)),
                  pl.BlockSpec((T, half), lambda b: (rb + b, (ATT_WIDTH + 2 * ATT_KVW) // half + 1)),
                  pl.BlockSpec((None, PAST_LEN, ATT_KVW), lambda b: (b, 0, 0)),
                  pl.BlockSpec((None, PAST_LEN, ATT_KVW), lambda b: (b, 0, 0)),
                  pl.BlockSpec((T, 128), lambda b: (0, 0)),
                  pl.BlockSpec((T, 128), lambda b: (0, 0))],
        out_specs=pl.BlockSpec((T, ATT_WIDTH), lambda b: (b, 0)),
        out_shape=jax.ShapeDtypeStruct((N_LATENT, ATT_WIDTH), BF16),
        scratch_shapes=[pltpu.VMEM((T, ATT_WIDTH), BF16),
                        pltpu.VMEM((ATT_KV, T, 128), BF16),
                        pltpu.VMEM((ATT_KV, T // CHUNK, ATT_HD, CHUNK), BF16),
                        pltpu.VMEM((ATT_KV, PAST_LEN, 128), BF16),
                        pltpu.VMEM((ATT_KV, 1, ATT_HD, PAST_LEN), BF16)],
        compiler_params=_params("arbitrary"),
        name="att_win",
    )(sink, proj, proj, proj, proj, proj, k_ctx, v_ctx, cos, sin)


def _rope_tables(T, hd, reps):
    rows = T // GRID_W
    row = np.repeat(np.arange(rows, dtype=np.float64), GRID_W)
    col = np.tile(np.arange(GRID_W, dtype=np.float64), rows)
    nf = hd // 4
    inv = ROPE_BASE ** (-np.arange(nf, dtype=np.float64) / nf)
    ang = np.concatenate([row[:, None] * inv[None, :], col[:, None] * inv[None, :]], axis=-1)
    cos, sin = np.cos(ang), np.sin(ang)
    return (jnp.asarray(np.tile(np.concatenate([cos, cos], axis=-1), (1, reps)), F32),
            jnp.asarray(np.tile(np.concatenate([-sin, sin], axis=-1), (1, reps)), F32))


def _retention_layer(h, w_in, decay_f, decay_b, gn, state, rope):
    z0 = 2 * RET_QK + RET_WIDTH
    proj = _in_proj(h, w_in, 2048, (z0, z0))
    dec = jnp.stack([decay_f, decay_b]).astype(F32)
    a_p, new_state = _retention_full(proj, dec, gn, SEQ, BATCH)
    (a_s,) = _retention(proj, dec, gn, DEC_SEQ, DEC_BATCH, N_PROMPT // DEC_SEQ, rope=rope, state=state)
    return a_p, a_s, new_state


def kernel(x_prompt, x_sample, c, c_ctx, state_l0_ret, state_l1_C, state_l1_n, state_l1_m, cache_l2_k, cache_l2_v, state_l3_ret, norm_l0, ada_w_l0, ada_b_l0, w_in_l0, w_out_l0, ret_decay_f_l0, ret_decay_b_l0, ret_gn_l0, norm_l1, ada_w_l1, ada_b_l1, w_in_l1, w_out_l1, conv_w_l1, conv_b_l1, wq_l1, wk_l1, wv_l1, wif_f_l1, bif_f_l1, wif_b_l1, bif_b_l1, gn_l1, skip_l1, norm_l2, ada_w_l2, ada_b_l2, w_in_l2, w_out_l2, sink_l2, norm_l3, ada_w_l3, ada_b_l3, w_in_l3, w_out_l3, ret_decay_f_l3, ret_decay_b_l3, ret_gn_l3, final_norm):
    ct = jnp.concatenate([c_ctx[:, None], c.T, jnp.zeros((D, MOD_ROWS - 1 - DEC_BATCH), F32)], axis=1)
    mod_all = _adaln(ct, (ada_w_l0, ada_w_l1, ada_w_l2, ada_w_l3), (ada_b_l0, ada_b_l1, ada_b_l2, ada_b_l3))
    mods = [mod_all[l] for l in range(mod_all.shape[0])]
    rope_ret = _rope_tables(DEC_SEQ, RET_DK, 1)
    rope_att = _rope_tables(DEC_SEQ, ATT_HD, 2)

    x0 = (x_prompt.reshape(N_PROMPT, D), x_sample.reshape(N_LATENT, D))

    a_p, a_s, new_l0_ret = _retention_layer((*x0, norm_l0, mods[0]), w_in_l0, ret_decay_f_l0, ret_decay_b_l0,
                                            ret_gn_l0, state_l0_ret, rope_ret)
    x, h = _out_proj(a_p, a_s, w_out_l0, x0, mods[0], norm_l1, mods[1], False)

    wif = jnp.concatenate([wif_f_l1, wif_b_l1], axis=1)
    bif = jnp.concatenate([bif_f_l1, bif_b_l1])
    oz, q, k, v, xc, g = _ml_in_proj(h, w_in_l1, conv_w_l1, conv_b_l1, wq_l1, wk_l1, wv_l1, wif, bif)
    a_p, new_l1_C, n_new, m_new = _ml_scan(oz, q, k, v, xc, g, gn_l1, skip_l1, SEQ, BATCH, 0, want_state=True)
    (a_s,) = _ml_scan(oz, q, k, v, xc, g, gn_l1, skip_l1, DEC_SEQ, DEC_BATCH, N_PROMPT // DEC_SEQ,
                      state=(state_l1_C, state_l1_n, state_l1_m))
    new_l1_n = n_new.reshape(BATCH, 2, ML_HEADS, ML_HD)
    new_l1_m = m_new[:, :, :, 0, 0]
    x, h = _out_proj(a_p, a_s, w_out_l1, x, mods[1], norm_l2, mods[2], False)

    proj = _in_proj(h, w_in_l2, 1280, (ATT_WIDTH + 2 * ATT_KVW,) * 2)
    a_p, k_t, v_t = _att_ctx(proj, sink_l2)
    new_l2_k = jnp.transpose(k_t, (0, 3, 1, 2))
    new_l2_v = jnp.transpose(v_t, (0, 3, 1, 2))
    a_s = _att_win(proj, sink_l2, cache_l2_k.reshape(DEC_BATCH, PAST_LEN, ATT_KVW),
                   cache_l2_v.reshape(DEC_BATCH, PAST_LEN, ATT_KVW), rope_att[0], rope_att[1])
    x, h = _out_proj(a_p, a_s, w_out_l2, x, mods[2], norm_l3, mods[3], False)

    a_p, a_s, new_l3_ret = _retention_layer(h, w_in_l3, ret_decay_f_l3, ret_decay_b_l3, ret_gn_l3, state_l3_ret,
                                            rope_ret)
    (y_p,) = _out_proj(a_p, a_s, w_out_l3, x, mods[3], final_norm, mods[3], True, rows=(0, N_PROMPT))
    (y_s,) = _out_proj(a_p, a_s, w_out_l3, x, mods[3], final_norm, mods[3], True, rows=(N_PROMPT, N_LATENT))

    y_prompt = y_p.reshape(BATCH, SEQ, D)
    y_sample = y_s.reshape(DEC_BATCH, DEC_SEQ, D)
    return (y_prompt, y_sample, new_l0_ret, new_l1_C, new_l1_n, new_l1_m, new_l2_k, new_l2_v, new_l3_ret)
```

```python
import functools

import jax
import jax.numpy as jnp
import numpy as np
from jax import lax
from jax.experimental import pallas as pl
from jax.experimental.pallas import tpu as pltpu

F32 = jnp.float32
BF16 = jnp.bfloat16

D = 1024
BATCH = 16
SEQ = 256
DEC_BATCH = 2
DEC_SEQ = 1024
PAST_LEN = 512
GRID_W = 64
CHUNK = 128
EPS = 1e-6
ROPE_BASE = 10000.0

N_PROMPT = BATCH * SEQ
N_LATENT = DEC_BATCH * DEC_SEQ
N_TOK = N_PROMPT + N_LATENT
MOD_ROWS = 8

RET_HEADS = 8
RET_DK = 128
RET_DV = 256
RET_QK = RET_HEADS * RET_DK
RET_WIDTH = RET_HEADS * RET_DV
RET_HB_LATENT = 4
ML_HB_PROMPT = 1
ML_HB_LATENT = 1

ML_HEADS = 4
ML_WIDTH = 2 * D
ML_HD = ML_WIDTH // ML_HEADS
ML_BLOCK = 4
ML_CONV = 5
ML_TILE = 256
ML_ROWS = 1024
ML_TN = 1024
ML_XT = ML_WIDTH // ML_TN
ML_EDGE_ROWS = sorted({r for e in range(0, ML_ROWS + 1, SEQ) for r in (e - 8, e) if 0 <= r < ML_ROWS})

ATT_HEADS = 16
ATT_KV = 4
ATT_HD = 64
ATT_GROUP = ATT_HEADS // ATT_KV
ATT_WIDTH = ATT_HEADS * ATT_HD
ATT_KVW = ATT_KV * ATT_HD
WINDOW = 128

OUT_SUB = 256
IN_SUB = 4

VMEM_LIMIT = 56 * 1024 * 1024


def _params(*sem):
    return pltpu.CompilerParams(dimension_semantics=sem, vmem_limit_bytes=VMEM_LIMIT)


def _mod_row(row0):
    return jnp.maximum((row0 - N_PROMPT) // DEC_SEQ + 1, 0)


def _dot(a, b):
    return jnp.dot(a, b, preferred_element_type=F32)


def _dot_nt(a, b):
    return lax.dot_general(a, b, (((1,), (1,)), ((), ())), preferred_element_type=F32)


def _dot_tn(a, b):
    return lax.dot_general(a, b, (((0,), (0,)), ((), ())), preferred_element_type=F32)


def _silu(x):
    return x * jax.nn.sigmoid(x)


def _log_sigmoid(x):
    return jnp.minimum(x, 0.0) - jnp.log1p(jnp.exp(-jnp.abs(x)))


def _rms(x, g):
    return x * lax.rsqrt(jnp.mean(x * x, axis=-1, keepdims=True) + EPS) * g


def _group_norm(x, g):
    xc = x - jnp.mean(x, axis=-1, keepdims=True)
    return xc * lax.rsqrt(jnp.mean(xc * xc, axis=-1, keepdims=True) + EPS) * g


def _two_source_specs(tm, width, tile0=0):
    n_p = N_PROMPT // tm
    return (pl.BlockSpec((tm, width), lambda i: (jnp.minimum(tile0 + i, n_p - 1), 0)),
            pl.BlockSpec((tm, width), lambda i: (jnp.maximum(tile0 + i - n_p, 0), 0)))


def _ada_kernel(ct_ref, *refs, nl, nt):
    o_ref = refs[-1]
    layer = pl.program_id(0) // nt
    s = _silu(ct_ref[...])
    for l in range(nl):
        @pl.when(layer == l)
        def _(w_ref=refs[l], b_ref=refs[nl + l]):
            w = w_ref[...]
            rows = [jnp.sum(w * s[:, r:r + 1], axis=0, keepdims=True) for r in range(1 + DEC_BATCH)]
            rows.append(jnp.zeros((MOD_ROWS - len(rows), w.shape[1]), F32))
            o_ref[...] = jnp.concatenate(rows, axis=0) + b_ref[...]


def _adaln(ct, ws, bs):
    tn = 1024
    nl, nt = len(ws), 3 * D // tn
    own = lambda l: (lambda j: (0, jnp.clip(j - l * nt, 0, nt - 1)))
    return pl.pallas_call(
        functools.partial(_ada_kernel, nl=nl, nt=nt),
        grid=(nl * nt,),
        in_specs=[pl.BlockSpec((D, MOD_ROWS), lambda j: (0, 0))]
        + [pl.BlockSpec((D, tn), own(l)) for l in range(nl)]
        + [pl.BlockSpec((1, tn), own(l)) for l in range(nl)],
        out_specs=pl.BlockSpec((None, MOD_ROWS, tn), lambda j: (j // nt, 0, j % nt)),
        out_shape=jax.ShapeDtypeStruct((nl, MOD_ROWS, 3 * D), F32),
        compiler_params=_params("arbitrary"),
        name="adaln",
    )(ct, *ws, *[b.reshape(1, 3 * D) for b in bs])


def _in_proj_kernel(*refs, tm, tn, acts, from_x):
    j = pl.program_id(0)
    i = pl.program_id(1)
    w_ref, o_ref, wb_ref = refs[-3:]
    sub = tm // IN_SUB

    @pl.when(i == 0)
    def _():
        wb_ref[...] = w_ref[...].astype(BF16)

    if from_x:
        xp_ref, xs_ref, g_ref, mod_ref = refs[:4]
        m = mod_ref[pl.ds(_mod_row(i * tm), 1), :]

        def lhs(rows):
            y = _rms(jnp.where(i < N_PROMPT // tm, xp_ref[rows, :], xs_ref[rows, :]), g_ref[...])
            return (y * (1.0 + m[:, D:2 * D]) + m[:, :D]).astype(BF16)
    else:
        def lhs(rows):
            return refs[0][rows, :]

    sig0, silu0 = acts
    gated = (j + 1) * tn > min(sig0, silu0)

    @pl.when(jnp.logical_not(gated))
    def _():
        if from_x:
            for s in range(IN_SUB):
                rows = slice(s * sub, (s + 1) * sub)
                o_ref[rows, :] = _dot(lhs(rows), wb_ref[...]).astype(BF16)
        else:
            o_ref[...] = _dot(refs[0][...], wb_ref[...]).astype(BF16)

    @pl.when(gated)
    def _():
        for s in range(IN_SUB):
            rows = slice(s * sub, (s + 1) * sub)
            acc = _dot(lhs(rows), wb_ref[...])
            col = j * tn + lax.broadcasted_iota(jnp.int32, acc.shape, 1)
            sig = jax.nn.sigmoid(acc)
            out = jnp.where(col >= silu0, acc * sig, jnp.where(col >= sig0, sig, acc))
            o_ref[rows, :] = out.astype(BF16)


def _in_proj(h, w, tn, acts):
    tm = 1024
    n = w.shape[1]
    from_x = isinstance(h, tuple)
    if from_x:
        xp, xs, g, mod = h
        n_p = N_PROMPT // tm
        lhs_specs = [pl.BlockSpec((tm, D), lambda j, i: (jnp.minimum(i, n_p - 1), 0)),
                     pl.BlockSpec((tm, D), lambda j, i: (jnp.maximum(i - n_p, 0), 0)),
                     pl.BlockSpec((1, D), lambda j, i: (0, 0)),
                     pl.BlockSpec((MOD_ROWS, 3 * D), lambda j, i: (0, 0))]
        lhs_args = [xp, xs, g.reshape(1, D), mod]
    else:
        lhs_specs = [pl.BlockSpec((tm, D), lambda j, i: (i, 0))]
        lhs_args = [h]
    return pl.pallas_call(
        functools.partial(_in_proj_kernel, tm=tm, tn=tn, acts=acts, from_x=from_x),
        grid=(n // tn, N_TOK // tm),
        in_specs=lhs_specs + [pl.BlockSpec((D, tn), lambda j, i: (0, j))],
        out_specs=pl.BlockSpec((tm, tn), lambda j, i: (i, j)),
        out_shape=jax.ShapeDtypeStruct((N_TOK, n), BF16),
        scratch_shapes=[pltpu.VMEM((D, tn), BF16)],
        compiler_params=_params("arbitrary", "arbitrary"),
        name="in_proj",
    )(*lhs_args, w)


def _out_proj_kernel(ap_ref, as_ref, w_ref, mod_ref, g_ref, modn_ref, *rest, tm, tile0, final, nx):
    x_refs, rest = rest[:nx], rest[nx:]
    wb_ref = rest[-1]

    @pl.when(pl.program_id(0) == 0)
    def _():
        wb_ref[...] = w_ref[...].astype(BF16)

    i = tile0 + pl.program_id(0)
    r = _mod_row(i * tm)
    gate = mod_ref[pl.ds(r, 1), :][:, 2 * D:]
    from_prompt = i < N_PROMPT // tm
    for s in range(tm // OUT_SUB):
        rows = slice(s * OUT_SUB, (s + 1) * OUT_SUB)
        a = jnp.where(from_prompt, ap_ref[rows, :], as_ref[rows, :])
        x = x_refs[0][rows, :] if nx == 1 else jnp.where(from_prompt, x_refs[0][rows, :], x_refs[1][rows, :])
        xn = x + gate * _dot(a, wb_ref[...])
        y = _rms(xn, g_ref[...])
        if final:
            rest[0][rows, :] = y
        else:
            mn = modn_ref[pl.ds(r, 1), :]
            rest[0][rows, :] = xn
            rest[1][rows, :] = (y * (1.0 + mn[:, D:2 * D]) + mn[:, :D]).astype(BF16)


def _out_proj(a_p, a_s, w, x, mod, g_next, mod_next, final, rows=(0, N_TOK)):
    tm = 512
    kw = w.shape[0]
    tile0, nt = rows[0] // tm, rows[1] // tm
    row = pl.BlockSpec((tm, D), lambda i: (i, 0))
    if final:
        out_specs, out_shape = [row], [jax.ShapeDtypeStruct((rows[1], D), F32)]
    else:
        out_specs = [row, row]
        out_shape = [jax.ShapeDtypeStruct((rows[1], D), F32), jax.ShapeDtypeStruct((rows[1], D), BF16)]
    if isinstance(x, tuple):
        x_specs, xs = list(_two_source_specs(tm, D, tile0)), list(x)
    else:
        x_specs, xs = [pl.BlockSpec((tm, D), lambda i: (tile0 + i, 0))], [x]
    return pl.pallas_call(
        functools.partial(_out_proj_kernel, tm=tm, tile0=tile0, final=final, nx=len(xs)),
        grid=(nt,),
        in_specs=[*_two_source_specs(tm, kw, tile0),
                  pl.BlockSpec((kw, D), lambda i: (0, 0)),
                  pl.BlockSpec((MOD_ROWS, 3 * D), lambda i: (0, 0)),
                  pl.BlockSpec((1, D), lambda i: (0, 0)),
                  pl.BlockSpec((MOD_ROWS, 3 * D), lambda i: (0, 0)),
                  *x_specs],
        out_specs=out_specs,
        out_shape=out_shape,
        scratch_shapes=[pltpu.VMEM((kw, D), BF16)],
        compiler_params=_params("arbitrary"),
        name="out_proj",
    )(a_p, a_s, w, mod, g_next.reshape(1, D), mod_next, *xs)


def _ret_kernel(dec_ref, q_ref, k_ref, v_ref, z_ref, gn_ref, cos_ref, sin_ref, s0_ref, o_ref,
                kv_ref, sf_ref, sb_ref, tab_ref, gblk_ref, kr_ref, *, T, HB):
    hg = pl.program_id(1)
    n = T // CHUNK
    L = CHUNK
    ii = lax.broadcasted_iota(jnp.int32, (L, L), 0).astype(F32)
    jj = lax.broadcasted_iota(jnp.int32, (L, L), 1).astype(F32)
    scale = RET_DK ** -0.5

    def chunk(c):
        return slice(c * L, (c + 1) * L)

    def rotate(x, c):
        return x * cos_ref[chunk(c), :] + pltpu.roll(x, RET_DK // 2, 1) * sin_ref[chunk(c), :]

    @pl.when(pl.program_id(0) == 0)
    def _():
        for hh in range(HB):
            head = hg * HB + hh
            lg_f = _log_sigmoid(jnp.full((1, RET_DV), dec_ref[0, head], F32))
            lg_b = _log_sigmoid(jnp.full((1, RET_DV), dec_ref[1, head], F32))
            lf, lb = lg_f[:, :L], lg_b[:, :L]
            tab_ref[head, 0] = jnp.exp(lf * (L - 1.0 - ii)) * scale
            tab_ref[head, 1] = jnp.exp(lb * ii) * scale
            tab_ref[head, 2] = jnp.exp(lf * (ii + 1.0))
            tab_ref[head, 3] = jnp.exp(lb * (L - ii))
            tab_ref[head, 4] = (jnp.where(ii >= jj, jnp.exp(lf * jnp.maximum(ii - jj, 0.0)), 0.0)
                                + jnp.where(jj >= ii, jnp.exp(lb * jnp.maximum(jj - ii, 0.0)), 0.0)) * scale
            gblk_ref[head, 0] = jnp.exp(lg_f * float(L))
            gblk_ref[head, 1] = jnp.exp(lg_b * float(L))

    for hh in range(HB):
        head = hg * HB + hh
        qs = slice(hh * RET_DK, (hh + 1) * RET_DK)
        vs = slice(hh * RET_DV, (hh + 1) * RET_DV)
        k_dec_f, k_dec_b, q_dec_f, q_dec_b, mask = (tab_ref[head, t] for t in range(5))
        g_f = gblk_ref[head, 0]
        g_b = gblk_ref[head, 1]

        for c in range(n):
            kc = rotate(k_ref[chunk(c), qs].astype(F32), c)
            kr_ref[chunk(c), qs] = kc.astype(BF16)
            kk = jnp.concatenate([kc * k_dec_f, kc * k_dec_b], axis=1).astype(BF16)
            kv_ref[hh, c] = _dot_tn(kk, v_ref[chunk(c), vs])
        S = s0_ref[0, hh]
        for c in range(n - 1):
            sf_ref[hh, c] = S.astype(BF16)
            S = S * g_f + kv_ref[hh, c, :RET_DK, :]
        sf_ref[hh, n - 1] = S.astype(BF16)
        S = s0_ref[1, hh]
        for c in reversed(range(1, n)):
            sb_ref[hh, c] = S.astype(BF16)
            S = S * g_b + kv_ref[hh, c, RET_DK:, :]
        sb_ref[hh, 0] = S.astype(BF16)

        for c in range(n):
            qf = rotate(q_ref[chunk(c), qs].astype(F32), c)
            lhs = [(_dot_nt(qf.astype(BF16), kr_ref[chunk(c), qs]) * mask).astype(BF16),
                   (qf * q_dec_f).astype(BF16), (qf * q_dec_b).astype(BF16)]
            rhs = [v_ref[chunk(c), vs], sf_ref[hh, c], sb_ref[hh, c]]
            o = _dot(jnp.concatenate(lhs, axis=1), jnp.concatenate(rhs, axis=0))
            zf = z_ref[chunk(c), vs].astype(F32)
            o_ref[chunk(c), vs] = (_group_norm(o, gn_ref[:, vs]) * zf).astype(BF16)


def _retention(proj, dec, gn, T, nb, row_blk0, rope, state):
    HB = RET_HB_LATENT
    ng = RET_HEADS // HB
    n = T // CHUNK
    qw, vw = HB * RET_DK, HB * RET_DV
    table = pl.BlockSpec((T, RET_DK), lambda b, h: (0, 0))
    return pl.pallas_call(
        functools.partial(_ret_kernel, T=T, HB=HB),
        grid=(nb, ng),
        in_specs=[pl.BlockSpec(memory_space=pltpu.SMEM),
                  pl.BlockSpec((T, qw), lambda b, h: (row_blk0 + b, h)),
                  pl.BlockSpec((T, qw), lambda b, h: (row_blk0 + b, ng + h)),
                  pl.BlockSpec((T, vw), lambda b, h: (row_blk0 + b, ng + h)),
                  pl.BlockSpec((T, vw), lambda b, h: (row_blk0 + b, 2 * ng + h)),
                  pl.BlockSpec((1, vw), lambda b, h: (0, h)),
                  table, table,
                  pl.BlockSpec((None, 2, HB, RET_DK, RET_DV), lambda b, h: (b, 0, h, 0, 0))],
        out_specs=pl.BlockSpec((T, vw), lambda b, h: (b, h)),
        out_shape=jax.ShapeDtypeStruct((nb * T, RET_WIDTH), BF16),
        scratch_shapes=[pltpu.VMEM((HB, n, 2 * RET_DK, RET_DV), F32),
                        pltpu.VMEM((HB, n, RET_DK, RET_DV), BF16),
                        pltpu.VMEM((HB, n, RET_DK, RET_DV), BF16),
                        pltpu.VMEM((RET_HEADS, 5, CHUNK, CHUNK), F32),
                        pltpu.VMEM((RET_HEADS, 2, 1, RET_DV), F32),
                        pltpu.VMEM((T, qw), BF16)],
        compiler_params=_params("arbitrary", "arbitrary"),
        name="retention",
    )(dec, proj, proj, proj, proj, gn.reshape(1, RET_WIDTH), *rope, state)


def _ret_full_kernel(dec_ref, q_ref, k_ref, v_ref, z_ref, gn_ref, o_ref, sn_ref, mask_ref, kdec_ref, *, T):
    scale = RET_DK ** -0.5

    @pl.when(pl.program_id(0) == 0)
    def _():
        ii = lax.broadcasted_iota(jnp.int32, (T, T), 0).astype(F32)
        jj = lax.broadcasted_iota(jnp.int32, (T, T), 1).astype(F32)
        row = lax.broadcasted_iota(jnp.int32, (T, RET_DK), 0).astype(F32)
        for head in range(RET_HEADS):
            lf = _log_sigmoid(jnp.full((1, T), dec_ref[0, head], F32))
            lb = _log_sigmoid(jnp.full((1, T), dec_ref[1, head], F32))
            mask_ref[head] = (jnp.where(ii >= jj, jnp.exp(lf * jnp.maximum(ii - jj, 0.0)), 0.0)
                              + jnp.where(jj >= ii, jnp.exp(lb * jnp.maximum(jj - ii, 0.0)), 0.0)) * scale
            kdec_ref[head, 0] = jnp.exp(lf[:, :RET_DK] * (T - 1.0 - row)) * scale
            kdec_ref[head, 1] = jnp.exp(lb[:, :RET_DK] * row) * scale

    for head in range(RET_HEADS):
        qs = slice(head * RET_DK, (head + 1) * RET_DK)
        vs = slice(head * RET_DV, (head + 1) * RET_DV)
        k = k_ref[:, qs]
        v = v_ref[:, vs]
        a = (_dot_nt(q_ref[:, qs], k) * mask_ref[head]).astype(BF16)
        o = _dot(a, v)
        o_ref[:, vs] = (_group_norm(o, gn_ref[:, vs]) * z_ref[:, vs].astype(F32)).astype(BF16)
        kf = k.astype(F32)
        kk = jnp.concatenate([kf * kdec_ref[head, 0], kf * kdec_ref[head, 1]], axis=1).astype(BF16)
        kv = _dot_tn(kk, v)
        sn_ref[0, head] = kv[:RET_DK]
        sn_ref[1, head] = kv[RET_DK:]


def _retention_full(proj, dec, gn, T, nb):
    return pl.pallas_call(
        functools.partial(_ret_full_kernel, T=T),
        grid=(nb,),
        in_specs=[pl.BlockSpec(memory_space=pltpu.SMEM),
                  pl.BlockSpec((T, RET_QK), lambda b: (b, 0)),
                  pl.BlockSpec((T, RET_QK), lambda b: (b, 1)),
                  pl.BlockSpec((T, RET_WIDTH), lambda b: (b, 1)),
                  pl.BlockSpec((T, RET_WIDTH), lambda b: (b, 2)),
                  pl.BlockSpec((1, RET_WIDTH), lambda b: (0, 0))],
        out_specs=[pl.BlockSpec((T, RET_WIDTH), lambda b: (b, 0)),
                   pl.BlockSpec((None, 2, RET_HEADS, RET_DK, RET_DV), lambda b: (b, 0, 0, 0, 0))],
        out_shape=[jax.ShapeDtypeStruct((nb * T, RET_WIDTH), BF16),
                   jax.ShapeDtypeStruct((nb, 2, RET_HEADS, RET_DK, RET_DV), F32)],
        scratch_shapes=[pltpu.VMEM((RET_HEADS, T, T), F32), pltpu.VMEM((RET_HEADS, 2, T, RET_DK), F32)],
        compiler_params=_params("arbitrary"),
        name="retention_full",
    )(dec, proj, proj, proj, proj, gn.reshape(1, RET_WIDTH))


def _ml_in_kernel(h_ref, w_ref, cw_ref, cb_ref, bq_ref, bk_ref, bv_ref, wq_ref, wk_ref, wv_ref, bias_ref,
                  oz_ref, q_ref, k_ref, v_ref, xc_ref, g_ref, wb_ref, bd_ref, conv_ref):
    j = pl.program_id(0)
    i = pl.program_id(1)
    nsub = ML_TN // ML_TILE

    @pl.when(i == 0)
    def _():
        wb_ref[...] = w_ref[...].astype(BF16)

    @pl.when((i == 0) & (j < ML_XT))
    def _():
        row = lax.broadcasted_iota(jnp.int32, (ML_TILE, ML_TILE), 0)
        col = lax.broadcasted_iota(jnp.int32, (ML_TILE, ML_TILE), 1)
        shift = ML_BLOCK.bit_length() - 1
        same_block = jnp.right_shift(row, shift) == jnp.right_shift(col, shift)
        for which, b_ref in enumerate((bq_ref, bk_ref, bv_ref)):
            for t in range(nsub):
                w = b_ref[t * ML_TILE:(t + 1) * ML_TILE, :]
                rep = w[:, ML_BLOCK - 1:ML_BLOCK]
                for dd in range(ML_BLOCK - 1):
                    rep = jnp.where((col & (ML_BLOCK - 1)) == dd, w[:, dd:dd + 1], rep)
                bd_ref[which, t] = jnp.where(same_block, rep, 0.0).astype(BF16)

    @pl.when(j >= ML_XT)
    def _():
        sub = ML_ROWS // IN_SUB
        for s in range(IN_SUB):
            rows = slice(s * sub, (s + 1) * sub)
            acc = _dot(h_ref[rows, :], wb_ref[...])
            col = j * ML_TN + lax.broadcasted_iota(jnp.int32, acc.shape, 1)
            sig = jax.nn.sigmoid(acc)
            oz_ref[rows, :] = jnp.where(col >= 2 * ML_WIDTH, acc * sig, sig).astype(BF16)

    @pl.when(j < ML_XT)
    def _():
        acc = _dot(h_ref[...], wb_ref[...])
        T = jnp.where(i < N_PROMPT // ML_ROWS, SEQ, DEC_SEQ)
        pad = ML_CONV // 2
        taps = [tap for tap in range(ML_CONV) if tap != pad]
        sub8 = lax.broadcasted_iota(jnp.int32, (8, ML_TILE), 0)
        g = jnp.where(j == 0, 1.0, 0.0) * bias_ref[...]
        for t in range(nsub):
            cols = slice(t * ML_TILE, (t + 1) * ML_TILE)
            x = acc[:, cols]
            shifted = {tap: pltpu.roll(x, (pad - tap) % ML_ROWS, 0) for tap in taps}
            conv = cb_ref[:, cols] + x * cw_ref[pad:pad + 1, cols]
            for tap in taps:
                conv = conv + shifted[tap] * cw_ref[tap:tap + 1, cols]
            conv_ref[...] = conv
            for r0 in ML_EDGE_ROWS:
                rows = slice(r0, r0 + 8)
                pos = (r0 + sub8) & (T - 1)
                fixed = cb_ref[:, cols] + x[rows] * cw_ref[pad:pad + 1, cols]
                for tap in taps:
                    ok = (pos + (tap - pad) >= 0) & (pos + (tap - pad) < T)
                    fixed = fixed + jnp.where(ok, shifted[tap][rows], 0.0) * cw_ref[tap:tap + 1, cols]
                conv_ref[rows, :] = fixed
            xcb = _silu(conv_ref[...]).astype(BF16)
            qb = _dot(xcb, bd_ref[0, t]).astype(BF16)
            kb = _dot(xcb, bd_ref[1, t]).astype(BF16)
            vb = _dot(x.astype(BF16), bd_ref[2, t]).astype(BF16)
            q_ref[:, cols] = qb
            k_ref[:, cols] = kb
            v_ref[:, cols] = vb
            xc_ref[:, cols] = xcb
            g = g + (_dot(qb, wq_ref[cols, :].astype(BF16)) + _dot(kb, wk_ref[cols, :].astype(BF16))
                     + _dot(vb, wv_ref[cols, :].astype(BF16)))
        g_ref[...] = g


def _ml_in_proj(h, w, conv_w, conv_b, wq, wk, wv, wif, bif):
    ng = wif.shape[1]
    ni = N_TOK // ML_ROWS
    xt = ML_XT
    wq, wk, wv = (a.reshape(ML_WIDTH, ML_BLOCK) for a in (wq, wk, wv))
    xcol = lambda j, i: (0, jnp.minimum(j, xt - 1))
    xrow = lambda off: (lambda j, i: (off + jnp.minimum(j, xt - 1), 0))
    x_out = pl.BlockSpec((ML_ROWS, ML_TN), lambda j, i: (jnp.where(j < xt, i, ni - 1), jnp.minimum(j, xt - 1)))
    return pl.pallas_call(
        _ml_in_kernel,
        grid=(3 * ML_WIDTH // ML_TN, ni),
        in_specs=[pl.BlockSpec((ML_ROWS, D), lambda j, i: (i, 0)),
                  pl.BlockSpec((D, ML_TN), lambda j, i: (0, j)),
                  pl.BlockSpec((ML_CONV, ML_TN), xcol),
                  pl.BlockSpec((1, ML_TN), xcol),
                  pl.BlockSpec((ML_TN, ML_BLOCK), xrow(0)),
                  pl.BlockSpec((ML_TN, ML_BLOCK), xrow(0)),
                  pl.BlockSpec((ML_TN, ML_BLOCK), xrow(0)),
                  pl.BlockSpec((ML_TN, ng), xrow(0)),
                  pl.BlockSpec((ML_TN, ng), xrow(xt)),
                  pl.BlockSpec((ML_TN, ng), xrow(2 * xt)),
                  pl.BlockSpec((1, ng), lambda j, i: (0, 0))],
        out_specs=[pl.BlockSpec((ML_ROWS, ML_TN), lambda j, i: (jnp.where(j < xt, 0, i), jnp.maximum(j - xt, 0))),
                   x_out, x_out, x_out, x_out,
                   pl.BlockSpec((None, ML_ROWS, ng),
                                lambda j, i: (jnp.minimum(j, xt - 1), jnp.where(j < xt, i, ni - 1), 0))],
        out_shape=[jax.ShapeDtypeStruct((N_TOK, 2 * ML_WIDTH), BF16)]
        + [jax.ShapeDtypeStruct((N_TOK, ML_WIDTH), BF16)] * 4
        + [jax.ShapeDtypeStruct((xt, N_TOK, ng), F32)],
        scratch_shapes=[pltpu.VMEM((D, ML_TN), BF16), pltpu.VMEM((3, ML_TN // ML_TILE, ML_TILE, ML_TILE), BF16),
                        pltpu.VMEM((ML_ROWS, ML_TILE), F32)],
        compiler_params=_params("arbitrary", "arbitrary"),
        name="mlstm_in_proj",
    )(h, w, conv_w, conv_b.reshape(1, ML_WIDTH), wq, wk, wv, wif, wif, wif, bif.reshape(1, ng))


def _dot_split(lhs_bf16, rhs):
    r1 = rhs.astype(BF16)
    e1 = rhs - r1.astype(F32)
    r2 = e1.astype(BF16)
    r3 = (e1 - r2.astype(F32)).astype(BF16)
    return _dot(lhs_bf16, r1) + _dot(lhs_bf16, r2) + _dot(lhs_bf16, r3)


def _ml_scan_kernel(*refs, T, HB, state_in, state_out):
    refs = list(refs)
    pos = 0
    m0_ref = c0_ref = n0_ref = cn_ref = nn_ref = mn_ref = None
    if state_in:
        m0_ref = refs[0]
        pos = 1
    q_ref, k_ref, v_ref, g_ref, op_ref, z_ref, xc_ref, gn_ref, sk_ref = refs[pos:pos + 9]
    pos += 9
    if state_in:
        c0_ref, n0_ref = refs[pos:pos + 2]
        pos += 2
    o_ref = refs[pos]
    pos += 1
    if state_out:
        cn_ref, nn_ref, mn_ref = refs[pos:pos + 3]
        pos += 3
    acc_ref, s_ref = refs[pos:pos + 2]

    for hh in range(HB):
        cols = slice(hh * ML_HD, (hh + 1) * ML_HD)

        def head_cols(ref):
            return ref.at[:, cols]

        def head_state(ref):
            return None if ref is None else ref.at[:, hh]

        _ml_scan_head(pl.program_id(0), pl.program_id(1) * HB + hh, m0_ref,
                      head_cols(q_ref), head_cols(k_ref), head_cols(v_ref), g_ref, head_cols(op_ref),
                      head_cols(z_ref), head_cols(xc_ref), head_cols(gn_ref), head_cols(sk_ref),
                      head_state(c0_ref), head_state(n0_ref), head_cols(o_ref),
                      head_state(cn_ref), head_state(nn_ref), head_state(mn_ref), acc_ref.at[hh], s_ref.at[hh],
                      T=T, state_in=state_in, state_out=state_out)


def _ml_scan_head(b, h, m0_ref, q_ref, k_ref, v_ref, g_ref, op_ref, z_ref, xc_ref, gn_ref, sk_ref,
                  c0_ref, n0_ref, o_ref, cn_ref, nn_ref, mn_ref, acc_ref, s_ref, *, T, state_in, state_out):
    n = T // CHUNK
    L = CHUNK
    scale = ML_HD ** -0.5

    def chunk(c):
        return slice(c * L, (c + 1) * L)

    ii = lax.broadcasted_iota(jnp.int32, (L, L), 0)
    jj = lax.broadcasted_iota(jnp.int32, (L, L), 1)
    neg = jnp.full((L, L), -jnp.inf, F32)
    zero = jnp.zeros((L, L), F32)
    tri = [jnp.where(jj <= ii, 1.0, 0.0).astype(BF16), jnp.where(jj >= ii, 1.0, 0.0).astype(BF16)]
    bias = [jnp.where(ii >= jj, zero, neg), jnp.where(ii <= jj, zero, neg)]

    g = g_ref[0]
    for part in range(1, ML_XT):
        g = g + g_ref[part]
    col = lax.broadcasted_iota(jnp.int32, g.shape, 1)
    g = jnp.where((col & (2 * ML_HEADS - 1)) >= ML_HEADS, _log_sigmoid(g), g)

    def gate_column(j):
        return jnp.sum(jnp.where(col == j, g, 0.0), axis=1, keepdims=True)


    for i in range(n):
        for j in range(n):
            s_ref[i, j] = _dot_nt(q_ref[chunk(i), :], k_ref[chunk(j), :])

    G = []
    for d in range(2):
        order = list(range(n)) if d == 0 else list(reversed(range(n)))
        gi_all = gate_column(d * 2 * ML_HEADS + h)
        gf_all = gate_column(d * 2 * ML_HEADS + ML_HEADS + h)
        m0 = jnp.full((1, 1), m0_ref[(b * 2 + d) * ML_HEADS + h], F32) if state_in else jnp.zeros((1, 1), F32)
        c_col, c_row, b_col, m_row = [None] * n, [None] * n, [None] * n, [None] * n
        offset = jnp.zeros((1, 1), F32)
        carry = m0
        for c in order:
            gi = gi_all[chunk(c)]
            gf = gf_all[chunk(c)]
            cum = _dot_split(tri[d], jnp.broadcast_to(gf, (L, L))) + offset
            offset = offset + jnp.sum(gf, axis=0, keepdims=True)
            cc = gi - cum
            c_col[c] = cc[:, :1]
            c_row[c] = jnp.transpose(cc)
            b_col[c] = cum[:, :1]
            m_row[c] = jnp.maximum(jnp.max(c_row[c] + bias[d], axis=1, keepdims=True), carry)
            carry = jnp.maximum(carry, jnp.max(c_col[c], axis=0, keepdims=True))
        G.append(dict(order=order, c_col=c_col, c_row=c_row, b_col=b_col, m_row=m_row, m0=m0,
                      m_end=carry, b_end=offset))

    done = set()
    for i in range(n):
        for d in range(2):
            e = G[d]
            js = list(range(0, i + 1)) if d == 0 else list(range(i, n))
            m_i = e["m_row"][i]
            parts = []
            tot = None
            for j in js:
                z = e["c_row"][j] - m_i
                if j == i:
                    z = z + bias[d]
                sw = s_ref[i, j] * (jnp.exp(z) * scale)
                tot = sw if tot is None else tot + sw
                parts.append(sw.astype(BF16))
            num = _dot(jnp.concatenate(parts, axis=1), v_ref[js[0] * L:(js[-1] + 1) * L, :])
            den = jnp.sum(tot, axis=1, keepdims=True)
            if state_in:
                qc = q_ref[chunk(i), :]
                w0 = jnp.exp(e["m0"] - m_i)
                num = num + _dot((qc.astype(F32) * w0).astype(BF16), c0_ref[d].astype(BF16))
                n0 = jnp.broadcast_to(n0_ref[d], (8, ML_HD)).astype(BF16)
                den = den + w0 * _dot_nt(qc, n0)[:, :1]
            hb = num / jnp.maximum(jnp.abs(den), jnp.exp(-(e["b_col"][i] + m_i)))
            if i not in done:
                done.add(i)
                acc_ref[chunk(i), :] = hb
            else:
                cell = _group_norm((acc_ref[chunk(i), :] + hb) * op_ref[chunk(i), :].astype(F32), gn_ref[...])
                mixed = cell + sk_ref[...] * xc_ref[chunk(i), :].astype(F32)
                o_ref[chunk(i), :] = (mixed * z_ref[chunk(i), :].astype(F32)).astype(BF16)

    if state_out:
        ones = jnp.ones((8, T), BF16)
        for d in range(2):
            e = G[d]
            kw = jnp.concatenate(
                [(k_ref[chunk(c), :].astype(F32) * (jnp.exp(e["c_col"][c] - e["m_end"]) * scale)).astype(BF16)
                 for c in range(n)], axis=0)
            C = _dot_tn(kw, v_ref[...])
            nvec = _dot(ones, kw)[:1]
            if state_in:
                w_end = jnp.exp(e["m0"] - e["m_end"])
                C = C + w_end * c0_ref[d]
                nvec = nvec + w_end * n0_ref[d]
            cn_ref[d] = C
            nn_ref[d] = nvec
            mn_ref[d] = jnp.broadcast_to(e["b_end"] + e["m_end"], mn_ref.shape[1:])


def _ml_scan(oz, q, k, v, xc, g, gn, skip, T, nb, row_blk0, state=None, want_state=False):
    H = ML_HEADS
    HB = ML_HB_PROMPT if T == SEQ else ML_HB_LATENT
    ng = H // HB
    hw = HB * ML_HD
    n = T // CHUNK
    tok = lambda b, h: (row_blk0 + b, h)
    head_vec = pl.BlockSpec((1, hw), lambda b, h: (0, h))
    both = lambda *tail: pl.BlockSpec((None, 2, HB, *tail), lambda b, h: (b, 0, h, 0, 0))
    in_specs, args = [], []
    if state is not None:
        c0, n0, m0 = state
        in_specs.append(pl.BlockSpec(memory_space=pltpu.SMEM))
        args.append(m0.reshape(nb * 2 * H))
    in_specs += [pl.BlockSpec((T, hw), tok)] * 3
    in_specs += [pl.BlockSpec((ML_XT, T, g.shape[2]), lambda b, h: (0, row_blk0 + b, 0)),
                 pl.BlockSpec((T, hw), tok),
                 pl.BlockSpec((T, hw), lambda b, h: (row_blk0 + b, ng + h)),
                 pl.BlockSpec((T, hw), tok), head_vec, head_vec]
    args += [q, k, v, g, oz, oz, xc, gn.reshape(1, ML_WIDTH), skip.reshape(1, ML_WIDTH)]
    if state is not None:
        in_specs += [both(ML_HD, ML_HD), both(1, ML_HD)]
        args += [c0, n0.reshape(nb, 2, H, 1, ML_HD)]
    out_specs = [pl.BlockSpec((T, hw), lambda b, h: (b, h))]
    out_shape = [jax.ShapeDtypeStruct((nb * T, ML_WIDTH), BF16)]
    if want_state:
        out_specs += [both(ML_HD, ML_HD), both(1, ML_HD), both(1, 128)]
        out_shape += [jax.ShapeDtypeStruct((nb, 2, H, ML_HD, ML_HD), F32),
                      jax.ShapeDtypeStruct((nb, 2, H, 1, ML_HD), F32),
                      jax.ShapeDtypeStruct((nb, 2, H, 1, 128), F32)]
    scratch = [pltpu.VMEM((HB, T, ML_HD), F32), pltpu.VMEM((HB, n, n, CHUNK, CHUNK), F32)]
    return pl.pallas_call(
        functools.partial(_ml_scan_kernel, T=T, HB=HB, state_in=state is not None, state_out=want_state),
        grid=(nb, ng),
        in_specs=in_specs, out_specs=out_specs, out_shape=out_shape, scratch_shapes=scratch,
        compiler_params=_params("arbitrary", "arbitrary"),
        name="mlstm_scan",
    )(*args)


def _att_rope(x, cos, sin):
    lane = lax.broadcasted_iota(jnp.int32, x.shape, 1)
    half = ATT_HD // 2
    rot = jnp.where((lane & (ATT_HD - 1)) < half, pltpu.roll(x, 128 - half, 1), pltpu.roll(x, half, 1))
    return x * cos + rot * sin


def _att_heads(qms, sinks, key_sets):
    scores = [[_dot_nt(k_both, qm) if bias is None else _dot_nt(k_both, qm) + bias
               for k_both, _, bias in key_sets] for qm in qms]
    ms = []
    for sink, per_set in zip(sinks, scores):
        m = sink
        for s in per_set:
            m = jnp.maximum(m, jnp.max(s, axis=0, keepdims=True))
        ms.append(m)
    probs = [[jnp.exp(s - m) for s in per_set] for m, per_set in zip(ms, scores)]
    outs = []
    for sink, m, per_set in zip(sinks, ms, probs):
        den = jnp.exp(sink - m)
        out = None
        for p, (_, v_t, _) in zip(per_set, key_sets):
            den = den + jnp.sum(p, axis=0, keepdims=True)
            o = _dot(v_t, p.astype(BF16))
            out = o if out is None else out + o
        outs.append(out * (1.0 / den))
    return outs


def _att_prepare_kv(k, v, kb_ref, vt_ref, kv, nblk):
    lane = lax.broadcasted_iota(jnp.int32, k.shape, 1)
    native = lane < ATT_HD if kv % 2 == 0 else lane >= ATT_HD
    kb_ref[kv] = jnp.where(native, k, pltpu.roll(k, ATT_HD, 1)).astype(BF16)
    r0 = (kv % 2) * ATT_HD
    blk = k.shape[0] // nblk
    for c in range(nblk):
        vt_ref[kv, c] = jnp.transpose(v[c * blk:(c + 1) * blk, :])[r0:r0 + ATT_HD, :].astype(BF16)


def _att_group(q_ref, z0_ref, z1_ref, o_ref, sink_ref, rows, kv, key_sets):
    lo = lax.broadcasted_iota(jnp.int32, (CHUNK, 128), 1) < ATT_HD
    zero = jnp.zeros((CHUNK, 128), BF16)
    qms, sinks = [], []
    for g in range(ATT_GROUP):
        col = (2 * kv + g // 2) * 128
        q = q_ref[rows, col:col + 128]
        qms.append(jnp.where(lo, q, zero) if g % 2 == 0 else jnp.where(lo, zero, q))
        sinks.append(jnp.full((1, CHUNK), sink_ref[kv * ATT_GROUP + g], F32))
    heads = _att_heads(qms, sinks, key_sets)
    for p in range(2):
        col = (2 * kv + p) * 128
        out = jnp.transpose(jnp.concatenate(heads[2 * p:2 * p + 2], axis=0))
        z_ref = z0_ref if col < ATT_WIDTH // 2 else z1_ref
        zc = col % (ATT_WIDTH // 2)
        zf = z_ref[rows, zc:zc + 128].astype(F32)
        o_ref[rows, col:col + 128] = (out * zf).astype(BF16)


def _att_ctx_kernel(sink_ref, q_ref, k_ref, v_ref, z0_ref, z1_ref, o_ref, kt_ref, vto_ref, qs_ref, kb_ref, vt_ref):
    T = q_ref.shape[0]
    qs_ref[...] = q_ref[...] * (ATT_HD ** -0.5)
    for kv in range(ATT_KV):
        slab = (kv // 2) * 128
        k = k_ref[:, slab:slab + 128].astype(F32)
        v = v_ref[:, slab:slab + 128].astype(F32)
        _att_prepare_kv(k, v, kb_ref, vt_ref, kv, 1)
        r0 = (kv % 2) * ATT_HD
        kt_ref[kv] = jnp.transpose(k)[r0:r0 + ATT_HD, :]
        vto_ref[kv] = jnp.transpose(v)[r0:r0 + ATT_HD, :]
    for c in range(T // CHUNK):
        rows = slice(c * CHUNK, (c + 1) * CHUNK)
        for kv in range(ATT_KV):
            _att_group(qs_ref, z0_ref, z1_ref, o_ref, sink_ref, rows, kv, [(kb_ref[kv], vt_ref[kv, 0], None)])


def _att_ctx(proj, sink):
    T = SEQ
    half = ATT_WIDTH // 2
    return pl.pallas_call(
        _att_ctx_kernel,
        grid=(BATCH,),
        in_specs=[pl.BlockSpec(memory_space=pltpu.SMEM),
                  pl.BlockSpec((T, ATT_WIDTH), lambda b: (b, 0)),
                  pl.BlockSpec((T, ATT_KVW), lambda b: (b, ATT_WIDTH // ATT_KVW)),
                  pl.BlockSpec((T, ATT_KVW), lambda b: (b, ATT_WIDTH // ATT_KVW + 1)),
                  pl.BlockSpec((T, half), lambda b: (b, (ATT_WIDTH + 2 * ATT_KVW) // half)),
                  pl.BlockSpec((T, half), lambda b: (b, (ATT_WIDTH + 2 * ATT_KVW) // half + 1))],
        out_specs=[pl.BlockSpec((T, ATT_WIDTH), lambda b: (b, 0)),
                   pl.BlockSpec((None, ATT_KV, ATT_HD, T), lambda b: (b, 0, 0, 0)),
                   pl.BlockSpec((None, ATT_KV, ATT_HD, T), lambda b: (b, 0, 0, 0))],
        out_shape=[jax.ShapeDtypeStruct((N_PROMPT, ATT_WIDTH), BF16),
                   jax.ShapeDtypeStruct((BATCH, ATT_KV, ATT_HD, T), F32),
                   jax.ShapeDtypeStruct((BATCH, ATT_KV, ATT_HD, T), F32)],
        scratch_shapes=[pltpu.VMEM((T, ATT_WIDTH), BF16), pltpu.VMEM((ATT_KV, T, 128), BF16),
                        pltpu.VMEM((ATT_KV, 1, ATT_HD, T), BF16)],
        compiler_params=_params("arbitrary"),
        name="att_ctx",
    )(sink, proj, proj, proj, proj, proj)


def _att_win_kernel(sink_ref, q_ref, k_ref, v_ref, z0_ref, z1_ref, kc_ref, vc_ref, cos_ref, sin_ref,
                    o_ref, qs_ref, kb_ref, vt_ref, kcb_ref, vct_ref):
    T = q_ref.shape[0]
    L = CHUNK
    n = T // L
    cos = cos_ref[...]
    sin = sin_ref[...]
    for c0 in range(0, ATT_WIDTH, 128):
        q = _att_rope(q_ref[:, c0:c0 + 128].astype(F32), cos, sin)
        qs_ref[:, c0:c0 + 128] = (q * (ATT_HD ** -0.5)).astype(BF16)
    for kv in range(ATT_KV):
        slab = (kv // 2) * 128
        _att_prepare_kv(_att_rope(k_ref[:, slab:slab + 128].astype(F32), cos, sin),
                        v_ref[:, slab:slab + 128].astype(F32), kb_ref, vt_ref, kv, n)
        _att_prepare_kv(kc_ref[:, slab:slab + 128], vc_ref[:, slab:slab + 128], kcb_ref, vct_ref, kv, 1)

    jj = lax.broadcasted_iota(jnp.int32, (L, L), 0)
    ii = lax.broadcasted_iota(jnp.int32, (L, L), 1)
    neg = jnp.full((L, L), -jnp.inf, F32)
    bias_prev = jnp.where(jj >= ii, 0.0, neg)
    bias_next = jnp.where(jj <= ii, 0.0, neg)

    def body(c, carry):
        rows = pl.ds(pl.multiple_of(c * L, L), L)
        c_prev = jnp.maximum(c - 1, 0)
        c_next = jnp.minimum(c + 1, n - 1)
        b_prev = jnp.where(c > 0, bias_prev, neg)
        b_next = jnp.where(c < n - 1, bias_next, neg)
        for kv in range(ATT_KV):
            def keys(cb):
                return kb_ref[kv, pl.ds(pl.multiple_of(cb * L, L), L), :]
            key_sets = [(keys(c_prev), vt_ref[kv, c_prev], b_prev),
                        (keys(c), vt_ref[kv, c], None),
                        (keys(c_next), vt_ref[kv, c_next], b_next),
                        (kcb_ref[kv], vct_ref[kv, 0], None)]
            _att_group(qs_ref, z0_ref, z1_ref, o_ref, sink_ref, rows, kv, key_sets)
        return carry

    lax.fori_loop(0, n, body, 0)


def _att_win(proj, sink, k_ctx, v_ctx, cos, sin):
    T = DEC_SEQ
    half = ATT_WIDTH // 2
    rb = N_PROMPT // T
    return pl.pallas_call(
        _att_win_kernel,
        grid=(DEC_BATCH,),
        in_specs=[pl.BlockSpec(memory_space=pltpu.SMEM),
                  pl.BlockSpec((T, ATT_WIDTH), lambda b: (rb + b, 0)),
                  pl.BlockSpec((T, ATT_KVW), lambda b: (rb + b, ATT_WIDTH // ATT_KVW)),
                  pl.BlockSpec((T, ATT_KVW), lambda b: (rb + b, ATT_WIDTH // ATT_KVW + 1)),
                  pl.BlockSpec((T, half), lambda b: (rb + b, (ATT_WIDTH + 2 * ATT_KVW) // half)),
                  pl.BlockSpec((T, half), lambda b: (rb + b, (ATT_WIDTH + 2 * ATT_KVW) // half + 1)),
                  pl.BlockSpec((None, PAST_LEN, ATT_KVW), lambda b: (b, 0, 0)),
                  pl.BlockSpec((None, PAST_LEN, ATT_KVW), lambda b: (b, 0, 0)),
                  pl.BlockSpec((T, 128), lambda b: (0, 0)),
                  pl.BlockSpec((T, 128), lambda b: (0, 0))],
        out_specs=pl.BlockSpec((T, ATT_WIDTH), lambda b: (b, 0)),
        out_shape=jax.ShapeDtypeStruct((N_LATENT, ATT_WIDTH), BF16),
        scratch_shapes=[pltpu.VMEM((T, ATT_WIDTH), BF16),
                        pltpu.VMEM((ATT_KV, T, 128), BF16),
                        pltpu.VMEM((ATT_KV, T // CHUNK, ATT_HD, CHUNK), BF16),
                        pltpu.VMEM((ATT_KV, PAST_LEN, 128), BF16),
                        pltpu.VMEM((ATT_KV, 1, ATT_HD, PAST_LEN), BF16)],
        compiler_params=_params("arbitrary"),
        name="att_win",
    )(sink, proj, proj, proj, proj, proj, k_ctx, v_ctx, cos, sin)


def _rope_tables(T, hd, reps):
    rows = T // GRID_W
    row = np.repeat(np.arange(rows, dtype=np.float64), GRID_W)
    col = np.tile(np.arange(GRID_W, dtype=np.float64), rows)
    nf = hd // 4
    inv = ROPE_BASE ** (-np.arange(nf, dtype=np.float64) / nf)
    ang = np.concatenate([row[:, None] * inv[None, :], col[:, None] * inv[None, :]], axis=-1)
    cos, sin = np.cos(ang), np.sin(ang)
    return (jnp.asarray(np.tile(np.concatenate([cos, cos], axis=-1), (1, reps)), F32),
            jnp.asarray(np.tile(np.concatenate([-sin, sin], axis=-1), (1, reps)), F32))


def _retention_layer(h, w_in, decay_f, decay_b, gn, state, rope):
    z0 = 2 * RET_QK + RET_WIDTH
    proj = _in_proj(h, w_in, 2048, (z0, z0))
    dec = jnp.stack([decay_f, decay_b]).astype(F32)
    a_p, new_state = _retention_full(proj, dec, gn, SEQ, BATCH)
    a_s = _retention(proj, dec, gn, DEC_SEQ, DEC_BATCH, N_PROMPT // DEC_SEQ, rope, state)
    return a_p, a_s, new_state


def kernel(x_prompt, x_sample, c, c_ctx, state_l0_ret, state_l1_C, state_l1_n, state_l1_m, cache_l2_k, cache_l2_v, state_l3_ret, norm_l0, ada_w_l0, ada_b_l0, w_in_l0, w_out_l0, ret_decay_f_l0, ret_decay_b_l0, ret_gn_l0, norm_l1, ada_w_l1, ada_b_l1, w_in_l1, w_out_l1, conv_w_l1, conv_b_l1, wq_l1, wk_l1, wv_l1, wif_f_l1, bif_f_l1, wif_b_l1, bif_b_l1, gn_l1, skip_l1, norm_l2, ada_w_l2, ada_b_l2, w_in_l2, w_out_l2, sink_l2, norm_l3, ada_w_l3, ada_b_l3, w_in_l3, w_out_l3, ret_decay_f_l3, ret_decay_b_l3, ret_gn_l3, final_norm):
    ct = jnp.concatenate([c_ctx[:, None], c.T, jnp.zeros((D, MOD_ROWS - 1 - DEC_BATCH), F32)], axis=1)
    mod_all = _adaln(ct, (ada_w_l0, ada_w_l1, ada_w_l2, ada_w_l3), (ada_b_l0, ada_b_l1, ada_b_l2, ada_b_l3))
    mods = [mod_all[l] for l in range(mod_all.shape[0])]
    rope_ret = _rope_tables(DEC_SEQ, RET_DK, 1)
    rope_att = _rope_tables(DEC_SEQ, ATT_HD, 2)

    x0 = (x_prompt.reshape(N_PROMPT, D), x_sample.reshape(N_LATENT, D))

    a_p, a_s, new_l0_ret = _retention_layer((*x0, norm_l0, mods[0]), w_in_l0, ret_decay_f_l0, ret_decay_b_l0,
                                            ret_gn_l0, state_l0_ret, rope_ret)
    x, h = _out_proj(a_p, a_s, w_out_l0, x0, mods[0], norm_l1, mods[1], False)

    wif = jnp.concatenate([wif_f_l1, wif_b_l1], axis=1)
    bif = jnp.concatenate([bif_f_l1, bif_b_l1])
    oz, q, k, v, xc, g = _ml_in_proj(h, w_in_l1, conv_w_l1, conv_b_l1, wq_l1, wk_l1, wv_l1, wif, bif)
    a_p, new_l1_C, n_new, m_new = _ml_scan(oz, q, k, v, xc, g, gn_l1, skip_l1, SEQ, BATCH, 0, want_state=True)
    (a_s,) = _ml_scan(oz, q, k, v, xc, g, gn_l1, skip_l1, DEC_SEQ, DEC_BATCH, N_PROMPT // DEC_SEQ,
                      state=(state_l1_C, state_l1_n, state_l1_m))
    new_l1_n = n_new.reshape(BATCH, 2, ML_HEADS, ML_HD)
    new_l1_m = m_new[:, :, :, 0, 0]
    x, h = _out_proj(a_p, a_s, w_out_l1, x, mods[1], norm_l2, mods[2], False)

    proj = _in_proj(h, w_in_l2, 1280, (ATT_WIDTH + 2 * ATT_KVW,) * 2)
    a_p, k_t, v_t = _att_ctx(proj, sink_l2)
    new_l2_k = jnp.transpose(k_t, (0, 3, 1, 2))
    new_l2_v = jnp.transpose(v_t, (0, 3, 1, 2))
    a_s = _att_win(proj, sink_l2, cache_l2_k.reshape(DEC_BATCH, PAST_LEN, ATT_KVW),
                   cache_l2_v.reshape(DEC_BATCH, PAST_LEN, ATT_KVW), rope_att[0], rope_att[1])
    x, h = _out_proj(a_p, a_s, w_out_l2, x, mods[2], norm_l3, mods[3], False)

    a_p, a_s, new_l3_ret = _retention_layer(h, w_in_l3, ret_decay_f_l3, ret_decay_b_l3, ret_gn_l3, state_l3_ret,
                                            rope_ret)
    (y_p,) = _out_proj(a_p, a_s, w_out_l3, x, mods[3], final_norm, mods[3], True, rows=(0, N_PROMPT))
    (y_s,) = _out_proj(a_p, a_s, w_out_l3, x, mods[3], final_norm, mods[3], True, rows=(N_PROMPT, N_LATENT))

    y_prompt = y_p.reshape(BATCH, SEQ, D)
    y_sample = y_s.reshape(DEC_BATCH, DEC_SEQ, D)
    return (y_prompt, y_sample, new_l0_ret, new_l1_C, new_l1_n, new_l1_m, new_l2_k, new_l2_v, new_l3_ret)
```

```python
import functools

import jax
import jax.numpy as jnp
import numpy as np
from jax import lax
from jax.experimental import pallas as pl
from jax.experimental.pallas import tpu as pltpu

F32 = jnp.float32
BF16 = jnp.bfloat16

D = 1024
BATCH = 16
SEQ = 256
DEC_BATCH = 2
DEC_SEQ = 1024
PAST_LEN = 512
GRID_W = 64
CHUNK = 128
EPS = 1e-6
ROPE_BASE = 10000.0

N_PROMPT = BATCH * SEQ
N_LATENT = DEC_BATCH * DEC_SEQ
N_TOK = N_PROMPT + N_LATENT
MOD_ROWS = 8

RET_HEADS = 8
RET_DK = 128
RET_DV = 256
RET_QK = RET_HEADS * RET_DK
RET_WIDTH = RET_HEADS * RET_DV
RET_HB_PROMPT = 8
RET_HB_LATENT = 4
ML_HB_PROMPT = 1
ML_HB_LATENT = 1

ML_HEADS = 4
ML_WIDTH = 2 * D
ML_HD = ML_WIDTH // ML_HEADS
ML_BLOCK = 4
ML_CONV = 5
ML_TILE = 256
ML_ROWS = 1024
ML_TN = 1024
ML_XT = ML_WIDTH // ML_TN
ML_EDGE_ROWS = sorted({r for e in range(0, ML_ROWS + 1, SEQ) for r in (e - 8, e) if 0 <= r < ML_ROWS})

ATT_HEADS = 16
ATT_KV = 4
ATT_HD = 64
ATT_GROUP = ATT_HEADS // ATT_KV
ATT_WIDTH = ATT_HEADS * ATT_HD
ATT_KVW = ATT_KV * ATT_HD
WINDOW = 128

OUT_SUB = 256
IN_SUB = 4

VMEM_LIMIT = 56 * 1024 * 1024


def _params(*sem):
    return pltpu.CompilerParams(dimension_semantics=sem, vmem_limit_bytes=VMEM_LIMIT)


def _mod_row(row0):
    return jnp.maximum((row0 - N_PROMPT) // DEC_SEQ + 1, 0)


def _dot(a, b):
    return jnp.dot(a, b, preferred_element_type=F32)


def _dot_nt(a, b):
    return lax.dot_general(a, b, (((1,), (1,)), ((), ())), preferred_element_type=F32)


def _dot_tn(a, b):
    return lax.dot_general(a, b, (((0,), (0,)), ((), ())), preferred_element_type=F32)


def _silu(x):
    return x * jax.nn.sigmoid(x)


def _log_sigmoid(x):
    return jnp.minimum(x, 0.0) - jnp.log1p(jnp.exp(-jnp.abs(x)))


def _rms(x, g):
    return x * lax.rsqrt(jnp.mean(x * x, axis=-1, keepdims=True) + EPS) * g


def _group_norm(x, g):
    xc = x - jnp.mean(x, axis=-1, keepdims=True)
    return xc * lax.rsqrt(jnp.mean(xc * xc, axis=-1, keepdims=True) + EPS) * g


def _two_source_specs(tm, width, tile0=0):
    n_p = N_PROMPT // tm
    return (pl.BlockSpec((tm, width), lambda i: (jnp.minimum(tile0 + i, n_p - 1), 0)),
            pl.BlockSpec((tm, width), lambda i: (jnp.maximum(tile0 + i - n_p, 0), 0)))


def _ada_kernel(ct_ref, *refs, nl, nt):
    o_ref = refs[-1]
    layer = pl.program_id(0) // nt
    s = _silu(ct_ref[...])
    for l in range(nl):
        @pl.when(layer == l)
        def _(w_ref=refs[l], b_ref=refs[nl + l]):
            w = w_ref[...]
            rows = [jnp.sum(w * s[:, r:r + 1], axis=0, keepdims=True) for r in range(1 + DEC_BATCH)]
            rows.append(jnp.zeros((MOD_ROWS - len(rows), w.shape[1]), F32))
            o_ref[...] = jnp.concatenate(rows, axis=0) + b_ref[...]


def _adaln(ct, ws, bs):
    tn = 1024
    nl, nt = len(ws), 3 * D // tn
    own = lambda l: (lambda j: (0, jnp.clip(j - l * nt, 0, nt - 1)))
    return pl.pallas_call(
        functools.partial(_ada_kernel, nl=nl, nt=nt),
        grid=(nl * nt,),
        in_specs=[pl.BlockSpec((D, MOD_ROWS), lambda j: (0, 0))]
        + [pl.BlockSpec((D, tn), own(l)) for l in range(nl)]
        + [pl.BlockSpec((1, tn), own(l)) for l in range(nl)],
        out_specs=pl.BlockSpec((None, MOD_ROWS, tn), lambda j: (j // nt, 0, j % nt)),
        out_shape=jax.ShapeDtypeStruct((nl, MOD_ROWS, 3 * D), F32),
        compiler_params=_params("arbitrary"),
        name="adaln",
    )(ct, *ws, *[b.reshape(1, 3 * D) for b in bs])


def _in_proj_kernel(*refs, tm, tn, acts, from_x):
    j = pl.program_id(0)
    i = pl.program_id(1)
    w_ref, o_ref, wb_ref = refs[-3:]
    sub = tm // IN_SUB

    @pl.when(i == 0)
    def _():
        wb_ref[...] = w_ref[...].astype(BF16)

    if from_x:
        xp_ref, xs_ref, g_ref, mod_ref = refs[:4]
        m = mod_ref[pl.ds(_mod_row(i * tm), 1), :]

        def lhs(rows):
            y = _rms(jnp.where(i < N_PROMPT // tm, xp_ref[rows, :], xs_ref[rows, :]), g_ref[...])
            return (y * (1.0 + m[:, D:2 * D]) + m[:, :D]).astype(BF16)
    else:
        def lhs(rows):
            return refs[0][rows, :]

    sig0, silu0 = acts
    gated = (j + 1) * tn > min(sig0, silu0)

    @pl.when(jnp.logical_not(gated))
    def _():
        if from_x:
            for s in range(IN_SUB):
                rows = slice(s * sub, (s + 1) * sub)
                o_ref[rows, :] = _dot(lhs(rows), wb_ref[...]).astype(BF16)
        else:
            o_ref[...] = _dot(refs[0][...], wb_ref[...]).astype(BF16)

    @pl.when(gated)
    def _():
        for s in range(IN_SUB):
            rows = slice(s * sub, (s + 1) * sub)
            acc = _dot(lhs(rows), wb_ref[...])
            col = j * tn + lax.broadcasted_iota(jnp.int32, acc.shape, 1)
            sig = jax.nn.sigmoid(acc)
            out = jnp.where(col >= silu0, acc * sig, jnp.where(col >= sig0, sig, acc))
            o_ref[rows, :] = out.astype(BF16)


def _in_proj(h, w, tn, acts):
    tm = 1024
    n = w.shape[1]
    from_x = isinstance(h, tuple)
    if from_x:
        xp, xs, g, mod = h
        n_p = N_PROMPT // tm
        lhs_specs = [pl.BlockSpec((tm, D), lambda j, i: (jnp.minimum(i, n_p - 1), 0)),
                     pl.BlockSpec((tm, D), lambda j, i: (jnp.maximum(i - n_p, 0), 0)),
                     pl.BlockSpec((1, D), lambda j, i: (0, 0)),
                     pl.BlockSpec((MOD_ROWS, 3 * D), lambda j, i: (0, 0))]
        lhs_args = [xp, xs, g.reshape(1, D), mod]
    else:
        lhs_specs = [pl.BlockSpec((tm, D), lambda j, i: (i, 0))]
        lhs_args = [h]
    return pl.pallas_call(
        functools.partial(_in_proj_kernel, tm=tm, tn=tn, acts=acts, from_x=from_x),
        grid=(n // tn, N_TOK // tm),
        in_specs=lhs_specs + [pl.BlockSpec((D, tn), lambda j, i: (0, j))],
        out_specs=pl.BlockSpec((tm, tn), lambda j, i: (i, j)),
        out_shape=jax.ShapeDtypeStruct((N_TOK, n), BF16),
        scratch_shapes=[pltpu.VMEM((D, tn), BF16)],
        compiler_params=_params("arbitrary", "arbitrary"),
        name="in_proj",
    )(*lhs_args, w)


def _out_proj_kernel(ap_ref, as_ref, w_ref, mod_ref, g_ref, modn_ref, *rest, tm, tile0, final, nx):
    x_refs, rest = rest[:nx], rest[nx:]
    wb_ref = rest[-1]

    @pl.when(pl.program_id(0) == 0)
    def _():
        wb_ref[...] = w_ref[...].astype(BF16)

    i = tile0 + pl.program_id(0)
    r = _mod_row(i * tm)
    gate = mod_ref[pl.ds(r, 1), :][:, 2 * D:]
    from_prompt = i < N_PROMPT // tm
    for s in range(tm // OUT_SUB):
        rows = slice(s * OUT_SUB, (s + 1) * OUT_SUB)
        a = jnp.where(from_prompt, ap_ref[rows, :], as_ref[rows, :])
        x = x_refs[0][rows, :] if nx == 1 else jnp.where(from_prompt, x_refs[0][rows, :], x_refs[1][rows, :])
        xn = x + gate * _dot(a, wb_ref[...])
        y = _rms(xn, g_ref[...])
        if final:
            rest[0][rows, :] = y
        else:
            mn = modn_ref[pl.ds(r, 1), :]
            rest[0][rows, :] = xn
            rest[1][rows, :] = (y * (1.0 + mn[:, D:2 * D]) + mn[:, :D]).astype(BF16)


def _out_proj(a_p, a_s, w, x, mod, g_next, mod_next, final, rows=(0, N_TOK)):
    tm = 512
    kw = w.shape[0]
    tile0, nt = rows[0] // tm, rows[1] // tm
    row = pl.BlockSpec((tm, D), lambda i: (i, 0))
    if final:
        out_specs, out_shape = [row], [jax.ShapeDtypeStruct((rows[1], D), F32)]
    else:
        out_specs = [row, row]
        out_shape = [jax.ShapeDtypeStruct((rows[1], D), F32), jax.ShapeDtypeStruct((rows[1], D), BF16)]
    if isinstance(x, tuple):
        x_specs, xs = list(_two_source_specs(tm, D, tile0)), list(x)
    else:
        x_specs, xs = [pl.BlockSpec((tm, D), lambda i: (tile0 + i, 0))], [x]
    return pl.pallas_call(
        functools.partial(_out_proj_kernel, tm=tm, tile0=tile0, final=final, nx=len(xs)),
        grid=(nt,),
        in_specs=[*_two_source_specs(tm, kw, tile0),
                  pl.BlockSpec((kw, D), lambda i: (0, 0)),
                  pl.BlockSpec((MOD_ROWS, 3 * D), lambda i: (0, 0)),
                  pl.BlockSpec((1, D), lambda i: (0, 0)),
                  pl.BlockSpec((MOD_ROWS, 3 * D), lambda i: (0, 0)),
                  *x_specs],
        out_specs=out_specs,
        out_shape=out_shape,
        scratch_shapes=[pltpu.VMEM((kw, D), BF16)],
        compiler_params=_params("arbitrary"),
        name="out_proj",
    )(a_p, a_s, w, mod, g_next.reshape(1, D), mod_next, *xs)


def _ret_kernel(*refs, T, HB, rope, state_in, state_out):
    refs = list(refs)
    dec_ref, q_ref, k_ref, v_ref, z_ref, gn_ref = refs[:6]
    pos = 6
    if rope:
        cos_ref, sin_ref = refs[pos:pos + 2]
        pos += 2
    if state_in:
        s0_ref = refs[pos]
        pos += 1
    o_ref = refs[pos]
    pos += 1
    if state_out:
        sn_ref = refs[pos]
        pos += 1
    kv_ref, sf_ref, sb_ref, tab_ref, gblk_ref = refs[pos:pos + 5]
    pos += 5
    if rope:
        kr_ref = refs[pos]

    hg = pl.program_id(1)
    n = T // CHUNK
    L = CHUNK
    ii = lax.broadcasted_iota(jnp.int32, (L, L), 0).astype(F32)
    jj = lax.broadcasted_iota(jnp.int32, (L, L), 1).astype(F32)
    scale = RET_DK ** -0.5

    def chunk(c):
        return slice(c * L, (c + 1) * L)

    def rotate(x, c):
        return x * cos_ref[chunk(c), :] + pltpu.roll(x, RET_DK // 2, 1) * sin_ref[chunk(c), :]

    @pl.when(pl.program_id(0) == 0)
    def _():
        for hh in range(HB):
            head = hg * HB + hh
            lg_f = _log_sigmoid(jnp.full((1, RET_DV), dec_ref[0, head], F32))
            lg_b = _log_sigmoid(jnp.full((1, RET_DV), dec_ref[1, head], F32))
            lf, lb = lg_f[:, :L], lg_b[:, :L]
            tab_ref[head, 0] = jnp.exp(lf * (L - 1.0 - ii)) * scale
            tab_ref[head, 1] = jnp.exp(lb * ii) * scale
            tab_ref[head, 2] = jnp.exp(lf * (ii + 1.0))
            tab_ref[head, 3] = jnp.exp(lb * (L - ii))
            tab_ref[head, 4] = (jnp.where(ii >= jj, jnp.exp(lf * jnp.maximum(ii - jj, 0.0)), 0.0)
                                + jnp.where(jj >= ii, jnp.exp(lb * jnp.maximum(jj - ii, 0.0)), 0.0)) * scale
            gblk_ref[head, 0] = jnp.exp(lg_f * float(L))
            gblk_ref[head, 1] = jnp.exp(lg_b * float(L))

    for hh in range(HB):
        head = hg * HB + hh
        qs = slice(hh * RET_DK, (hh + 1) * RET_DK)
        vs = slice(hh * RET_DV, (hh + 1) * RET_DV)
        k_dec_f, k_dec_b, q_dec_f, q_dec_b, mask = (tab_ref[head, t] for t in range(5))
        g_f = gblk_ref[head, 0]
        g_b = gblk_ref[head, 1]

        for c in range(n):
            kc = k_ref[chunk(c), qs].astype(F32)
            if rope:
                kc = rotate(kc, c)
                kr_ref[chunk(c), qs] = kc.astype(BF16)
            kk = jnp.concatenate([kc * k_dec_f, kc * k_dec_b], axis=1).astype(BF16)
            kv_ref[hh, c] = _dot_tn(kk, v_ref[chunk(c), vs])
        S = s0_ref[0, hh] if state_in else None
        has_f = []
        for c in range(n):
            has_f.append(S is not None)
            if S is not None:
                sf_ref[hh, c] = S.astype(BF16)
            kvc = kv_ref[hh, c, :RET_DK, :]
            S = kvc if S is None else S * g_f + kvc
        if state_out:
            sn_ref[0, hh] = S
        S = s0_ref[1, hh] if state_in else None
        has_b = [False] * n
        for c in reversed(range(n)):
            has_b[c] = S is not None
            if S is not None:
                sb_ref[hh, c] = S.astype(BF16)
            kvc = kv_ref[hh, c, RET_DK:, :]
            S = kvc if S is None else S * g_b + kvc
        if state_out:
            sn_ref[1, hh] = S

        for c in range(n):
            qc = q_ref[chunk(c), qs]
            qf = qc.astype(F32)
            if rope:
                qf = rotate(qf, c)
                qc = qf.astype(BF16)
                kc = kr_ref[chunk(c), qs]
            else:
                kc = k_ref[chunk(c), qs]
            lhs = [(_dot_nt(qc, kc) * mask).astype(BF16)]
            rhs = [v_ref[chunk(c), vs]]
            if has_f[c]:
                lhs.append((qf * q_dec_f).astype(BF16))
                rhs.append(sf_ref[hh, c])
            if has_b[c]:
                lhs.append((qf * q_dec_b).astype(BF16))
                rhs.append(sb_ref[hh, c])
            o = _dot(jnp.concatenate(lhs, axis=1), jnp.concatenate(rhs, axis=0))
            zf = z_ref[chunk(c), vs].astype(F32)
            o_ref[chunk(c), vs] = (_group_norm(o, gn_ref[:, vs]) * zf).astype(BF16)


def _retention(proj, dec, gn, T, nb, row_blk0, rope=None, state=None, want_state=False):
    HB = RET_HB_PROMPT if T == SEQ else RET_HB_LATENT
    ng = RET_HEADS // HB
    n = T // CHUNK
    qw, vw = HB * RET_DK, HB * RET_DV
    in_specs = [pl.BlockSpec(memory_space=pltpu.SMEM),
                pl.BlockSpec((T, qw), lambda b, h: (row_blk0 + b, h)),
                pl.BlockSpec((T, qw), lambda b, h: (row_blk0 + b, ng + h)),
                pl.BlockSpec((T, vw), lambda b, h: (row_blk0 + b, ng + h)),
                pl.BlockSpec((T, vw), lambda b, h: (row_blk0 + b, 2 * ng + h)),
                pl.BlockSpec((1, vw), lambda b, h: (0, h))]
    args = [dec, proj, proj, proj, proj, gn.reshape(1, RET_WIDTH)]
    if rope is not None:
        in_specs += [pl.BlockSpec((T, RET_DK), lambda b, h: (0, 0))] * 2
        args += list(rope)
    if state is not None:
        in_specs.append(pl.BlockSpec((None, 2, HB, RET_DK, RET_DV), lambda b, h: (b, 0, h, 0, 0)))
        args.append(state)
    out_specs = [pl.BlockSpec((T, vw), lambda b, h: (b, h))]
    out_shape = [jax.ShapeDtypeStruct((nb * T, RET_WIDTH), BF16)]
    if want_state:
        out_specs.append(pl.BlockSpec((None, 2, HB, RET_DK, RET_DV), lambda b, h: (b, 0, h, 0, 0)))
        out_shape.append(jax.ShapeDtypeStruct((nb, 2, RET_HEADS, RET_DK, RET_DV), F32))
    scratch = [pltpu.VMEM((HB, n, 2 * RET_DK, RET_DV), F32),
               pltpu.VMEM((HB, n, RET_DK, RET_DV), BF16),
               pltpu.VMEM((HB, n, RET_DK, RET_DV), BF16),
               pltpu.VMEM((RET_HEADS, 5, CHUNK, CHUNK), F32),
               pltpu.VMEM((RET_HEADS, 2, 1, RET_DV), F32)]
    if rope is not None:
        scratch.append(pltpu.VMEM((T, qw), BF16))
    return pl.pallas_call(
        functools.partial(_ret_kernel, T=T, HB=HB, rope=rope is not None, state_in=state is not None,
                          state_out=want_state),
        grid=(nb, ng),
        in_specs=in_specs, out_specs=out_specs, out_shape=out_shape, scratch_shapes=scratch,
        compiler_params=_params("arbitrary", "arbitrary"),
        name="retention",
    )(*args)


def _ret_full_kernel(dec_ref, q_ref, k_ref, v_ref, z_ref, gn_ref, o_ref, sn_ref, mask_ref, kdec_ref, *, T):
    scale = RET_DK ** -0.5

    @pl.when(pl.program_id(0) == 0)
    def _():
        ii = lax.broadcasted_iota(jnp.int32, (T, T), 0).astype(F32)
        jj = lax.broadcasted_iota(jnp.int32, (T, T), 1).astype(F32)
        row = lax.broadcasted_iota(jnp.int32, (T, RET_DK), 0).astype(F32)
        for head in range(RET_HEADS):
            lf = _log_sigmoid(jnp.full((1, T), dec_ref[0, head], F32))
            lb = _log_sigmoid(jnp.full((1, T), dec_ref[1, head], F32))
            mask_ref[head] = (jnp.where(ii >= jj, jnp.exp(lf * jnp.maximum(ii - jj, 0.0)), 0.0)
                              + jnp.where(jj >= ii, jnp.exp(lb * jnp.maximum(jj - ii, 0.0)), 0.0)) * scale
            kdec_ref[head, 0] = jnp.exp(lf[:, :RET_DK] * (T - 1.0 - row)) * scale
            kdec_ref[head, 1] = jnp.exp(lb[:, :RET_DK] * row) * scale

    for head in range(RET_HEADS):
        qs = slice(head * RET_DK, (head + 1) * RET_DK)
        vs = slice(head * RET_DV, (head + 1) * RET_DV)
        k = k_ref[:, qs]
        v = v_ref[:, vs]
        a = (_dot_nt(q_ref[:, qs], k) * mask_ref[head]).astype(BF16)
        o = _dot(a, v)
        o_ref[:, vs] = (_group_norm(o, gn_ref[:, vs]) * z_ref[:, vs].astype(F32)).astype(BF16)
        kf = k.astype(F32)
        kk = jnp.concatenate([kf * kdec_ref[head, 0], kf * kdec_ref[head, 1]], axis=1).astype(BF16)
        kv = _dot_tn(kk, v)
        sn_ref[0, head] = kv[:RET_DK]
        sn_ref[1, head] = kv[RET_DK:]


def _retention_full(proj, dec, gn, T, nb):
    return pl.pallas_call(
        functools.partial(_ret_full_kernel, T=T),
        grid=(nb,),
        in_specs=[pl.BlockSpec(memory_space=pltpu.SMEM),
                  pl.BlockSpec((T, RET_QK), lambda b: (b, 0)),
                  pl.BlockSpec((T, RET_QK), lambda b: (b, 1)),
                  pl.BlockSpec((T, RET_WIDTH), lambda b: (b, 1)),
                  pl.BlockSpec((T, RET_WIDTH), lambda b: (b, 2)),
                  pl.BlockSpec((1, RET_WIDTH), lambda b: (0, 0))],
        out_specs=[pl.BlockSpec((T, RET_WIDTH), lambda b: (b, 0)),
                   pl.BlockSpec((None, 2, RET_HEADS, RET_DK, RET_DV), lambda b: (b, 0, 0, 0, 0))],
        out_shape=[jax.ShapeDtypeStruct((nb * T, RET_WIDTH), BF16),
                   jax.ShapeDtypeStruct((nb, 2, RET_HEADS, RET_DK, RET_DV), F32)],
        scratch_shapes=[pltpu.VMEM((RET_HEADS, T, T), F32), pltpu.VMEM((RET_HEADS, 2, T, RET_DK), F32)],
        compiler_params=_params("arbitrary"),
        name="retention_full",
    )(dec, proj, proj, proj, proj, gn.reshape(1, RET_WIDTH))


def _ml_in_kernel(h_ref, w_ref, cw_ref, cb_ref, bq_ref, bk_ref, bv_ref, wq_ref, wk_ref, wv_ref, bias_ref,
                  oz_ref, q_ref, k_ref, v_ref, xc_ref, g_ref, wb_ref, bd_ref, conv_ref):
    j = pl.program_id(0)
    i = pl.program_id(1)
    nsub = ML_TN // ML_TILE

    @pl.when(i == 0)
    def _():
        wb_ref[...] = w_ref[...].astype(BF16)

    @pl.when((i == 0) & (j < ML_XT))
    def _():
        row = lax.broadcasted_iota(jnp.int32, (ML_TILE, ML_TILE), 0)
        col = lax.broadcasted_iota(jnp.int32, (ML_TILE, ML_TILE), 1)
        shift = ML_BLOCK.bit_length() - 1
        same_block = jnp.right_shift(row, shift) == jnp.right_shift(col, shift)
        for which, b_ref in enumerate((bq_ref, bk_ref, bv_ref)):
            for t in range(nsub):
                w = b_ref[t * ML_TILE:(t + 1) * ML_TILE, :]
                rep = w[:, ML_BLOCK - 1:ML_BLOCK]
                for dd in range(ML_BLOCK - 1):
                    rep = jnp.where((col & (ML_BLOCK - 1)) == dd, w[:, dd:dd + 1], rep)
                bd_ref[which, t] = jnp.where(same_block, rep, 0.0).astype(BF16)

    @pl.when(j >= ML_XT)
    def _():
        sub = ML_ROWS // IN_SUB
        for s in range(IN_SUB):
            rows = slice(s * sub, (s + 1) * sub)
            acc = _dot(h_ref[rows, :], wb_ref[...])
            col = j * ML_TN + lax.broadcasted_iota(jnp.int32, acc.shape, 1)
            sig = jax.nn.sigmoid(acc)
            oz_ref[rows, :] = jnp.where(col >= 2 * ML_WIDTH, acc * sig, sig).astype(BF16)

    @pl.when(j < ML_XT)
    def _():
        acc = _dot(h_ref[...], wb_ref[...])
        T = jnp.where(i < N_PROMPT // ML_ROWS, SEQ, DEC_SEQ)
        pad = ML_CONV // 2
        taps = [tap for tap in range(ML_CONV) if tap != pad]
        sub8 = lax.broadcasted_iota(jnp.int32, (8, ML_TILE), 0)
        g = jnp.where(j == 0, 1.0, 0.0) * bias_ref[...]
        for t in range(nsub):
            cols = slice(t * ML_TILE, (t + 1) * ML_TILE)
            x = acc[:, cols]
            shifted = {tap: pltpu.roll(x, (pad - tap) % ML_ROWS, 0) for tap in taps}
            conv = cb_ref[:, cols] + x * cw_ref[pad:pad + 1, cols]
            for tap in taps:
                conv = conv + shifted[tap] * cw_ref[tap:tap + 1, cols]
            conv_ref[...] = conv
            for r0 in ML_EDGE_ROWS:
                rows = slice(r0, r0 + 8)
                pos = (r0 + sub8) & (T - 1)
                fixed = cb_ref[:, cols] + x[rows] * cw_ref[pad:pad + 1, cols]
                for tap in taps:
                    ok = (pos + (tap - pad) >= 0) & (pos + (tap - pad) < T)
                    fixed = fixed + jnp.where(ok, shifted[tap][rows], 0.0) * cw_ref[tap:tap + 1, cols]
                conv_ref[rows, :] = fixed
            xcb = _silu(conv_ref[...]).astype(BF16)
            qb = _dot(xcb, bd_ref[0, t]).astype(BF16)
            kb = _dot(xcb, bd_ref[1, t]).astype(BF16)
            vb = _dot(x.astype(BF16), bd_ref[2, t]).astype(BF16)
            q_ref[:, cols] = qb
            k_ref[:, cols] = kb
            v_ref[:, cols] = vb
            xc_ref[:, cols] = xcb
            g = g + (_dot(qb, wq_ref[cols, :].astype(BF16)) + _dot(kb, wk_ref[cols, :].astype(BF16))
                     + _dot(vb, wv_ref[cols, :].astype(BF16)))
        g_ref[...] = g


def _ml_in_proj(h, w, conv_w, conv_b, wq, wk, wv, wif, bif):
    ng = wif.shape[1]
    ni = N_TOK // ML_ROWS
    xt = ML_XT
    wq, wk, wv = (a.reshape(ML_WIDTH, ML_BLOCK) for a in (wq, wk, wv))
    xcol = lambda j, i: (0, jnp.minimum(j, xt - 1))
    xrow = lambda off: (lambda j, i: (off + jnp.minimum(j, xt - 1), 0))
    x_out = pl.BlockSpec((ML_ROWS, ML_TN), lambda j, i: (jnp.where(j < xt, i, ni - 1), jnp.minimum(j, xt - 1)))
    return pl.pallas_call(
        _ml_in_kernel,
        grid=(3 * ML_WIDTH // ML_TN, ni),
        in_specs=[pl.BlockSpec((ML_ROWS, D), lambda j, i: (i, 0)),
                  pl.BlockSpec((D, ML_TN), lambda j, i: (0, j)),
                  pl.BlockSpec((ML_CONV, ML_TN), xcol),
                  pl.BlockSpec((1, ML_TN), xcol),
                  pl.BlockSpec((ML_TN, ML_BLOCK), xrow(0)),
                  pl.BlockSpec((ML_TN, ML_BLOCK), xrow(0)),
                  pl.BlockSpec((ML_TN, ML_BLOCK), xrow(0)),
                  pl.BlockSpec((ML_TN, ng), xrow(0)),
                  pl.BlockSpec((ML_TN, ng), xrow(xt)),
                  pl.BlockSpec((ML_TN, ng), xrow(2 * xt)),
                  pl.BlockSpec((1, ng), lambda j, i: (0, 0))],
        out_specs=[pl.BlockSpec((ML_ROWS, ML_TN), lambda j, i: (jnp.where(j < xt, 0, i), jnp.maximum(j - xt, 0))),
                   x_out, x_out, x_out, x_out,
                   pl.BlockSpec((None, ML_ROWS, ng),
                                lambda j, i: (jnp.minimum(j, xt - 1), jnp.where(j < xt, i, ni - 1), 0))],
        out_shape=[jax.ShapeDtypeStruct((N_TOK, 2 * ML_WIDTH), BF16)]
        + [jax.ShapeDtypeStruct((N_TOK, ML_WIDTH), BF16)] * 4
        + [jax.ShapeDtypeStruct((xt, N_TOK, ng), F32)],
        scratch_shapes=[pltpu.VMEM((D, ML_TN), BF16), pltpu.VMEM((3, ML_TN // ML_TILE, ML_TILE, ML_TILE), BF16),
                        pltpu.VMEM((ML_ROWS, ML_TILE), F32)],
        compiler_params=_params("arbitrary", "arbitrary"),
        name="mlstm_in_proj",
    )(h, w, conv_w, conv_b.reshape(1, ML_WIDTH), wq, wk, wv, wif, wif, wif, bif.reshape(1, ng))


def _dot_split(lhs_bf16, rhs):
    r1 = rhs.astype(BF16)
    e1 = rhs - r1.astype(F32)
    r2 = e1.astype(BF16)
    r3 = (e1 - r2.astype(F32)).astype(BF16)
    return _dot(lhs_bf16, r1) + _dot(lhs_bf16, r2) + _dot(lhs_bf16, r3)


def _ml_scan_kernel(*refs, T, HB, state_in, state_out):
    refs = list(refs)
    pos = 0
    m0_ref = c0_ref = n0_ref = cn_ref = nn_ref = mn_ref = None
    if state_in:
        m0_ref = refs[0]
        pos = 1
    q_ref, k_ref, v_ref, g_ref, op_ref, z_ref, xc_ref, gn_ref, sk_ref = refs[pos:pos + 9]
    pos += 9
    if state_in:
        c0_ref, n0_ref = refs[pos:pos + 2]
        pos += 2
    o_ref = refs[pos]
    pos += 1
    if state_out:
        cn_ref, nn_ref, mn_ref = refs[pos:pos + 3]
        pos += 3
    acc_ref, s_ref = refs[pos:pos + 2]

    for hh in range(HB):
        cols = slice(hh * ML_HD, (hh + 1) * ML_HD)

        def head_cols(ref):
            return ref.at[:, cols]

        def head_state(ref):
            return None if ref is None else ref.at[:, hh]

        _ml_scan_head(pl.program_id(0), pl.program_id(1) * HB + hh, m0_ref,
                      head_cols(q_ref), head_cols(k_ref), head_cols(v_ref), g_ref, head_cols(op_ref),
                      head_cols(z_ref), head_cols(xc_ref), head_cols(gn_ref), head_cols(sk_ref),
                      head_state(c0_ref), head_state(n0_ref), head_cols(o_ref),
                      head_state(cn_ref), head_state(nn_ref), head_state(mn_ref), acc_ref.at[hh], s_ref.at[hh],
                      T=T, state_in=state_in, state_out=state_out)


def _ml_scan_head(b, h, m0_ref, q_ref, k_ref, v_ref, g_ref, op_ref, z_ref, xc_ref, gn_ref, sk_ref,
                  c0_ref, n0_ref, o_ref, cn_ref, nn_ref, mn_ref, acc_ref, s_ref, *, T, state_in, state_out):
    n = T // CHUNK
    L = CHUNK
    scale = ML_HD ** -0.5

    def chunk(c):
        return slice(c * L, (c + 1) * L)

    ii = lax.broadcasted_iota(jnp.int32, (L, L), 0)
    jj = lax.broadcasted_iota(jnp.int32, (L, L), 1)
    neg = jnp.full((L, L), -jnp.inf, F32)
    zero = jnp.zeros((L, L), F32)
    tri = [jnp.where(jj <= ii, 1.0, 0.0).astype(BF16), jnp.where(jj >= ii, 1.0, 0.0).astype(BF16)]
    bias = [jnp.where(ii >= jj, zero, neg), jnp.where(ii <= jj, zero, neg)]

    g = g_ref[0]
    for part in range(1, ML_XT):
        g = g + g_ref[part]
    col = lax.broadcasted_iota(jnp.int32, g.shape, 1)
    g = jnp.where((col & (2 * ML_HEADS - 1)) >= ML_HEADS, _log_sigmoid(g), g)

    def gate_column(j):
        return jnp.sum(jnp.where(col == j, g, 0.0), axis=1, keepdims=True)


    for i in range(n):
        for j in range(n):
            s_ref[i, j] = _dot_nt(q_ref[chunk(i), :], k_ref[chunk(j), :])

    G = []
    for d in range(2):
        order = list(range(n)) if d == 0 else list(reversed(range(n)))
        gi_all = gate_column(d * 2 * ML_HEADS + h)
        gf_all = gate_column(d * 2 * ML_HEADS + ML_HEADS + h)
        m0 = jnp.full((1, 1), m0_ref[(b * 2 + d) * ML_HEADS + h], F32) if state_in else jnp.zeros((1, 1), F32)
        c_col, c_row, b_col, m_row = [None] * n, [None] * n, [None] * n, [None] * n
        offset = jnp.zeros((1, 1), F32)
        carry = m0
        for c in order:
            gi = gi_all[chunk(c)]
            gf = gf_all[chunk(c)]
            cum = _dot_split(tri[d], jnp.broadcast_to(gf, (L, L))) + offset
            offset = offset + jnp.sum(gf, axis=0, keepdims=True)
            cc = gi - cum
            c_col[c] = cc[:, :1]
            c_row[c] = jnp.transpose(cc)
            b_col[c] = cum[:, :1]
            m_row[c] = jnp.maximum(jnp.max(c_row[c] + bias[d], axis=1, keepdims=True), carry)
            carry = jnp.maximum(carry, jnp.max(c_col[c], axis=0, keepdims=True))
        G.append(dict(order=order, c_col=c_col, c_row=c_row, b_col=b_col, m_row=m_row, m0=m0,
                      m_end=carry, b_end=offset))

    done = set()
    for i in range(n):
        for d in range(2):
            e = G[d]
            js = list(range(0, i + 1)) if d == 0 else list(range(i, n))
            m_i = e["m_row"][i]
            parts = []
            tot = None
            for j in js:
                z = e["c_row"][j] - m_i
                if j == i:
                    z = z + bias[d]
                sw = s_ref[i, j] * (jnp.exp(z) * scale)
                tot = sw if tot is None else tot + sw
                parts.append(sw.astype(BF16))
            num = _dot(jnp.concatenate(parts, axis=1), v_ref[js[0] * L:(js[-1] + 1) * L, :])
            den = jnp.sum(tot, axis=1, keepdims=True)
            if state_in:
                qc = q_ref[chunk(i), :]
                w0 = jnp.exp(e["m0"] - m_i)
                num = num + _dot((qc.astype(F32) * w0).astype(BF16), c0_ref[d].astype(BF16))
                n0 = jnp.broadcast_to(n0_ref[d], (8, ML_HD)).astype(BF16)
                den = den + w0 * _dot_nt(qc, n0)[:, :1]
            hb = num / jnp.maximum(jnp.abs(den), jnp.exp(-(e["b_col"][i] + m_i)))
            if i not in done:
                done.add(i)
                acc_ref[chunk(i), :] = hb
            else:
                cell = _group_norm((acc_ref[chunk(i), :] + hb) * op_ref[chunk(i), :].astype(F32), gn_ref[...])
                mixed = cell + sk_ref[...] * xc_ref[chunk(i), :].astype(F32)
                o_ref[chunk(i), :] = (mixed * z_ref[chunk(i), :].astype(F32)).astype(BF16)

    if state_out:
        ones = jnp.ones((8, T), BF16)
        for d in range(2):
            e = G[d]
            kw = jnp.concatenate(
                [(k_ref[chunk(c), :].astype(F32) * (jnp.exp(e["c_col"][c] - e["m_end"]) * scale)).astype(BF16)
                 for c in range(n)], axis=0)
            C = _dot_tn(kw, v_ref[...])
            nvec = _dot(ones, kw)[:1]
            if state_in:
                w_end = jnp.exp(e["m0"] - e["m_end"])
                C = C + w_end * c0_ref[d]
                nvec = nvec + w_end * n0_ref[d]
            cn_ref[d] = C
            nn_ref[d] = nvec
            mn_ref[d] = jnp.broadcast_to(e["b_end"] + e["m_end"], mn_ref.shape[1:])


def _ml_scan(oz, q, k, v, xc, g, gn, skip, T, nb, row_blk0, state=None, want_state=False):
    H = ML_HEADS
    HB = ML_HB_PROMPT if T == SEQ else ML_HB_LATENT
    ng = H // HB
    hw = HB * ML_HD
    n = T // CHUNK
    tok = lambda b, h: (row_blk0 + b, h)
    head_vec = pl.BlockSpec((1, hw), lambda b, h: (0, h))
    both = lambda *tail: pl.BlockSpec((None, 2, HB, *tail), lambda b, h: (b, 0, h, 0, 0))
    in_specs, args = [], []
    if state is not None:
        c0, n0, m0 = state
        in_specs.append(pl.BlockSpec(memory_space=pltpu.SMEM))
        args.append(m0.reshape(nb * 2 * H))
    in_specs += [pl.BlockSpec((T, hw), tok)] * 3
    in_specs += [pl.BlockSpec((ML_XT, T, g.shape[2]), lambda b, h: (0, row_blk0 + b, 0)),
                 pl.BlockSpec((T, hw), tok),
                 pl.BlockSpec((T, hw), lambda b, h: (row_blk0 + b, ng + h)),
                 pl.BlockSpec((T, hw), tok), head_vec, head_vec]
    args += [q, k, v, g, oz, oz, xc, gn.reshape(1, ML_WIDTH), skip.reshape(1, ML_WIDTH)]
    if state is not None:
        in_specs += [both(ML_HD, ML_HD), both(1, ML_HD)]
        args += [c0, n0.reshape(nb, 2, H, 1, ML_HD)]
    out_specs = [pl.BlockSpec((T, hw), lambda b, h: (b, h))]
    out_shape = [jax.ShapeDtypeStruct((nb * T, ML_WIDTH), BF16)]
    if want_state:
        out_specs += [both(ML_HD, ML_HD), both(1, ML_HD), both(1, 128)]
        out_shape += [jax.ShapeDtypeStruct((nb, 2, H, ML_HD, ML_HD), F32),
                      jax.ShapeDtypeStruct((nb, 2, H, 1, ML_HD), F32),
                      jax.ShapeDtypeStruct((nb, 2, H, 1, 128), F32)]
    scratch = [pltpu.VMEM((HB, T, ML_HD), F32), pltpu.VMEM((HB, n, n, CHUNK, CHUNK), F32)]
    return pl.pallas_call(
        functools.partial(_ml_scan_kernel, T=T, HB=HB, state_in=state is not None, state_out=want_state),
        grid=(nb, ng),
        in_specs=in_specs, out_specs=out_specs, out_shape=out_shape, scratch_shapes=scratch,
        compiler_params=_params("arbitrary", "arbitrary"),
        name="mlstm_scan",
    )(*args)


def _att_rope(x, cos, sin):
    lane = lax.broadcasted_iota(jnp.int32, x.shape, 1)
    half = ATT_HD // 2
    rot = jnp.where((lane & (ATT_HD - 1)) < half, pltpu.roll(x, 128 - half, 1), pltpu.roll(x, half, 1))
    return x * cos + rot * sin


def _att_heads(qms, sinks, key_sets):
    scores = [[_dot_nt(k_both, qm) if bias is None else _dot_nt(k_both, qm) + bias
               for k_both, _, bias in key_sets] for qm in qms]
    ms = []
    for sink, per_set in zip(sinks, scores):
        m = sink
        for s in per_set:
            m = jnp.maximum(m, jnp.max(s, axis=0, keepdims=True))
        ms.append(m)
    probs = [[jnp.exp(s - m) for s in per_set] for m, per_set in zip(ms, scores)]
    outs = []
    for sink, m, per_set in zip(sinks, ms, probs):
        den = jnp.exp(sink - m)
        out = None
        for p, (_, v_t, _) in zip(per_set, key_sets):
            den = den + jnp.sum(p, axis=0, keepdims=True)
            o = _dot(v_t, p.astype(BF16))
            out = o if out is None else out + o
        outs.append(out * (1.0 / den))
    return outs


def _att_prepare_kv(k, v, kb_ref, vt_ref, kv, nblk):
    lane = lax.broadcasted_iota(jnp.int32, k.shape, 1)
    native = lane < ATT_HD if kv % 2 == 0 else lane >= ATT_HD
    kb_ref[kv] = jnp.where(native, k, pltpu.roll(k, ATT_HD, 1)).astype(BF16)
    r0 = (kv % 2) * ATT_HD
    blk = k.shape[0] // nblk
    for c in range(nblk):
        vt_ref[kv, c] = jnp.transpose(v[c * blk:(c + 1) * blk, :])[r0:r0 + ATT_HD, :].astype(BF16)


def _att_group(q_ref, z0_ref, z1_ref, o_ref, sink_ref, rows, kv, key_sets):
    lo = lax.broadcasted_iota(jnp.int32, (CHUNK, 128), 1) < ATT_HD
    zero = jnp.zeros((CHUNK, 128), BF16)
    qms, sinks = [], []
    for g in range(ATT_GROUP):
        col = (2 * kv + g // 2) * 128
        q = q_ref[rows, col:col + 128]
        qms.append(jnp.where(lo, q, zero) if g % 2 == 0 else jnp.where(lo, zero, q))
        sinks.append(jnp.full((1, CHUNK), sink_ref[kv * ATT_GROUP + g], F32))
    heads = _att_heads(qms, sinks, key_sets)
    for p in range(2):
        col = (2 * kv + p) * 128
        out = jnp.transpose(jnp.concatenate(heads[2 * p:2 * p + 2], axis=0))
        z_ref = z0_ref if col < ATT_WIDTH // 2 else z1_ref
        zc = col % (ATT_WIDTH // 2)
        zf = z_ref[rows, zc:zc + 128].astype(F32)
        o_ref[rows, col:col + 128] = (out * zf).astype(BF16)


def _att_ctx_kernel(sink_ref, q_ref, k_ref, v_ref, z0_ref, z1_ref, o_ref, kt_ref, vto_ref, qs_ref, kb_ref, vt_ref):
    T = q_ref.shape[0]
    qs_ref[...] = q_ref[...] * (ATT_HD ** -0.5)
    for kv in range(ATT_KV):
        slab = (kv // 2) * 128
        k = k_ref[:, slab:slab + 128].astype(F32)
        v = v_ref[:, slab:slab + 128].astype(F32)
        _att_prepare_kv(k, v, kb_ref, vt_ref, kv, 1)
        r0 = (kv % 2) * ATT_HD
        kt_ref[kv] = jnp.transpose(k)[r0:r0 + ATT_HD, :]
        vto_ref[kv] = jnp.transpose(v)[r0:r0 + ATT_HD, :]
    for c in range(T // CHUNK):
        rows = slice(c * CHUNK, (c + 1) * CHUNK)
        for kv in range(ATT_KV):
            _att_group(qs_ref, z0_ref, z1_ref, o_ref, sink_ref, rows, kv, [(kb_ref[kv], vt_ref[kv, 0], None)])


def _att_ctx(proj, sink):
    T = SEQ
    half = ATT_WIDTH // 2
    return pl.pallas_call(
        _att_ctx_kernel,
        grid=(BATCH,),
        in_specs=[pl.BlockSpec(memory_space=pltpu.SMEM),
                  pl.BlockSpec((T, ATT_WIDTH), lambda b: (b, 0)),
                  pl.BlockSpec((T, ATT_KVW), lambda b: (b, ATT_WIDTH // ATT_KVW)),
                  pl.BlockSpec((T, ATT_KVW), lambda b: (b, ATT_WIDTH // ATT_KVW + 1)),
                  pl.BlockSpec((T, half), lambda b: (b, (ATT_WIDTH + 2 * ATT_KVW) // half)),
                  pl.BlockSpec((T, half), lambda b: (b, (ATT_WIDTH + 2 * ATT_KVW) // half + 1))],
        out_specs=[pl.BlockSpec((T, ATT_WIDTH), lambda b: (b, 0)),
                   pl.BlockSpec((None, ATT_KV, ATT_HD, T), lambda b: (b, 0, 0, 0)),
                   pl.BlockSpec((None, ATT_KV, ATT_HD, T), lambda b: (b, 0, 0, 0))],
        out_shape=[jax.ShapeDtypeStruct((N_PROMPT, ATT_WIDTH), BF16),
                   jax.ShapeDtypeStruct((BATCH, ATT_KV, ATT_HD, T), F32),
                   jax.ShapeDtypeStruct((BATCH, ATT_KV, ATT_HD, T), F32)],
        scratch_shapes=[pltpu.VMEM((T, ATT_WIDTH), BF16), pltpu.VMEM((ATT_KV, T, 128), BF16),
                        pltpu.VMEM((ATT_KV, 1, ATT_HD, T), BF16)],
        compiler_params=_params("arbitrary"),
        name="att_ctx",
    )(sink, proj, proj, proj, proj, proj)


def _att_win_kernel(sink_ref, q_ref, k_ref, v_ref, z0_ref, z1_ref, kc_ref, vc_ref, cos_ref, sin_ref,
                    o_ref, qs_ref, kb_ref, vt_ref, kcb_ref, vct_ref):
    T = q_ref.shape[0]
    L = CHUNK
    n = T // L
    cos = cos_ref[...]
    sin = sin_ref[...]
    for c0 in range(0, ATT_WIDTH, 128):
        q = _att_rope(q_ref[:, c0:c0 + 128].astype(F32), cos, sin)
        qs_ref[:, c0:c0 + 128] = (q * (ATT_HD ** -0.5)).astype(BF16)
    for kv in range(ATT_KV):
        slab = (kv // 2) * 128
        _att_prepare_kv(_att_rope(k_ref[:, slab:slab + 128].astype(F32), cos, sin),
                        v_ref[:, slab:slab + 128].astype(F32), kb_ref, vt_ref, kv, n)
        _att_prepare_kv(kc_ref[:, slab:slab + 128], vc_ref[:, slab:slab + 128], kcb_ref, vct_ref, kv, 1)

    jj = lax.broadcasted_iota(jnp.int32, (L, L), 0)
    ii = lax.broadcasted_iota(jnp.int32, (L, L), 1)
    neg = jnp.full((L, L), -jnp.inf, F32)
    bias_prev = jnp.where(jj >= ii, 0.0, neg)
    bias_next = jnp.where(jj <= ii, 0.0, neg)

    def body(c, carry):
        rows = pl.ds(pl.multiple_of(c * L, L), L)
        c_prev = jnp.maximum(c - 1, 0)
        c_next = jnp.minimum(c + 1, n - 1)
        b_prev = jnp.where(c > 0, bias_prev, neg)
        b_next = jnp.where(c < n - 1, bias_next, neg)
        for kv in range(ATT_KV):
            def keys(cb):
                return kb_ref[kv, pl.ds(pl.multiple_of(cb * L, L), L), :]
            key_sets = [(keys(c_prev), vt_ref[kv, c_prev], b_prev),
                        (keys(c), vt_ref[kv, c], None),
                        (keys(c_next), vt_ref[kv, c_next], b_next),
                        (kcb_ref[kv], vct_ref[kv, 0], None)]
            _att_group(qs_ref, z0_ref, z1_ref, o_ref, sink_ref, rows, kv, key_sets)
        return carry

    lax.fori_loop(0, n, body, 0)


def _att_win(proj, sink, k_ctx, v_ctx, cos, sin):
    T = DEC_SEQ
    half = ATT_WIDTH // 2
    rb = N_PROMPT // T
    return pl.pallas_call(
        _att_win_kernel,
        grid=(DEC_BATCH,),
        in_specs=[pl.BlockSpec(memory_space=pltpu.SMEM),
                  pl.BlockSpec((T, ATT_WIDTH), lambda b: (rb + b, 0)),
                  pl.BlockSpec((T, ATT_KVW), lambda b: (rb + b, ATT_WIDTH // ATT_KVW)),
                  pl.BlockSpec((T, ATT_KVW), lambda b: (rb + b, ATT_WIDTH // ATT_KVW + 1)),
                  pl.BlockSpec((T, half), lambda b: (rb + b, (ATT_WIDTH + 2 * ATT_KVW) // half)),
                  pl.BlockSpec((T, half), lambda b: (rb + b, (ATT_WIDTH + 2 * ATT_KVW) // half + 1)),
                  pl.BlockSpec((None, PAST_LEN, ATT_KVW), lambda b: (b, 0, 0)),
                  pl.BlockSpec((None, PAST_LEN, ATT_KVW), lambda b: (b, 0, 0)),
                  pl.BlockSpec((T, 128), lambda b: (0, 0)),
                  pl.BlockSpec((T, 128), lambda b: (0, 0))],
        out_specs=pl.BlockSpec((T, ATT_WIDTH), lambda b: (b, 0)),
        out_shape=jax.ShapeDtypeStruct((N_LATENT, ATT_WIDTH), BF16),
        scratch_shapes=[pltpu.VMEM((T, ATT_WIDTH), BF16),
                        pltpu.VMEM((ATT_KV, T, 128), BF16),
                        pltpu.VMEM((ATT_KV, T // CHUNK, ATT_HD, CHUNK), BF16),
                        pltpu.VMEM((ATT_KV, PAST_LEN, 128), BF16),
                        pltpu.VMEM((ATT_KV, 1, ATT_HD, PAST_LEN), BF16)],
        compiler_params=_params("arbitrary"),
        name="att_win",
    )(sink, proj, proj, proj, proj, proj, k_ctx, v_ctx, cos, sin)


def _rope_tables(T, hd, reps):
    rows = T // GRID_W
    row = np.repeat(np.arange(rows, dtype=np.float64), GRID_W)
    col = np.tile(np.arange(GRID_W, dtype=np.float64), rows)
    nf = hd // 4
    inv = ROPE_BASE ** (-np.arange(nf, dtype=np.float64) / nf)
    ang = np.concatenate([row[:, None] * inv[None, :], col[:, None] * inv[None, :]], axis=-1)
    cos, sin = np.cos(ang), np.sin(ang)
    return (jnp.asarray(np.tile(np.concatenate([cos, cos], axis=-1), (1, reps)), F32),
            jnp.asarray(np.tile(np.concatenate([-sin, sin], axis=-1), (1, reps)), F32))


def _retention_layer(h, w_in, decay_f, decay_b, gn, state, rope):
    z0 = 2 * RET_QK + RET_WIDTH
    proj = _in_proj(h, w_in, 2048, (z0, z0))
    dec = jnp.stack([decay_f, decay_b]).astype(F32)
    a_p, new_state = _retention_full(proj, dec, gn, SEQ, BATCH)
    (a_s,) = _retention(proj, dec, gn, DEC_SEQ, DEC_BATCH, N_PROMPT // DEC_SEQ, rope=rope, state=state)
    return a_p, a_s, new_state


def kernel(x_prompt, x_sample, c, c_ctx, state_l0_ret, state_l1_C, state_l1_n, state_l1_m, cache_l2_k, cache_l2_v, state_l3_ret, norm_l0, ada_w_l0, ada_b_l0, w_in_l0, w_out_l0, ret_decay_f_l0, ret_decay_b_l0, ret_gn_l0, norm_l1, ada_w_l1, ada_b_l1, w_in_l1, w_out_l1, conv_w_l1, conv_b_l1, wq_l1, wk_l1, wv_l1, wif_f_l1, bif_f_l1, wif_b_l1, bif_b_l1, gn_l1, skip_l1, norm_l2, ada_w_l2, ada_b_l2, w_in_l2, w_out_l2, sink_l2, norm_l3, ada_w_l3, ada_b_l3, w_in_l3, w_out_l3, ret_decay_f_l3, ret_decay_b_l3, ret_gn_l3, final_norm):
    ct = jnp.concatenate([c_ctx[:, None], c.T, jnp.zeros((D, MOD_ROWS - 1 - DEC_BATCH), F32)], axis=1)
    mod_all = _adaln(ct, (ada_w_l0, ada_w_l1, ada_w_l2, ada_w_l3), (ada_b_l0, ada_b_l1, ada_b_l2, ada_b_l3))
    mods = [mod_all[l] for l in range(mod_all.shape[0])]
    rope_ret = _rope_tables(DEC_SEQ, RET_DK, 1)
    rope_att = _rope_tables(DEC_SEQ, ATT_HD, 2)

    x0 = (x_prompt.reshape(N_PROMPT, D), x_sample.reshape(N_LATENT, D))

    a_p, a_s, new_l0_ret = _retention_layer((*x0, norm_l0, mods[0]), w_in_l0, ret_decay_f_l0, ret_decay_b_l0,
                                            ret_gn_l0, state_l0_ret, rope_ret)
    x, h = _out_proj(a_p, a_s, w_out_l0, x0, mods[0], norm_l1, mods[1], False)

    wif = jnp.concatenate([wif_f_l1, wif_b_l1], axis=1)
    bif = jnp.concatenate([bif_f_l1, bif_b_l1])
    oz, q, k, v, xc, g = _ml_in_proj(h, w_in_l1, conv_w_l1, conv_b_l1, wq_l1, wk_l1, wv_l1, wif, bif)
    a_p, new_l1_C, n_new, m_new = _ml_scan(oz, q, k, v, xc, g, gn_l1, skip_l1, SEQ, BATCH, 0, want_state=True)
    (a_s,) = _ml_scan(oz, q, k, v, xc, g, gn_l1, skip_l1, DEC_SEQ, DEC_BATCH, N_PROMPT // DEC_SEQ,
                      state=(state_l1_C, state_l1_n, state_l1_m))
    new_l1_n = n_new.reshape(BATCH, 2, ML_HEADS, ML_HD)
    new_l1_m = m_new[:, :, :, 0, 0]
    x, h = _out_proj(a_p, a_s, w_out_l1, x, mods[1], norm_l2, mods[2], False)

    proj = _in_proj(h, w_in_l2, 2 * ATT_WIDTH + 2 * ATT_KVW, (ATT_WIDTH + 2 * ATT_KVW,) * 2)
    a_p, k_t, v_t = _att_ctx(proj, sink_l2)
    new_l2_k = jnp.transpose(k_t, (0, 3, 1, 2))
    new_l2_v = jnp.transpose(v_t, (0, 3, 1, 2))
    a_s = _att_win(proj, sink_l2, cache_l2_k.reshape(DEC_BATCH, PAST_LEN, ATT_KVW),
                   cache_l2_v.reshape(DEC_BATCH, PAST_LEN, ATT_KVW), rope_att[0], rope_att[1])
    x, h = _out_proj(a_p, a_s, w_out_l2, x, mods[2], norm_l3, mods[3], False)

    a_p, a_s, new_l3_ret = _retention_layer(h, w_in_l3, ret_decay_f_l3, ret_decay_b_l3, ret_gn_l3, state_l3_ret,
                                            rope_ret)
    (y_p,) = _out_proj(a_p, a_s, w_out_l3, x, mods[3], final_norm, mods[3], True, rows=(0, N_PROMPT))
    (y_s,) = _out_proj(a_p, a_s, w_out_l3, x, mods[3], final_norm, mods[3], True, rows=(N_PROMPT, N_LATENT))

    y_prompt = y_p.reshape(BATCH, SEQ, D)
    y_sample = y_s.reshape(DEC_BATCH, DEC_SEQ, D)
    return (y_prompt, y_sample, new_l0_ret, new_l1_C, new_l1_n, new_l1_m, new_l2_k, new_l2_v, new_l3_ret)
```
